```python
import jax, jax.numpy as jnp
from jax import lax
import numpy as np


D_MODEL = 1024
BATCH = 4
SEQ = 4096
DEPTH = 2
DEC_BATCH = 32
DEC_SEQ = 1
PAST_LEN = 8192
PAGE_SIZE = 128

N_EVEN = (DEPTH + 1) // 2
N_ODD = DEPTH // 2
EPS = 1e-6
W_A = D_MODEL
CONV_W = 3
H_B = 16
HD_B = 64
W_B = H_B * HD_B
DIL_PATTERNS = ((128, 1), (512, 4), (2048, 16))
MAX_WINDOW = 2048
POOL_SIZES = (2, 4, 8, 16)
N_POOL = 4
W_C = D_MODEL
G_C = W_C // N_POOL
POOL_MAX = 16
H_D = 4
DK_D = 256
DV_D = 256
W_D = H_D * DV_D
RET_CHUNK = 128

kernel_name = 'hybrid_conv_dilattn_pool_retention_step'


def rmsnorm(x, g):
    xf = x.astype(jnp.float32)
    y = xf * lax.rsqrt(jnp.mean(xf * xf, axis=-1, keepdims=True) + EPS)
    return (y * g.astype(jnp.float32)).astype(x.dtype)


def adaln(c, w, b):
    mod = (jax.nn.silu(c) @ w + b)[:, None, :]
    return jnp.split(mod, 3, axis=-1)


def split_cols(z, widths):
    idx = np.cumsum(widths)[:-1].tolist()
    return jnp.split(z, idx, axis=-1)


def alibi_slopes(n):
    return jnp.asarray(2.0 ** (-8.0 * np.arange(1, n + 1) / n), dtype=jnp.float32)


def retention_log_decay():
    return jnp.asarray(np.log(1.0 - 2.0 ** (-5.0 - np.arange(H_D))), dtype=jnp.float32)


def short_conv(bg, cg, xv, cw, cb, prev):
    u = cg * xv
    ext = jnp.concatenate([prev.astype(u.dtype), u], axis=1)
    L = u.shape[1]
    conv = cb + sum(ext[:, j:j + L] * cw[j] for j in range(CONV_W))
    return bg * conv, ext[:, L:]


def combine_patterns(outs, lses):
    wts = jax.nn.softmax(jnp.stack(lses), axis=0)
    return jnp.sum(wts[..., None] * jnp.stack(outs), axis=0)


def dilated_attn_prompt(q, k, v, slopes):
    Bn, S, H, Dh = q.shape
    qf = q.astype(jnp.float32) * (Dh ** -0.5)
    kf = k.astype(jnp.float32)
    vf = v.astype(jnp.float32)
    outs, lses = [], []
    for w, d in DIL_PATTERNS:
        nk = w // d
        span = nk * d
        sp = -(-S // span) * span
        nb = sp // span
        pad = ((0, 0), (0, sp - S), (0, 0), (0, 0))
        qs, ks, vs = (jnp.pad(a, pad).reshape(Bn, nb, nk, d, H, Dh) for a in (qf, kf, vf))

        def with_prev(a):
            prev = jnp.pad(a, ((0, 0), (1, 0), (0, 0), (0, 0), (0, 0), (0, 0)))[:, :nb]
            return jnp.concatenate([prev, a], axis=2)

        kk, vv = with_prev(ks), with_prev(vs)
        s = jnp.einsum('bnirhd,bnjrhd->bnrhij', qs, kk)
        qi = jnp.arange(nk)[:, None]
        kj = jnp.arange(2 * nk)[None, :]
        dist = nk + qi - kj
        blk = jnp.arange(nb)[:, None, None]
        valid = (dist >= 0) & (dist <= nk) & (blk * nk + kj - nk >= 0)
        bias = -slopes[:, None, None] * (dist * d).astype(jnp.float32)
        s = jnp.where(valid[None, :, None, None], s + bias, -jnp.inf)
        m = jnp.max(s, axis=-1, keepdims=True)
        p = jnp.exp(s - m)
        l = jnp.sum(p, axis=-1)
        o = jnp.einsum('bnrhij,bnjrhd->bnirhd', p, vv) / jnp.transpose(l, (0, 1, 4, 2, 3))[..., None]
        lse = jnp.transpose(m[..., 0] + jnp.log(l), (0, 1, 4, 2, 3))
        outs.append(o.reshape(Bn, sp, H, Dh)[:, :S])
        lses.append(lse.reshape(Bn, sp, H)[:, :S])
    return combine_patterns(outs, lses)


def dilated_attn_decode(q, k, v, buf_k, buf_v, slopes):
    Bn, L, H, Dh = q.shape
    wb = buf_k.shape[1]
    qf = q.astype(jnp.float32) * (Dh ** -0.5)
    allk = jnp.concatenate([buf_k.astype(jnp.float32), k.astype(jnp.float32)], axis=1)
    allv = jnp.concatenate([buf_v.astype(jnp.float32), v.astype(jnp.float32)], axis=1)
    outs, lses = [], []
    for w, d in DIL_PATTERNS:
        nk = w // d
        steps = jnp.arange(nk + 1)
        idx = wb + jnp.arange(L)[:, None] - steps[None, :] * d
        valid = idx >= 0
        idx = jnp.maximum(idx, 0)
        kg, vg = allk[:, idx], allv[:, idx]
        s = jnp.einsum('blhd,blkhd->blhk', qf, kg) - slopes[:, None] * (steps * d).astype(jnp.float32)
        s = jnp.where(valid[None, :, None, :], s, -jnp.inf)
        m = jnp.max(s, axis=-1, keepdims=True)
        p = jnp.exp(s - m)
        l = jnp.sum(p, axis=-1)
        outs.append(jnp.einsum('blhk,blkhd->blhd', p, vg) / l[..., None])
        lses.append(m[..., 0] + jnp.log(l))
    return combine_patterns(outs, lses)


def pool_mix(u, prev, pos0, pw, ps):
    Bn, L, _ = u.shape
    P = POOL_MAX - 1
    ext = jnp.concatenate([prev.astype(u.dtype), u], axis=1)
    ef = ext.astype(jnp.float32)
    cs = jnp.concatenate([jnp.zeros((Bn, 1, W_C), jnp.float32), jnp.cumsum(ef, axis=1)], axis=1)
    pos = (pos0 + jnp.arange(L)).astype(jnp.float32)[None, :, None]
    hi = cs[:, P + 1:P + 1 + L]
    groups = []
    for gi, w in enumerate(POOL_SIZES):
        sl = slice(gi * G_C, (gi + 1) * G_C)
        lo = cs[:, P + 1 - w:P + 1 - w + L, sl]
        cnt = jnp.minimum(float(w), pos + 1.0)
        groups.append((hi[..., sl] - lo) / cnt - ef[:, P:, sl])
    pooled = jnp.stack(groups, axis=2)
    mixed = jnp.einsum('blgc,gce->blge', pooled, pw.astype(jnp.float32)).reshape(Bn, L, W_C)
    return (mixed * ps.astype(jnp.float32)).astype(u.dtype), ext[:, L:]


def retention_chunk(state, q, k, v, log_g):
    L = q.shape[1]
    t = jnp.arange(L, dtype=jnp.float32)
    diff = t[:, None] - t[None, :]
    decay = jnp.where(diff >= 0, jnp.exp(jnp.maximum(diff, 0.0)[None] * log_g[:, None, None]), 0.0)
    scores = jnp.einsum('bihk,bjhk->bhij', q, k) * decay
    inner = jnp.einsum('bhij,bjhv->bihv', scores, v)
    cross = jnp.einsum('bihk,bhkv->bihv', q, state) * jnp.exp((t + 1.0)[:, None] * log_g[None, :])[None, :, :, None]
    kd = k * jnp.exp((L - 1.0 - t)[:, None] * log_g[None, :])[None, :, :, None]
    new_state = jnp.exp(L * log_g)[None, :, None, None] * state + jnp.einsum('bjhk,bjhv->bhkv', kd, v)
    return new_state, inner + cross


def retention_prompt(q, k, v, log_g):
    Bn, S, H, Dk = q.shape
    nc = S // RET_CHUNK

    def chunks(a):
        return a.reshape(Bn, nc, RET_CHUNK, H, a.shape[-1]).transpose(1, 0, 2, 3, 4)

    s0 = jnp.zeros((Bn, H, Dk, v.shape[-1]), jnp.float32)
    s_fin, o = lax.scan(lambda st, xs: retention_chunk(st, xs[0], xs[1], xs[2], log_g), s0,
                        (chunks(q), chunks(k), chunks(v)))
    return s_fin, o.transpose(1, 0, 2, 3, 4).reshape(Bn, S, H, v.shape[-1])


def even_layer(x, c, g, aw, ab, w_in, cw, cb, w_out, conv_prev, buf_k, buf_v):
    Bn, L, _ = x.shape
    shift, scale, gate = adaln(c, aw, ab)
    h = rmsnorm(x, g) * (1 + scale) + shift
    z = h @ w_in
    bg, cg, xv, ga, q, k, v, gb = split_cols(z, [W_A, W_A, W_A, W_A, W_B, W_B, W_B, W_B])
    if conv_prev is None:
        conv_prev = jnp.zeros((Bn, CONV_W - 1, W_A), x.dtype)
    y_a, conv_new = short_conv(bg, cg, xv, cw, cb, conv_prev)
    y_a = y_a * jax.nn.silu(ga)
    q, k, v = (a.reshape(Bn, L, H_B, HD_B) for a in (q, k, v))
    slopes = alibi_slopes(H_B)
    if buf_k is None:
        o = dilated_attn_prompt(q, k, v, slopes)
        keep = min(MAX_WINDOW, L)
        k_new, v_new = k[:, L - keep:], v[:, L - keep:]
    else:
        o = dilated_attn_decode(q, k, v, buf_k, buf_v, slopes)
        k_new, v_new = k, v
    y_b = o.reshape(Bn, L, W_B).astype(x.dtype) * jax.nn.silu(gb)
    out = jnp.concatenate([y_a, y_b], axis=-1) @ w_out
    return x + gate * out, conv_new, k_new, v_new


def odd_layer(x, c, g, aw, ab, w_in, pw, ps, w_out, pool_prev, ret_prev, pos0):
    Bn, L, _ = x.shape
    shift, scale, gate = adaln(c, aw, ab)
    h = rmsnorm(x, g) * (1 + scale) + shift
    z = h @ w_in
    u, gc, q, k, v, gd = split_cols(z, [W_C, W_C, H_D * DK_D, H_D * DK_D, W_D, W_D])
    if pool_prev is None:
        pool_prev = jnp.zeros((Bn, POOL_MAX - 1, W_C), x.dtype)
    y_c, pool_new = pool_mix(u, pool_prev, pos0, pw, ps)
    y_c = y_c * jax.nn.silu(gc)
    qf = q.reshape(Bn, L, H_D, DK_D).astype(jnp.float32)
    kf = k.reshape(Bn, L, H_D, DK_D).astype(jnp.float32) * (DK_D ** -0.5)
    vf = v.reshape(Bn, L, H_D, DV_D).astype(jnp.float32)
    log_g = retention_log_decay()
    if ret_prev is None:
        ret_new, o = retention_prompt(qf, kf, vf, log_g)
    else:
        ret_new, o = retention_chunk(ret_prev.astype(jnp.float32), qf, kf, vf, log_g)
    o = o * lax.rsqrt(jnp.mean(o * o, axis=-1, keepdims=True) + EPS)
    y_d = o.reshape(Bn, L, W_D).astype(x.dtype) * jax.nn.silu(gd)
    out = jnp.concatenate([y_c, y_d], axis=-1) @ w_out
    return x + gate * out, pool_new, ret_new.astype(x.dtype)


def setup_inputs(seed: int = 0) -> dict:
    key = jax.random.key(seed)
    keys = jax.random.split(key, 32)
    cnt = [0]

    def nrm(shape, s):
        kk = keys[cnt[0]]
        cnt[0] += 1
        return jax.random.normal(kk, shape, jnp.float32) * s

    d = D_MODEL
    wb = min(MAX_WINDOW, PAST_LEN)
    in_e = 4 * W_A + 4 * W_B
    in_o = 2 * W_C + 2 * H_D * DK_D + 2 * W_D
    return {
        'x_prompt': nrm((BATCH, SEQ, d), 1.0),
        'x_sample': nrm((DEC_BATCH, DEC_SEQ, d), 1.0),
        'c_prompt': nrm((BATCH, d), 1.0),
        'c_sample': nrm((DEC_BATCH, d), 1.0),
        'state_conv': nrm((N_EVEN, DEC_BATCH, CONV_W - 1, W_A), 1.0),
        'cache_win_k': nrm((N_EVEN, DEC_BATCH, wb, H_B, HD_B), 1.0),
        'cache_win_v': nrm((N_EVEN, DEC_BATCH, wb, H_B, HD_B), 1.0),
        'state_pool': nrm((N_ODD, DEC_BATCH, POOL_MAX - 1, W_C), 1.0),
        'state_ret': nrm((N_ODD, DEC_BATCH, H_D, DK_D, DV_D), 0.1),
        'norm_e': 1.0 + nrm((N_EVEN, d), 0.02),
        'ada_w_e': nrm((N_EVEN, d, 3 * d), 0.3 * d ** -0.5),
        'ada_b_e': nrm((N_EVEN, 3 * d), 0.02),
        'w_in_e': nrm((N_EVEN, d, in_e), d ** -0.5),
        'conv_w': nrm((N_EVEN, CONV_W, W_A), CONV_W ** -0.5),
        'conv_b': nrm((N_EVEN, W_A), 0.02),
        'w_out_e': nrm((N_EVEN, W_A + W_B, d), (W_A + W_B) ** -0.5),
        'norm_o': 1.0 + nrm((N_ODD, d), 0.02),
        'ada_w_o': nrm((N_ODD, d, 3 * d), 0.3 * d ** -0.5),
        'ada_b_o': nrm((N_ODD, 3 * d), 0.02),
        'w_in_o': nrm((N_ODD, d, in_o), d ** -0.5),
        'pool_w': nrm((N_ODD, N_POOL, G_C, G_C), G_C ** -0.5),
        'pool_scale': 1.0 + nrm((N_ODD, W_C), 0.1),
        'w_out_o': nrm((N_ODD, W_C + W_D, d), (W_C + W_D) ** -0.5),
        'norm_f': 1.0 + nrm((d,), 0.02),
    }


def reference(x_prompt, x_sample, c_prompt, c_sample, state_conv, cache_win_k, cache_win_v,
              state_pool, state_ret, norm_e, ada_w_e, ada_b_e, w_in_e, conv_w, conv_b, w_out_e,
              norm_o, ada_w_o, ada_b_o, w_in_o, pool_w, pool_scale, w_out_o, norm_f):
    xp, xs = x_prompt, x_sample
    conv_p, conv_s, kp_l, ks_l, vp_l, vs_l = [], [], [], [], [], []
    pool_p, pool_s, ret_p, ret_s = [], [], [], []
    for layer in range(DEPTH):
        i = layer // 2
        if layer % 2 == 0:
            pe = (norm_e[i], ada_w_e[i], ada_b_e[i], w_in_e[i], conv_w[i], conv_b[i], w_out_e[i])
            xp, cst, kn, vn = even_layer(xp, c_prompt, *pe, None, None, None)
            conv_p.append(cst); kp_l.append(kn); vp_l.append(vn)
            xs, cst, kn, vn = even_layer(xs, c_sample, *pe, state_conv[i], cache_win_k[i], cache_win_v[i])
            conv_s.append(cst); ks_l.append(kn); vs_l.append(vn)
        else:
            po = (norm_o[i], ada_w_o[i], ada_b_o[i], w_in_o[i], pool_w[i], pool_scale[i], w_out_o[i])
            xp, pst, rst = odd_layer(xp, c_prompt, *po, None, None, 0)
            pool_p.append(pst); ret_p.append(rst)
            xs, pst, rst = odd_layer(xs, c_sample, *po, state_pool[i], state_ret[i], PAST_LEN)
            pool_s.append(pst); ret_s.append(rst)
    y_prompt = rmsnorm(xp, norm_f)
    y_sample = rmsnorm(xs, norm_f)
    return (y_prompt, y_sample, jnp.stack(conv_p), jnp.stack(conv_s), jnp.stack(kp_l), jnp.stack(ks_l),
            jnp.stack(vp_l), jnp.stack(vs_l), jnp.stack(pool_p), jnp.stack(pool_s), jnp.stack(ret_p), jnp.stack(ret_s))
```

```python
import functools

import numpy as np
import jax
import jax.numpy as jnp
from jax import lax
from jax.experimental import pallas as pl
from jax.experimental.pallas import tpu as pltpu

F32 = jnp.float32
BF16 = jnp.bfloat16

D_MODEL = 1024
EPS = 1e-6
H_B = 16
HD_B = 64
N_PAIR = H_B // 2
LANES = 128
NK = 128
DILATIONS = (1, 4, 16)
POOL_SIZES = (2, 4, 8, 16)
POOL_PREV = 16
H_D = 4
DK_D = 256
RET_CHUNK = 128
TN = 256
MASK_DIST = 1e9
MASK_ADD = 1e30
VMEM_LIMIT = 56 * 1024 * 1024

RET_LOG_G = [float(np.log(1.0 - 2.0 ** (-5.0 - h))) for h in range(H_D)]


def _cparams(sem):
    return pltpu.CompilerParams(dimension_semantics=sem, vmem_limit_bytes=VMEM_LIMIT)


def _silu(x):
    return x * (1.0 / (1.0 + jnp.exp(-x)))


def _dot(a, b):
    return jnp.dot(a, b, preferred_element_type=F32)


def _dot_nt(a, b):
    return lax.dot_general(a, b, (((1,), (1,)), ((), ())), preferred_element_type=F32)


def _rms(x):
    return x * lax.rsqrt(jnp.mean(x * x, axis=-1, keepdims=True) + EPS)


def _adaln_kernel(c_ref, we_ref, be_ref, wo_ref, bo_ref, me_ref, mo_ref):
    sc = _silu(c_ref[...]).astype(BF16)
    me_ref[...] = _dot(sc, we_ref[...].astype(BF16)) + be_ref[...]
    mo_ref[...] = _dot(sc, wo_ref[...].astype(BF16)) + bo_ref[...]


def _adaln(c_all, we, be, wo, bo):
    rows = c_all.shape[0]
    tn = 512
    n = 3 * D_MODEL
    wspec = pl.BlockSpec((D_MODEL, tn), lambda j: (0, j))
    bspec = pl.BlockSpec((1, tn), lambda j: (0, j))
    ospec = pl.BlockSpec((rows, tn), lambda j: (0, j))
    return pl.pallas_call(
        _adaln_kernel,
        grid=(n // tn,),
        in_specs=[pl.BlockSpec((rows, D_MODEL), lambda j: (0, 0)), wspec, bspec, wspec, bspec],
        out_specs=[ospec, ospec],
        out_shape=[jax.ShapeDtypeStruct((rows, n), F32)] * 2,
        compiler_params=_cparams(("arbitrary",)),
        name="adaln",
    )(c_all, we, be.reshape(1, n), wo, bo.reshape(1, n))


def _norm_mod(x, g, mod):
    shift = mod[:, 0:D_MODEL]
    scale = mod[:, D_MODEL:2 * D_MODEL]
    return _rms(x) * g * (1.0 + scale) + shift


def _shift_rows(u, k, prev_rows):
    row = lax.broadcasted_iota(jnp.int32, u.shape, 0)
    out = pltpu.roll(u, k, 0)
    for idx, pr in enumerate(prev_rows):
        out = jnp.where(row == idx, pr, out)
    return out


def _even_front_kernel(x_ref, mod_ref, g_ref, wbg, wcg, wxv, wga, wq, wk, wv, wgb, cw_ref, cb_ref,
                       ya_ref, q_ref, k_ref, v_ref, sgb_ref, kn_ref, vn_ref, cn_ref,
                       h_scr, carry_scr, *, tiles_per_batch, first_kept):
    i = pl.program_id(0)
    j = pl.program_id(1)
    il = i % tiles_per_batch

    @pl.when(j == 0)
    def _():
        h_scr[...] = _norm_mod(x_ref[...], g_ref[...], mod_ref[...]).astype(BF16)

    h = h_scr[...]
    u = _dot(h, wcg[...]) * _dot(h, wxv[...])
    tm = u.shape[0]
    @pl.when(i == 0)
    def _():
        carry_scr[j] = jnp.zeros((8, TN), F32)

    prev = jnp.where(il == 0, 0.0, carry_scr[j])
    p2, p1 = prev[6:7, :], prev[7:8, :]
    u1 = _shift_rows(u, 1, [p1])
    u2 = _shift_rows(u, 2, [p2, p1])
    cw = cw_ref[...]
    conv = cb_ref[...] + cw[0:1, :] * u2 + cw[1:2, :] * u1 + cw[2:3, :] * u
    tail = u[tm - 8:tm, :]
    carry_scr[j] = tail

    @pl.when(il == tiles_per_batch - 1)
    def _():
        cn_ref[...] = tail

    ya =_dot(h, wbg[...]) * conv * _silu(_dot(h, wga[...]))
    ya_ref[...] = ya.astype(BF16)

    def put_pairs(ref, val):
        val = val.astype(BF16)
        for e in range(TN // LANES):
            ref[e] = val[:, e * LANES:(e + 1) * LANES]

    put_pairs(q_ref, _dot(h, wq[...]) * (HD_B ** -0.5))
    k = _dot(h, wk[...])
    v = _dot(h, wv[...])

    @pl.when(il >= first_kept)
    def _():
        kn_ref[...] = k.T
        vn_ref[...] = v.T

    put_pairs(k_ref, k)
    put_pairs(v_ref, v)
    put_pairs(sgb_ref, _silu(_dot(h, wgb[...])))


def _even_front_prompt(x2d, mod, g, w_in, cw, cb, batch, seq, keep, tm):
    rows = batch * seq
    tpb = seq // tm
    nj = D_MODEL // TN
    off = (seq - keep) // tm
    ppc = TN // LANES

    def wspec(s):
        return pl.BlockSpec((D_MODEL, TN), lambda i, j, s=s: (0, s * nj + j))

    pair_spec = pl.BlockSpec((None, ppc, tm, LANES), lambda i, j: (i // tpb, j, i % tpb, 0))
    pair_shape = jax.ShapeDtypeStruct((batch, N_PAIR, seq, LANES), BF16)

    def keep_map(i, j):
        il = i % tpb
        kept = il >= off
        return (i // tpb, jnp.where(kept, j, 0), jnp.where(kept, il - off, 0))

    def tail_map(i, j):
        return (i // tpb, 0, jnp.where(i % tpb == tpb - 1, j, 0))

    keep_spec = pl.BlockSpec((None, TN, tm), keep_map)
    keep_shape = jax.ShapeDtypeStruct((batch, D_MODEL, keep), F32)
    return pl.pallas_call(
        functools.partial(_even_front_kernel, tiles_per_batch=tpb, first_kept=off),
        grid=(rows // tm, nj),
        in_specs=[
            pl.BlockSpec((tm, D_MODEL), lambda i, j: (i, 0)),
            pl.BlockSpec((None, 1, 3 * D_MODEL), lambda i, j: (i // tpb, 0, 0)),
            pl.BlockSpec((1, D_MODEL), lambda i, j: (0, 0)),
        ] + [wspec(s) for s in range(8)] + [
            pl.BlockSpec((3, TN), lambda i, j: (0, j)),
            pl.BlockSpec((1, TN), lambda i, j: (0, j)),
        ],
        out_specs=[
            pl.BlockSpec((tm, TN), lambda i, j: (i, j)),
            pair_spec, pair_spec, pair_spec, pair_spec,
            keep_spec, keep_spec,
            pl.BlockSpec((None, 8, TN), tail_map),
        ],
        out_shape=[
            jax.ShapeDtypeStruct((rows, D_MODEL), BF16),
            pair_shape, pair_shape, pair_shape, pair_shape,
            keep_shape, keep_shape,
            jax.ShapeDtypeStruct((batch, 8, D_MODEL), F32),
        ],
        scratch_shapes=[pltpu.VMEM((tm, D_MODEL), BF16), pltpu.VMEM((nj, 8, TN), F32)],
        compiler_params=_cparams(("arbitrary", "arbitrary")),
        name="even_front_prompt",
    )(x2d, mod.reshape(batch, 1, 3 * D_MODEL), g.reshape(1, D_MODEL), *([w_in] * 8), cw, cb.reshape(1, D_MODEL))


def _attn_prompt_kernel(slope_ref, dist_ref, q_ref, k_ref, v_ref, sgb_ref,
                        yb_ref, x32_scr, s4_scr, s16_scr, o_scr, lse_scr, bias_scr, *, seq):
    hp = pl.program_id(1)
    lane = lax.broadcasted_iota(jnp.int32, (1, LANES), 1)
    half = [lane < HD_B, lane >= HD_B]
    dist = dist_ref[...]
    for p, d in enumerate(DILATIONS):
        for e in range(2):
            bias_scr[2 * p + e] = dist * (slope_ref[2 * hp + e] * float(d))

    for a, ref in enumerate((q_ref, k_ref, v_ref)):
        x32_scr[...] = ref[...].astype(F32)
        for d, st in ((4, s4_scr), (16, s16_scr)):
            for r in range(d):
                st[a, r] = x32_scr[pl.ds(r, seq // d, stride=d), :].astype(BF16)

    def load(a, d, r, start, size):
        rows = pl.ds(pl.multiple_of(start, NK), size)
        if d == 1:
            return (q_ref, k_ref, v_ref)[a][rows, :]
        return (s4_scr if d == 4 else s16_scr)[a, r, rows, :]

    def tile(p, d, r, n, first):
        q = load(0, d, r, n * NK, NK)
        if first:
            kk = load(1, d, r, 0, NK)
            vv = load(2, d, r, 0, NK)
        else:
            kk = load(1, d, r, (n - 1) * NK, 2 * NK)
            vv = load(2, d, r, (n - 1) * NK, 2 * NK)
        zero = jnp.zeros_like(kk)
        o = None
        lse = None
        for e in range(2):
            s = _dot_nt(q, jnp.where(half[e], kk, zero))
            bias = bias_scr[2 * p + e]
            s = s - (bias[:, NK:] if first else bias)
            m = jnp.max(s, axis=-1, keepdims=True)
            pe = jnp.exp(s - m)
            l = jnp.sum(pe, axis=-1, keepdims=True)
            oe = _dot(pe.astype(BF16), jnp.where(half[e], vv, zero)) * (1.0 / l)
            le = m + jnp.log(l)
            o = oe if o is None else o + oe
            lse = le if lse is None else jnp.where(half[0], lse, le)
        rows = pl.ds(r + d * NK * n, NK, stride=d) if d > 1 else pl.ds(pl.multiple_of(n * NK, NK), NK)
        o_scr[p, rows, :] = o
        lse_scr[p, rows, :] = jnp.broadcast_to(lse, (NK, LANES))

    for p, d in enumerate(DILATIONS):
        nb = seq // (d * NK)

        def first_body(r, c, p=p, d=d):
            tile(p, d, r, 0, True)
            return c

        def rest_body(t, c, p=p, d=d, nb=nb):
            tile(p, d, t // (nb - 1), 1 + t % (nb - 1), False)
            return c

        lax.fori_loop(0, d, first_body, 0)
        lax.fori_loop(0, d * (nb - 1), rest_body, 0)

    rc = 512
    for c in range(seq // rc):
        rows = pl.ds(c * rc, rc)
        ls = [lse_scr[p, rows, :] for p in range(3)]
        m = jnp.maximum(jnp.maximum(ls[0], ls[1]), ls[2])
        ws = [jnp.exp(x - m) for x in ls]
        num = ws[0] * o_scr[0, rows, :] + ws[1] * o_scr[1, rows, :] + ws[2] * o_scr[2, rows, :]
        out = num * (1.0 / (ws[0] + ws[1] + ws[2]))
        yb_ref[rows, :] = (out * sgb_ref[rows, :].astype(F32)).astype(BF16)


def _band_distance():
    qi = np.arange(NK)[:, None]
    kj = np.arange(2 * NK)[None, :]
    dist = NK + qi - kj
    return jnp.asarray(np.where((dist >= 0) & (dist <= NK), dist, MASK_DIST), dtype=F32)


def _alibi_slopes():
    return jnp.asarray(2.0 ** (-8.0 * np.arange(1, H_B + 1) / H_B), dtype=F32)


def _attn_prompt(q, k, v, sgb, batch, seq):
    pair_spec = pl.BlockSpec((None, None, seq, LANES), lambda b, hp: (b, hp, 0, 0))
    return pl.pallas_call(
        functools.partial(_attn_prompt_kernel, seq=seq),
        grid=(batch, N_PAIR),
        in_specs=[pl.BlockSpec(memory_space=pltpu.SMEM),
                  pl.BlockSpec((NK, 2 * NK), lambda b, hp: (0, 0)),
                  pair_spec, pair_spec, pair_spec, pair_spec],
        out_specs=pl.BlockSpec((None, seq, LANES), lambda b, hp: (b, 0, hp)),
        out_shape=jax.ShapeDtypeStruct((batch, seq, D_MODEL), BF16),
        scratch_shapes=[pltpu.VMEM((seq, LANES), F32),
                        pltpu.VMEM((3, 4, seq // 4, LANES), BF16),
                        pltpu.VMEM((3, 16, seq // 16, LANES), BF16),
                        pltpu.VMEM((3, seq, LANES), F32), pltpu.VMEM((3, seq, LANES), F32),
                        pltpu.VMEM((6, NK, 2 * NK), F32)],
        compiler_params=_cparams(("arbitrary", "arbitrary")),
        name="attn_prompt",
    )(_alibi_slopes(), _band_distance(), q, k, v, sgb)


def _outproj_mid_kernel(ya_ref, yb_ref, x_ref, mod_ref, w1_ref, w2_ref, g_ref, mod2_ref, x1_ref, h_ref):
    gate = mod_ref[:, 2 * D_MODEL:3 * D_MODEL]
    x1 = x_ref[...] + gate * (_dot(ya_ref[...], w1_ref[...]) + _dot(yb_ref[...], w2_ref[...]))
    x1_ref[...] = x1
    h_ref[...] = _norm_mod(x1, g_ref[...], mod2_ref[...]).astype(BF16)


def _outproj_final_kernel(ya_ref, yb_ref, x_ref, mod_ref, w1_ref, w2_ref, g_ref, y_ref):
    gate = mod_ref[:, 2 * D_MODEL:3 * D_MODEL]
    x1 = x_ref[...] + gate * (_dot(ya_ref[...], w1_ref[...]) + _dot(yb_ref[...], w2_ref[...]))
    y_ref[...] = _rms(x1) * g_ref[...]


def _outproj(ya, yb, x2d, mod, w_out, g, mod2, tm, rows_per_mod):
    rows = x2d.shape[0]
    half = w_out.shape[0] // 2
    row_spec = pl.BlockSpec((tm, D_MODEL), lambda i: (i, 0))
    if rows_per_mod == 1:
        mod_spec = pl.BlockSpec((tm, 3 * D_MODEL), lambda i: (i, 0))
        mods = (mod, mod2)
    else:
        tpb = rows_per_mod // tm
        mod_spec = pl.BlockSpec((None, 1, 3 * D_MODEL), lambda i: (i // tpb, 0, 0))
        mods = tuple(None if m is None else m.reshape(-1, 1, 3 * D_MODEL) for m in (mod, mod2))
    w1_spec = pl.BlockSpec((half, D_MODEL), lambda i: (0, 0))
    w2_spec = pl.BlockSpec((half, D_MODEL), lambda i: (1, 0))
    g_spec = pl.BlockSpec((1, D_MODEL), lambda i: (0, 0))
    common = dict(grid=(rows // tm,), compiler_params=_cparams(("arbitrary",)))
    if mod2 is None:
        return pl.pallas_call(
            _outproj_final_kernel,
            in_specs=[row_spec, row_spec, row_spec, mod_spec, w1_spec, w2_spec, g_spec],
            out_specs=row_spec,
            out_shape=jax.ShapeDtypeStruct((rows, D_MODEL), F32),
            name="outproj_final", **common,
        )(ya, yb, x2d, mods[0], w_out, w_out, g.reshape(1, D_MODEL))
    return pl.pallas_call(
        _outproj_mid_kernel,
        in_specs=[row_spec, row_spec, row_spec, mod_spec, w1_spec, w2_spec, g_spec, mod_spec],
        out_specs=[row_spec, row_spec],
        out_shape=[jax.ShapeDtypeStruct((rows, D_MODEL), F32), jax.ShapeDtypeStruct((rows, D_MODEL), BF16)],
        name="outproj_mid", **common,
    )(ya, yb, x2d, mods[0], w_out, w_out, g.reshape(1, D_MODEL), mods[1])


def _pick_by_chunk(j, vals):
    out = vals[-1]
    for idx in range(len(vals) - 2, -1, -1):
        out = jnp.where(j == idx, vals[idx], out)
    return out


def _odd_front_kernel(h_ref, wu, wgc, wq, wk, wv, wgd, pw_ref, ps_ref,
                      yc_ref, q_ref, k_ref, v_ref, sgd_ref, pn_ref, carry_scr, *, tiles_per_batch):
    i = pl.program_id(0)
    j = pl.program_id(1)
    il = i % tiles_per_batch
    h = h_ref[...]
    u = _dot(h, wu[...])
    tm = u.shape[0]
    @pl.when(i == 0)
    def _():
        carry_scr[j] = jnp.zeros((POOL_PREV, TN), F32)

    prev = jnp.where(il == 0, 0.0, carry_scr[j])
    tail = u[tm - POOL_PREV:tm, :]
    carry_scr[j] = tail

    @pl.when(il == tiles_per_batch - 1)
    def _():
        pn_ref[...] = tail

    ext = jnp.concatenate([prev, u], axis=0)
    sums = []
    s = ext
    for k in (1, 2, 4, 8):
        s = s + pltpu.roll(s, k, 0)
        sums.append(s[POOL_PREV:, :])
    win = _pick_by_chunk(j, sums)
    width = _pick_by_chunk(j, [float(w) for w in POOL_SIZES])
    pos = (il * tm + lax.broadcasted_iota(jnp.int32, (tm, 1), 0)).astype(F32)
    cnt = jnp.minimum(width, pos + 1.0)
    pooled = win / cnt - u
    mixed = _dot(pooled.astype(BF16), pw_ref[...]) * ps_ref[...]
    yc_ref[...] = (mixed * _silu(_dot(h, wgc[...]))).astype(BF16)
    q_ref[...] = _dot(h, wq[...]).astype(BF16)
    k_ref[...] = (_dot(h, wk[...]) * (DK_D ** -0.5)).astype(BF16)
    v_ref[...] = _dot(h, wv[...]).astype(BF16)
    sgd_ref[...] = _silu(_dot(h, wgd[...])).astype(BF16)


def _odd_front_prompt(h2d, w_in, pw, ps, batch, seq, tm):
    rows = batch * seq
    tpb = seq // tm
    nj = D_MODEL // TN

    def wspec(s):
        return pl.BlockSpec((D_MODEL, TN), lambda i, j, s=s: (0, s * nj + j))

    head_spec = pl.BlockSpec((None, None, tm, TN), lambda i, j: (i // tpb, j, i % tpb, 0))
    head_shape = jax.ShapeDtypeStruct((batch, H_D, seq, DK_D), BF16)
    tok_spec = pl.BlockSpec((tm, TN), lambda i, j: (i, j))
    tok_shape = jax.ShapeDtypeStruct((rows, D_MODEL), BF16)
    return pl.pallas_call(
        functools.partial(_odd_front_kernel, tiles_per_batch=tpb),
        grid=(rows // tm, nj),
        in_specs=[pl.BlockSpec((tm, D_MODEL), lambda i, j: (i, 0))] + [wspec(s) for s in range(6)] + [
            pl.BlockSpec((None, TN, TN), lambda i, j: (j, 0, 0)),
            pl.BlockSpec((1, TN), lambda i, j: (0, j)),
        ],
        out_specs=[tok_spec, head_spec, head_spec, head_spec, tok_spec,
                   pl.BlockSpec((None, POOL_PREV, TN),
                                lambda i, j: (i // tpb, 0, jnp.where(i % tpb == tpb - 1, j, 0)))],
        out_shape=[tok_shape, head_shape, head_shape, head_shape, tok_shape,
                   jax.ShapeDtypeStruct((batch, POOL_PREV, D_MODEL), F32)],
        scratch_shapes=[pltpu.VMEM((nj, POOL_PREV, TN), F32)],
        compiler_params=_cparams(("arbitrary", "arbitrary")),
        name="odd_front_prompt",
    )(h2d, *([w_in] * 6), pw, ps.reshape(1, D_MODEL))


def _ret_prompt_kernel(gl_ref, q_ref, k_ref, v_ref, sgd_ref, dec_ref, cdec_ref, kdec_ref,
                       yd_ref, st_ref, state_scr, *, seq):
    h = pl.program_id(1)
    state_scr[...] = jnp.zeros_like(state_scr)
    dec = dec_ref[...]
    cdec = cdec_ref[...]
    kdec = kdec_ref[...]
    g_len = gl_ref[h]

    def chunk(c, carry):
        rows = pl.ds(pl.multiple_of(c * RET_CHUNK, RET_CHUNK), RET_CHUNK)
        q = q_ref[rows, :]
        k = k_ref[rows, :]
        v = v_ref[rows, :]
        state = state_scr[...]
        scores = _dot_nt(q, k) * dec
        o = _dot(scores.astype(BF16), v) + _dot(q, state.astype(BF16)) * cdec
        kd = (k.astype(F32) * kdec).T.astype(BF16)
        state_scr[...] = g_len * state + _dot(kd, v)
        yd_ref[rows, :] = (_rms(o) * sgd_ref[rows, :].astype(F32)).astype(BF16)
        return carry

    lax.fori_loop(0, seq // RET_CHUNK, chunk, 0)
    st_ref[...] = state_scr[...]


def _ret_consts():
    t = np.arange(RET_CHUNK, dtype=np.float64)
    diff = t[:, None] - t[None, :]
    lg = np.asarray(RET_LOG_G)[:, None, None]
    dec = np.where(diff >= 0, np.exp(np.maximum(diff, 0.0)[None] * lg), 0.0)
    cdec = np.broadcast_to(np.exp((t + 1.0)[None, :, None] * lg), (H_D, RET_CHUNK, DK_D))
    kdec = np.broadcast_to(np.exp((RET_CHUNK - 1.0 - t)[None, :, None] * lg), (H_D, RET_CHUNK, DK_D))
    g_len = np.exp(RET_CHUNK * np.asarray(RET_LOG_G))
    return (jnp.asarray(g_len, F32), jnp.asarray(dec, F32), jnp.asarray(cdec, F32), jnp.asarray(kdec, F32))


def _ret_prompt(q, k, v, sgd, batch, seq):
    g_len, dec, cdec, kdec = _ret_consts()
    head_spec = pl.BlockSpec((None, None, seq, DK_D), lambda b, h: (b, h, 0, 0))
    tok_spec = pl.BlockSpec((None, seq, DK_D), lambda b, h: (b, 0, h))
    return pl.pallas_call(
        functools.partial(_ret_prompt_kernel, seq=seq),
        grid=(batch, H_D),
        in_specs=[pl.BlockSpec(memory_space=pltpu.SMEM), head_spec, head_spec, head_spec, tok_spec,
                  pl.BlockSpec((None, RET_CHUNK, RET_CHUNK), lambda b, h: (h, 0, 0)),
                  pl.BlockSpec((None, RET_CHUNK, DK_D), lambda b, h: (h, 0, 0)),
                  pl.BlockSpec((None, RET_CHUNK, DK_D), lambda b, h: (h, 0, 0))],
        out_specs=[tok_spec, pl.BlockSpec((None, None, DK_D, DK_D), lambda b, h: (b, h, 0, 0))],
        out_shape=[jax.ShapeDtypeStruct((batch, seq, D_MODEL), BF16),
                   jax.ShapeDtypeStruct((batch, H_D, DK_D, DK_D), F32)],
        scratch_shapes=[pltpu.VMEM((DK_D, DK_D), F32)],
        compiler_params=_cparams(("arbitrary", "arbitrary")),
        name="ret_prompt",
    )(g_len, q, k, v, sgd.reshape(batch, seq, D_MODEL), dec, cdec, kdec)


def _even_front_sample_kernel(x_ref, mod_ref, g_ref, wbg, wcg, wxv, wga, wq, wk, wv, wgb, cw_ref, cb_ref,
                              prev_ref, ya_ref, q_ref, k_ref, v_ref, sgb_ref, cn_ref):
    h = _norm_mod(x_ref[...], g_ref[...], mod_ref[...]).astype(BF16)
    u = _dot(h, wcg[...]) * _dot(h, wxv[...])
    cw = cw_ref[...]
    conv = cb_ref[...] + cw[0:1, :] * prev_ref[0] + cw[1:2, :] * prev_ref[1] + cw[2:3, :] * u
    cn_ref[0] = prev_ref[1]
    cn_ref[1] = u
    ya_ref[...] = (_dot(h, wbg[...]) * conv * _silu(_dot(h, wga[...]))).astype(BF16)
    q_ref[...] = _dot(h, wq[...]) * (HD_B ** -0.5)
    k_ref[...] = _dot(h, wk[...])
    v_ref[...] = _dot(h, wv[...])
    sgb_ref[...] = _silu(_dot(h, wgb[...]))


def _even_front_sample(x2d, mod, g, w_in, cw, cb, prev_t):
    rows = x2d.shape[0]
    nj = D_MODEL // TN

    def wspec(s):
        return pl.BlockSpec((D_MODEL, TN), lambda j, s=s: (0, s * nj + j))

    full = pl.BlockSpec((rows, D_MODEL), lambda j: (0, 0))
    col = pl.BlockSpec((rows, TN), lambda j: (0, j))
    st = pl.BlockSpec((2, rows, TN), lambda j: (0, 0, j))
    colf = jax.ShapeDtypeStruct((rows, D_MODEL), F32)
    return pl.pallas_call(
        _even_front_sample_kernel,
        grid=(nj,),
        in_specs=[full, pl.BlockSpec((rows, 3 * D_MODEL), lambda j: (0, 0)),
                  pl.BlockSpec((1, D_MODEL), lambda j: (0, 0))] + [wspec(s) for s in range(8)] + [
            pl.BlockSpec((3, TN), lambda j: (0, j)), pl.BlockSpec((1, TN), lambda j: (0, j)), st],
        out_specs=[col, col, col, col, col, st],
        out_shape=[jax.ShapeDtypeStruct((rows, D_MODEL), BF16), colf, colf, colf, colf,
                   jax.ShapeDtypeStruct((2, rows, D_MODEL), F32)],
        compiler_params=_cparams(("arbitrary",)),
        name="even_front_sample",
    )(x2d, mod, g.reshape(1, D_MODEL), *([w_in] * 8), cw, cb.reshape(1, D_MODEL), prev_t)


def _attn_sample_kernel(slope_ref, pos_ref, q_ref, kn_ref, vn_ref, sgb_ref, kt_ref, vt_ref, yb_ref):
    gw = q_ref.shape[-1]
    row = lax.broadcasted_iota(jnp.int32, (8, gw), 0)
    col = lax.broadcasted_iota(jnp.int32, (8, gw), 1)
    own = (col >= row * HD_B) & (col < (row + 1) * HD_B)
    qm = jnp.where(own, q_ref[...], 0.0)
    s_self = jnp.sum(qm * kn_ref[...], axis=-1, keepdims=True)
    s_all = _dot(qm.astype(BF16), kt_ref[...].astype(BF16))
    s_all = s_all - slope_ref[:, 0:1] * pos_ref[0:1, :]
    v_self = vn_ref[...]
    probs, stats = [], []
    for p in range(len(DILATIONS)):
        s = s_all - pos_ref[p + 1:p + 2, :]
        m = jnp.maximum(jnp.max(s, axis=-1, keepdims=True), s_self)
        pe = jnp.exp(s - m)
        p_self = jnp.exp(s_self - m)
        probs.append(pe)
        stats.append((m, jnp.sum(pe, axis=-1, keepdims=True) + p_self, p_self))
    o_all = _dot_nt(jnp.concatenate(probs, axis=0).astype(BF16), vt_ref[...].astype(BF16))
    outs, lses = [], []
    for p, (m, l, p_self) in enumerate(stats):
        outs.append((o_all[8 * p:8 * p + 8, :] + p_self * v_self) * (1.0 / l))
        lses.append(m + jnp.log(l))
    m = jnp.maximum(jnp.maximum(lses[0], lses[1]), lses[2])
    ws = [jnp.exp(x - m) for x in lses]
    o = (ws[0] * outs[0] + ws[1] * outs[1] + ws[2] * outs[2]) * (1.0 / (ws[0] + ws[1] + ws[2]))
    o = jnp.sum(jnp.where(own, o, 0.0), axis=0, keepdims=True)
    yb_ref[...] = (o * sgb_ref[...]).astype(BF16)


def _attn_sample(q, kn, vn, sgb, cache_kt, cache_vt):
    rows, _, wb = cache_kt.shape
    gh = 4
    gw = gh * HD_B
    ng = H_B // gh
    back = wb - np.arange(wb, dtype=np.float64)
    pos = [back] + [np.where((back % d == 0) & (back <= NK * d), 0.0, MASK_ADD) for d in DILATIONS]
    slopes = (2.0 ** (-8.0 * np.arange(1, H_B + 1) / H_B)).reshape(ng, gh)
    slopes = np.concatenate([slopes, np.zeros((ng, 8 - gh))], axis=1)
    slope_arr = jnp.asarray(np.broadcast_to(slopes[:, :, None], (ng, 8, LANES)), F32)

    def r3(a):
        return a.reshape(rows, 1, D_MODEL)

    row_spec = pl.BlockSpec((None, 1, gw), lambda b, g: (b, 0, g))
    t_spec = pl.BlockSpec((None, gw, wb), lambda b, g: (b, g, 0))
    return pl.pallas_call(
        _attn_sample_kernel,
        grid=(rows, ng),
        in_specs=[pl.BlockSpec((None, 8, LANES), lambda b, g: (g, 0, 0)),
                  pl.BlockSpec((4, wb), lambda b, g: (0, 0)),
                  row_spec, row_spec, row_spec, row_spec, t_spec, t_spec],
        out_specs=row_spec,
        out_shape=jax.ShapeDtypeStruct((rows, 1, D_MODEL), BF16),
        compiler_params=_cparams(("arbitrary", "arbitrary")),
        name="attn_sample",
    )(slope_arr, jnp.asarray(np.stack(pos), F32), r3(q), r3(kn), r3(vn), r3(sgb), cache_kt, cache_vt
      ).reshape(rows, D_MODEL)


def _odd_front_sample_kernel(h_ref, wu, wgc, wq, wk, wv, wgd, pw_ref, ps_ref, prev_ref,
                             yc_ref, q_ref, k_ref, v_ref, sgd_ref, pn_ref):
    j = pl.program_id(0)
    h = h_ref[...]
    u = _dot(h, wu[...])
    n_prev = prev_ref.shape[0]
    sums = []
    s = jnp.zeros_like(u)
    back = 0
    for w in POOL_SIZES:
        while back < w - 1:
            s = s + prev_ref[n_prev - 1 - back]
            back += 1
        sums.append(s)
    win = _pick_by_chunk(j, sums) + u
    inv_w = _pick_by_chunk(j, [1.0 / w for w in POOL_SIZES])
    pooled = win * inv_w - u
    mixed = _dot(pooled.astype(BF16), pw_ref[...]) * ps_ref[...]
    yc_ref[...] = (mixed * _silu(_dot(h, wgc[...]))).astype(BF16)
    q_ref[...] = _dot(h, wq[...])
    k_ref[...] = _dot(h, wk[...]) * (DK_D ** -0.5)
    v_ref[...] = _dot(h, wv[...])
    sgd_ref[...] = _silu(_dot(h, wgd[...]))
    for t in range(n_prev - 1):
        pn_ref[t] = prev_ref[t + 1]
    pn_ref[n_prev - 1] = u


def _odd_front_sample(h2d, w_in, pw, ps, prev_t):
    rows = h2d.shape[0]
    n_prev = prev_t.shape[0]
    nj = D_MODEL // TN

    def wspec(s):
        return pl.BlockSpec((D_MODEL, TN), lambda j, s=s: (0, s * nj + j))

    col = pl.BlockSpec((rows, TN), lambda j: (0, j))
    st = pl.BlockSpec((n_prev, rows, TN), lambda j: (0, 0, j))
    colf = jax.ShapeDtypeStruct((rows, D_MODEL), F32)
    return pl.pallas_call(
        _odd_front_sample_kernel,
        grid=(nj,),
        in_specs=[pl.BlockSpec((rows, D_MODEL), lambda j: (0, 0))] + [wspec(s) for s in range(6)] + [
            pl.BlockSpec((None, TN, TN), lambda j: (j, 0, 0)), pl.BlockSpec((1, TN), lambda j: (0, j)), st],
        out_specs=[col, col, col, col, col, st],
        out_shape=[jax.ShapeDtypeStruct((rows, D_MODEL), BF16), colf, colf, colf, colf,
                   jax.ShapeDtypeStruct((n_prev, rows, D_MODEL), F32)],
        compiler_params=_cparams(("arbitrary",)),
        name="odd_front_sample",
    )(h2d, *([w_in] * 6), pw, ps.reshape(1, D_MODEL), prev_t)


def _ret_sample_kernel(q_ref, k_ref, v_ref, sgd_ref, st_ref, yd_ref, sn_ref):
    row = lax.broadcasted_iota(jnp.int32, (DK_D, DK_D), 0)
    col = lax.broadcasted_iota(jnp.int32, (DK_D, DK_D), 1)
    for h in range(H_D):
        g = float(np.exp(RET_LOG_G[h]))
        cols = slice(h * DK_D, (h + 1) * DK_D)
        q = q_ref[:, cols]
        k = k_ref[:, cols]
        v = v_ref[:, cols]
        state = st_ref[h]
        qk = jnp.sum(q * k, axis=-1, keepdims=True)
        q8 = jnp.broadcast_to(q, (8, DK_D)).astype(BF16)
        cross = _dot(q8, state.astype(BF16))[0:1, :]
        o = qk * v + g * cross
        k_diag = jnp.where(row == col, jnp.broadcast_to(k, (DK_D, DK_D)), 0.0).astype(BF16)
        v_rows = jnp.broadcast_to(v, (DK_D, DK_D)).astype(BF16)
        sn_ref[h] = g * state + _dot(k_diag, v_rows)
        yd_ref[:, cols] = (_rms(o) * sgd_ref[:, cols]).astype(BF16)


def _ret_sample(q, k, v, sgd, state):
    rows = q.shape[0]

    def r3(a):
        return a.reshape(rows, 1, D_MODEL)

    row_spec = pl.BlockSpec((None, 1, D_MODEL), lambda b: (b, 0, 0))
    st_spec = pl.BlockSpec((None, H_D, DK_D, DK_D), lambda b: (b, 0, 0, 0))
    yd, sn = pl.pallas_call(
        _ret_sample_kernel,
        grid=(rows,),
        in_specs=[row_spec, row_spec, row_spec, row_spec, st_spec],
        out_specs=[row_spec, st_spec],
        out_shape=[jax.ShapeDtypeStruct((rows, 1, D_MODEL), BF16),
                   jax.ShapeDtypeStruct((rows, H_D, DK_D, DK_D), F32)],
        compiler_params=_cparams(("arbitrary",)),
        name="ret_sample",
    )(r3(q), r3(k), r3(v), r3(sgd), state)
    return yd.reshape(rows, D_MODEL), sn


def kernel(x_prompt, x_sample, c_prompt, c_sample, state_conv, cache_win_k, cache_win_v, state_pool, state_ret,
           norm_e, ada_w_e, ada_b_e, w_in_e, conv_w, conv_b, w_out_e, norm_o, ada_w_o, ada_b_o, w_in_o,
           pool_w, pool_scale, w_out_o, norm_f):
    batch, seq, d = x_prompt.shape
    sb = x_sample.shape[0]
    assert d == D_MODEL and x_sample.shape[1] == 1
    assert norm_e.shape[0] == 1 and norm_o.shape[0] == 1
    wb = cache_win_k.shape[2]
    keep = min(wb, seq)
    assert wb == DILATIONS[-1] * NK and seq % (DILATIONS[-1] * NK) == 0
    tm = 512

    n_c = batch + sb
    pad = (-n_c) % 8
    c_all = jnp.concatenate([c_prompt, c_sample, jnp.zeros((pad, d), F32)], axis=0)
    mod_e, mod_o = _adaln(c_all, ada_w_e[0], ada_b_e[0], ada_w_o[0], ada_b_o[0])
    mod_e_p, mod_e_s = mod_e[:batch], mod_e[batch:n_c]
    mod_o_p, mod_o_s = mod_o[:batch], mod_o[batch:n_c]

    w_in_e16 = w_in_e[0].astype(BF16)
    w_out_e16 = w_out_e[0].astype(BF16)
    w_in_o16 = w_in_o[0].astype(BF16)
    w_out_o16 = w_out_o[0].astype(BF16)
    pool_w16 = pool_w[0].astype(BF16)

    xp2d = x_prompt.reshape(batch * seq, d)
    ya, q, k, v, sgb, k_new, v_new, conv_tail = _even_front_prompt(
        xp2d, mod_e_p, norm_e[0], w_in_e16, conv_w[0], conv_b[0], batch, seq, keep, tm)
    yb = _attn_prompt(q, k, v, sgb, batch, seq).reshape(batch * seq, d)
    x1, h1 = _outproj(ya, yb, xp2d, mod_e_p, w_out_e16, norm_o[0], mod_o_p, tm, seq)
    yc, rq, rk, rv, sgd, pool_tail = _odd_front_prompt(h1, w_in_o16, pool_w16, pool_scale[0], batch, seq, tm)
    yd, ret_p = _ret_prompt(rq, rk, rv, sgd, batch, seq)
    y_prompt = _outproj(yc, yd.reshape(batch * seq, d), x1, mod_o_p, w_out_o16, norm_f, None, tm, seq)

    xs2d = x_sample.reshape(sb, d)
    conv_prev_t = jnp.transpose(state_conv[0], (1, 0, 2))
    ya_s, q_s, k_s, v_s, sgb_s, conv_s_t = _even_front_sample(
        xs2d, mod_e_s, norm_e[0], w_in_e16, conv_w[0], conv_b[0], conv_prev_t)
    cache_kt = jnp.transpose(cache_win_k[0], (0, 2, 3, 1)).reshape(sb, d, wb)
    cache_vt = jnp.transpose(cache_win_v[0], (0, 2, 3, 1)).reshape(sb, d, wb)
    yb_s = _attn_sample(q_s, k_s, v_s, sgb_s, cache_kt, cache_vt)
    x1_s, h1_s = _outproj(ya_s, yb_s, xs2d, mod_e_s, w_out_e16, norm_o[0], mod_o_s, sb, 1)
    pool_prev_t = jnp.transpose(state_pool[0], (1, 0, 2))
    yc_s, rq_s, rk_s, rv_s, sgd_s, pool_s_t = _odd_front_sample(h1_s, w_in_o16, pool_w16, pool_scale[0], pool_prev_t)
    yd_s, ret_s = _ret_sample(rq_s, rk_s, rv_s, sgd_s, state_ret[0])
    y_sample = _outproj(yc_s, yd_s, x1_s, mod_o_s, w_out_o16, norm_f, None, sb, 1)

    return (
        y_prompt.reshape(batch, seq, d),
        y_sample.reshape(sb, 1, d),
        conv_tail[:, 6:8][None],
        jnp.transpose(conv_s_t, (1, 0, 2))[None],
        jnp.transpose(k_new.reshape(batch, H_B, HD_B, keep), (0, 3, 1, 2))[None],
        k_s.reshape(1, sb, 1, H_B, HD_B),
        jnp.transpose(v_new.reshape(batch, H_B, HD_B, keep), (0, 3, 1, 2))[None],
        v_s.reshape(1, sb, 1, H_B, HD_B),
        pool_tail[:, 1:][None],
        jnp.transpose(pool_s_t, (1, 0, 2))[None],
        ret_p[None],
        ret_s[None],
    )
```

```python
import functools

import numpy as np
import jax
import jax.numpy as jnp
from jax import lax
from jax.experimental import pallas as pl
from jax.experimental.pallas import tpu as pltpu

F32 = jnp.float32
BF16 = jnp.bfloat16

D_MODEL = 1024
EPS = 1e-6
H_B = 16
HD_B = 64
N_PAIR = H_B // 2
LANES = 128
NK = 128
DILATIONS = (1, 4, 16)
POOL_SIZES = (2, 4, 8, 16)
POOL_PREV = 16
H_D = 4
DK_D = 256
RET_CHUNK = 128
TN = 256
MASK_DIST = 1e9
LOG2E = float(np.log2(np.e))
MASK_ADD = 1e30
VMEM_LIMIT = 56 * 1024 * 1024

RET_LOG_G = [float(np.log(1.0 - 2.0 ** (-5.0 - h))) for h in range(H_D)]


def _cparams(sem):
    return pltpu.CompilerParams(dimension_semantics=sem, vmem_limit_bytes=VMEM_LIMIT)


def _silu(x):
    return x * (1.0 / (1.0 + jnp.exp(-x)))


def _dot(a, b):
    return jnp.dot(a, b, preferred_element_type=F32)


def _dot_nt(a, b):
    return lax.dot_general(a, b, (((1,), (1,)), ((), ())), preferred_element_type=F32)


def _rms(x):
    return x * lax.rsqrt(jnp.mean(x * x, axis=-1, keepdims=True) + EPS)


def _adaln_kernel(c_ref, we_ref, be_ref, wo_ref, bo_ref, me_ref, mo_ref):
    sc = _silu(c_ref[...]).astype(BF16)
    me_ref[...] = _dot(sc, we_ref[...].astype(BF16)) + be_ref[...]
    mo_ref[...] = _dot(sc, wo_ref[...].astype(BF16)) + bo_ref[...]


def _adaln(c_all, we, be, wo, bo):
    rows = c_all.shape[0]
    tn = 512
    n = 3 * D_MODEL
    wspec = pl.BlockSpec((D_MODEL, tn), lambda j: (0, j))
    bspec = pl.BlockSpec((1, tn), lambda j: (0, j))
    ospec = pl.BlockSpec((rows, tn), lambda j: (0, j))
    return pl.pallas_call(
        _adaln_kernel,
        grid=(n // tn,),
        in_specs=[pl.BlockSpec((rows, D_MODEL), lambda j: (0, 0)), wspec, bspec, wspec, bspec],
        out_specs=[ospec, ospec],
        out_shape=[jax.ShapeDtypeStruct((rows, n), F32)] * 2,
        compiler_params=_cparams(("arbitrary",)),
        name="adaln",
    )(c_all, we, be.reshape(1, n), wo, bo.reshape(1, n))


def _norm_mod(x, g, mod):
    shift = mod[:, 0:D_MODEL]
    scale = mod[:, D_MODEL:2 * D_MODEL]
    return _rms(x) * g * (1.0 + scale) + shift


def _shift_rows(u, k, prev_rows):
    row = lax.broadcasted_iota(jnp.int32, u.shape, 0)
    out = pltpu.roll(u, k, 0)
    for idx, pr in enumerate(prev_rows):
        out = jnp.where(row == idx, pr, out)
    return out


def _even_front_kernel(x_ref, mod_ref, g_ref, wbg, wcg, wxv, wga, wq, wk, wv, wgb, cw_ref, cb_ref,
                       ya_ref, q_ref, k_ref, v_ref, sgb_ref, kn_ref, vn_ref, cn_ref,
                       h_scr, carry_scr, *, tiles_per_batch, first_kept):
    i = pl.program_id(0)
    j = pl.program_id(1)
    il = i % tiles_per_batch

    @pl.when(j == 0)
    def _():
        h_scr[...] = _norm_mod(x_ref[...], g_ref[...], mod_ref[...]).astype(BF16)

    h = h_scr[...]
    u = _dot(h, wcg[...]) * _dot(h, wxv[...])
    tm = u.shape[0]
    @pl.when(i == 0)
    def _():
        carry_scr[j] = jnp.zeros((8, TN), F32)

    prev = jnp.where(il == 0, 0.0, carry_scr[j])
    p2, p1 = prev[6:7, :], prev[7:8, :]
    u1 = _shift_rows(u, 1, [p1])
    u2 = _shift_rows(u, 2, [p2, p1])
    cw = cw_ref[...]
    conv = cb_ref[...] + cw[0:1, :] * u2 + cw[1:2, :] * u1 + cw[2:3, :] * u
    tail = u[tm - 8:tm, :]
    carry_scr[j] = tail

    @pl.when(il == tiles_per_batch - 1)
    def _():
        cn_ref[...] = tail

    ya =_dot(h, wbg[...]) * conv * _silu(_dot(h, wga[...]))
    ya_ref[...] = ya.astype(BF16)

    def put_pairs(ref, val):
        val = val.astype(BF16)
        for e in range(TN // LANES):
            ref[e] = val[:, e * LANES:(e + 1) * LANES]

    put_pairs(q_ref, _dot(h, wq[...]) * (HD_B ** -0.5 * LOG2E))
    k = _dot(h, wk[...])
    v = _dot(h, wv[...])

    @pl.when(il >= first_kept)
    def _():
        kn_ref[...] = k.T
        vn_ref[...] = v.T

    put_pairs(k_ref, k)
    put_pairs(v_ref, v)
    put_pairs(sgb_ref, _silu(_dot(h, wgb[...])))


def _even_front_prompt(x2d, mod, g, w_in, cw, cb, batch, seq, keep, tm):
    rows = batch * seq
    tpb = seq // tm
    nj = D_MODEL // TN
    off = (seq - keep) // tm
    ppc = TN // LANES

    def wspec(s):
        return pl.BlockSpec((D_MODEL, TN), lambda i, j, s=s: (0, s * nj + j))

    pair_spec = pl.BlockSpec((None, ppc, tm, LANES), lambda i, j: (i // tpb, j, i % tpb, 0))
    pair_shape = jax.ShapeDtypeStruct((batch, N_PAIR, seq, LANES), BF16)

    def keep_map(i, j):
        il = i % tpb
        kept = il >= off
        return (i // tpb, jnp.where(kept, j, 0), jnp.where(kept, il - off, 0))

    def tail_map(i, j):
        return (i // tpb, 0, jnp.where(i % tpb == tpb - 1, j, 0))

    keep_spec = pl.BlockSpec((None, TN, tm), keep_map)
    keep_shape = jax.ShapeDtypeStruct((batch, D_MODEL, keep), F32)
    return pl.pallas_call(
        functools.partial(_even_front_kernel, tiles_per_batch=tpb, first_kept=off),
        grid=(rows // tm, nj),
        in_specs=[
            pl.BlockSpec((tm, D_MODEL), lambda i, j: (i, 0)),
            pl.BlockSpec((None, 1, 3 * D_MODEL), lambda i, j: (i // tpb, 0, 0)),
            pl.BlockSpec((1, D_MODEL), lambda i, j: (0, 0)),
        ] + [wspec(s) for s in range(8)] + [
            pl.BlockSpec((3, TN), lambda i, j: (0, j)),
            pl.BlockSpec((1, TN), lambda i, j: (0, j)),
        ],
        out_specs=[
            pl.BlockSpec((tm, TN), lambda i, j: (i, j)),
            pair_spec, pair_spec, pair_spec, pair_spec,
            keep_spec, keep_spec,
            pl.BlockSpec((None, 8, TN), tail_map),
        ],
        out_shape=[
            jax.ShapeDtypeStruct((rows, D_MODEL), BF16),
            pair_shape, pair_shape, pair_shape, pair_shape,
            keep_shape, keep_shape,
            jax.ShapeDtypeStruct((batch, 8, D_MODEL), F32),
        ],
        scratch_shapes=[pltpu.VMEM((tm, D_MODEL), BF16), pltpu.VMEM((nj, 8, TN), F32)],
        compiler_params=_cparams(("arbitrary", "arbitrary")),
        name="even_front_prompt",
    )(x2d, mod.reshape(batch, 1, 3 * D_MODEL), g.reshape(1, D_MODEL), *([w_in] * 8), cw, cb.reshape(1, D_MODEL))


ATTN_UNROLL = 4


def _attn_prompt_kernel(slope_ref, dist_ref, q_ref, k_ref, v_ref, sgb_ref, yb_ref,
                        x32_scr, st4_scr, q4_scr, q16_scr, kv1_scr, kv4_scr, kv16_scr,
                        acc_scr, m_scr, l_scr, bias_scr, *, seq):
    hp = pl.program_id(1)
    lane = lax.broadcasted_iota(jnp.int32, (1, LANES), 1)
    half0 = lane < HD_B
    dist = dist_ref[...]
    for p, d in enumerate(DILATIONS):
        for e in range(2):
            bias = dist * (slope_ref[2 * hp + e] * (float(d) * LOG2E))
            bias_scr[2 * p + e] = bias
            bias_scr[6 + 2 * p + e] = jnp.concatenate(
                [jnp.full((NK, NK), MASK_DIST, F32), bias[:, NK:]], axis=1)

    zpad = jnp.zeros((NK, LANES), BF16)
    for a, ref in enumerate((q_ref, k_ref, v_ref)):
        x = ref[...]
        if a > 0:
            kv1_scr[a - 1, 0, pl.ds(0, NK), :] = zpad
            kv1_scr[a - 1, 0, pl.ds(NK, seq), :] = x
        x32_scr[...] = x.astype(F32)
        for r in range(4):
            s = x32_scr[pl.ds(r, seq // 4, stride=4), :]
            st4_scr[r] = s
            if a == 0:
                q4_scr[r] = s.astype(BF16)
            else:
                kv4_scr[a - 1, r, pl.ds(0, NK), :] = zpad
                kv4_scr[a - 1, r, pl.ds(NK, seq // 4), :] = s.astype(BF16)
        for r in range(16):
            s = st4_scr[r % 4, pl.ds(r // 4, seq // 16, stride=4), :].astype(BF16)
            if a == 0:
                q16_scr[r] = s
            else:
                kv16_scr[a - 1, r, pl.ds(0, NK), :] = zpad
                kv16_scr[a - 1, r, pl.ds(NK, seq // 16), :] = s

    ones = jnp.ones((2 * NK, LANES), BF16)

    def tile(p, d, r, n):
        q0 = pl.ds(pl.multiple_of(n * NK, NK), NK)
        k0 = pl.ds(pl.multiple_of(n * NK, NK), 2 * NK)
        if d == 1:
            q, kv = q_ref[q0, :], kv1_scr
        else:
            q, kv = (q4_scr if d == 4 else q16_scr)[r, q0, :], (kv4_scr if d == 4 else kv16_scr)
        kk = kv[0, r, k0, :]
        vv1 = jnp.concatenate([kv[1, r, k0, :], ones], axis=1)
        first = (n == 0).astype(jnp.int32)
        res = []
        for e in range(2):
            qe = jnp.where(half0 if e == 0 else jnp.logical_not(half0), q, jnp.zeros_like(q))
            s = _dot_nt(qe, kk) - bias_scr[6 * first + 2 * p + e]
            m = jnp.max(s, axis=-1, keepdims=True)
            pe = jnp.exp2(s - m).astype(BF16)
            res.append((_dot(pe, vv1), m))
        (a0, m0), (a1, m1) = res
        rows = pl.ds(r + d * NK * n, NK, stride=d) if d > 1 else q0
        acc_scr[p, rows, :] = jnp.where(half0, a0[:, :LANES], a1[:, :LANES])
        l_scr[p, rows, :] = jnp.where(half0, a0[:, LANES:], a1[:, LANES:])
        m_scr[p, rows, :] = jnp.where(half0, m0, m1)

    for p, d in enumerate(DILATIONS):
        def body(it, c, p=p, d=d):
            for u in range(ATTN_UNROLL):
                t = it * ATTN_UNROLL + u
                if d == 1:
                    tile(p, d, 0, t)
                else:
                    tile(p, d, t % d, t // d)
            return c

        lax.fori_loop(0, seq // NK // ATTN_UNROLL, body, 0)

    rc = 512
    for c in range(seq // rc):
        rows = pl.ds(c * rc, rc)
        ms = [m_scr[p, rows, :] for p in range(3)]
        m = jnp.maximum(jnp.maximum(ms[0], ms[1]), ms[2])
        ws = [jnp.exp2(x - m) for x in ms]
        num = ws[0] * acc_scr[0, rows, :] + ws[1] * acc_scr[1, rows, :] + ws[2] * acc_scr[2, rows, :]
        den = ws[0] * l_scr[0, rows, :] + ws[1] * l_scr[1, rows, :] + ws[2] * l_scr[2, rows, :]
        yb_ref[rows, :] = (num * (1.0 / den) * sgb_ref[rows, :].astype(F32)).astype(BF16)


def _band_distance():
    qi = np.arange(NK)[:, None]
    kj = np.arange(2 * NK)[None, :]
    dist = NK + qi - kj
    return jnp.asarray(np.where((dist >= 0) & (dist <= NK), dist, MASK_DIST), dtype=F32)


def _alibi_slopes():
    return jnp.asarray(2.0 ** (-8.0 * np.arange(1, H_B + 1) / H_B), dtype=F32)


def _attn_prompt(q, k, v, sgb, batch, seq):
    pair_spec = pl.BlockSpec((None, None, seq, LANES), lambda b, hp: (b, hp, 0, 0))
    stat = pltpu.VMEM((len(DILATIONS), seq, LANES), F32)
    return pl.pallas_call(
        functools.partial(_attn_prompt_kernel, seq=seq),
        grid=(batch, N_PAIR),
        in_specs=[pl.BlockSpec(memory_space=pltpu.SMEM),
                  pl.BlockSpec((NK, 2 * NK), lambda b, hp: (0, 0)),
                  pair_spec, pair_spec, pair_spec, pair_spec],
        out_specs=pl.BlockSpec((None, seq, LANES), lambda b, hp: (b, 0, hp)),
        out_shape=jax.ShapeDtypeStruct((batch, seq, D_MODEL), BF16),
        scratch_shapes=[pltpu.VMEM((seq, LANES), F32),
                        pltpu.VMEM((4, seq // 4, LANES), F32),
                        pltpu.VMEM((4, seq // 4, LANES), BF16),
                        pltpu.VMEM((16, seq // 16, LANES), BF16),
                        pltpu.VMEM((2, 1, NK + seq, LANES), BF16),
                        pltpu.VMEM((2, 4, NK + seq // 4, LANES), BF16),
                        pltpu.VMEM((2, 16, NK + seq // 16, LANES), BF16),
                        stat, stat, stat,
                        pltpu.VMEM((12, NK, 2 * NK), F32)],
        compiler_params=_cparams(("arbitrary", "arbitrary")),
        name="attn_prompt",
    )(_alibi_slopes(), _band_distance(), q, k, v, sgb)


def _outproj_mid_kernel(ya_ref, yb_ref, x_ref, mod_ref, w1_ref, w2_ref, g_ref, mod2_ref, x1_ref, h_ref):
    gate = mod_ref[:, 2 * D_MODEL:3 * D_MODEL]
    x1 = x_ref[...] + gate * (_dot(ya_ref[...], w1_ref[...]) + _dot(yb_ref[...], w2_ref[...]))
    x1_ref[...] = x1
    h_ref[...] = _norm_mod(x1, g_ref[...], mod2_ref[...]).astype(BF16)


def _outproj_final_kernel(ya_ref, yb_ref, x_ref, mod_ref, w1_ref, w2_ref, g_ref, y_ref):
    gate = mod_ref[:, 2 * D_MODEL:3 * D_MODEL]
    x1 = x_ref[...] + gate * (_dot(ya_ref[...], w1_ref[...]) + _dot(yb_ref[...], w2_ref[...]))
    y_ref[...] = _rms(x1) * g_ref[...]


def _outproj(ya, yb, x2d, mod, w_out, g, mod2, tm, rows_per_mod):
    rows = x2d.shape[0]
    half = w_out.shape[0] // 2
    row_spec = pl.BlockSpec((tm, D_MODEL), lambda i: (i, 0))
    if rows_per_mod == 1:
        mod_spec = pl.BlockSpec((tm, 3 * D_MODEL), lambda i: (i, 0))
        mods = (mod, mod2)
    else:
        tpb = rows_per_mod // tm
        mod_spec = pl.BlockSpec((None, 1, 3 * D_MODEL), lambda i: (i // tpb, 0, 0))
        mods = tuple(None if m is None else m.reshape(-1, 1, 3 * D_MODEL) for m in (mod, mod2))
    w1_spec = pl.BlockSpec((half, D_MODEL), lambda i: (0, 0))
    w2_spec = pl.BlockSpec((half, D_MODEL), lambda i: (1, 0))
    g_spec = pl.BlockSpec((1, D_MODEL), lambda i: (0, 0))
    common = dict(grid=(rows // tm,), compiler_params=_cparams(("arbitrary",)))
    if mod2 is None:
        return pl.pallas_call(
            _outproj_final_kernel,
            in_specs=[row_spec, row_spec, row_spec, mod_spec, w1_spec, w2_spec, g_spec],
            out_specs=row_spec,
            out_shape=jax.ShapeDtypeStruct((rows, D_MODEL), F32),
            name="outproj_final", **common,
        )(ya, yb, x2d, mods[0], w_out, w_out, g.reshape(1, D_MODEL))
    return pl.pallas_call(
        _outproj_mid_kernel,
        in_specs=[row_spec, row_spec, row_spec, mod_spec, w1_spec, w2_spec, g_spec, mod_spec],
        out_specs=[row_spec, row_spec],
        out_shape=[jax.ShapeDtypeStruct((rows, D_MODEL), F32), jax.ShapeDtypeStruct((rows, D_MODEL), BF16)],
        name="outproj_mid", **common,
    )(ya, yb, x2d, mods[0], w_out, w_out, g.reshape(1, D_MODEL), mods[1])


def _pick_by_chunk(j, vals):
    out = vals[-1]
    for idx in range(len(vals) - 2, -1, -1):
        out = jnp.where(j == idx, vals[idx], out)
    return out


def _odd_front_kernel(h_ref, wu, wgc, wq, wk, wv, wgd, pw_ref, ps_ref,
                      yc_ref, q_ref, k_ref, v_ref, sgd_ref, pn_ref, carry_scr, *, tiles_per_batch):
    i = pl.program_id(0)
    j = pl.program_id(1)
    il = i % tiles_per_batch
    h = h_ref[...]
    u = _dot(h, wu[...])
    tm = u.shape[0]
    @pl.when(i == 0)
    def _():
        carry_scr[j] = jnp.zeros((POOL_PREV, TN), F32)

    prev = jnp.where(il == 0, 0.0, carry_scr[j])
    tail = u[tm - POOL_PREV:tm, :]
    carry_scr[j] = tail

    @pl.when(il == tiles_per_batch - 1)
    def _():
        pn_ref[...] = tail

    ext = jnp.concatenate([prev, u], axis=0)
    sums = []
    s = ext
    for k in (1, 2, 4, 8):
        s = s + pltpu.roll(s, k, 0)
        sums.append(s[POOL_PREV:, :])
    win = _pick_by_chunk(j, sums)
    width = _pick_by_chunk(j, [float(w) for w in POOL_SIZES])
    pos = (il * tm + lax.broadcasted_iota(jnp.int32, (tm, 1), 0)).astype(F32)
    cnt = jnp.minimum(width, pos + 1.0)
    pooled = win / cnt - u
    mixed = _dot(pooled.astype(BF16), pw_ref[...]) * ps_ref[...]
    yc_ref[...] = (mixed * _silu(_dot(h, wgc[...]))).astype(BF16)
    q_ref[...] = _dot(h, wq[...]).astype(BF16)
    k_ref[...] = (_dot(h, wk[...]) * (DK_D ** -0.5)).astype(BF16)
    v_ref[...] = _dot(h, wv[...]).astype(BF16)
    sgd_ref[...] = _silu(_dot(h, wgd[...])).astype(BF16)


def _odd_front_prompt(h2d, w_in, pw, ps, batch, seq, tm):
    rows = batch * seq
    tpb = seq // tm
    nj = D_MODEL // TN

    def wspec(s):
        return pl.BlockSpec((D_MODEL, TN), lambda i, j, s=s: (0, s * nj + j))

    head_spec = pl.BlockSpec((None, None, tm, TN), lambda i, j: (i // tpb, j, i % tpb, 0))
    head_shape = jax.ShapeDtypeStruct((batch, H_D, seq, DK_D), BF16)
    tok_spec = pl.BlockSpec((tm, TN), lambda i, j: (i, j))
    tok_shape = jax.ShapeDtypeStruct((rows, D_MODEL), BF16)
    return pl.pallas_call(
        functools.partial(_odd_front_kernel, tiles_per_batch=tpb),
        grid=(rows // tm, nj),
        in_specs=[pl.BlockSpec((tm, D_MODEL), lambda i, j: (i, 0))] + [wspec(s) for s in range(6)] + [
            pl.BlockSpec((None, TN, TN), lambda i, j: (j, 0, 0)),
            pl.BlockSpec((1, TN), lambda i, j: (0, j)),
        ],
        out_specs=[tok_spec, head_spec, head_spec, head_spec, tok_spec,
                   pl.BlockSpec((None, POOL_PREV, TN),
                                lambda i, j: (i // tpb, 0, jnp.where(i % tpb == tpb - 1, j, 0)))],
        out_shape=[tok_shape, head_shape, head_shape, head_shape, tok_shape,
                   jax.ShapeDtypeStruct((batch, POOL_PREV, D_MODEL), F32)],
        scratch_shapes=[pltpu.VMEM((nj, POOL_PREV, TN), F32)],
        compiler_params=_cparams(("arbitrary", "arbitrary")),
        name="odd_front_prompt",
    )(h2d, *([w_in] * 6), pw, ps.reshape(1, D_MODEL))


def _ret_prompt_kernel(gl_ref, q_ref, k_ref, v_ref, sgd_ref, dec_ref, cdec_ref, kdec_ref,
                       yd_ref, st_ref, state_scr, *, seq):
    h = pl.program_id(1)
    state_scr[...] = jnp.zeros_like(state_scr)
    dec = dec_ref[...]
    cdec = cdec_ref[...]
    kdec = kdec_ref[...]
    g_len = gl_ref[h]

    def chunk(c, carry):
        rows = pl.ds(pl.multiple_of(c * RET_CHUNK, RET_CHUNK), RET_CHUNK)
        q = q_ref[rows, :]
        k = k_ref[rows, :]
        v = v_ref[rows, :]
        state = state_scr[...]
        scores = _dot_nt(q, k) * dec
        o = _dot(scores.astype(BF16), v) + _dot(q, state.astype(BF16)) * cdec
        kd = (k.astype(F32) * kdec).T.astype(BF16)
        state_scr[...] = g_len * state + _dot(kd, v)
        yd_ref[rows, :] = (_rms(o) * sgd_ref[rows, :].astype(F32)).astype(BF16)
        return carry

    lax.fori_loop(0, seq // RET_CHUNK, chunk, 0)
    st_ref[...] = state_scr[...]


def _ret_consts():
    t = np.arange(RET_CHUNK, dtype=np.float64)
    diff = t[:, None] - t[None, :]
    lg = np.asarray(RET_LOG_G)[:, None, None]
    dec = np.where(diff >= 0, np.exp(np.maximum(diff, 0.0)[None] * lg), 0.0)
    cdec = np.broadcast_to(np.exp((t + 1.0)[None, :, None] * lg), (H_D, RET_CHUNK, DK_D))
    kdec = np.broadcast_to(np.exp((RET_CHUNK - 1.0 - t)[None, :, None] * lg), (H_D, RET_CHUNK, DK_D))
    g_len = np.exp(RET_CHUNK * np.asarray(RET_LOG_G))
    return (jnp.asarray(g_len, F32), jnp.asarray(dec, F32), jnp.asarray(cdec, F32), jnp.asarray(kdec, F32))


def _ret_prompt(q, k, v, sgd, batch, seq):
    g_len, dec, cdec, kdec = _ret_consts()
    head_spec = pl.BlockSpec((None, None, seq, DK_D), lambda b, h: (b, h, 0, 0))
    tok_spec = pl.BlockSpec((None, seq, DK_D), lambda b, h: (b, 0, h))
    return pl.pallas_call(
        functools.partial(_ret_prompt_kernel, seq=seq),
        grid=(batch, H_D),
        in_specs=[pl.BlockSpec(memory_space=pltpu.SMEM), head_spec, head_spec, head_spec, tok_spec,
                  pl.BlockSpec((None, RET_CHUNK, RET_CHUNK), lambda b, h: (h, 0, 0)),
                  pl.BlockSpec((None, RET_CHUNK, DK_D), lambda b, h: (h, 0, 0)),
                  pl.BlockSpec((None, RET_CHUNK, DK_D), lambda b, h: (h, 0, 0))],
        out_specs=[tok_spec, pl.BlockSpec((None, None, DK_D, DK_D), lambda b, h: (b, h, 0, 0))],
        out_shape=[jax.ShapeDtypeStruct((batch, seq, D_MODEL), BF16),
                   jax.ShapeDtypeStruct((batch, H_D, DK_D, DK_D), F32)],
        scratch_shapes=[pltpu.VMEM((DK_D, DK_D), F32)],
        compiler_params=_cparams(("arbitrary", "arbitrary")),
        name="ret_prompt",
    )(g_len, q, k, v, sgd.reshape(batch, seq, D_MODEL), dec, cdec, kdec)


def _even_front_sample_kernel(x_ref, mod_ref, g_ref, wbg, wcg, wxv, wga, wq, wk, wv, wgb, cw_ref, cb_ref,
                              prev_ref, ya_ref, q_ref, k_ref, v_ref, sgb_ref, cn_ref):
    h = _norm_mod(x_ref[...], g_ref[...], mod_ref[...]).astype(BF16)
    u = _dot(h, wcg[...]) * _dot(h, wxv[...])
    cw = cw_ref[...]
    conv = cb_ref[...] + cw[0:1, :] * prev_ref[0] + cw[1:2, :] * prev_ref[1] + cw[2:3, :] * u
    cn_ref[0] = prev_ref[1]
    cn_ref[1] = u
    ya_ref[...] = (_dot(h, wbg[...]) * conv * _silu(_dot(h, wga[...]))).astype(BF16)
    q_ref[...] = _dot(h, wq[...]) * (HD_B ** -0.5)
    k_ref[...] = _dot(h, wk[...])
    v_ref[...] = _dot(h, wv[...])
    sgb_ref[...] = _silu(_dot(h, wgb[...]))


def _even_front_sample(x2d, mod, g, w_in, cw, cb, prev_t):
    rows = x2d.shape[0]
    nj = D_MODEL // TN

    def wspec(s):
        return pl.BlockSpec((D_MODEL, TN), lambda j, s=s: (0, s * nj + j))

    full = pl.BlockSpec((rows, D_MODEL), lambda j: (0, 0))
    col = pl.BlockSpec((rows, TN), lambda j: (0, j))
    st = pl.BlockSpec((2, rows, TN), lambda j: (0, 0, j))
    colf = jax.ShapeDtypeStruct((rows, D_MODEL), F32)
    return pl.pallas_call(
        _even_front_sample_kernel,
        grid=(nj,),
        in_specs=[full, pl.BlockSpec((rows, 3 * D_MODEL), lambda j: (0, 0)),
                  pl.BlockSpec((1, D_MODEL), lambda j: (0, 0))] + [wspec(s) for s in range(8)] + [
            pl.BlockSpec((3, TN), lambda j: (0, j)), pl.BlockSpec((1, TN), lambda j: (0, j)), st],
        out_specs=[col, col, col, col, col, st],
        out_shape=[jax.ShapeDtypeStruct((rows, D_MODEL), BF16), colf, colf, colf, colf,
                   jax.ShapeDtypeStruct((2, rows, D_MODEL), F32)],
        compiler_params=_cparams(("arbitrary",)),
        name="even_front_sample",
    )(x2d, mod, g.reshape(1, D_MODEL), *([w_in] * 8), cw, cb.reshape(1, D_MODEL), prev_t)


def _attn_sample_kernel(slope_ref, pos_ref, q_ref, kn_ref, vn_ref, sgb_ref, kt_ref, vt_ref, yb_ref):
    gw = q_ref.shape[-1]
    row = lax.broadcasted_iota(jnp.int32, (8, gw), 0)
    col = lax.broadcasted_iota(jnp.int32, (8, gw), 1)
    own = (col >= row * HD_B) & (col < (row + 1) * HD_B)
    qm = jnp.where(own, q_ref[...], 0.0)
    s_self = jnp.sum(qm * kn_ref[...], axis=-1, keepdims=True)
    s_all = _dot(qm.astype(BF16), kt_ref[...].astype(BF16))
    s_all = s_all - slope_ref[:, 0:1] * pos_ref[0:1, :]
    v_self = vn_ref[...]
    probs, stats = [], []
    for p in range(len(DILATIONS)):
        s = s_all - pos_ref[p + 1:p + 2, :]
        m = jnp.maximum(jnp.max(s, axis=-1, keepdims=True), s_self)
        pe = jnp.exp(s - m)
        p_self = jnp.exp(s_self - m)
        probs.append(pe)
        stats.append((m, jnp.sum(pe, axis=-1, keepdims=True) + p_self, p_self))
    o_all = _dot_nt(jnp.concatenate(probs, axis=0).astype(BF16), vt_ref[...].astype(BF16))
    outs, lses = [], []
    for p, (m, l, p_self) in enumerate(stats):
        outs.append((o_all[8 * p:8 * p + 8, :] + p_self * v_self) * (1.0 / l))
        lses.append(m + jnp.log(l))
    m = jnp.maximum(jnp.maximum(lses[0], lses[1]), lses[2])
    ws = [jnp.exp(x - m) for x in lses]
    o = (ws[0] * outs[0] + ws[1] * outs[1] + ws[2] * outs[2]) * (1.0 / (ws[0] + ws[1] + ws[2]))
    o = jnp.sum(jnp.where(own, o, 0.0), axis=0, keepdims=True)
    yb_ref[...] = (o * sgb_ref[...]).astype(BF16)


def _attn_sample(q, kn, vn, sgb, cache_kt, cache_vt):
    rows, _, wb = cache_kt.shape
    gh = 4
    gw = gh * HD_B
    ng = H_B // gh
    back = wb - np.arange(wb, dtype=np.float64)
    pos = [back] + [np.where((back % d == 0) & (back <= NK * d), 0.0, MASK_ADD) for d in DILATIONS]
    slopes = (2.0 ** (-8.0 * np.arange(1, H_B + 1) / H_B)).reshape(ng, gh)
    slopes = np.concatenate([slopes, np.zeros((ng, 8 - gh))], axis=1)
    slope_arr = jnp.asarray(np.broadcast_to(slopes[:, :, None], (ng, 8, LANES)), F32)

    def r3(a):
        return a.reshape(rows, 1, D_MODEL)

    row_spec = pl.BlockSpec((None, 1, gw), lambda b, g: (b, 0, g))
    t_spec = pl.BlockSpec((None, gw, wb), lambda b, g: (b, g, 0))
    return pl.pallas_call(
        _attn_sample_kernel,
        grid=(rows, ng),
        in_specs=[pl.BlockSpec((None, 8, LANES), lambda b, g: (g, 0, 0)),
                  pl.BlockSpec((4, wb), lambda b, g: (0, 0)),
                  row_spec, row_spec, row_spec, row_spec, t_spec, t_spec],
        out_specs=row_spec,
        out_shape=jax.ShapeDtypeStruct((rows, 1, D_MODEL), BF16),
        compiler_params=_cparams(("arbitrary", "arbitrary")),
        name="attn_sample",
    )(slope_arr, jnp.asarray(np.stack(pos), F32), r3(q), r3(kn), r3(vn), r3(sgb), cache_kt, cache_vt
      ).reshape(rows, D_MODEL)


def _odd_front_sample_kernel(h_ref, wu, wgc, wq, wk, wv, wgd, pw_ref, ps_ref, prev_ref,
                             yc_ref, q_ref, k_ref, v_ref, sgd_ref, pn_ref):
    j = pl.program_id(0)
    h = h_ref[...]
    u = _dot(h, wu[...])
    n_prev = prev_ref.shape[0]
    sums = []
    s = jnp.zeros_like(u)
    back = 0
    for w in POOL_SIZES:
        while back < w - 1:
            s = s + prev_ref[n_prev - 1 - back]
            back += 1
        sums.append(s)
    win = _pick_by_chunk(j, sums) + u
    inv_w = _pick_by_chunk(j, [1.0 / w for w in POOL_SIZES])
    pooled = win * inv_w - u
    mixed = _dot(pooled.astype(BF16), pw_ref[...]) * ps_ref[...]
    yc_ref[...] = (mixed * _silu(_dot(h, wgc[...]))).astype(BF16)
    q_ref[...] = _dot(h, wq[...])
    k_ref[...] = _dot(h, wk[...]) * (DK_D ** -0.5)
    v_ref[...] = _dot(h, wv[...])
    sgd_ref[...] = _silu(_dot(h, wgd[...]))
    for t in range(n_prev - 1):
        pn_ref[t] = prev_ref[t + 1]
    pn_ref[n_prev - 1] = u


def _odd_front_sample(h2d, w_in, pw, ps, prev_t):
    rows = h2d.shape[0]
    n_prev = prev_t.shape[0]
    nj = D_MODEL // TN

    def wspec(s):
        return pl.BlockSpec((D_MODEL, TN), lambda j, s=s: (0, s * nj + j))

    col = pl.BlockSpec((rows, TN), lambda j: (0, j))
    st = pl.BlockSpec((n_prev, rows, TN), lambda j: (0, 0, j))
    colf = jax.ShapeDtypeStruct((rows, D_MODEL), F32)
    return pl.pallas_call(
        _odd_front_sample_kernel,
        grid=(nj,),
        in_specs=[pl.BlockSpec((rows, D_MODEL), lambda j: (0, 0))] + [wspec(s) for s in range(6)] + [
            pl.BlockSpec((None, TN, TN), lambda j: (j, 0, 0)), pl.BlockSpec((1, TN), lambda j: (0, j)), st],
        out_specs=[col, col, col, col, col, st],
        out_shape=[jax.ShapeDtypeStruct((rows, D_MODEL), BF16), colf, colf, colf, colf,
                   jax.ShapeDtypeStruct((n_prev, rows, D_MODEL), F32)],
        compiler_params=_cparams(("arbitrary",)),
        name="odd_front_sample",
    )(h2d, *([w_in] * 6), pw, ps.reshape(1, D_MODEL), prev_t)


def _ret_sample_kernel(q_ref, k_ref, v_ref, sgd_ref, st_ref, yd_ref, sn_ref):
    row = lax.broadcasted_iota(jnp.int32, (DK_D, DK_D), 0)
    col = lax.broadcasted_iota(jnp.int32, (DK_D, DK_D), 1)
    for h in range(H_D):
        g = float(np.exp(RET_LOG_G[h]))
        cols = slice(h * DK_D, (h + 1) * DK_D)
        q = q_ref[:, cols]
        k = k_ref[:, cols]
        v = v_ref[:, cols]
        state = st_ref[h]
        qk = jnp.sum(q * k, axis=-1, keepdims=True)
        q8 = jnp.broadcast_to(q, (8, DK_D)).astype(BF16)
        cross = _dot(q8, state.astype(BF16))[0:1, :]
        o = qk * v + g * cross
        k_diag = jnp.where(row == col, jnp.broadcast_to(k, (DK_D, DK_D)), 0.0).astype(BF16)
        v_rows = jnp.broadcast_to(v, (DK_D, DK_D)).astype(BF16)
        sn_ref[h] = g * state + _dot(k_diag, v_rows)
        yd_ref[:, cols] = (_rms(o) * sgd_ref[:, cols]).astype(BF16)


def _ret_sample(q, k, v, sgd, state):
    rows = q.shape[0]

    def r3(a):
        return a.reshape(rows, 1, D_MODEL)

    row_spec = pl.BlockSpec((None, 1, D_MODEL), lambda b: (b, 0, 0))
    st_spec = pl.BlockSpec((None, H_D, DK_D, DK_D), lambda b: (b, 0, 0, 0))
    yd, sn = pl.pallas_call(
        _ret_sample_kernel,
        grid=(rows,),
        in_specs=[row_spec, row_spec, row_spec, row_spec, st_spec],
        out_specs=[row_spec, st_spec],
        out_shape=[jax.ShapeDtypeStruct((rows, 1, D_MODEL), BF16),
                   jax.ShapeDtypeStruct((rows, H_D, DK_D, DK_D), F32)],
        compiler_params=_cparams(("arbitrary",)),
        name="ret_sample",
    )(r3(q), r3(k), r3(v), r3(sgd), state)
    return yd.reshape(rows, D_MODEL), sn


def kernel(x_prompt, x_sample, c_prompt, c_sample, state_conv, cache_win_k, cache_win_v, state_pool, state_ret,
           norm_e, ada_w_e, ada_b_e, w_in_e, conv_w, conv_b, w_out_e, norm_o, ada_w_o, ada_b_o, w_in_o,
           pool_w, pool_scale, w_out_o, norm_f):
    batch, seq, d = x_prompt.shape
    sb = x_sample.shape[0]
    assert d == D_MODEL and x_sample.shape[1] == 1
    assert norm_e.shape[0] == 1 and norm_o.shape[0] == 1
    wb = cache_win_k.shape[2]
    keep = min(wb, seq)
    assert wb == DILATIONS[-1] * NK and seq % (DILATIONS[-1] * NK) == 0
    tm = 512

    n_c = batch + sb
    pad = (-n_c) % 8
    c_all = jnp.concatenate([c_prompt, c_sample, jnp.zeros((pad, d), F32)], axis=0)
    mod_e, mod_o = _adaln(c_all, ada_w_e[0], ada_b_e[0], ada_w_o[0], ada_b_o[0])
    mod_e_p, mod_e_s = mod_e[:batch], mod_e[batch:n_c]
    mod_o_p, mod_o_s = mod_o[:batch], mod_o[batch:n_c]

    w_in_e16 = w_in_e[0].astype(BF16)
    w_out_e16 = w_out_e[0].astype(BF16)
    w_in_o16 = w_in_o[0].astype(BF16)
    w_out_o16 = w_out_o[0].astype(BF16)
    pool_w16 = pool_w[0].astype(BF16)

    xp2d = x_prompt.reshape(batch * seq, d)
    ya, q, k, v, sgb, k_new, v_new, conv_tail = _even_front_prompt(
        xp2d, mod_e_p, norm_e[0], w_in_e16, conv_w[0], conv_b[0], batch, seq, keep, tm)
    yb = _attn_prompt(q, k, v, sgb, batch, seq).reshape(batch * seq, d)
    x1, h1 = _outproj(ya, yb, xp2d, mod_e_p, w_out_e16, norm_o[0], mod_o_p, tm, seq)
    yc, rq, rk, rv, sgd, pool_tail = _odd_front_prompt(h1, w_in_o16, pool_w16, pool_scale[0], batch, seq, tm)
    yd, ret_p = _ret_prompt(rq, rk, rv, sgd, batch, seq)
    y_prompt = _outproj(yc, yd.reshape(batch * seq, d), x1, mod_o_p, w_out_o16, norm_f, None, tm, seq)

    xs2d = x_sample.reshape(sb, d)
    conv_prev_t = jnp.transpose(state_conv[0], (1, 0, 2))
    ya_s, q_s, k_s, v_s, sgb_s, conv_s_t = _even_front_sample(
        xs2d, mod_e_s, norm_e[0], w_in_e16, conv_w[0], conv_b[0], conv_prev_t)
    cache_kt = jnp.transpose(cache_win_k[0], (0, 2, 3, 1)).reshape(sb, d, wb)
    cache_vt = jnp.transpose(cache_win_v[0], (0, 2, 3, 1)).reshape(sb, d, wb)
    yb_s = _attn_sample(q_s, k_s, v_s, sgb_s, cache_kt, cache_vt)
    x1_s, h1_s = _outproj(ya_s, yb_s, xs2d, mod_e_s, w_out_e16, norm_o[0], mod_o_s, sb, 1)
    pool_prev_t = jnp.transpose(state_pool[0], (1, 0, 2))
    yc_s, rq_s, rk_s, rv_s, sgd_s, pool_s_t = _odd_front_sample(h1_s, w_in_o16, pool_w16, pool_scale[0], pool_prev_t)
    yd_s, ret_s = _ret_sample(rq_s, rk_s, rv_s, sgd_s, state_ret[0])
    y_sample = _outproj(yc_s, yd_s, x1_s, mod_o_s, w_out_o16, norm_f, None, sb, 1)

    return (
        y_prompt.reshape(batch, seq, d),
        y_sample.reshape(sb, 1, d),
        conv_tail[:, 6:8][None],
        jnp.transpose(conv_s_t, (1, 0, 2))[None],
        jnp.transpose(k_new.reshape(batch, H_B, HD_B, keep), (0, 3, 1, 2))[None],
        k_s.reshape(1, sb, 1, H_B, HD_B),
        jnp.transpose(v_new.reshape(batch, H_B, HD_B, keep), (0, 3, 1, 2))[None],
        v_s.reshape(1, sb, 1, H_B, HD_B),
        pool_tail[:, 1:][None],
        jnp.transpose(pool_s_t, (1, 0, 2))[None],
        ret_p[None],
        ret_s[None],
    )
```

```python
import functools

import numpy as np
import jax
import jax.numpy as jnp
from jax import lax
from jax.experimental import pallas as pl
from jax.experimental.pallas import tpu as pltpu

F32 = jnp.float32
BF16 = jnp.bfloat16

D_MODEL = 1024
EPS = 1e-6
H_B = 16
HD_B = 64
N_PAIR = H_B // 2
LANES = 128
NK = 128
DILATIONS = (1, 4, 16)
POOL_SIZES = (2, 4, 8, 16)
POOL_PREV = 16
H_D = 4
DK_D = 256
RET_CHUNK = 128
TN = 256
MASK_DIST = 1e9
LOG2E = float(np.log2(np.e))
MASK_ADD = 1e30
VMEM_LIMIT = 56 * 1024 * 1024

RET_LOG_G = [float(np.log(1.0 - 2.0 ** (-5.0 - h))) for h in range(H_D)]


def _cparams(sem):
    return pltpu.CompilerParams(dimension_semantics=sem, vmem_limit_bytes=VMEM_LIMIT)


def _silu(x):
    return x * (1.0 / (1.0 + jnp.exp(-x)))


def _dot(a, b):
    return jnp.dot(a, b, preferred_element_type=F32)


def _dot_nt(a, b):
    return lax.dot_general(a, b, (((1,), (1,)), ((), ())), preferred_element_type=F32)


def _rms(x):
    return x * lax.rsqrt(jnp.mean(x * x, axis=-1, keepdims=True) + EPS)


def _adaln_kernel(c_ref, we_ref, be_ref, wo_ref, bo_ref, me_ref, mo_ref):
    sc = _silu(c_ref[...]).astype(BF16)
    me_ref[...] = _dot(sc, we_ref[...].astype(BF16)) + be_ref[...]
    mo_ref[...] = _dot(sc, wo_ref[...].astype(BF16)) + bo_ref[...]


def _adaln(c_all, we, be, wo, bo):
    rows = c_all.shape[0]
    tn = 512
    n = 3 * D_MODEL
    wspec = pl.BlockSpec((D_MODEL, tn), lambda j: (0, j))
    bspec = pl.BlockSpec((1, tn), lambda j: (0, j))
    ospec = pl.BlockSpec((rows, tn), lambda j: (0, j))
    return pl.pallas_call(
        _adaln_kernel,
        grid=(n // tn,),
        in_specs=[pl.BlockSpec((rows, D_MODEL), lambda j: (0, 0)), wspec, bspec, wspec, bspec],
        out_specs=[ospec, ospec],
        out_shape=[jax.ShapeDtypeStruct((rows, n), F32)] * 2,
        compiler_params=_cparams(("arbitrary",)),
        name="adaln",
    )(c_all, we, be.reshape(1, n), wo, bo.reshape(1, n))


def _norm_mod(x, g, mod):
    shift = mod[:, 0:D_MODEL]
    scale = mod[:, D_MODEL:2 * D_MODEL]
    return _rms(x) * g * (1.0 + scale) + shift


def _shift_rows(u, k, prev_rows):
    row = lax.broadcasted_iota(jnp.int32, u.shape, 0)
    out = pltpu.roll(u, k, 0)
    for idx, pr in enumerate(prev_rows):
        out = jnp.where(row == idx, pr, out)
    return out


def _even_front_kernel(x_ref, mod_ref, g_ref, wbg, wcg, wxv, wga, wq, wk, wv, wgb, cw_ref, cb_ref,
                       ya_ref, q_ref, k_ref, v_ref, sgb_ref, kn_ref, vn_ref, cn_ref,
                       h_scr, carry_scr, *, tiles_per_batch, first_kept):
    i = pl.program_id(0)
    j = pl.program_id(1)
    il = i % tiles_per_batch

    @pl.when(j == 0)
    def _():
        h_scr[...] = _norm_mod(x_ref[...], g_ref[...], mod_ref[...]).astype(BF16)

    h = h_scr[...]
    u = _dot(h, wcg[...]) * _dot(h, wxv[...])
    tm = u.shape[0]
    @pl.when(i == 0)
    def _():
        carry_scr[j] = jnp.zeros((8, TN), F32)

    prev = jnp.where(il == 0, 0.0, carry_scr[j])
    p2, p1 = prev[6:7, :], prev[7:8, :]
    u1 = _shift_rows(u, 1, [p1])
    u2 = _shift_rows(u, 2, [p2, p1])
    cw = cw_ref[...]
    conv = cb_ref[...] + cw[0:1, :] * u2 + cw[1:2, :] * u1 + cw[2:3, :] * u
    tail = u[tm - 8:tm, :]
    carry_scr[j] = tail

    @pl.when(il == tiles_per_batch - 1)
    def _():
        cn_ref[...] = tail

    ya =_dot(h, wbg[...]) * conv * _silu(_dot(h, wga[...]))
    ya_ref[...] = ya.astype(BF16)

    def put_pairs(ref, val):
        val = val.astype(BF16)
        for e in range(TN // LANES):
            ref[e] = val[:, e * LANES:(e + 1) * LANES]

    put_pairs(q_ref, _dot(h, wq[...]) * (HD_B ** -0.5 * LOG2E))
    k = _dot(h, wk[...])
    v = _dot(h, wv[...])

    @pl.when(il >= first_kept)
    def _():
        kn_ref[...] = k.T
        vn_ref[...] = v.T

    put_pairs(k_ref, k)
    put_pairs(v_ref, v)
    put_pairs(sgb_ref, _silu(_dot(h, wgb[...])))


def _even_front_prompt(x2d, mod, g, w_in, cw, cb, batch, seq, keep, tm):
    rows = batch * seq
    tpb = seq // tm
    nj = D_MODEL // TN
    off = (seq - keep) // tm
    ppc = TN // LANES

    def wspec(s):
        return pl.BlockSpec((D_MODEL, TN), lambda i, j, s=s: (0, s * nj + j))

    pair_spec = pl.BlockSpec((None, ppc, tm, LANES), lambda i, j: (i // tpb, j, i % tpb, 0))
    pair_shape = jax.ShapeDtypeStruct((batch, N_PAIR, seq, LANES), BF16)

    def keep_map(i, j):
        il = i % tpb
        kept = il >= off
        return (i // tpb, jnp.where(kept, j, 0), jnp.where(kept, il - off, 0))

    def tail_map(i, j):
        return (i // tpb, 0, jnp.where(i % tpb == tpb - 1, j, 0))

    keep_spec = pl.BlockSpec((None, TN, tm), keep_map)
    keep_shape = jax.ShapeDtypeStruct((batch, D_MODEL, keep), F32)
    return pl.pallas_call(
        functools.partial(_even_front_kernel, tiles_per_batch=tpb, first_kept=off),
        grid=(rows // tm, nj),
        in_specs=[
            pl.BlockSpec((tm, D_MODEL), lambda i, j: (i, 0)),
            pl.BlockSpec((None, 1, 3 * D_MODEL), lambda i, j: (i // tpb, 0, 0)),
            pl.BlockSpec((1, D_MODEL), lambda i, j: (0, 0)),
        ] + [wspec(s) for s in range(8)] + [
            pl.BlockSpec((3, TN), lambda i, j: (0, j)),
            pl.BlockSpec((1, TN), lambda i, j: (0, j)),
        ],
        out_specs=[
            pl.BlockSpec((tm, TN), lambda i, j: (i, j)),
            pair_spec, pair_spec, pair_spec, pair_spec,
            keep_spec, keep_spec,
            pl.BlockSpec((None, 8, TN), tail_map),
        ],
        out_shape=[
            jax.ShapeDtypeStruct((rows, D_MODEL), BF16),
            pair_shape, pair_shape, pair_shape, pair_shape,
            keep_shape, keep_shape,
            jax.ShapeDtypeStruct((batch, 8, D_MODEL), F32),
        ],
        scratch_shapes=[pltpu.VMEM((tm, D_MODEL), BF16), pltpu.VMEM((nj, 8, TN), F32)],
        compiler_params=_cparams(("arbitrary", "arbitrary")),
        name="even_front_prompt",
    )(x2d, mod.reshape(batch, 1, 3 * D_MODEL), g.reshape(1, D_MODEL), *([w_in] * 8), cw, cb.reshape(1, D_MODEL))


ATTN_UNROLL = 4


def _attn_prompt_kernel(slope_ref, dist_ref, q_ref, k_ref, v_ref, sgb_ref, yb_ref,
                        x32_scr, st4_scr, q4_scr, q16_scr, kv1_scr, kv4_scr, kv16_scr,
                        acc_scr, m_scr, l_scr, bias_scr, *, seq):
    hp = pl.program_id(1)
    lane = lax.broadcasted_iota(jnp.int32, (1, LANES), 1)
    half0 = lane < HD_B
    dist = dist_ref[...]
    for p, d in enumerate(DILATIONS):
        for e in range(2):
            bias = dist * (slope_ref[2 * hp + e] * (float(d) * LOG2E))
            bias_scr[2 * p + e] = bias
            bias_scr[6 + 2 * p + e] = jnp.concatenate(
                [jnp.full((NK, NK), MASK_DIST, F32), bias[:, NK:]], axis=1)

    zpad = jnp.zeros((NK, LANES), BF16)
    for a, ref in enumerate((q_ref, k_ref, v_ref)):
        x = ref[...]
        if a > 0:
            kv1_scr[a - 1, 0, pl.ds(0, NK), :] = zpad
            kv1_scr[a - 1, 0, pl.ds(NK, seq), :] = x
        x32_scr[...] = x.astype(F32)
        for r in range(4):
            s = x32_scr[pl.ds(r, seq // 4, stride=4), :]
            st4_scr[r] = s
            if a == 0:
                q4_scr[r] = s.astype(BF16)
            else:
                kv4_scr[a - 1, r, pl.ds(0, NK), :] = zpad
                kv4_scr[a - 1, r, pl.ds(NK, seq // 4), :] = s.astype(BF16)
        for r in range(16):
            s = st4_scr[r % 4, pl.ds(r // 4, seq // 16, stride=4), :].astype(BF16)
            if a == 0:
                q16_scr[r] = s
            else:
                kv16_scr[a - 1, r, pl.ds(0, NK), :] = zpad
                kv16_scr[a - 1, r, pl.ds(NK, seq // 16), :] = s

    ones = jnp.ones((2 * NK, LANES), BF16)

    def tile(p, d, r, n):
        q0 = pl.ds(pl.multiple_of(n * NK, NK), NK)
        k0 = pl.ds(pl.multiple_of(n * NK, NK), 2 * NK)
        if d == 1:
            q, kv = q_ref[q0, :], kv1_scr
        else:
            q, kv = (q4_scr if d == 4 else q16_scr)[r, q0, :], (kv4_scr if d == 4 else kv16_scr)
        kk = kv[0, r, k0, :]
        vv1 = jnp.concatenate([kv[1, r, k0, :], ones], axis=1)
        first = jnp.where(n == 0, 1, 0)
        res = []
        for e in range(2):
            qe = jnp.where(half0 if e == 0 else jnp.logical_not(half0), q, jnp.zeros_like(q))
            s = _dot_nt(qe, kk) - bias_scr[6 * first + 2 * p + e]
            m = jnp.max(s, axis=-1, keepdims=True)
            pe = jnp.exp2(s - m).astype(BF16)
            res.append((_dot(pe, vv1), m))
        (a0, m0), (a1, m1) = res
        rows = pl.ds(r + d * NK * n, NK, stride=d) if d > 1 else q0
        acc_scr[p, rows, :] = jnp.where(half0, a0[:, :LANES], a1[:, :LANES])
        l_scr[p, rows, :] = jnp.where(half0, a0[:, LANES:], a1[:, LANES:])
        m_scr[p, rows, :] = jnp.where(half0, m0, m1)

    for p, d in enumerate(DILATIONS):
        def body(it, c, p=p, d=d):
            for u in range(ATTN_UNROLL):
                t = it * ATTN_UNROLL + u
                if d == 1:
                    tile(p, d, 0, t)
                else:
                    tile(p, d, t % d, t // d)
            return c

        lax.fori_loop(0, seq // NK // ATTN_UNROLL, body, 0)

    rc = 512
    for c in range(seq // rc):
        rows = pl.ds(c * rc, rc)
        ms = [m_scr[p, rows, :] for p in range(3)]
        m = jnp.maximum(jnp.maximum(ms[0], ms[1]), ms[2])
        ws = [jnp.exp2(x - m) for x in ms]
        num = ws[0] * acc_scr[0, rows, :] + ws[1] * acc_scr[1, rows, :] + ws[2] * acc_scr[2, rows, :]
        den = ws[0] * l_scr[0, rows, :] + ws[1] * l_scr[1, rows, :] + ws[2] * l_scr[2, rows, :]
        yb_ref[rows, :] = (num * (1.0 / den) * sgb_ref[rows, :].astype(F32)).astype(BF16)


def _band_distance():
    qi = np.arange(NK)[:, None]
    kj = np.arange(2 * NK)[None, :]
    dist = NK + qi - kj
    return jnp.asarray(np.where((dist >= 0) & (dist <= NK), dist, MASK_DIST), dtype=F32)


def _alibi_slopes():
    return jnp.asarray(2.0 ** (-8.0 * np.arange(1, H_B + 1) / H_B), dtype=F32)


def _attn_prompt(q, k, v, sgb, batch, seq):
    pair_spec = pl.BlockSpec((None, None, seq, LANES), lambda b, hp: (b, hp, 0, 0))
    stat = pltpu.VMEM((len(DILATIONS), seq, LANES), F32)
    return pl.pallas_call(
        functools.partial(_attn_prompt_kernel, seq=seq),
        grid=(batch, N_PAIR),
        in_specs=[pl.BlockSpec(memory_space=pltpu.SMEM),
                  pl.BlockSpec((NK, 2 * NK), lambda b, hp: (0, 0)),
                  pair_spec, pair_spec, pair_spec, pair_spec],
        out_specs=pl.BlockSpec((None, seq, LANES), lambda b, hp: (b, 0, hp)),
        out_shape=jax.ShapeDtypeStruct((batch, seq, D_MODEL), BF16),
        scratch_shapes=[pltpu.VMEM((seq, LANES), F32),
                        pltpu.VMEM((4, seq // 4, LANES), F32),
                        pltpu.VMEM((4, seq // 4, LANES), BF16),
                        pltpu.VMEM((16, seq // 16, LANES), BF16),
                        pltpu.VMEM((2, 1, NK + seq, LANES), BF16),
                        pltpu.VMEM((2, 4, NK + seq // 4, LANES), BF16),
                        pltpu.VMEM((2, 16, NK + seq // 16, LANES), BF16),
                        stat, stat, stat,
                        pltpu.VMEM((12, NK, 2 * NK), F32)],
        compiler_params=_cparams(("arbitrary", "arbitrary")),
        name="attn_prompt",
    )(_alibi_slopes(), _band_distance(), q, k, v, sgb)


def _outproj_mid_kernel(ya_ref, yb_ref, x_ref, mod_ref, w1_ref, w2_ref, g_ref, mod2_ref, x1_ref, h_ref):
    gate = mod_ref[:, 2 * D_MODEL:3 * D_MODEL]
    x1 = x_ref[...] + gate * (_dot(ya_ref[...], w1_ref[...]) + _dot(yb_ref[...], w2_ref[...]))
    x1_ref[...] = x1
    h_ref[...] = _norm_mod(x1, g_ref[...], mod2_ref[...]).astype(BF16)


def _outproj_final_kernel(ya_ref, yb_ref, x_ref, mod_ref, w1_ref, w2_ref, g_ref, y_ref):
    gate = mod_ref[:, 2 * D_MODEL:3 * D_MODEL]
    x1 = x_ref[...] + gate * (_dot(ya_ref[...], w1_ref[...]) + _dot(yb_ref[...], w2_ref[...]))
    y_ref[...] = _rms(x1) * g_ref[...]


def _outproj(ya, yb, x2d, mod, w_out, g, mod2, tm, rows_per_mod):
    rows = x2d.shape[0]
    half = w_out.shape[0] // 2
    row_spec = pl.BlockSpec((tm, D_MODEL), lambda i: (i, 0))
    if rows_per_mod == 1:
        mod_spec = pl.BlockSpec((tm, 3 * D_MODEL), lambda i: (i, 0))
        mods = (mod, mod2)
    else:
        tpb = rows_per_mod // tm
        mod_spec = pl.BlockSpec((None, 1, 3 * D_MODEL), lambda i: (i // tpb, 0, 0))
        mods = tuple(None if m is None else m.reshape(-1, 1, 3 * D_MODEL) for m in (mod, mod2))
    w1_spec = pl.BlockSpec((half, D_MODEL), lambda i: (0, 0))
    w2_spec = pl.BlockSpec((half, D_MODEL), lambda i: (1, 0))
    g_spec = pl.BlockSpec((1, D_MODEL), lambda i: (0, 0))
    common = dict(grid=(rows // tm,), compiler_params=_cparams(("arbitrary",)))
    if mod2 is None:
        return pl.pallas_call(
            _outproj_final_kernel,
            in_specs=[row_spec, row_spec, row_spec, mod_spec, w1_spec, w2_spec, g_spec],
            out_specs=row_spec,
            out_shape=jax.ShapeDtypeStruct((rows, D_MODEL), F32),
            name="outproj_final", **common,
        )(ya, yb, x2d, mods[0], w_out, w_out, g.reshape(1, D_MODEL))
    return pl.pallas_call(
        _outproj_mid_kernel,
        in_specs=[row_spec, row_spec, row_spec, mod_spec, w1_spec, w2_spec, g_spec, mod_spec],
        out_specs=[row_spec, row_spec],
        out_shape=[jax.ShapeDtypeStruct((rows, D_MODEL), F32), jax.ShapeDtypeStruct((rows, D_MODEL), BF16)],
        name="outproj_mid", **common,
    )(ya, yb, x2d, mods[0], w_out, w_out, g.reshape(1, D_MODEL), mods[1])


def _pick_by_chunk(j, vals):
    out = vals[-1]
    for idx in range(len(vals) - 2, -1, -1):
        out = jnp.where(j == idx, vals[idx], out)
    return out


def _odd_front_kernel(h_ref, wu, wgc, wq, wk, wv, wgd, pw_ref, ps_ref,
                      yc_ref, q_ref, kt_ref, v_ref, sgd_ref, pn_ref, carry_scr, *, tiles_per_batch):
    i = pl.program_id(0)
    j = pl.program_id(1)
    il = i % tiles_per_batch
    h = h_ref[...]
    u = _dot(h, wu[...])
    tm = u.shape[0]
    @pl.when(i == 0)
    def _():
        carry_scr[j] = jnp.zeros((POOL_PREV, TN), F32)

    prev = jnp.where(il == 0, 0.0, carry_scr[j])
    tail = u[tm - POOL_PREV:tm, :]
    carry_scr[j] = tail

    @pl.when(il == tiles_per_batch - 1)
    def _():
        pn_ref[...] = tail

    ext = jnp.concatenate([prev, u], axis=0)
    sums = []
    s = ext
    for k in (1, 2, 4, 8):
        s = s + pltpu.roll(s, k, 0)
        sums.append(s[POOL_PREV:, :])
    win = _pick_by_chunk(j, sums)
    width = _pick_by_chunk(j, [float(w) for w in POOL_SIZES])
    pos = (il * tm + lax.broadcasted_iota(jnp.int32, (tm, 1), 0)).astype(F32)
    cnt = jnp.minimum(width, pos + 1.0)
    pooled = win / cnt - u
    mixed = _dot(pooled.astype(BF16), pw_ref[...]) * ps_ref[...]
    yc_ref[...] = (mixed * _silu(_dot(h, wgc[...]))).astype(BF16)
    q_ref[...] = _dot(h, wq[...]).astype(BF16)
    kt_ref[...] = (_dot(h, wk[...]) * (DK_D ** -0.5)).T.astype(BF16)
    v_ref[...] = _dot(h, wv[...]).astype(BF16)
    sgd_ref[...] = _silu(_dot(h, wgd[...])).astype(BF16)


def _odd_front_prompt(h2d, w_in, pw, ps, batch, seq, tm):
    rows = batch * seq
    tpb = seq // tm
    nj = D_MODEL // TN

    def wspec(s):
        return pl.BlockSpec((D_MODEL, TN), lambda i, j, s=s: (0, s * nj + j))

    head_spec = pl.BlockSpec((None, None, tm, TN), lambda i, j: (i // tpb, j, i % tpb, 0))
    head_shape = jax.ShapeDtypeStruct((batch, H_D, seq, DK_D), BF16)
    tok_spec = pl.BlockSpec((tm, TN), lambda i, j: (i, j))
    tok_shape = jax.ShapeDtypeStruct((rows, D_MODEL), BF16)
    return pl.pallas_call(
        functools.partial(_odd_front_kernel, tiles_per_batch=tpb),
        grid=(rows // tm, nj),
        in_specs=[pl.BlockSpec((tm, D_MODEL), lambda i, j: (i, 0))] + [wspec(s) for s in range(6)] + [
            pl.BlockSpec((None, TN, TN), lambda i, j: (j, 0, 0)),
            pl.BlockSpec((1, TN), lambda i, j: (0, j)),
        ],
        out_specs=[tok_spec, head_spec,
                   pl.BlockSpec((None, None, TN, tm), lambda i, j: (i // tpb, j, 0, i % tpb)),
                   head_spec, tok_spec,
                   pl.BlockSpec((None, POOL_PREV, TN),
                                lambda i, j: (i // tpb, 0, jnp.where(i % tpb == tpb - 1, j, 0)))],
        out_shape=[tok_shape, head_shape, jax.ShapeDtypeStruct((batch, H_D, DK_D, seq), BF16),
                   head_shape, tok_shape,
                   jax.ShapeDtypeStruct((batch, POOL_PREV, D_MODEL), F32)],
        scratch_shapes=[pltpu.VMEM((nj, POOL_PREV, TN), F32)],
        compiler_params=_cparams(("arbitrary", "arbitrary")),
        name="odd_front_prompt",
    )(h2d, *([w_in] * 6), pw, ps.reshape(1, D_MODEL))


RET_SBLK = 1024


def _ret_prompt_kernel(q_ref, kt_ref, v_ref, sgd_ref, dec_ref, cdec_ref, kdec_ref,
                       yd_ref, st_ref, state_scr, *, sblk, n_sblk):
    sb = pl.program_id(1)

    @pl.when(sb == 0)
    def _():
        state_scr[...] = jnp.zeros_like(state_scr)

    def chunk(c, carry):
        rows = pl.ds(pl.multiple_of(c * RET_CHUNK, RET_CHUNK), RET_CHUNK)
        for h in range(H_D):
            cols = slice(h * DK_D, (h + 1) * DK_D)
            q = q_ref[h, rows, :]
            kt = kt_ref[h, :, rows]
            v = v_ref[h, rows, :]
            state = state_scr[h]
            scores = _dot(q, kt) * dec_ref[h]
            o = _dot(scores.astype(BF16), v) + _dot(q, state.astype(BF16)) * cdec_ref[h]
            kd = (kt.astype(F32) * kdec_ref[h]).astype(BF16)
            state_scr[h] = float(np.exp(RET_CHUNK * RET_LOG_G[h])) * state + _dot(kd, v)
            yd_ref[rows, cols] = (_rms(o) * sgd_ref[rows, cols].astype(F32)).astype(BF16)
        return carry

    lax.fori_loop(0, sblk // RET_CHUNK, chunk, 0, unroll=4)

    @pl.when(sb == n_sblk - 1)
    def _():
        st_ref[...] = state_scr[...]


def _ret_consts():
    t = np.arange(RET_CHUNK, dtype=np.float64)
    diff = t[:, None] - t[None, :]
    lg = np.asarray(RET_LOG_G)[:, None, None]
    dec = np.where(diff >= 0, np.exp(np.maximum(diff, 0.0)[None] * lg), 0.0)
    cdec = np.broadcast_to(np.exp((t + 1.0)[None, :, None] * lg), (H_D, RET_CHUNK, DK_D))
    kdec = np.broadcast_to(np.exp((RET_CHUNK - 1.0 - t)[None, None, :] * lg), (H_D, DK_D, RET_CHUNK))
    return jnp.asarray(dec, F32), jnp.asarray(cdec, F32), jnp.asarray(kdec, F32)


def _ret_prompt(q, kt, v, sgd, batch, seq):
    dec, cdec, kdec = _ret_consts()
    n_sblk = seq // RET_SBLK
    head_spec = pl.BlockSpec((None, H_D, RET_SBLK, DK_D), lambda b, s: (b, 0, s, 0))
    tok_spec = pl.BlockSpec((None, RET_SBLK, D_MODEL), lambda b, s: (b, s, 0))

    def const_spec(shape):
        return pl.BlockSpec(shape, lambda b, s: (0, 0, 0))

    return pl.pallas_call(
        functools.partial(_ret_prompt_kernel, sblk=RET_SBLK, n_sblk=n_sblk),
        grid=(batch, n_sblk),
        in_specs=[head_spec, pl.BlockSpec((None, H_D, DK_D, RET_SBLK), lambda b, s: (b, 0, 0, s)), head_spec,
                  tok_spec, const_spec(dec.shape), const_spec(cdec.shape), const_spec(kdec.shape)],
        out_specs=[tok_spec, pl.BlockSpec((None, H_D, DK_D, DK_D), lambda b, s: (b, 0, 0, 0))],
        out_shape=[jax.ShapeDtypeStruct((batch, seq, D_MODEL), BF16),
                   jax.ShapeDtypeStruct((batch, H_D, DK_D, DK_D), F32)],
        scratch_shapes=[pltpu.VMEM((H_D, DK_D, DK_D), F32)],
        compiler_params=_cparams(("arbitrary", "arbitrary")),
        name="ret_prompt",
    )(q, kt, v, sgd.reshape(batch, seq, D_MODEL), dec, cdec, kdec)


def _even_front_sample_kernel(x_ref, mod_ref, g_ref, wbg, wcg, wxv, wga, wq, wk, wv, wgb, cw_ref, cb_ref,
                              prev_ref, ya_ref, q_ref, k_ref, v_ref, sgb_ref, cn_ref):
    h = _norm_mod(x_ref[...], g_ref[...], mod_ref[...]).astype(BF16)
    u = _dot(h, wcg[...]) * _dot(h, wxv[...])
    cw = cw_ref[...]
    conv = cb_ref[...] + cw[0:1, :] * prev_ref[0] + cw[1:2, :] * prev_ref[1] + cw[2:3, :] * u
    cn_ref[0] = prev_ref[1]
    cn_ref[1] = u
    ya_ref[...] = (_dot(h, wbg[...]) * conv * _silu(_dot(h, wga[...]))).astype(BF16)
    q_ref[...] = _dot(h, wq[...]) * (HD_B ** -0.5)
    k_ref[...] = _dot(h, wk[...])
    v_ref[...] = _dot(h, wv[...])
    sgb_ref[...] = _silu(_dot(h, wgb[...]))


def _even_front_sample(x2d, mod, g, w_in, cw, cb, prev_t):
    rows = x2d.shape[0]
    nj = D_MODEL // TN

    def wspec(s):
        return pl.BlockSpec((D_MODEL, TN), lambda j, s=s: (0, s * nj + j))

    full = pl.BlockSpec((rows, D_MODEL), lambda j: (0, 0))
    col = pl.BlockSpec((rows, TN), lambda j: (0, j))
    st = pl.BlockSpec((2, rows, TN), lambda j: (0, 0, j))
    colf = jax.ShapeDtypeStruct((rows, D_MODEL), F32)
    return pl.pallas_call(
        _even_front_sample_kernel,
        grid=(nj,),
        in_specs=[full, pl.BlockSpec((rows, 3 * D_MODEL), lambda j: (0, 0)),
                  pl.BlockSpec((1, D_MODEL), lambda j: (0, 0))] + [wspec(s) for s in range(8)] + [
            pl.BlockSpec((3, TN), lambda j: (0, j)), pl.BlockSpec((1, TN), lambda j: (0, j)), st],
        out_specs=[col, col, col, col, col, st],
        out_shape=[jax.ShapeDtypeStruct((rows, D_MODEL), BF16), colf, colf, colf, colf,
                   jax.ShapeDtypeStruct((2, rows, D_MODEL), F32)],
        compiler_params=_cparams(("arbitrary",)),
        name="even_front_sample",
    )(x2d, mod, g.reshape(1, D_MODEL), *([w_in] * 8), cw, cb.reshape(1, D_MODEL), prev_t)


def _attn_sample_kernel(slope_ref, pos_ref, q_ref, kn_ref, vn_ref, sgb_ref, kt_ref, vt_ref, yb_ref):
    gw = q_ref.shape[-1]
    row = lax.broadcasted_iota(jnp.int32, (8, gw), 0)
    col = lax.broadcasted_iota(jnp.int32, (8, gw), 1)
    own = (col >= row * HD_B) & (col < (row + 1) * HD_B)
    qm = jnp.where(own, q_ref[...], 0.0)
    s_self = jnp.sum(qm * kn_ref[...], axis=-1, keepdims=True)
    s_all = _dot(qm.astype(BF16), kt_ref[...].astype(BF16))
    s_all = s_all - slope_ref[:, 0:1] * pos_ref[0:1, :]
    v_self = vn_ref[...]
    probs, stats = [], []
    for p in range(len(DILATIONS)):
        s = s_all - pos_ref[p + 1:p + 2, :]
        m = jnp.maximum(jnp.max(s, axis=-1, keepdims=True), s_self)
        pe = jnp.exp(s - m)
        p_self = jnp.exp(s_self - m)
        probs.append(pe)
        stats.append((m, jnp.sum(pe, axis=-1, keepdims=True) + p_self, p_self))
    o_all = _dot_nt(jnp.concatenate(probs, axis=0).astype(BF16), vt_ref[...].astype(BF16))
    outs, lses = [], []
    for p, (m, l, p_self) in enumerate(stats):
        outs.append((o_all[8 * p:8 * p + 8, :] + p_self * v_self) * (1.0 / l))
        lses.append(m + jnp.log(l))
    m = jnp.maximum(jnp.maximum(lses[0], lses[1]), lses[2])
    ws = [jnp.exp(x - m) for x in lses]
    o = (ws[0] * outs[0] + ws[1] * outs[1] + ws[2] * outs[2]) * (1.0 / (ws[0] + ws[1] + ws[2]))
    o = jnp.sum(jnp.where(own, o, 0.0), axis=0, keepdims=True)
    yb_ref[...] = (o * sgb_ref[...]).astype(BF16)


def _attn_sample(q, kn, vn, sgb, cache_kt, cache_vt):
    rows, _, wb = cache_kt.shape
    gh = 4
    gw = gh * HD_B
    ng = H_B // gh
    back = wb - np.arange(wb, dtype=np.float64)
    pos = [back] + [np.where((back % d == 0) & (back <= NK * d), 0.0, MASK_ADD) for d in DILATIONS]
    slopes = (2.0 ** (-8.0 * np.arange(1, H_B + 1) / H_B)).reshape(ng, gh)
    slopes = np.concatenate([slopes, np.zeros((ng, 8 - gh))], axis=1)
    slope_arr = jnp.asarray(np.broadcast_to(slopes[:, :, None], (ng, 8, LANES)), F32)

    def r3(a):
        return a.reshape(rows, 1, D_MODEL)

    row_spec = pl.BlockSpec((None, 1, gw), lambda b, g: (b, 0, g))
    t_spec = pl.BlockSpec((None, gw, wb), lambda b, g: (b, g, 0))
    return pl.pallas_call(
        _attn_sample_kernel,
        grid=(rows, ng),
        in_specs=[pl.BlockSpec((None, 8, LANES), lambda b, g: (g, 0, 0)),
                  pl.BlockSpec((4, wb), lambda b, g: (0, 0)),
                  row_spec, row_spec, row_spec, row_spec, t_spec, t_spec],
        out_specs=row_spec,
        out_shape=jax.ShapeDtypeStruct((rows, 1, D_MODEL), BF16),
        compiler_params=_cparams(("arbitrary", "arbitrary")),
        name="attn_sample",
    )(slope_arr, jnp.asarray(np.stack(pos), F32), r3(q), r3(kn), r3(vn), r3(sgb), cache_kt, cache_vt
      ).reshape(rows, D_MODEL)


def _odd_front_sample_kernel(h_ref, wu, wgc, wq, wk, wv, wgd, pw_ref, ps_ref, prev_ref,
                             yc_ref, q_ref, k_ref, v_ref, sgd_ref, pn_ref):
    j = pl.program_id(0)
    h = h_ref[...]
    u = _dot(h, wu[...])
    n_prev = prev_ref.shape[0]
    sums = []
    s = jnp.zeros_like(u)
    back = 0
    for w in POOL_SIZES:
        while back < w - 1:
            s = s + prev_ref[n_prev - 1 - back]
            back += 1
        sums.append(s)
    win = _pick_by_chunk(j, sums) + u
    inv_w = _pick_by_chunk(j, [1.0 / w for w in POOL_SIZES])
    pooled = win * inv_w - u
    mixed = _dot(pooled.astype(BF16), pw_ref[...]) * ps_ref[...]
    yc_ref[...] = (mixed * _silu(_dot(h, wgc[...]))).astype(BF16)
    q_ref[...] = _dot(h, wq[...])
    k_ref[...] = _dot(h, wk[...]) * (DK_D ** -0.5)
    v_ref[...] = _dot(h, wv[...])
    sgd_ref[...] = _silu(_dot(h, wgd[...]))
    for t in range(n_prev - 1):
        pn_ref[t] = prev_ref[t + 1]
    pn_ref[n_prev - 1] = u


def _odd_front_sample(h2d, w_in, pw, ps, prev_t):
    rows = h2d.shape[0]
    n_prev = prev_t.shape[0]
    nj = D_MODEL // TN

    def wspec(s):
        return pl.BlockSpec((D_MODEL, TN), lambda j, s=s: (0, s * nj + j))

    col = pl.BlockSpec((rows, TN), lambda j: (0, j))
    st = pl.BlockSpec((n_prev, rows, TN), lambda j: (0, 0, j))
    colf = jax.ShapeDtypeStruct((rows, D_MODEL), F32)
    return pl.pallas_call(
        _odd_front_sample_kernel,
        grid=(nj,),
        in_specs=[pl.BlockSpec((rows, D_MODEL), lambda j: (0, 0))] + [wspec(s) for s in range(6)] + [
            pl.BlockSpec((None, TN, TN), lambda j: (j, 0, 0)), pl.BlockSpec((1, TN), lambda j: (0, j)), st],
        out_specs=[col, col, col, col, col, st],
        out_shape=[jax.ShapeDtypeStruct((rows, D_MODEL), BF16), colf, colf, colf, colf,
                   jax.ShapeDtypeStruct((n_prev, rows, D_MODEL), F32)],
        compiler_params=_cparams(("arbitrary",)),
        name="odd_front_sample",
    )(h2d, *([w_in] * 6), pw, ps.reshape(1, D_MODEL), prev_t)


def _ret_sample_kernel(q_ref, k_ref, v_ref, sgd_ref, st_ref, yd_ref, sn_ref):
    row = lax.broadcasted_iota(jnp.int32, (DK_D, DK_D), 0)
    col = lax.broadcasted_iota(jnp.int32, (DK_D, DK_D), 1)
    for h in range(H_D):
        g = float(np.exp(RET_LOG_G[h]))
        cols = slice(h * DK_D, (h + 1) * DK_D)
        q = q_ref[:, cols]
        k = k_ref[:, cols]
        v = v_ref[:, cols]
        state = st_ref[h]
        qk = jnp.sum(q * k, axis=-1, keepdims=True)
        q8 = jnp.broadcast_to(q, (8, DK_D)).astype(BF16)
        cross = _dot(q8, state.astype(BF16))[0:1, :]
        o = qk * v + g * cross
        k_diag = jnp.where(row == col, jnp.broadcast_to(k, (DK_D, DK_D)), 0.0).astype(BF16)
        v_rows = jnp.broadcast_to(v, (DK_D, DK_D)).astype(BF16)
        sn_ref[h] = g * state + _dot(k_diag, v_rows)
        yd_ref[:, cols] = (_rms(o) * sgd_ref[:, cols]).astype(BF16)


def _ret_sample(q, k, v, sgd, state):
    rows = q.shape[0]

    def r3(a):
        return a.reshape(rows, 1, D_MODEL)

    row_spec = pl.BlockSpec((None, 1, D_MODEL), lambda b: (b, 0, 0))
    st_spec = pl.BlockSpec((None, H_D, DK_D, DK_D), lambda b: (b, 0, 0, 0))
    yd, sn = pl.pallas_call(
        _ret_sample_kernel,
        grid=(rows,),
        in_specs=[row_spec, row_spec, row_spec, row_spec, st_spec],
        out_specs=[row_spec, st_spec],
        out_shape=[jax.ShapeDtypeStruct((rows, 1, D_MODEL), BF16),
                   jax.ShapeDtypeStruct((rows, H_D, DK_D, DK_D), F32)],
        compiler_params=_cparams(("arbitrary",)),
        name="ret_sample",
    )(r3(q), r3(k), r3(v), r3(sgd), state)
    return yd.reshape(rows, D_MODEL), sn


def kernel(x_prompt, x_sample, c_prompt, c_sample, state_conv, cache_win_k, cache_win_v, state_pool, state_ret,
           norm_e, ada_w_e, ada_b_e, w_in_e, conv_w, conv_b, w_out_e, norm_o, ada_w_o, ada_b_o, w_in_o,
           pool_w, pool_scale, w_out_o, norm_f):
    batch, seq, d = x_prompt.shape
    sb = x_sample.shape[0]
    assert d == D_MODEL and x_sample.shape[1] == 1
    assert norm_e.shape[0] == 1 and norm_o.shape[0] == 1
    wb = cache_win_k.shape[2]
    keep = min(wb, seq)
    assert wb == DILATIONS[-1] * NK and seq % (DILATIONS[-1] * NK) == 0
    tm = 512

    n_c = batch + sb
    pad = (-n_c) % 8
    c_all = jnp.concatenate([c_prompt, c_sample, jnp.zeros((pad, d), F32)], axis=0)
    mod_e, mod_o = _adaln(c_all, ada_w_e[0], ada_b_e[0], ada_w_o[0], ada_b_o[0])
    mod_e_p, mod_e_s = mod_e[:batch], mod_e[batch:n_c]
    mod_o_p, mod_o_s = mod_o[:batch], mod_o[batch:n_c]

    w_in_e16 = w_in_e[0].astype(BF16)
    w_out_e16 = w_out_e[0].astype(BF16)
    w_in_o16 = w_in_o[0].astype(BF16)
    w_out_o16 = w_out_o[0].astype(BF16)
    pool_w16 = pool_w[0].astype(BF16)

    xp2d = x_prompt.reshape(batch * seq, d)
    ya, q, k, v, sgb, k_new, v_new, conv_tail = _even_front_prompt(
        xp2d, mod_e_p, norm_e[0], w_in_e16, conv_w[0], conv_b[0], batch, seq, keep, tm)
    yb = _attn_prompt(q, k, v, sgb, batch, seq).reshape(batch * seq, d)
    x1, h1 = _outproj(ya, yb, xp2d, mod_e_p, w_out_e16, norm_o[0], mod_o_p, tm, seq)
    yc, rq, rk, rv, sgd, pool_tail = _odd_front_prompt(h1, w_in_o16, pool_w16, pool_scale[0], batch, seq, tm)
    yd, ret_p = _ret_prompt(rq, rk, rv, sgd, batch, seq)
    y_prompt = _outproj(yc, yd.reshape(batch * seq, d), x1, mod_o_p, w_out_o16, norm_f, None, tm, seq)

    xs2d = x_sample.reshape(sb, d)
    conv_prev_t = jnp.transpose(state_conv[0], (1, 0, 2))
    ya_s, q_s, k_s, v_s, sgb_s, conv_s_t = _even_front_sample(
        xs2d, mod_e_s, norm_e[0], w_in_e16, conv_w[0], conv_b[0], conv_prev_t)
    cache_kt = jnp.transpose(cache_win_k[0], (0, 2, 3, 1)).reshape(sb, d, wb)
    cache_vt = jnp.transpose(cache_win_v[0], (0, 2, 3, 1)).reshape(sb, d, wb)
    yb_s = _attn_sample(q_s, k_s, v_s, sgb_s, cache_kt, cache_vt)
    x1_s, h1_s = _outproj(ya_s, yb_s, xs2d, mod_e_s, w_out_e16, norm_o[0], mod_o_s, sb, 1)
    pool_prev_t = jnp.transpose(state_pool[0], (1, 0, 2))
    yc_s, rq_s, rk_s, rv_s, sgd_s, pool_s_t = _odd_front_sample(h1_s, w_in_o16, pool_w16, pool_scale[0], pool_prev_t)
    yd_s, ret_s = _ret_sample(rq_s, rk_s, rv_s, sgd_s, state_ret[0])
    y_sample = _outproj(yc_s, yd_s, x1_s, mod_o_s, w_out_o16, norm_f, None, sb, 1)

    return (
        y_prompt.reshape(batch, seq, d),
        y_sample.reshape(sb, 1, d),
        conv_tail[:, 6:8][None],
        jnp.transpose(conv_s_t, (1, 0, 2))[None],
        jnp.transpose(k_new.reshape(batch, H_B, HD_B, keep), (0, 3, 1, 2))[None],
        k_s.reshape(1, sb, 1, H_B, HD_B),
        jnp.transpose(v_new.reshape(batch, H_B, HD_B, keep), (0, 3, 1, 2))[None],
        v_s.reshape(1, sb, 1, H_B, HD_B),
        pool_tail[:, 1:][None],
        jnp.transpose(pool_s_t, (1, 0, 2))[None],
        ret_p[None],
        ret_s[None],
    )
```

```python
import functools

import numpy as np
import jax
import jax.numpy as jnp
from jax import lax
from jax.experimental import pallas as pl
from jax.experimental.pallas import tpu as pltpu

F32 = jnp.float32
BF16 = jnp.bfloat16

D_MODEL = 1024
EPS = 1e-6
H_B = 16
HD_B = 64
N_PAIR = H_B // 2
LANES = 128
NK = 128
DILATIONS = (1, 4, 16)
POOL_SIZES = (2, 4, 8, 16)
POOL_PREV = 16
H_D = 4
DK_D = 256
RET_CHUNK = 128
TN = 256
MASK_DIST = 1e9
LOG2E = float(np.log2(np.e))
MASK_ADD = 1e30
VMEM_LIMIT = 56 * 1024 * 1024

RET_LOG_G = [float(np.log(1.0 - 2.0 ** (-5.0 - h))) for h in range(H_D)]


def _cparams(sem):
    return pltpu.CompilerParams(dimension_semantics=sem, vmem_limit_bytes=VMEM_LIMIT)


def _silu(x):
    return x * (1.0 / (1.0 + jnp.exp(-x)))


def _dot(a, b):
    return jnp.dot(a, b, preferred_element_type=F32)


def _dot_nt(a, b):
    return lax.dot_general(a, b, (((1,), (1,)), ((), ())), preferred_element_type=F32)


def _rms(x):
    return x * lax.rsqrt(jnp.mean(x * x, axis=-1, keepdims=True) + EPS)


def _adaln_kernel(c_ref, we_ref, be_ref, wo_ref, bo_ref, me_ref, mo_ref):
    sc = _silu(c_ref[...]).astype(BF16)
    me_ref[...] = _dot(sc, we_ref[...].astype(BF16)) + be_ref[...]
    mo_ref[...] = _dot(sc, wo_ref[...].astype(BF16)) + bo_ref[...]


def _adaln(c_all, we, be, wo, bo):
    rows = c_all.shape[0]
    tn = 512
    n = 3 * D_MODEL
    wspec = pl.BlockSpec((D_MODEL, tn), lambda j: (0, j))
    bspec = pl.BlockSpec((1, tn), lambda j: (0, j))
    ospec = pl.BlockSpec((rows, tn), lambda j: (0, j))
    return pl.pallas_call(
        _adaln_kernel,
        grid=(n // tn,),
        in_specs=[pl.BlockSpec((rows, D_MODEL), lambda j: (0, 0)), wspec, bspec, wspec, bspec],
        out_specs=[ospec, ospec],
        out_shape=[jax.ShapeDtypeStruct((rows, n), F32)] * 2,
        compiler_params=_cparams(("arbitrary",)),
        name="adaln",
    )(c_all, we, be.reshape(1, n), wo, bo.reshape(1, n))


def _norm_mod(x, g, mod):
    shift = mod[:, 0:D_MODEL]
    scale = mod[:, D_MODEL:2 * D_MODEL]
    return _rms(x) * g * (1.0 + scale) + shift


def _shift_rows(u, k, prev_rows):
    row = lax.broadcasted_iota(jnp.int32, u.shape, 0)
    out = pltpu.roll(u, k, 0)
    for idx, pr in enumerate(prev_rows):
        out = jnp.where(row == idx, pr, out)
    return out


N_SEG_E = 8
FRONT_PHASES = 2


def _chunk_major(w, n_seg):
    k = w.shape[0]
    return w.reshape(k, n_seg, -1, TN).transpose(0, 2, 1, 3).reshape(k, -1)


def _even_front_kernel(x_ref, mod_ref, g_ref, w_ref, cw_ref, cb_ref,
                       ya_ref, q_ref, k_ref, v_ref, sgb_ref, kn_ref, vn_ref, cn_ref,
                       h_scr, carry_scr, z_a, z_b, *, tiles_per_batch, n_chunks):
    nj = D_MODEL // TN
    s = pl.program_id(0)
    c_out = jnp.maximum(s - 1, 0)
    j_out = c_out % nj
    il_out = (c_out // nj) % tiles_per_batch

    @pl.when(s == 0)
    def _():
        z_b[...] = jnp.zeros_like(z_b)
        carry_scr[...] = jnp.zeros_like(carry_scr)

    @pl.when((s < n_chunks) & (s % nj == 0))
    def _():
        h_scr[...] = _norm_mod(x_ref[...], g_ref[...], mod_ref[...]).astype(BF16)

    tm = h_scr.shape[0]
    rb = tm // FRONT_PHASES
    seg_per_phase = N_SEG_E // FRONT_PHASES

    def step(z_w, z_r):
        cw = cw_ref[...]
        cb = cb_ref[...]

        def phase(t, prev):
            for e in range(seg_per_phase):
                seg = t * seg_per_phase + e
                cols = pl.ds(pl.multiple_of(seg * TN, TN), TN)
                z_w[seg] = _dot(h_scr[...], w_ref[:, cols])
            rows = pl.ds(pl.multiple_of(t * rb, rb), rb)
            bg, cg, xv, ga, q, k, v, gb = (z_r[seg, rows, :] for seg in range(N_SEG_E))
            u = cg * xv
            p2, p1 = prev[6:7, :], prev[7:8, :]
            u1 = _shift_rows(u, 1, [p1])
            u2 = _shift_rows(u, 2, [p2, p1])
            conv = cb + cw[0:1, :] * u2 + cw[1:2, :] * u1 + cw[2:3, :] * u
            ya_ref[rows, :] = (bg * conv * _silu(ga)).astype(BF16)
            kn_ref[:, rows] = k.T
            vn_ref[:, rows] = v.T
            for ref, val in ((q_ref, q * (HD_B ** -0.5 * LOG2E)),
                             (k_ref, k), (v_ref, v), (sgb_ref, _silu(gb))):
                val = val.astype(BF16)
                for e in range(TN // LANES):
                    ref[e, rows, :] = val[:, e * LANES:(e + 1) * LANES]
            return u[rb - 8:rb, :]

        prev0 = jnp.where(il_out == 0, 0.0, carry_scr[j_out])
        tail = lax.fori_loop(0, FRONT_PHASES, phase, prev0)
        carry_scr[j_out] = tail
        cn_ref[...] = tail

    @pl.when(s % 2 == 0)
    def _():
        step(z_a, z_b)

    @pl.when(s % 2 == 1)
    def _():
        step(z_b, z_a)


def _even_front_prompt(x2d, mod, g, w_in_cm, cw, cb, batch, seq, keep, tm):
    rows = batch * seq
    tpb = seq // tm
    nj = D_MODEL // TN
    n_chunks = (rows // tm) * nj
    off = (seq - keep) // tm
    ppc = TN // LANES

    def c_in(s):
        return jnp.minimum(s, n_chunks - 1)

    def c_out(s):
        return jnp.maximum(s - 1, 0)

    def out_ij(s):
        c = c_out(s)
        return c // nj, c % nj

    def pair_map(s):
        i, j = out_ij(s)
        return (i // tpb, j, i % tpb, 0)

    def keep_map(s):
        i, j = out_ij(s)
        il = i % tpb
        kept = il >= off
        return (i // tpb, jnp.where(kept, j, 0), jnp.where(kept, il - off, 0))

    def tail_map(s):
        i, j = out_ij(s)
        return (i // tpb, 0, jnp.where(i % tpb == tpb - 1, j, 0))

    pair_spec = pl.BlockSpec((None, ppc, tm, LANES), pair_map)
    pair_shape = jax.ShapeDtypeStruct((batch, N_PAIR, seq, LANES), BF16)
    keep_spec = pl.BlockSpec((None, TN, tm), keep_map)
    keep_shape = jax.ShapeDtypeStruct((batch, D_MODEL, keep), F32)
    return pl.pallas_call(
        functools.partial(_even_front_kernel, tiles_per_batch=tpb, n_chunks=n_chunks),
        grid=(n_chunks + 1,),
        in_specs=[
            pl.BlockSpec((tm, D_MODEL), lambda s: (c_in(s) // nj, 0)),
            pl.BlockSpec((None, 1, 3 * D_MODEL), lambda s: (c_in(s) // nj // tpb, 0, 0)),
            pl.BlockSpec((1, D_MODEL), lambda s: (0, 0)),
            pl.BlockSpec((D_MODEL, N_SEG_E * TN), lambda s: (0, c_in(s) % nj)),
            pl.BlockSpec((3, TN), lambda s: (0, c_out(s) % nj)),
            pl.BlockSpec((1, TN), lambda s: (0, c_out(s) % nj)),
        ],
        out_specs=[
            pl.BlockSpec((tm, TN), lambda s: out_ij(s)),
            pair_spec, pair_spec, pair_spec, pair_spec,
            keep_spec, keep_spec,
            pl.BlockSpec((None, 8, TN), tail_map),
        ],
        out_shape=[
            jax.ShapeDtypeStruct((rows, D_MODEL), BF16),
            pair_shape, pair_shape, pair_shape, pair_shape,
            keep_shape, keep_shape,
            jax.ShapeDtypeStruct((batch, 8, D_MODEL), F32),
        ],
        scratch_shapes=[pltpu.VMEM((tm, D_MODEL), BF16), pltpu.VMEM((nj, 8, TN), F32),
                        pltpu.VMEM((N_SEG_E, tm, TN), F32), pltpu.VMEM((N_SEG_E, tm, TN), F32)],
        compiler_params=_cparams(("arbitrary",)),
        name="even_front_prompt",
    )(x2d, mod.reshape(batch, 1, 3 * D_MODEL), g.reshape(1, D_MODEL), w_in_cm, cw, cb.reshape(1, D_MODEL))


ATTN_UNROLL = 4


def _attn_prompt_kernel(slope_ref, dist_ref, q_ref, k_ref, v_ref, sgb_ref, yb_ref,
                        x32_scr, st4_scr, q4_scr, q16_scr, kv1_scr, kv4_scr, kv16_scr,
                        acc_scr, m_scr, l_scr, bias_scr, *, seq):
    hp = pl.program_id(1)
    lane = lax.broadcasted_iota(jnp.int32, (1, LANES), 1)
    half0 = lane < HD_B
    dist = dist_ref[...]
    for p, d in enumerate(DILATIONS):
        for e in range(2):
            bias = dist * (slope_ref[2 * hp + e] * (float(d) * LOG2E))
            bias_scr[2 * p + e] = bias
            bias_scr[6 + 2 * p + e] = jnp.concatenate(
                [jnp.full((NK, NK), MASK_DIST, F32), bias[:, NK:]], axis=1)

    zpad = jnp.zeros((NK, LANES), BF16)
    for a, ref in enumerate((q_ref, k_ref, v_ref)):
        x = ref[...]
        if a > 0:
            kv1_scr[a - 1, 0, pl.ds(0, NK), :] = zpad
            kv1_scr[a - 1, 0, pl.ds(NK, seq), :] = x
        x32_scr[...] = x.astype(F32)
        for r in range(4):
            s = x32_scr[pl.ds(r, seq // 4, stride=4), :]
            st4_scr[r] = s
            if a == 0:
                q4_scr[r] = s.astype(BF16)
            else:
                kv4_scr[a - 1, r, pl.ds(0, NK), :] = zpad
                kv4_scr[a - 1, r, pl.ds(NK, seq // 4), :] = s.astype(BF16)
        for r in range(16):
            s = st4_scr[r % 4, pl.ds(r // 4, seq // 16, stride=4), :].astype(BF16)
            if a == 0:
                q16_scr[r] = s
            else:
                kv16_scr[a - 1, r, pl.ds(0, NK), :] = zpad
                kv16_scr[a - 1, r, pl.ds(NK, seq // 16), :] = s

    ones = jnp.ones((2 * NK, LANES), BF16)

    def tile(p, d, r, n):
        q0 = pl.ds(pl.multiple_of(n * NK, NK), NK)
        k0 = pl.ds(pl.multiple_of(n * NK, NK), 2 * NK)
        if d == 1:
            q, kv = q_ref[q0, :], kv1_scr
        else:
            q, kv = (q4_scr if d == 4 else q16_scr)[r, q0, :], (kv4_scr if d == 4 else kv16_scr)
        kk = kv[0, r, k0, :]
        vv1 = jnp.concatenate([kv[1, r, k0, :], ones], axis=1)
        first = jnp.where(n == 0, 1, 0)
        res = []
        for e in range(2):
            qe = jnp.where(half0 if e == 0 else jnp.logical_not(half0), q, jnp.zeros_like(q))
            s = _dot_nt(qe, kk) - bias_scr[6 * first + 2 * p + e]
            m = jnp.max(s, axis=-1, keepdims=True)
            pe = jnp.exp2(s - m).astype(BF16)
            res.append((_dot(pe, vv1), m))
        (a0, m0), (a1, m1) = res
        rows = pl.ds(r + d * NK * n, NK, stride=d) if d > 1 else q0
        acc_scr[p, rows, :] = jnp.where(half0, a0[:, :LANES], a1[:, :LANES])
        l_scr[p, rows, :] = jnp.where(half0, a0[:, LANES:], a1[:, LANES:])
        m_scr[p, rows, :] = jnp.where(half0, m0, m1)

    for p, d in enumerate(DILATIONS):
        def body(it, c, p=p, d=d):
            for u in range(ATTN_UNROLL):
                t = it * ATTN_UNROLL + u
                if d == 1:
                    tile(p, d, 0, t)
                else:
                    tile(p, d, t % d, t // d)
            return c

        lax.fori_loop(0, seq // NK // ATTN_UNROLL, body, 0)

    rc = 512
    for c in range(seq // rc):
        rows = pl.ds(c * rc, rc)
        ms = [m_scr[p, rows, :] for p in range(3)]
        m = jnp.maximum(jnp.maximum(ms[0], ms[1]), ms[2])
        ws = [jnp.exp2(x - m) for x in ms]
        num = ws[0] * acc_scr[0, rows, :] + ws[1] * acc_scr[1, rows, :] + ws[2] * acc_scr[2, rows, :]
        den = ws[0] * l_scr[0, rows, :] + ws[1] * l_scr[1, rows, :] + ws[2] * l_scr[2, rows, :]
        yb_ref[rows, :] = (num * (1.0 / den) * sgb_ref[rows, :].astype(F32)).astype(BF16)


def _band_distance():
    qi = np.arange(NK)[:, None]
    kj = np.arange(2 * NK)[None, :]
    dist = NK + qi - kj
    return jnp.asarray(np.where((dist >= 0) & (dist <= NK), dist, MASK_DIST), dtype=F32)


def _alibi_slopes():
    return jnp.asarray(2.0 ** (-8.0 * np.arange(1, H_B + 1) / H_B), dtype=F32)


def _attn_prompt(q, k, v, sgb, batch, seq):
    pair_spec = pl.BlockSpec((None, None, seq, LANES), lambda b, hp: (b, hp, 0, 0))
    stat = pltpu.VMEM((len(DILATIONS), seq, LANES), F32)
    return pl.pallas_call(
        functools.partial(_attn_prompt_kernel, seq=seq),
        grid=(batch, N_PAIR),
        in_specs=[pl.BlockSpec(memory_space=pltpu.SMEM),
                  pl.BlockSpec((NK, 2 * NK), lambda b, hp: (0, 0)),
                  pair_spec, pair_spec, pair_spec, pair_spec],
        out_specs=pl.BlockSpec((None, seq, LANES), lambda b, hp: (b, 0, hp)),
        out_shape=jax.ShapeDtypeStruct((batch, seq, D_MODEL), BF16),
        scratch_shapes=[pltpu.VMEM((seq, LANES), F32),
                        pltpu.VMEM((4, seq // 4, LANES), F32),
                        pltpu.VMEM((4, seq // 4, LANES), BF16),
                        pltpu.VMEM((16, seq // 16, LANES), BF16),
                        pltpu.VMEM((2, 1, NK + seq, LANES), BF16),
                        pltpu.VMEM((2, 4, NK + seq // 4, LANES), BF16),
                        pltpu.VMEM((2, 16, NK + seq // 16, LANES), BF16),
                        stat, stat, stat,
                        pltpu.VMEM((12, NK, 2 * NK), F32)],
        compiler_params=_cparams(("arbitrary", "arbitrary")),
        name="attn_prompt",
    )(_alibi_slopes(), _band_distance(), q, k, v, sgb)


def _outproj_mid_kernel(ya_ref, yb_ref, x_ref, mod_ref, w1_ref, w2_ref, g_ref, mod2_ref, x1_ref, h_ref):
    gate = mod_ref[:, 2 * D_MODEL:3 * D_MODEL]
    x1 = x_ref[...] + gate * (_dot(ya_ref[...], w1_ref[...]) + _dot(yb_ref[...], w2_ref[...]))
    x1_ref[...] = x1
    h_ref[...] = _norm_mod(x1, g_ref[...], mod2_ref[...]).astype(BF16)


def _outproj_final_kernel(ya_ref, yb_ref, x_ref, mod_ref, w1_ref, w2_ref, g_ref, y_ref):
    gate = mod_ref[:, 2 * D_MODEL:3 * D_MODEL]
    x1 = x_ref[...] + gate * (_dot(ya_ref[...], w1_ref[...]) + _dot(yb_ref[...], w2_ref[...]))
    y_ref[...] = _rms(x1) * g_ref[...]


def _outproj(ya, yb, x2d, mod, w_out, g, mod2, tm, rows_per_mod):
    rows = x2d.shape[0]
    half = w_out.shape[0] // 2
    row_spec = pl.BlockSpec((tm, D_MODEL), lambda i: (i, 0))
    if rows_per_mod == 1:
        mod_spec = pl.BlockSpec((tm, 3 * D_MODEL), lambda i: (i, 0))
        mods = (mod, mod2)
    else:
        tpb = rows_per_mod // tm
        mod_spec = pl.BlockSpec((None, 1, 3 * D_MODEL), lambda i: (i // tpb, 0, 0))
        mods = tuple(None if m is None else m.reshape(-1, 1, 3 * D_MODEL) for m in (mod, mod2))
    w1_spec = pl.BlockSpec((half, D_MODEL), lambda i: (0, 0))
    w2_spec = pl.BlockSpec((half, D_MODEL), lambda i: (1, 0))
    g_spec = pl.BlockSpec((1, D_MODEL), lambda i: (0, 0))
    common = dict(grid=(rows // tm,), compiler_params=_cparams(("arbitrary",)))
    if mod2 is None:
        return pl.pallas_call(
            _outproj_final_kernel,
            in_specs=[row_spec, row_spec, row_spec, mod_spec, w1_spec, w2_spec, g_spec],
            out_specs=row_spec,
            out_shape=jax.ShapeDtypeStruct((rows, D_MODEL), F32),
            name="outproj_final", **common,
        )(ya, yb, x2d, mods[0], w_out, w_out, g.reshape(1, D_MODEL))
    return pl.pallas_call(
        _outproj_mid_kernel,
        in_specs=[row_spec, row_spec, row_spec, mod_spec, w1_spec, w2_spec, g_spec, mod_spec],
        out_specs=[row_spec, row_spec],
        out_shape=[jax.ShapeDtypeStruct((rows, D_MODEL), F32), jax.ShapeDtypeStruct((rows, D_MODEL), BF16)],
        name="outproj_mid", **common,
    )(ya, yb, x2d, mods[0], w_out, w_out, g.reshape(1, D_MODEL), mods[1])


def _pick_by_chunk(j, vals):
    out = vals[-1]
    for idx in range(len(vals) - 2, -1, -1):
        out = jnp.where(j == idx, vals[idx], out)
    return out


N_SEG_O = 6


def _odd_front_kernel(h_ref, w_ref, pw_ref, ps_ref,
                      yc_ref, q_ref, kt_ref, v_ref, sgd_ref, pn_ref,
                      carry_scr, z_a, z_b, *, tiles_per_batch):
    nj = D_MODEL // TN
    s = pl.program_id(0)
    c_out = jnp.maximum(s - 1, 0)
    j_out = c_out % nj
    il_out = (c_out // nj) % tiles_per_batch
    tm = h_ref.shape[0]
    rb = tm // FRONT_PHASES
    seg_per_phase = N_SEG_O // FRONT_PHASES

    @pl.when(s == 0)
    def _():
        z_b[...] = jnp.zeros_like(z_b)
        carry_scr[...] = jnp.zeros_like(carry_scr)

    def step(z_w, z_r):
        width = _pick_by_chunk(j_out, [float(w) for w in POOL_SIZES])
        pw = pw_ref[...]
        ps = ps_ref[...]

        def phase(t, prev):
            for e in range(seg_per_phase):
                seg = t * seg_per_phase + e
                cols = pl.ds(pl.multiple_of(seg * TN, TN), TN)
                z_w[seg] = _dot(h_ref[...], w_ref[:, cols])
            rows = pl.ds(pl.multiple_of(t * rb, rb), rb)
            u, gc, q, k, v, gd = (z_r[seg, rows, :] for seg in range(N_SEG_O))
            ext = jnp.concatenate([prev, u], axis=0)
            sums = []
            acc = ext
            for sh in (1, 2, 4, 8):
                acc = acc + pltpu.roll(acc, sh, 0)
                sums.append(acc[POOL_PREV:, :])
            win = _pick_by_chunk(j_out, sums)
            pos = (il_out * tm + t * rb + lax.broadcasted_iota(jnp.int32, (rb, 1), 0)).astype(F32)
            pooled = win / jnp.minimum(width, pos + 1.0) - u
            mixed = _dot(pooled.astype(BF16), pw) * ps
            yc_ref[rows, :] = (mixed * _silu(gc)).astype(BF16)
            q_ref[rows, :] = q.astype(BF16)
            kt_ref[:, rows] = (k * (DK_D ** -0.5)).T.astype(BF16)
            v_ref[rows, :] = v.astype(BF16)
            sgd_ref[rows, :] = _silu(gd).astype(BF16)
            return u[rb - POOL_PREV:rb, :]

        prev0 = jnp.where(il_out == 0, 0.0, carry_scr[j_out])
        tail = lax.fori_loop(0, FRONT_PHASES, phase, prev0)
        carry_scr[j_out] = tail
        pn_ref[...] = tail

    @pl.when(s % 2 == 0)
    def _():
        step(z_a, z_b)

    @pl.when(s % 2 == 1)
    def _():
        step(z_b, z_a)


def _odd_front_prompt(h2d, w_in_cm, pw, ps, batch, seq, tm):
    rows = batch * seq
    tpb = seq // tm
    nj = D_MODEL // TN
    n_chunks = (rows // tm) * nj

    def c_in(s):
        return jnp.minimum(s, n_chunks - 1)

    def out_ij(s):
        c = jnp.maximum(s - 1, 0)
        return c // nj, c % nj

    def head_map(s):
        i, j = out_ij(s)
        return (i // tpb, j, i % tpb, 0)

    def kt_map(s):
        i, j = out_ij(s)
        return (i // tpb, j, 0, i % tpb)

    def tail_map(s):
        i, j = out_ij(s)
        return (i // tpb, 0, jnp.where(i % tpb == tpb - 1, j, 0))

    head_spec = pl.BlockSpec((None, None, tm, TN), head_map)
    head_shape = jax.ShapeDtypeStruct((batch, H_D, seq, DK_D), BF16)
    tok_spec = pl.BlockSpec((tm, TN), out_ij)
    tok_shape = jax.ShapeDtypeStruct((rows, D_MODEL), BF16)
    zbuf = pltpu.VMEM((N_SEG_O, tm, TN), F32)
    return pl.pallas_call(
        functools.partial(_odd_front_kernel, tiles_per_batch=tpb),
        grid=(n_chunks + 1,),
        in_specs=[pl.BlockSpec((tm, D_MODEL), lambda s: (c_in(s) // nj, 0)),
                  pl.BlockSpec((D_MODEL, N_SEG_O * TN), lambda s: (0, c_in(s) % nj)),
                  pl.BlockSpec((None, TN, TN), lambda s: (out_ij(s)[1], 0, 0)),
                  pl.BlockSpec((1, TN), lambda s: (0, out_ij(s)[1]))],
        out_specs=[tok_spec, head_spec, pl.BlockSpec((None, None, TN, tm), kt_map), head_spec, tok_spec,
                   pl.BlockSpec((None, POOL_PREV, TN), tail_map)],
        out_shape=[tok_shape, head_shape, jax.ShapeDtypeStruct((batch, H_D, DK_D, seq), BF16),
                   head_shape, tok_shape,
                   jax.ShapeDtypeStruct((batch, POOL_PREV, D_MODEL), F32)],
        scratch_shapes=[pltpu.VMEM((nj, POOL_PREV, TN), F32), zbuf, zbuf],
        compiler_params=_cparams(("arbitrary",)),
        name="odd_front_prompt",
    )(h2d, w_in_cm, pw, ps.reshape(1, D_MODEL))


RET_SBLK = 1024


def _ret_prompt_kernel(q_ref, kt_ref, v_ref, sgd_ref, dec_ref, cdec_ref, kdec_ref,
                       yd_ref, st_ref, state_scr, *, sblk, n_sblk):
    sb = pl.program_id(1)

    @pl.when(sb == 0)
    def _():
        state_scr[...] = jnp.zeros_like(state_scr)

    def chunk(c, carry):
        rows = pl.ds(pl.multiple_of(c * RET_CHUNK, RET_CHUNK), RET_CHUNK)
        for h in range(H_D):
            cols = slice(h * DK_D, (h + 1) * DK_D)
            q = q_ref[h, rows, :]
            kt = kt_ref[h, :, rows]
            v = v_ref[h, rows, :]
            state = state_scr[h]
            scores = _dot(q, kt) * dec_ref[h]
            o = _dot(scores.astype(BF16), v) + _dot(q, state.astype(BF16)) * cdec_ref[h]
            kd = (kt.astype(F32) * kdec_ref[h]).astype(BF16)
            state_scr[h] = float(np.exp(RET_CHUNK * RET_LOG_G[h])) * state + _dot(kd, v)
            yd_ref[rows, cols] = (_rms(o) * sgd_ref[rows, cols].astype(F32)).astype(BF16)
        return carry

    lax.fori_loop(0, sblk // RET_CHUNK, chunk, 0, unroll=4)

    @pl.when(sb == n_sblk - 1)
    def _():
        st_ref[...] = state_scr[...]


def _ret_consts():
    t = np.arange(RET_CHUNK, dtype=np.float64)
    diff = t[:, None] - t[None, :]
    lg = np.asarray(RET_LOG_G)[:, None, None]
    dec = np.where(diff >= 0, np.exp(np.maximum(diff, 0.0)[None] * lg), 0.0)
    cdec = np.broadcast_to(np.exp((t + 1.0)[None, :, None] * lg), (H_D, RET_CHUNK, DK_D))
    kdec = np.broadcast_to(np.exp((RET_CHUNK - 1.0 - t)[None, None, :] * lg), (H_D, DK_D, RET_CHUNK))
    return jnp.asarray(dec, F32), jnp.asarray(cdec, F32), jnp.asarray(kdec, F32)


def _ret_prompt(q, kt, v, sgd, batch, seq):
    dec, cdec, kdec = _ret_consts()
    n_sblk = seq // RET_SBLK
    head_spec = pl.BlockSpec((None, H_D, RET_SBLK, DK_D), lambda b, s: (b, 0, s, 0))
    tok_spec = pl.BlockSpec((None, RET_SBLK, D_MODEL), lambda b, s: (b, s, 0))

    def const_spec(shape):
        return pl.BlockSpec(shape, lambda b, s: (0, 0, 0))

    return pl.pallas_call(
        functools.partial(_ret_prompt_kernel, sblk=RET_SBLK, n_sblk=n_sblk),
        grid=(batch, n_sblk),
        in_specs=[head_spec, pl.BlockSpec((None, H_D, DK_D, RET_SBLK), lambda b, s: (b, 0, 0, s)), head_spec,
                  tok_spec, const_spec(dec.shape), const_spec(cdec.shape), const_spec(kdec.shape)],
        out_specs=[tok_spec, pl.BlockSpec((None, H_D, DK_D, DK_D), lambda b, s: (b, 0, 0, 0))],
        out_shape=[jax.ShapeDtypeStruct((batch, seq, D_MODEL), BF16),
                   jax.ShapeDtypeStruct((batch, H_D, DK_D, DK_D), F32)],
        scratch_shapes=[pltpu.VMEM((H_D, DK_D, DK_D), F32)],
        compiler_params=_cparams(("arbitrary", "arbitrary")),
        name="ret_prompt",
    )(q, kt, v, sgd.reshape(batch, seq, D_MODEL), dec, cdec, kdec)


def _even_front_sample_kernel(x_ref, mod_ref, g_ref, wbg, wcg, wxv, wga, wq, wk, wv, wgb, cw_ref, cb_ref,
                              prev_ref, ya_ref, q_ref, k_ref, v_ref, sgb_ref, cn_ref):
    h = _norm_mod(x_ref[...], g_ref[...], mod_ref[...]).astype(BF16)
    u = _dot(h, wcg[...]) * _dot(h, wxv[...])
    cw = cw_ref[...]
    conv = cb_ref[...] + cw[0:1, :] * prev_ref[0] + cw[1:2, :] * prev_ref[1] + cw[2:3, :] * u
    cn_ref[0] = prev_ref[1]
    cn_ref[1] = u
    ya_ref[...] = (_dot(h, wbg[...]) * conv * _silu(_dot(h, wga[...]))).astype(BF16)
    q_ref[...] = _dot(h, wq[...]) * (HD_B ** -0.5)
    k_ref[...] = _dot(h, wk[...])
    v_ref[...] = _dot(h, wv[...])
    sgb_ref[...] = _silu(_dot(h, wgb[...]))


def _even_front_sample(x2d, mod, g, w_in, cw, cb, prev_t):
    rows = x2d.shape[0]
    nj = D_MODEL // TN

    def wspec(s):
        return pl.BlockSpec((D_MODEL, TN), lambda j, s=s: (0, j * N_SEG_E + s))

    full = pl.BlockSpec((rows, D_MODEL), lambda j: (0, 0))
    col = pl.BlockSpec((rows, TN), lambda j: (0, j))
    st = pl.BlockSpec((2, rows, TN), lambda j: (0, 0, j))
    colf = jax.ShapeDtypeStruct((rows, D_MODEL), F32)
    return pl.pallas_call(
        _even_front_sample_kernel,
        grid=(nj,),
        in_specs=[full, pl.BlockSpec((rows, 3 * D_MODEL), lambda j: (0, 0)),
                  pl.BlockSpec((1, D_MODEL), lambda j: (0, 0))] + [wspec(s) for s in range(8)] + [
            pl.BlockSpec((3, TN), lambda j: (0, j)), pl.BlockSpec((1, TN), lambda j: (0, j)), st],
        out_specs=[col, col, col, col, col, st],
        out_shape=[jax.ShapeDtypeStruct((rows, D_MODEL), BF16), colf, colf, colf, colf,
                   jax.ShapeDtypeStruct((2, rows, D_MODEL), F32)],
        compiler_params=_cparams(("arbitrary",)),
        name="even_front_sample",
    )(x2d, mod, g.reshape(1, D_MODEL), *([w_in] * 8), cw, cb.reshape(1, D_MODEL), prev_t)


def _attn_sample_kernel(slope_ref, pos_ref, q_ref, kn_ref, vn_ref, sgb_ref, kt_ref, vt_ref, yb_ref):
    gw = q_ref.shape[-1]
    row = lax.broadcasted_iota(jnp.int32, (8, gw), 0)
    col = lax.broadcasted_iota(jnp.int32, (8, gw), 1)
    own = (col >= row * HD_B) & (col < (row + 1) * HD_B)
    qm = jnp.where(own, q_ref[...], 0.0)
    s_self = jnp.sum(qm * kn_ref[...], axis=-1, keepdims=True)
    s_all = _dot(qm.astype(BF16), kt_ref[...].astype(BF16))
    s_all = s_all - slope_ref[:, 0:1] * pos_ref[0:1, :]
    v_self = vn_ref[...]
    probs, stats = [], []
    for p in range(len(DILATIONS)):
        s = s_all - pos_ref[p + 1:p + 2, :]
        m = jnp.maximum(jnp.max(s, axis=-1, keepdims=True), s_self)
        pe = jnp.exp(s - m)
        p_self = jnp.exp(s_self - m)
        probs.append(pe)
        stats.append((m, jnp.sum(pe, axis=-1, keepdims=True) + p_self, p_self))
    o_all = _dot_nt(jnp.concatenate(probs, axis=0).astype(BF16), vt_ref[...].astype(BF16))
    outs, lses = [], []
    for p, (m, l, p_self) in enumerate(stats):
        outs.append((o_all[8 * p:8 * p + 8, :] + p_self * v_self) * (1.0 / l))
        lses.append(m + jnp.log(l))
    m = jnp.maximum(jnp.maximum(lses[0], lses[1]), lses[2])
    ws = [jnp.exp(x - m) for x in lses]
    o = (ws[0] * outs[0] + ws[1] * outs[1] + ws[2] * outs[2]) * (1.0 / (ws[0] + ws[1] + ws[2]))
    o = jnp.sum(jnp.where(own, o, 0.0), axis=0, keepdims=True)
    yb_ref[...] = (o * sgb_ref[...]).astype(BF16)


def _attn_sample(q, kn, vn, sgb, cache_kt, cache_vt):
    rows, _, wb = cache_kt.shape
    gh = 4
    gw = gh * HD_B
    ng = H_B // gh
    back = wb - np.arange(wb, dtype=np.float64)
    pos = [back] + [np.where((back % d == 0) & (back <= NK * d), 0.0, MASK_ADD) for d in DILATIONS]
    slopes = (2.0 ** (-8.0 * np.arange(1, H_B + 1) / H_B)).reshape(ng, gh)
    slopes = np.concatenate([slopes, np.zeros((ng, 8 - gh))], axis=1)
    slope_arr = jnp.asarray(np.broadcast_to(slopes[:, :, None], (ng, 8, LANES)), F32)

    def r3(a):
        return a.reshape(rows, 1, D_MODEL)

    row_spec = pl.BlockSpec((None, 1, gw), lambda b, g: (b, 0, g))
    t_spec = pl.BlockSpec((None, gw, wb), lambda b, g: (b, g, 0))
    return pl.pallas_call(
        _attn_sample_kernel,
        grid=(rows, ng),
        in_specs=[pl.BlockSpec((None, 8, LANES), lambda b, g: (g, 0, 0)),
                  pl.BlockSpec((4, wb), lambda b, g: (0, 0)),
                  row_spec, row_spec, row_spec, row_spec, t_spec, t_spec],
        out_specs=row_spec,
        out_shape=jax.ShapeDtypeStruct((rows, 1, D_MODEL), BF16),
        compiler_params=_cparams(("arbitrary", "arbitrary")),
        name="attn_sample",
    )(slope_arr, jnp.asarray(np.stack(pos), F32), r3(q), r3(kn), r3(vn), r3(sgb), cache_kt, cache_vt
      ).reshape(rows, D_MODEL)


def _odd_front_sample_kernel(h_ref, wu, wgc, wq, wk, wv, wgd, pw_ref, ps_ref, prev_ref,
                             yc_ref, q_ref, k_ref, v_ref, sgd_ref, pn_ref):
    j = pl.program_id(0)
    h = h_ref[...]
    u = _dot(h, wu[...])
    n_prev = prev_ref.shape[0]
    sums = []
    s = jnp.zeros_like(u)
    back = 0
    for w in POOL_SIZES:
        while back < w - 1:
            s = s + prev_ref[n_prev - 1 - back]
            back += 1
        sums.append(s)
    win = _pick_by_chunk(j, sums) + u
    inv_w = _pick_by_chunk(j, [1.0 / w for w in POOL_SIZES])
    pooled = win * inv_w - u
    mixed = _dot(pooled.astype(BF16), pw_ref[...]) * ps_ref[...]
    yc_ref[...] = (mixed * _silu(_dot(h, wgc[...]))).astype(BF16)
    q_ref[...] = _dot(h, wq[...])
    k_ref[...] = _dot(h, wk[...]) * (DK_D ** -0.5)
    v_ref[...] = _dot(h, wv[...])
    sgd_ref[...] = _silu(_dot(h, wgd[...]))
    for t in range(n_prev - 1):
        pn_ref[t] = prev_ref[t + 1]
    pn_ref[n_prev - 1] = u


def _odd_front_sample(h2d, w_in, pw, ps, prev_t):
    rows = h2d.shape[0]
    n_prev = prev_t.shape[0]
    nj = D_MODEL // TN

    def wspec(s):
        return pl.BlockSpec((D_MODEL, TN), lambda j, s=s: (0, j * N_SEG_O + s))

    col = pl.BlockSpec((rows, TN), lambda j: (0, j))
    st = pl.BlockSpec((n_prev, rows, TN), lambda j: (0, 0, j))
    colf = jax.ShapeDtypeStruct((rows, D_MODEL), F32)
    return pl.pallas_call(
        _odd_front_sample_kernel,
        grid=(nj,),
        in_specs=[pl.BlockSpec((rows, D_MODEL), lambda j: (0, 0))] + [wspec(s) for s in range(6)] + [
            pl.BlockSpec((None, TN, TN), lambda j: (j, 0, 0)), pl.BlockSpec((1, TN), lambda j: (0, j)), st],
        out_specs=[col, col, col, col, col, st],
        out_shape=[jax.ShapeDtypeStruct((rows, D_MODEL), BF16), colf, colf, colf, colf,
                   jax.ShapeDtypeStruct((n_prev, rows, D_MODEL), F32)],
        compiler_params=_cparams(("arbitrary",)),
        name="odd_front_sample",
    )(h2d, *([w_in] * 6), pw, ps.reshape(1, D_MODEL), prev_t)


def _ret_sample_kernel(q_ref, k_ref, v_ref, sgd_ref, st_ref, yd_ref, sn_ref):
    row = lax.broadcasted_iota(jnp.int32, (DK_D, DK_D), 0)
    col = lax.broadcasted_iota(jnp.int32, (DK_D, DK_D), 1)
    for h in range(H_D):
        g = float(np.exp(RET_LOG_G[h]))
        cols = slice(h * DK_D, (h + 1) * DK_D)
        q = q_ref[:, cols]
        k = k_ref[:, cols]
        v = v_ref[:, cols]
        state = st_ref[h]
        qk = jnp.sum(q * k, axis=-1, keepdims=True)
        q8 = jnp.broadcast_to(q, (8, DK_D)).astype(BF16)
        cross = _dot(q8, state.astype(BF16))[0:1, :]
        o = qk * v + g * cross
        k_diag = jnp.where(row == col, jnp.broadcast_to(k, (DK_D, DK_D)), 0.0).astype(BF16)
        v_rows = jnp.broadcast_to(v, (DK_D, DK_D)).astype(BF16)
        sn_ref[h] = g * state + _dot(k_diag, v_rows)
        yd_ref[:, cols] = (_rms(o) * sgd_ref[:, cols]).astype(BF16)


def _ret_sample(q, k, v, sgd, state):
    rows = q.shape[0]

    def r3(a):
        return a.reshape(rows, 1, D_MODEL)

    row_spec = pl.BlockSpec((None, 1, D_MODEL), lambda b: (b, 0, 0))
    st_spec = pl.BlockSpec((None, H_D, DK_D, DK_D), lambda b: (b, 0, 0, 0))
    yd, sn = pl.pallas_call(
        _ret_sample_kernel,
        grid=(rows,),
        in_specs=[row_spec, row_spec, row_spec, row_spec, st_spec],
        out_specs=[row_spec, st_spec],
        out_shape=[jax.ShapeDtypeStruct((rows, 1, D_MODEL), BF16),
                   jax.ShapeDtypeStruct((rows, H_D, DK_D, DK_D), F32)],
        compiler_params=_cparams(("arbitrary",)),
        name="ret_sample",
    )(r3(q), r3(k), r3(v), r3(sgd), state)
    return yd.reshape(rows, D_MODEL), sn


def kernel(x_prompt, x_sample, c_prompt, c_sample, state_conv, cache_win_k, cache_win_v, state_pool, state_ret,
           norm_e, ada_w_e, ada_b_e, w_in_e, conv_w, conv_b, w_out_e, norm_o, ada_w_o, ada_b_o, w_in_o,
           pool_w, pool_scale, w_out_o, norm_f):
    batch, seq, d = x_prompt.shape
    sb = x_sample.shape[0]
    assert d == D_MODEL and x_sample.shape[1] == 1
    assert norm_e.shape[0] == 1 and norm_o.shape[0] == 1
    wb = cache_win_k.shape[2]
    keep = min(wb, seq)
    assert wb == DILATIONS[-1] * NK and seq % (DILATIONS[-1] * NK) == 0
    tm = 512

    n_c = batch + sb
    pad = (-n_c) % 8
    c_all = jnp.concatenate([c_prompt, c_sample, jnp.zeros((pad, d), F32)], axis=0)
    mod_e, mod_o = _adaln(c_all, ada_w_e[0], ada_b_e[0], ada_w_o[0], ada_b_o[0])
    mod_e_p, mod_e_s = mod_e[:batch], mod_e[batch:n_c]
    mod_o_p, mod_o_s = mod_o[:batch], mod_o[batch:n_c]

    w_in_e16 = _chunk_major(w_in_e[0], N_SEG_E).astype(BF16)
    w_out_e16 = w_out_e[0].astype(BF16)
    w_in_o16 = _chunk_major(w_in_o[0], N_SEG_O).astype(BF16)
    w_out_o16 = w_out_o[0].astype(BF16)
    pool_w16 = pool_w[0].astype(BF16)

    xp2d = x_prompt.reshape(batch * seq, d)
    ya, q, k, v, sgb, k_new, v_new, conv_tail = _even_front_prompt(
        xp2d, mod_e_p, norm_e[0], w_in_e16, conv_w[0], conv_b[0], batch, seq, keep, tm)
    yb = _attn_prompt(q, k, v, sgb, batch, seq).reshape(batch * seq, d)
    x1, h1 = _outproj(ya, yb, xp2d, mod_e_p, w_out_e16, norm_o[0], mod_o_p, tm, seq)
    yc, rq, rk, rv, sgd, pool_tail = _odd_front_prompt(h1, w_in_o16, pool_w16, pool_scale[0], batch, seq, tm)
    yd, ret_p = _ret_prompt(rq, rk, rv, sgd, batch, seq)
    y_prompt = _outproj(yc, yd.reshape(batch * seq, d), x1, mod_o_p, w_out_o16, norm_f, None, tm, seq)

    xs2d = x_sample.reshape(sb, d)
    conv_prev_t = jnp.transpose(state_conv[0], (1, 0, 2))
    ya_s, q_s, k_s, v_s, sgb_s, conv_s_t = _even_front_sample(
        xs2d, mod_e_s, norm_e[0], w_in_e16, conv_w[0], conv_b[0], conv_prev_t)
    cache_kt = jnp.transpose(cache_win_k[0], (0, 2, 3, 1)).reshape(sb, d, wb)
    cache_vt = jnp.transpose(cache_win_v[0], (0, 2, 3, 1)).reshape(sb, d, wb)
    yb_s = _attn_sample(q_s, k_s, v_s, sgb_s, cache_kt, cache_vt)
    x1_s, h1_s = _outproj(ya_s, yb_s, xs2d, mod_e_s, w_out_e16, norm_o[0], mod_o_s, sb, 1)
    pool_prev_t = jnp.transpose(state_pool[0], (1, 0, 2))
    yc_s, rq_s, rk_s, rv_s, sgd_s, pool_s_t = _odd_front_sample(h1_s, w_in_o16, pool_w16, pool_scale[0], pool_prev_t)
    yd_s, ret_s = _ret_sample(rq_s, rk_s, rv_s, sgd_s, state_ret[0])
    y_sample = _outproj(yc_s, yd_s, x1_s, mod_o_s, w_out_o16, norm_f, None, sb, 1)

    return (
        y_prompt.reshape(batch, seq, d),
        y_sample.reshape(sb, 1, d),
        conv_tail[:, 6:8][None],
        jnp.transpose(conv_s_t, (1, 0, 2))[None],
        jnp.transpose(k_new.reshape(batch, H_B, HD_B, keep), (0, 3, 1, 2))[None],
        k_s.reshape(1, sb, 1, H_B, HD_B),
        jnp.transpose(v_new.reshape(batch, H_B, HD_B, keep), (0, 3, 1, 2))[None],
        v_s.reshape(1, sb, 1, H_B, HD_B),
        pool_tail[:, 1:][None],
        jnp.transpose(pool_s_t, (1, 0, 2))[None],
        ret_p[None],
        ret_s[None],
    )
```

```python
import functools

import numpy as np
import jax
import jax.numpy as jnp
from jax import lax
from jax.experimental import pallas as pl
from jax.experimental.pallas import tpu as pltpu

F32 = jnp.float32
BF16 = jnp.bfloat16

D_MODEL = 1024
EPS = 1e-6
H_B = 16
HD_B = 64
N_PAIR = H_B // 2
LANES = 128
NK = 128
DILATIONS = (1, 4, 16)
POOL_SIZES = (2, 4, 8, 16)
POOL_PREV = 16
H_D = 4
DK_D = 256
RET_CHUNK = 128
TN = 256
MASK_DIST = 1e9
LOG2E = float(np.log2(np.e))
MASK_ADD = 1e30
VMEM_LIMIT = 56 * 1024 * 1024

RET_LOG_G = [float(np.log(1.0 - 2.0 ** (-5.0 - h))) for h in range(H_D)]


def _cparams(sem):
    return pltpu.CompilerParams(dimension_semantics=sem, vmem_limit_bytes=VMEM_LIMIT)


def _silu(x):
    return x * (1.0 / (1.0 + jnp.exp(-x)))


def _dot(a, b):
    return jnp.dot(a, b, preferred_element_type=F32)


def _dot_nt(a, b):
    return lax.dot_general(a, b, (((1,), (1,)), ((), ())), preferred_element_type=F32)


def _rms(x):
    return x * lax.rsqrt(jnp.mean(x * x, axis=-1, keepdims=True) + EPS)


def _adaln_kernel(c_ref, we_ref, be_ref, wo_ref, bo_ref, me_ref, mo_ref):
    sc = _silu(c_ref[...]).astype(BF16)
    me_ref[...] = _dot(sc, we_ref[...].astype(BF16)) + be_ref[...]
    mo_ref[...] = _dot(sc, wo_ref[...].astype(BF16)) + bo_ref[...]


def _adaln(c_all, we, be, wo, bo):
    rows = c_all.shape[0]
    tn = 512
    n = 3 * D_MODEL
    wspec = pl.BlockSpec((D_MODEL, tn), lambda j: (0, j))
    bspec = pl.BlockSpec((1, tn), lambda j: (0, j))
    ospec = pl.BlockSpec((rows, tn), lambda j: (0, j))
    return pl.pallas_call(
        _adaln_kernel,
        grid=(n // tn,),
        in_specs=[pl.BlockSpec((rows, D_MODEL), lambda j: (0, 0)), wspec, bspec, wspec, bspec],
        out_specs=[ospec, ospec],
        out_shape=[jax.ShapeDtypeStruct((rows, n), F32)] * 2,
        compiler_params=_cparams(("arbitrary",)),
        name="adaln",
    )(c_all, we, be.reshape(1, n), wo, bo.reshape(1, n))


def _norm_mod(x, g, mod):
    shift = mod[:, 0:D_MODEL]
    scale = mod[:, D_MODEL:2 * D_MODEL]
    return _rms(x) * g * (1.0 + scale) + shift


def _shift_rows(u, k, prev_rows):
    row = lax.broadcasted_iota(jnp.int32, u.shape, 0)
    out = pltpu.roll(u, k, 0)
    for idx, pr in enumerate(prev_rows):
        out = jnp.where(row == idx, pr, out)
    return out


N_SEG_E = 8
FRONT_PHASES = 2


def _even_front_kernel(x_ref, mod_ref, g_ref, *refs, tiles_per_batch, n_chunks):
    ws = refs[:N_SEG_E]
    cw_ref, cb_ref = refs[N_SEG_E:N_SEG_E + 2]
    ya_ref, q_ref, k_ref, v_ref, sgb_ref, kn_ref, vn_ref, cn_ref = refs[N_SEG_E + 2:N_SEG_E + 10]
    h_scr, carry_scr, z_a, z_b = refs[N_SEG_E + 10:]
    nj = D_MODEL // TN
    s = pl.program_id(0)
    c_out = jnp.maximum(s - 1, 0)
    j_out = c_out % nj
    il_out = (c_out // nj) % tiles_per_batch

    @pl.when(s == 0)
    def _():
        z_b[...] = jnp.zeros_like(z_b)
        carry_scr[...] = jnp.zeros_like(carry_scr)

    @pl.when((s < n_chunks) & (s % nj == 0))
    def _():
        h_scr[...] = _norm_mod(x_ref[...], g_ref[...], mod_ref[...]).astype(BF16)

    tm = h_scr.shape[0]
    rb = tm // FRONT_PHASES
    seg_per_phase = N_SEG_E // FRONT_PHASES

    def phase(t, z_w, z_r):
        for seg in range(t * seg_per_phase, (t + 1) * seg_per_phase):
            z_w[seg] = _dot(h_scr[...], ws[seg][...])
        rows = slice(t * rb, (t + 1) * rb)
        bg, cg, xv, ga, q, k, v, gb = (z_r[seg, rows, :] for seg in range(N_SEG_E))
        u = cg * xv
        prev = carry_scr[j_out]
        if t == 0:
            prev = jnp.where(il_out == 0, 0.0, prev)
        p2, p1 = prev[6:7, :], prev[7:8, :]
        u1 = _shift_rows(u, 1, [p1])
        u2 = _shift_rows(u, 2, [p2, p1])
        cw = cw_ref[...]
        conv = cb_ref[...] + cw[0:1, :] * u2 + cw[1:2, :] * u1 + cw[2:3, :] * u
        tail = u[rb - 8:rb, :]
        carry_scr[j_out] = tail
        if t == FRONT_PHASES - 1:
            cn_ref[...] = tail
        ya_ref[rows, :] = (bg * conv * _silu(ga)).astype(BF16)
        kn_ref[:, rows] = k.T
        vn_ref[:, rows] = v.T
        for ref, val in ((q_ref, q * (HD_B ** -0.5 * LOG2E)),
                         (k_ref, k), (v_ref, v), (sgb_ref, _silu(gb))):
            val = val.astype(BF16)
            for e in range(TN // LANES):
                ref[e, rows, :] = val[:, e * LANES:(e + 1) * LANES]

    for parity, (z_w, z_r) in enumerate(((z_a, z_b), (z_b, z_a))):
        for t in range(FRONT_PHASES):
            pl.when(s % 2 == parity)(functools.partial(phase, t, z_w, z_r))


def _even_front_prompt(x2d, mod, g, w_in, cw, cb, batch, seq, keep, tm):
    rows = batch * seq
    tpb = seq // tm
    nj = D_MODEL // TN
    n_chunks = (rows // tm) * nj
    off = (seq - keep) // tm
    ppc = TN // LANES

    def c_in(s):
        return jnp.minimum(s, n_chunks - 1)

    def c_out(s):
        return jnp.maximum(s - 1, 0)

    def out_ij(s):
        c = c_out(s)
        return c // nj, c % nj

    def pair_map(s):
        i, j = out_ij(s)
        return (i // tpb, j, i % tpb, 0)

    def keep_map(s):
        i, j = out_ij(s)
        il = i % tpb
        kept = il >= off
        return (i // tpb, jnp.where(kept, j, 0), jnp.where(kept, il - off, 0))

    def tail_map(s):
        i, j = out_ij(s)
        return (i // tpb, 0, jnp.where(i % tpb == tpb - 1, j, 0))

    pair_spec = pl.BlockSpec((None, ppc, tm, LANES), pair_map)
    pair_shape = jax.ShapeDtypeStruct((batch, N_PAIR, seq, LANES), BF16)
    keep_spec = pl.BlockSpec((None, TN, tm), keep_map)
    keep_shape = jax.ShapeDtypeStruct((batch, D_MODEL, keep), F32)
    return pl.pallas_call(
        functools.partial(_even_front_kernel, tiles_per_batch=tpb, n_chunks=n_chunks),
        grid=(n_chunks + 1,),
        in_specs=[
            pl.BlockSpec((tm, D_MODEL), lambda s: (c_in(s) // nj, 0)),
            pl.BlockSpec((None, 1, 3 * D_MODEL), lambda s: (c_in(s) // nj // tpb, 0, 0)),
            pl.BlockSpec((1, D_MODEL), lambda s: (0, 0)),
        ] + [pl.BlockSpec((D_MODEL, TN), lambda s, seg=seg: (0, seg * nj + c_in(s) % nj)) for seg in range(N_SEG_E)] + [
            pl.BlockSpec((3, TN), lambda s: (0, c_out(s) % nj)),
            pl.BlockSpec((1, TN), lambda s: (0, c_out(s) % nj)),
        ],
        out_specs=[
            pl.BlockSpec((tm, TN), lambda s: out_ij(s)),
            pair_spec, pair_spec, pair_spec, pair_spec,
            keep_spec, keep_spec,
            pl.BlockSpec((None, 8, TN), tail_map),
        ],
        out_shape=[
            jax.ShapeDtypeStruct((rows, D_MODEL), BF16),
            pair_shape, pair_shape, pair_shape, pair_shape,
            keep_shape, keep_shape,
            jax.ShapeDtypeStruct((batch, 8, D_MODEL), F32),
        ],
        scratch_shapes=[pltpu.VMEM((tm, D_MODEL), BF16), pltpu.VMEM((nj, 8, TN), F32),
                        pltpu.VMEM((N_SEG_E, tm, TN), F32), pltpu.VMEM((N_SEG_E, tm, TN), F32)],
        compiler_params=_cparams(("arbitrary",)),
        name="even_front_prompt",
    )(x2d, mod.reshape(batch, 1, 3 * D_MODEL), g.reshape(1, D_MODEL), *([w_in] * N_SEG_E), cw, cb.reshape(1, D_MODEL))


ATTN_UNROLL = 4


def _attn_prompt_kernel(slope_ref, dist_ref, q_ref, k_ref, v_ref, sgb_ref, yb_ref,
                        x32_scr, st4_scr, q4_scr, q16_scr, kv1_scr, kv4_scr, kv16_scr,
                        acc_scr, m_scr, l_scr, bias_scr, *, seq):
    hp = pl.program_id(1)
    lane = lax.broadcasted_iota(jnp.int32, (1, LANES), 1)
    half0 = lane < HD_B
    dist = dist_ref[...]
    for p, d in enumerate(DILATIONS):
        for e in range(2):
            bias = dist * (slope_ref[2 * hp + e] * (float(d) * LOG2E))
            bias_scr[2 * p + e] = bias
            bias_scr[6 + 2 * p + e] = jnp.concatenate(
                [jnp.full((NK, NK), MASK_DIST, F32), bias[:, NK:]], axis=1)

    zpad = jnp.zeros((NK, LANES), BF16)
    for a, ref in enumerate((q_ref, k_ref, v_ref)):
        x = ref[...]
        if a > 0:
            kv1_scr[a - 1, 0, pl.ds(0, NK), :] = zpad
            kv1_scr[a - 1, 0, pl.ds(NK, seq), :] = x
        x32_scr[...] = x.astype(F32)
        for r in range(4):
            s = x32_scr[pl.ds(r, seq // 4, stride=4), :]
            st4_scr[r] = s
            if a == 0:
                q4_scr[r] = s.astype(BF16)
            else:
                kv4_scr[a - 1, r, pl.ds(0, NK), :] = zpad
                kv4_scr[a - 1, r, pl.ds(NK, seq // 4), :] = s.astype(BF16)
        for r in range(16):
            s = st4_scr[r % 4, pl.ds(r // 4, seq // 16, stride=4), :].astype(BF16)
            if a == 0:
                q16_scr[r] = s
            else:
                kv16_scr[a - 1, r, pl.ds(0, NK), :] = zpad
                kv16_scr[a - 1, r, pl.ds(NK, seq // 16), :] = s

    ones = jnp.ones((2 * NK, LANES), BF16)

    def tile(p, d, r, n):
        q0 = pl.ds(pl.multiple_of(n * NK, NK), NK)
        k0 = pl.ds(pl.multiple_of(n * NK, NK), 2 * NK)
        if d == 1:
            q, kv = q_ref[q0, :], kv1_scr
        else:
            q, kv = (q4_scr if d == 4 else q16_scr)[r, q0, :], (kv4_scr if d == 4 else kv16_scr)
        kk = kv[0, r, k0, :]
        vv1 = jnp.concatenate([kv[1, r, k0, :], ones], axis=1)
        first = jnp.where(n == 0, 1, 0)
        res = []
        for e in range(2):
            qe = jnp.where(half0 if e == 0 else jnp.logical_not(half0), q, jnp.zeros_like(q))
            s = _dot_nt(qe, kk) - bias_scr[6 * first + 2 * p + e]
            m = jnp.max(s, axis=-1, keepdims=True)
            pe = jnp.exp2(s - m).astype(BF16)
            res.append((_dot(pe, vv1), m))
        (a0, m0), (a1, m1) = res
        rows = pl.ds(r + d * NK * n, NK, stride=d) if d > 1 else q0
        acc_scr[p, rows, :] = jnp.where(half0, a0[:, :LANES], a1[:, :LANES])
        l_scr[p, rows, :] = jnp.where(half0, a0[:, LANES:], a1[:, LANES:])
        m_scr[p, rows, :] = jnp.where(half0, m0, m1)

    for p, d in enumerate(DILATIONS):
        def body(it, c, p=p, d=d):
            for u in range(ATTN_UNROLL):
                t = it * ATTN_UNROLL + u
                if d == 1:
                    tile(p, d, 0, t)
                else:
                    tile(p, d, t % d, t // d)
            return c

        lax.fori_loop(0, seq // NK // ATTN_UNROLL, body, 0)

    rc = 512
    for c in range(seq // rc):
        rows = pl.ds(c * rc, rc)
        ms = [m_scr[p, rows, :] for p in range(3)]
        m = jnp.maximum(jnp.maximum(ms[0], ms[1]), ms[2])
        ws = [jnp.exp2(x - m) for x in ms]
        num = ws[0] * acc_scr[0, rows, :] + ws[1] * acc_scr[1, rows, :] + ws[2] * acc_scr[2, rows, :]
        den = ws[0] * l_scr[0, rows, :] + ws[1] * l_scr[1, rows, :] + ws[2] * l_scr[2, rows, :]
        yb_ref[rows, :] = (num * (1.0 / den) * sgb_ref[rows, :].astype(F32)).astype(BF16)


def _band_distance():
    qi = np.arange(NK)[:, None]
    kj = np.arange(2 * NK)[None, :]
    dist = NK + qi - kj
    return jnp.asarray(np.where((dist >= 0) & (dist <= NK), dist, MASK_DIST), dtype=F32)


def _alibi_slopes():
    return jnp.asarray(2.0 ** (-8.0 * np.arange(1, H_B + 1) / H_B), dtype=F32)


def _attn_prompt(q, k, v, sgb, batch, seq):
    pair_spec = pl.BlockSpec((None, None, seq, LANES), lambda b, hp: (b, hp, 0, 0))
    stat = pltpu.VMEM((len(DILATIONS), seq, LANES), F32)
    return pl.pallas_call(
        functools.partial(_attn_prompt_kernel, seq=seq),
        grid=(batch, N_PAIR),
        in_specs=[pl.BlockSpec(memory_space=pltpu.SMEM),
                  pl.BlockSpec((NK, 2 * NK), lambda b, hp: (0, 0)),
                  pair_spec, pair_spec, pair_spec, pair_spec],
        out_specs=pl.BlockSpec((None, seq, LANES), lambda b, hp: (b, 0, hp)),
        out_shape=jax.ShapeDtypeStruct((batch, seq, D_MODEL), BF16),
        scratch_shapes=[pltpu.VMEM((seq, LANES), F32),
                        pltpu.VMEM((4, seq // 4, LANES), F32),
                        pltpu.VMEM((4, seq // 4, LANES), BF16),
                        pltpu.VMEM((16, seq // 16, LANES), BF16),
                        pltpu.VMEM((2, 1, NK + seq, LANES), BF16),
                        pltpu.VMEM((2, 4, NK + seq // 4, LANES), BF16),
                        pltpu.VMEM((2, 16, NK + seq // 16, LANES), BF16),
                        stat, stat, stat,
                        pltpu.VMEM((12, NK, 2 * NK), F32)],
        compiler_params=_cparams(("arbitrary", "arbitrary")),
        name="attn_prompt",
    )(_alibi_slopes(), _band_distance(), q, k, v, sgb)


def _outproj_mid_kernel(ya_ref, yb_ref, x_ref, mod_ref, w1_ref, w2_ref, g_ref, mod2_ref, x1_ref, h_ref):
    gate = mod_ref[:, 2 * D_MODEL:3 * D_MODEL]
    x1 = x_ref[...] + gate * (_dot(ya_ref[...], w1_ref[...]) + _dot(yb_ref[...], w2_ref[...]))
    x1_ref[...] = x1
    h_ref[...] = _norm_mod(x1, g_ref[...], mod2_ref[...]).astype(BF16)


def _outproj_final_kernel(ya_ref, yb_ref, x_ref, mod_ref, w1_ref, w2_ref, g_ref, y_ref):
    gate = mod_ref[:, 2 * D_MODEL:3 * D_MODEL]
    x1 = x_ref[...] + gate * (_dot(ya_ref[...], w1_ref[...]) + _dot(yb_ref[...], w2_ref[...]))
    y_ref[...] = _rms(x1) * g_ref[...]


def _outproj(ya, yb, x2d, mod, w_out, g, mod2, tm, rows_per_mod):
    rows = x2d.shape[0]
    half = w_out.shape[0] // 2
    row_spec = pl.BlockSpec((tm, D_MODEL), lambda i: (i, 0))
    if rows_per_mod == 1:
        mod_spec = pl.BlockSpec((tm, 3 * D_MODEL), lambda i: (i, 0))
        mods = (mod, mod2)
    else:
        tpb = rows_per_mod // tm
        mod_spec = pl.BlockSpec((None, 1, 3 * D_MODEL), lambda i: (i // tpb, 0, 0))
        mods = tuple(None if m is None else m.reshape(-1, 1, 3 * D_MODEL) for m in (mod, mod2))
    w1_spec = pl.BlockSpec((half, D_MODEL), lambda i: (0, 0))
    w2_spec = pl.BlockSpec((half, D_MODEL), lambda i: (1, 0))
    g_spec = pl.BlockSpec((1, D_MODEL), lambda i: (0, 0))
    common = dict(grid=(rows // tm,), compiler_params=_cparams(("arbitrary",)))
    if mod2 is None:
        return pl.pallas_call(
            _outproj_final_kernel,
            in_specs=[row_spec, row_spec, row_spec, mod_spec, w1_spec, w2_spec, g_spec],
            out_specs=row_spec,
            out_shape=jax.ShapeDtypeStruct((rows, D_MODEL), F32),
            name="outproj_final", **common,
        )(ya, yb, x2d, mods[0], w_out, w_out, g.reshape(1, D_MODEL))
    return pl.pallas_call(
        _outproj_mid_kernel,
        in_specs=[row_spec, row_spec, row_spec, mod_spec, w1_spec, w2_spec, g_spec, mod_spec],
        out_specs=[row_spec, row_spec],
        out_shape=[jax.ShapeDtypeStruct((rows, D_MODEL), F32), jax.ShapeDtypeStruct((rows, D_MODEL), BF16)],
        name="outproj_mid", **common,
    )(ya, yb, x2d, mods[0], w_out, w_out, g.reshape(1, D_MODEL), mods[1])


def _pick_by_chunk(j, vals):
    out = vals[-1]
    for idx in range(len(vals) - 2, -1, -1):
        out = jnp.where(j == idx, vals[idx], out)
    return out


N_SEG_O = 6


def _odd_front_kernel(h_ref, *refs, tiles_per_batch):
    ws = refs[:N_SEG_O]
    pw_ref, ps_ref = refs[N_SEG_O:N_SEG_O + 2]
    yc_ref, q_ref, kt_ref, v_ref, sgd_ref, pn_ref = refs[N_SEG_O + 2:N_SEG_O + 8]
    carry_scr, z_a, z_b = refs[N_SEG_O + 8:]
    nj = D_MODEL // TN
    s = pl.program_id(0)
    c_out = jnp.maximum(s - 1, 0)
    j_out = c_out % nj
    il_out = (c_out // nj) % tiles_per_batch
    tm = h_ref.shape[0]
    rb = tm // FRONT_PHASES
    seg_per_phase = N_SEG_O // FRONT_PHASES

    @pl.when(s == 0)
    def _():
        z_b[...] = jnp.zeros_like(z_b)
        carry_scr[...] = jnp.zeros_like(carry_scr)

    def phase(t, z_w, z_r):
        for seg in range(t * seg_per_phase, (t + 1) * seg_per_phase):
            z_w[seg] = _dot(h_ref[...], ws[seg][...])
        rows = slice(t * rb, (t + 1) * rb)
        u, gc, q, k, v, gd = (z_r[seg, rows, :] for seg in range(N_SEG_O))
        prev = carry_scr[j_out]
        if t == 0:
            prev = jnp.where(il_out == 0, 0.0, prev)
        tail = u[rb - POOL_PREV:rb, :]
        carry_scr[j_out] = tail
        if t == FRONT_PHASES - 1:
            pn_ref[...] = tail
        ext = jnp.concatenate([prev, u], axis=0)
        sums = []
        acc = ext
        for sh in (1, 2, 4, 8):
            acc = acc + pltpu.roll(acc, sh, 0)
            sums.append(acc[POOL_PREV:, :])
        win = _pick_by_chunk(j_out, sums)
        width = _pick_by_chunk(j_out, [float(w) for w in POOL_SIZES])
        pos = (il_out * tm + t * rb + lax.broadcasted_iota(jnp.int32, (rb, 1), 0)).astype(F32)
        pooled = win / jnp.minimum(width, pos + 1.0) - u
        mixed = _dot(pooled.astype(BF16), pw_ref[...]) * ps_ref[...]
        yc_ref[rows, :] = (mixed * _silu(gc)).astype(BF16)
        q_ref[rows, :] = q.astype(BF16)
        kt_ref[:, rows] = (k * (DK_D ** -0.5)).T.astype(BF16)
        v_ref[rows, :] = v.astype(BF16)
        sgd_ref[rows, :] = _silu(gd).astype(BF16)

    for parity, (z_w, z_r) in enumerate(((z_a, z_b), (z_b, z_a))):
        for t in range(FRONT_PHASES):
            pl.when(s % 2 == parity)(functools.partial(phase, t, z_w, z_r))


def _odd_front_prompt(h2d, w_in, pw, ps, batch, seq, tm):
    rows = batch * seq
    tpb = seq // tm
    nj = D_MODEL // TN
    n_chunks = (rows // tm) * nj

    def c_in(s):
        return jnp.minimum(s, n_chunks - 1)

    def out_ij(s):
        c = jnp.maximum(s - 1, 0)
        return c // nj, c % nj

    def head_map(s):
        i, j = out_ij(s)
        return (i // tpb, j, i % tpb, 0)

    def kt_map(s):
        i, j = out_ij(s)
        return (i // tpb, j, 0, i % tpb)

    def tail_map(s):
        i, j = out_ij(s)
        return (i // tpb, 0, jnp.where(i % tpb == tpb - 1, j, 0))

    head_spec = pl.BlockSpec((None, None, tm, TN), head_map)
    head_shape = jax.ShapeDtypeStruct((batch, H_D, seq, DK_D), BF16)
    tok_spec = pl.BlockSpec((tm, TN), out_ij)
    tok_shape = jax.ShapeDtypeStruct((rows, D_MODEL), BF16)
    zbuf = pltpu.VMEM((N_SEG_O, tm, TN), F32)
    return pl.pallas_call(
        functools.partial(_odd_front_kernel, tiles_per_batch=tpb),
        grid=(n_chunks + 1,),
        in_specs=[pl.BlockSpec((tm, D_MODEL), lambda s: (c_in(s) // nj, 0))] + [
                  pl.BlockSpec((D_MODEL, TN), lambda s, seg=seg: (0, seg * nj + c_in(s) % nj))
                  for seg in range(N_SEG_O)] + [
                  pl.BlockSpec((None, TN, TN), lambda s: (out_ij(s)[1], 0, 0)),
                  pl.BlockSpec((1, TN), lambda s: (0, out_ij(s)[1]))],
        out_specs=[tok_spec, head_spec, pl.BlockSpec((None, None, TN, tm), kt_map), head_spec, tok_spec,
                   pl.BlockSpec((None, POOL_PREV, TN), tail_map)],
        out_shape=[tok_shape, head_shape, jax.ShapeDtypeStruct((batch, H_D, DK_D, seq), BF16),
                   head_shape, tok_shape,
                   jax.ShapeDtypeStruct((batch, POOL_PREV, D_MODEL), F32)],
        scratch_shapes=[pltpu.VMEM((nj, POOL_PREV, TN), F32), zbuf, zbuf],
        compiler_params=_cparams(("arbitrary",)),
        name="odd_front_prompt",
    )(h2d, *([w_in] * N_SEG_O), pw, ps.reshape(1, D_MODEL))


RET_SBLK = 1024


def _ret_prompt_kernel(q_ref, kt_ref, v_ref, sgd_ref, dec_ref, cdec_ref, kdec_ref,
                       yd_ref, st_ref, state_scr, *, sblk, n_sblk):
    sb = pl.program_id(1)

    @pl.when(sb == 0)
    def _():
        state_scr[...] = jnp.zeros_like(state_scr)

    def chunk(c, carry):
        rows = pl.ds(pl.multiple_of(c * RET_CHUNK, RET_CHUNK), RET_CHUNK)
        for h in range(H_D):
            cols = slice(h * DK_D, (h + 1) * DK_D)
            q = q_ref[h, rows, :]
            kt = kt_ref[h, :, rows]
            v = v_ref[h, rows, :]
            state = state_scr[h]
            scores = _dot(q, kt) * dec_ref[h]
            o = _dot(scores.astype(BF16), v) + _dot(q, state.astype(BF16)) * cdec_ref[h]
            kd = (kt.astype(F32) * kdec_ref[h]).astype(BF16)
            state_scr[h] = float(np.exp(RET_CHUNK * RET_LOG_G[h])) * state + _dot(kd, v)
            yd_ref[rows, cols] = (_rms(o) * sgd_ref[rows, cols].astype(F32)).astype(BF16)
        return carry

    lax.fori_loop(0, sblk // RET_CHUNK, chunk, 0, unroll=4)

    @pl.when(sb == n_sblk - 1)
    def _():
        st_ref[...] = state_scr[...]


def _ret_consts():
    t = np.arange(RET_CHUNK, dtype=np.float64)
    diff = t[:, None] - t[None, :]
    lg = np.asarray(RET_LOG_G)[:, None, None]
    dec = np.where(diff >= 0, np.exp(np.maximum(diff, 0.0)[None] * lg), 0.0)
    cdec = np.broadcast_to(np.exp((t + 1.0)[None, :, None] * lg), (H_D, RET_CHUNK, DK_D))
    kdec = np.broadcast_to(np.exp((RET_CHUNK - 1.0 - t)[None, None, :] * lg), (H_D, DK_D, RET_CHUNK))
    return jnp.asarray(dec, F32), jnp.asarray(cdec, F32), jnp.asarray(kdec, F32)


def _ret_prompt(q, kt, v, sgd, batch, seq):
    dec, cdec, kdec = _ret_consts()
    n_sblk = seq // RET_SBLK
    head_spec = pl.BlockSpec((None, H_D, RET_SBLK, DK_D), lambda b, s: (b, 0, s, 0))
    tok_spec = pl.BlockSpec((None, RET_SBLK, D_MODEL), lambda b, s: (b, s, 0))

    def const_spec(shape):
        return pl.BlockSpec(shape, lambda b, s: (0, 0, 0))

    return pl.pallas_call(
        functools.partial(_ret_prompt_kernel, sblk=RET_SBLK, n_sblk=n_sblk),
        grid=(batch, n_sblk),
        in_specs=[head_spec, pl.BlockSpec((None, H_D, DK_D, RET_SBLK), lambda b, s: (b, 0, 0, s)), head_spec,
                  tok_spec, const_spec(dec.shape), const_spec(cdec.shape), const_spec(kdec.shape)],
        out_specs=[tok_spec, pl.BlockSpec((None, H_D, DK_D, DK_D), lambda b, s: (b, 0, 0, 0))],
        out_shape=[jax.ShapeDtypeStruct((batch, seq, D_MODEL), BF16),
                   jax.ShapeDtypeStruct((batch, H_D, DK_D, DK_D), F32)],
        scratch_shapes=[pltpu.VMEM((H_D, DK_D, DK_D), F32)],
        compiler_params=_cparams(("arbitrary", "arbitrary")),
        name="ret_prompt",
    )(q, kt, v, sgd.reshape(batch, seq, D_MODEL), dec, cdec, kdec)


def _even_front_sample_kernel(x_ref, mod_ref, g_ref, wbg, wcg, wxv, wga, wq, wk, wv, wgb, cw_ref, cb_ref,
                              prev_ref, ya_ref, q_ref, k_ref, v_ref, sgb_ref, cn_ref):
    h = _norm_mod(x_ref[...], g_ref[...], mod_ref[...]).astype(BF16)
    u = _dot(h, wcg[...]) * _dot(h, wxv[...])
    cw = cw_ref[...]
    conv = cb_ref[...] + cw[0:1, :] * prev_ref[0] + cw[1:2, :] * prev_ref[1] + cw[2:3, :] * u
    cn_ref[0] = prev_ref[1]
    cn_ref[1] = u
    ya_ref[...] = (_dot(h, wbg[...]) * conv * _silu(_dot(h, wga[...]))).astype(BF16)
    q_ref[...] = _dot(h, wq[...]) * (HD_B ** -0.5)
    k_ref[...] = _dot(h, wk[...])
    v_ref[...] = _dot(h, wv[...])
    sgb_ref[...] = _silu(_dot(h, wgb[...]))


def _even_front_sample(x2d, mod, g, w_in, cw, cb, prev_t):
    rows = x2d.shape[0]
    nj = D_MODEL // TN

    def wspec(s):
        return pl.BlockSpec((D_MODEL, TN), lambda j, s=s: (0, s * nj + j))

    full = pl.BlockSpec((rows, D_MODEL), lambda j: (0, 0))
    col = pl.BlockSpec((rows, TN), lambda j: (0, j))
    st = pl.BlockSpec((2, rows, TN), lambda j: (0, 0, j))
    colf = jax.ShapeDtypeStruct((rows, D_MODEL), F32)
    return pl.pallas_call(
        _even_front_sample_kernel,
        grid=(nj,),
        in_specs=[full, pl.BlockSpec((rows, 3 * D_MODEL), lambda j: (0, 0)),
                  pl.BlockSpec((1, D_MODEL), lambda j: (0, 0))] + [wspec(s) for s in range(8)] + [
            pl.BlockSpec((3, TN), lambda j: (0, j)), pl.BlockSpec((1, TN), lambda j: (0, j)), st],
        out_specs=[col, col, col, col, col, st],
        out_shape=[jax.ShapeDtypeStruct((rows, D_MODEL), BF16), colf, colf, colf, colf,
                   jax.ShapeDtypeStruct((2, rows, D_MODEL), F32)],
        compiler_params=_cparams(("arbitrary",)),
        name="even_front_sample",
    )(x2d, mod, g.reshape(1, D_MODEL), *([w_in] * 8), cw, cb.reshape(1, D_MODEL), prev_t)


def _attn_sample_kernel(slope_ref, pos_ref, q_ref, kn_ref, vn_ref, sgb_ref, kt_ref, vt_ref, yb_ref):
    gw = q_ref.shape[-1]
    row = lax.broadcasted_iota(jnp.int32, (8, gw), 0)
    col = lax.broadcasted_iota(jnp.int32, (8, gw), 1)
    own = (col >= row * HD_B) & (col < (row + 1) * HD_B)
    qm = jnp.where(own, q_ref[...], 0.0)
    s_self = jnp.sum(qm * kn_ref[...], axis=-1, keepdims=True)
    s_all = _dot(qm.astype(BF16), kt_ref[...].astype(BF16))
    s_all = s_all - slope_ref[:, 0:1] * pos_ref[0:1, :]
    v_self = vn_ref[...]
    probs, stats = [], []
    for p in range(len(DILATIONS)):
        s = s_all - pos_ref[p + 1:p + 2, :]
        m = jnp.maximum(jnp.max(s, axis=-1, keepdims=True), s_self)
        pe = jnp.exp(s - m)
        p_self = jnp.exp(s_self - m)
        probs.append(pe)
        stats.append((m, jnp.sum(pe, axis=-1, keepdims=True) + p_self, p_self))
    o_all = _dot_nt(jnp.concatenate(probs, axis=0).astype(BF16), vt_ref[...].astype(BF16))
    outs, lses = [], []
    for p, (m, l, p_self) in enumerate(stats):
        outs.append((o_all[8 * p:8 * p + 8, :] + p_self * v_self) * (1.0 / l))
        lses.append(m + jnp.log(l))
    m = jnp.maximum(jnp.maximum(lses[0], lses[1]), lses[2])
    ws = [jnp.exp(x - m) for x in lses]
    o = (ws[0] * outs[0] + ws[1] * outs[1] + ws[2] * outs[2]) * (1.0 / (ws[0] + ws[1] + ws[2]))
    o = jnp.sum(jnp.where(own, o, 0.0), axis=0, keepdims=True)
    yb_ref[...] = (o * sgb_ref[...]).astype(BF16)


def _attn_sample(q, kn, vn, sgb, cache_kt, cache_vt):
    rows, _, wb = cache_kt.shape
    gh = 4
    gw = gh * HD_B
    ng = H_B // gh
    back = wb - np.arange(wb, dtype=np.float64)
    pos = [back] + [np.where((back % d == 0) & (back <= NK * d), 0.0, MASK_ADD) for d in DILATIONS]
    slopes = (2.0 ** (-8.0 * np.arange(1, H_B + 1) / H_B)).reshape(ng, gh)
    slopes = np.concatenate([slopes, np.zeros((ng, 8 - gh))], axis=1)
    slope_arr = jnp.asarray(np.broadcast_to(slopes[:, :, None], (ng, 8, LANES)), F32)

    def r3(a):
        return a.reshape(rows, 1, D_MODEL)

    row_spec = pl.BlockSpec((None, 1, gw), lambda b, g: (b, 0, g))
    t_spec = pl.BlockSpec((None, gw, wb), lambda b, g: (b, g, 0))
    return pl.pallas_call(
        _attn_sample_kernel,
        grid=(rows, ng),
        in_specs=[pl.BlockSpec((None, 8, LANES), lambda b, g: (g, 0, 0)),
                  pl.BlockSpec((4, wb), lambda b, g: (0, 0)),
                  row_spec, row_spec, row_spec, row_spec, t_spec, t_spec],
        out_specs=row_spec,
        out_shape=jax.ShapeDtypeStruct((rows, 1, D_MODEL), BF16),
        compiler_params=_cparams(("arbitrary", "arbitrary")),
        name="attn_sample",
    )(slope_arr, jnp.asarray(np.stack(pos), F32), r3(q), r3(kn), r3(vn), r3(sgb), cache_kt, cache_vt
      ).reshape(rows, D_MODEL)


def _odd_front_sample_kernel(h_ref, wu, wgc, wq, wk, wv, wgd, pw_ref, ps_ref, prev_ref,
                             yc_ref, q_ref, k_ref, v_ref, sgd_ref, pn_ref):
    j = pl.program_id(0)
    h = h_ref[...]
    u = _dot(h, wu[...])
    n_prev = prev_ref.shape[0]
    sums = []
    s = jnp.zeros_like(u)
    back = 0
    for w in POOL_SIZES:
        while back < w - 1:
            s = s + prev_ref[n_prev - 1 - back]
            back += 1
        sums.append(s)
    win = _pick_by_chunk(j, sums) + u
    inv_w = _pick_by_chunk(j, [1.0 / w for w in POOL_SIZES])
    pooled = win * inv_w - u
    mixed = _dot(pooled.astype(BF16), pw_ref[...]) * ps_ref[...]
    yc_ref[...] = (mixed * _silu(_dot(h, wgc[...]))).astype(BF16)
    q_ref[...] = _dot(h, wq[...])
    k_ref[...] = _dot(h, wk[...]) * (DK_D ** -0.5)
    v_ref[...] = _dot(h, wv[...])
    sgd_ref[...] = _silu(_dot(h, wgd[...]))
    for t in range(n_prev - 1):
        pn_ref[t] = prev_ref[t + 1]
    pn_ref[n_prev - 1] = u


def _odd_front_sample(h2d, w_in, pw, ps, prev_t):
    rows = h2d.shape[0]
    n_prev = prev_t.shape[0]
    nj = D_MODEL // TN

    def wspec(s):
        return pl.BlockSpec((D_MODEL, TN), lambda j, s=s: (0, s * nj + j))

    col = pl.BlockSpec((rows, TN), lambda j: (0, j))
    st = pl.BlockSpec((n_prev, rows, TN), lambda j: (0, 0, j))
    colf = jax.ShapeDtypeStruct((rows, D_MODEL), F32)
    return pl.pallas_call(
        _odd_front_sample_kernel,
        grid=(nj,),
        in_specs=[pl.BlockSpec((rows, D_MODEL), lambda j: (0, 0))] + [wspec(s) for s in range(6)] + [
            pl.BlockSpec((None, TN, TN), lambda j: (j, 0, 0)), pl.BlockSpec((1, TN), lambda j: (0, j)), st],
        out_specs=[col, col, col, col, col, st],
        out_shape=[jax.ShapeDtypeStruct((rows, D_MODEL), BF16), colf, colf, colf, colf,
                   jax.ShapeDtypeStruct((n_prev, rows, D_MODEL), F32)],
        compiler_params=_cparams(("arbitrary",)),
        name="odd_front_sample",
    )(h2d, *([w_in] * 6), pw, ps.reshape(1, D_MODEL), prev_t)


def _ret_sample_kernel(q_ref, k_ref, v_ref, sgd_ref, st_ref, yd_ref, sn_ref):
    row = lax.broadcasted_iota(jnp.int32, (DK_D, DK_D), 0)
    col = lax.broadcasted_iota(jnp.int32, (DK_D, DK_D), 1)
    for h in range(H_D):
        g = float(np.exp(RET_LOG_G[h]))
        cols = slice(h * DK_D, (h + 1) * DK_D)
        q = q_ref[:, cols]
        k = k_ref[:, cols]
        v = v_ref[:, cols]
        state = st_ref[h]
        qk = jnp.sum(q * k, axis=-1, keepdims=True)
        q8 = jnp.broadcast_to(q, (8, DK_D)).astype(BF16)
        cross = _dot(q8, state.astype(BF16))[0:1, :]
        o = qk * v + g * cross
        k_diag = jnp.where(row == col, jnp.broadcast_to(k, (DK_D, DK_D)), 0.0).astype(BF16)
        v_rows = jnp.broadcast_to(v, (DK_D, DK_D)).astype(BF16)
        sn_ref[h] = g * state + _dot(k_diag, v_rows)
        yd_ref[:, cols] = (_rms(o) * sgd_ref[:, cols]).astype(BF16)


def _ret_sample(q, k, v, sgd, state):
    rows = q.shape[0]

    def r3(a):
        return a.reshape(rows, 1, D_MODEL)

    row_spec = pl.BlockSpec((None, 1, D_MODEL), lambda b: (b, 0, 0))
    st_spec = pl.BlockSpec((None, H_D, DK_D, DK_D), lambda b: (b, 0, 0, 0))
    yd, sn = pl.pallas_call(
        _ret_sample_kernel,
        grid=(rows,),
        in_specs=[row_spec, row_spec, row_spec, row_spec, st_spec],
        out_specs=[row_spec, st_spec],
        out_shape=[jax.ShapeDtypeStruct((rows, 1, D_MODEL), BF16),
                   jax.ShapeDtypeStruct((rows, H_D, DK_D, DK_D), F32)],
        compiler_params=_cparams(("arbitrary",)),
        name="ret_sample",
    )(r3(q), r3(k), r3(v), r3(sgd), state)
    return yd.reshape(rows, D_MODEL), sn


def kernel(x_prompt, x_sample, c_prompt, c_sample, state_conv, cache_win_k, cache_win_v, state_pool, state_ret,
           norm_e, ada_w_e, ada_b_e, w_in_e, conv_w, conv_b, w_out_e, norm_o, ada_w_o, ada_b_o, w_in_o,
           pool_w, pool_scale, w_out_o, norm_f):
    batch, seq, d = x_prompt.shape
    sb = x_sample.shape[0]
    assert d == D_MODEL and x_sample.shape[1] == 1
    assert norm_e.shape[0] == 1 and norm_o.shape[0] == 1
    wb = cache_win_k.shape[2]
    keep = min(wb, seq)
    assert wb == DILATIONS[-1] * NK and seq % (DILATIONS[-1] * NK) == 0
    tm = 512

    n_c = batch + sb
    pad = (-n_c) % 8
    c_all = jnp.concatenate([c_prompt, c_sample, jnp.zeros((pad, d), F32)], axis=0)
    mod_e, mod_o = _adaln(c_all, ada_w_e[0], ada_b_e[0], ada_w_o[0], ada_b_o[0])
    mod_e_p, mod_e_s = mod_e[:batch], mod_e[batch:n_c]
    mod_o_p, mod_o_s = mod_o[:batch], mod_o[batch:n_c]

    w_in_e16 = w_in_e[0].astype(BF16)
    w_out_e16 = w_out_e[0].astype(BF16)
    w_in_o16 = w_in_o[0].astype(BF16)
    w_out_o16 = w_out_o[0].astype(BF16)
    pool_w16 = pool_w[0].astype(BF16)

    xp2d = x_prompt.reshape(batch * seq, d)
    ya, q, k, v, sgb, k_new, v_new, conv_tail = _even_front_prompt(
        xp2d, mod_e_p, norm_e[0], w_in_e16, conv_w[0], conv_b[0], batch, seq, keep, tm)
    yb = _attn_prompt(q, k, v, sgb, batch, seq).reshape(batch * seq, d)
    x1, h1 = _outproj(ya, yb, xp2d, mod_e_p, w_out_e16, norm_o[0], mod_o_p, tm, seq)
    yc, rq, rk, rv, sgd, pool_tail = _odd_front_prompt(h1, w_in_o16, pool_w16, pool_scale[0], batch, seq, tm)
    yd, ret_p = _ret_prompt(rq, rk, rv, sgd, batch, seq)
    y_prompt = _outproj(yc, yd.reshape(batch * seq, d), x1, mod_o_p, w_out_o16, norm_f, None, tm, seq)

    xs2d = x_sample.reshape(sb, d)
    conv_prev_t = jnp.transpose(state_conv[0], (1, 0, 2))
    ya_s, q_s, k_s, v_s, sgb_s, conv_s_t = _even_front_sample(
        xs2d, mod_e_s, norm_e[0], w_in_e16, conv_w[0], conv_b[0], conv_prev_t)
    cache_kt = jnp.transpose(cache_win_k[0], (0, 2, 3, 1)).reshape(sb, d, wb)
    cache_vt = jnp.transpose(cache_win_v[0], (0, 2, 3, 1)).reshape(sb, d, wb)
    yb_s = _attn_sample(q_s, k_s, v_s, sgb_s, cache_kt, cache_vt)
    x1_s, h1_s = _outproj(ya_s, yb_s, xs2d, mod_e_s, w_out_e16, norm_o[0], mod_o_s, sb, 1)
    pool_prev_t = jnp.transpose(state_pool[0], (1, 0, 2))
    yc_s, rq_s, rk_s, rv_s, sgd_s, pool_s_t = _odd_front_sample(h1_s, w_in_o16, pool_w16, pool_scale[0], pool_prev_t)
    yd_s, ret_s = _ret_sample(rq_s, rk_s, rv_s, sgd_s, state_ret[0])
    y_sample = _outproj(yc_s, yd_s, x1_s, mod_o_s, w_out_o16, norm_f, None, sb, 1)

    return (
        y_prompt.reshape(batch, seq, d),
        y_sample.reshape(sb, 1, d),
        conv_tail[:, 6:8][None],
        jnp.transpose(conv_s_t, (1, 0, 2))[None],
        jnp.transpose(k_new.reshape(batch, H_B, HD_B, keep), (0, 3, 1, 2))[None],
        k_s.reshape(1, sb, 1, H_B, HD_B),
        jnp.transpose(v_new.reshape(batch, H_B, HD_B, keep), (0, 3, 1, 2))[None],
        v_s.reshape(1, sb, 1, H_B, HD_B),
        pool_tail[:, 1:][None],
        jnp.transpose(pool_s_t, (1, 0, 2))[None],
        ret_p[None],
        ret_s[None],
    )
```

```python
import functools

import numpy as np
import jax
import jax.numpy as jnp
from jax import lax
from jax.experimental import pallas as pl
from jax.experimental.pallas import tpu as pltpu

F32 = jnp.float32
BF16 = jnp.bfloat16
WORD = jnp.uint32

D_MODEL = 1024
EPS = 1e-6
H_B = 16
HD_B = 64
N_PAIR = H_B // 2
LANES = 128
NK = 128
DILATIONS = (1, 4, 16)
POOL_SIZES = (2, 4, 8, 16)
POOL_PREV = 16
H_D = 4
DK_D = 256
RET_CHUNK = 128
TN = 256
MASK_DIST = 1e9
LOG2E = float(np.log2(np.e))
MASK_ADD = 1e30
VMEM_LIMIT = 56 * 1024 * 1024

RET_LOG_G = [float(np.log(1.0 - 2.0 ** (-5.0 - h))) for h in range(H_D)]


def _cparams(sem):
    return pltpu.CompilerParams(dimension_semantics=sem, vmem_limit_bytes=VMEM_LIMIT)


def _silu(x):
    return x * (1.0 / (1.0 + jnp.exp(-x)))


def _dot(a, b):
    return jnp.dot(a, b, preferred_element_type=F32)


def _dot_nt(a, b):
    return lax.dot_general(a, b, (((1,), (1,)), ((), ())), preferred_element_type=F32)


def _rms(x):
    return x * lax.rsqrt(jnp.mean(x * x, axis=-1, keepdims=True) + EPS)


def _adaln_kernel(c_ref, we_ref, be_ref, wo_ref, bo_ref, me_ref, mo_ref):
    sc = _silu(c_ref[...]).astype(BF16)
    me_ref[...] = _dot(sc, we_ref[...].astype(BF16)) + be_ref[...]
    mo_ref[...] = _dot(sc, wo_ref[...].astype(BF16)) + bo_ref[...]


def _adaln(c_all, we, be, wo, bo):
    rows = c_all.shape[0]
    tn = 512
    n = 3 * D_MODEL
    wspec = pl.BlockSpec((D_MODEL, tn), lambda j: (0, j))
    bspec = pl.BlockSpec((1, tn), lambda j: (0, j))
    ospec = pl.BlockSpec((rows, tn), lambda j: (0, j))
    return pl.pallas_call(
        _adaln_kernel,
        grid=(n // tn,),
        in_specs=[pl.BlockSpec((rows, D_MODEL), lambda j: (0, 0)), wspec, bspec, wspec, bspec],
        out_specs=[ospec, ospec],
        out_shape=[jax.ShapeDtypeStruct((rows, n), F32)] * 2,
        compiler_params=_cparams(("arbitrary",)),
        name="adaln",
    )(c_all, we, be.reshape(1, n), wo, bo.reshape(1, n))


def _norm_mod(x, g, mod):
    shift = mod[:, 0:D_MODEL]
    scale = mod[:, D_MODEL:2 * D_MODEL]
    return _rms(x) * g * (1.0 + scale) + shift


def _shift_rows(u, k, prev_rows):
    row = lax.broadcasted_iota(jnp.int32, u.shape, 0)
    out = pltpu.roll(u, k, 0)
    for idx, pr in enumerate(prev_rows):
        out = jnp.where(row == idx, pr, out)
    return out


N_SEG_E = 8
FRONT_PHASES = 2


def _even_front_kernel(x_ref, mod_ref, g_ref, *refs, tiles_per_batch, n_chunks):
    ws = refs[:N_SEG_E]
    cw_ref, cb_ref = refs[N_SEG_E:N_SEG_E + 2]
    ya_ref, q_ref, k_ref, v_ref, sgb_ref, kn_ref, vn_ref, cn_ref = refs[N_SEG_E + 2:N_SEG_E + 10]
    h_scr, carry_scr, z_a, z_b = refs[N_SEG_E + 10:]
    nj = D_MODEL // TN
    s = pl.program_id(0)
    c_out = jnp.maximum(s - 1, 0)
    j_out = c_out % nj
    il_out = (c_out // nj) % tiles_per_batch

    @pl.when(s == 0)
    def _():
        z_b[...] = jnp.zeros_like(z_b)
        carry_scr[...] = jnp.zeros_like(carry_scr)

    @pl.when((s < n_chunks) & (s % nj == 0))
    def _():
        h_scr[...] = _norm_mod(x_ref[...], g_ref[...], mod_ref[...]).astype(BF16)

    tm = h_scr.shape[0]
    rb = tm // FRONT_PHASES
    seg_per_phase = N_SEG_E // FRONT_PHASES

    def phase(t, z_w, z_r):
        for seg in range(t * seg_per_phase, (t + 1) * seg_per_phase):
            z_w[seg] = _dot(h_scr[...], ws[seg][...])
        rows = slice(t * rb, (t + 1) * rb)
        bg, cg, xv, ga, q, k, v, gb = (z_r[seg, rows, :] for seg in range(N_SEG_E))
        u = cg * xv
        prev = carry_scr[j_out]
        if t == 0:
            prev = jnp.where(il_out == 0, 0.0, prev)
        p2, p1 = prev[6:7, :], prev[7:8, :]
        u1 = _shift_rows(u, 1, [p1])
        u2 = _shift_rows(u, 2, [p2, p1])
        cw = cw_ref[...]
        conv = cb_ref[...] + cw[0:1, :] * u2 + cw[1:2, :] * u1 + cw[2:3, :] * u
        tail = u[rb - 8:rb, :]
        carry_scr[j_out] = tail
        if t == FRONT_PHASES - 1:
            cn_ref[...] = tail
        ya_ref[rows, :] = (bg * conv * _silu(ga)).astype(BF16)
        kn_ref[:, rows] = k.T
        vn_ref[:, rows] = v.T
        wrows = slice(t * rb // 2, (t + 1) * rb // 2)
        for ref, val in ((q_ref, q * (HD_B ** -0.5 * LOG2E)),
                         (k_ref, k), (v_ref, v)):
            words = pltpu.bitcast(val.astype(BF16), WORD)
            for e in range(TN // LANES):
                ref[e, wrows, :] = words[:, e * LANES:(e + 1) * LANES]
        sgb = _silu(gb).astype(BF16)
        for e in range(TN // LANES):
            sgb_ref[e, rows, :] = sgb[:, e * LANES:(e + 1) * LANES]

    for parity, (z_w, z_r) in enumerate(((z_a, z_b), (z_b, z_a))):
        for t in range(FRONT_PHASES):
            pl.when(s % 2 == parity)(functools.partial(phase, t, z_w, z_r))


def _even_front_prompt(x2d, mod, g, w_in, cw, cb, batch, seq, keep, tm):
    rows = batch * seq
    tpb = seq // tm
    nj = D_MODEL // TN
    n_chunks = (rows // tm) * nj
    off = (seq - keep) // tm
    ppc = TN // LANES

    def c_in(s):
        return jnp.minimum(s, n_chunks - 1)

    def c_out(s):
        return jnp.maximum(s - 1, 0)

    def out_ij(s):
        c = c_out(s)
        return c // nj, c % nj

    def pair_map(s):
        i, j = out_ij(s)
        return (i // tpb, j, i % tpb, 0)

    def keep_map(s):
        i, j = out_ij(s)
        il = i % tpb
        kept = il >= off
        return (i // tpb, jnp.where(kept, j, 0), jnp.where(kept, il - off, 0))

    def tail_map(s):
        i, j = out_ij(s)
        return (i // tpb, 0, jnp.where(i % tpb == tpb - 1, j, 0))

    pair_spec = pl.BlockSpec((None, ppc, tm, LANES), pair_map)
    pair_shape = jax.ShapeDtypeStruct((batch, N_PAIR, seq, LANES), BF16)
    word_spec = pl.BlockSpec((None, ppc, tm // 2, LANES), pair_map)
    word_shape = jax.ShapeDtypeStruct((batch, N_PAIR, seq // 2, LANES), WORD)
    keep_spec = pl.BlockSpec((None, TN, tm), keep_map)
    keep_shape = jax.ShapeDtypeStruct((batch, D_MODEL, keep), F32)
    return pl.pallas_call(
        functools.partial(_even_front_kernel, tiles_per_batch=tpb, n_chunks=n_chunks),
        grid=(n_chunks + 1,),
        in_specs=[
            pl.BlockSpec((tm, D_MODEL), lambda s: (c_in(s) // nj, 0)),
            pl.BlockSpec((None, 1, 3 * D_MODEL), lambda s: (c_in(s) // nj // tpb, 0, 0)),
            pl.BlockSpec((1, D_MODEL), lambda s: (0, 0)),
        ] + [pl.BlockSpec((D_MODEL, TN), lambda s, seg=seg: (0, seg * nj + c_in(s) % nj)) for seg in range(N_SEG_E)] + [
            pl.BlockSpec((3, TN), lambda s: (0, c_out(s) % nj)),
            pl.BlockSpec((1, TN), lambda s: (0, c_out(s) % nj)),
        ],
        out_specs=[
            pl.BlockSpec((tm, TN), lambda s: out_ij(s)),
            word_spec, word_spec, word_spec, pair_spec,
            keep_spec, keep_spec,
            pl.BlockSpec((None, 8, TN), tail_map),
        ],
        out_shape=[
            jax.ShapeDtypeStruct((rows, D_MODEL), BF16),
            word_shape, word_shape, word_shape, pair_shape,
            keep_shape, keep_shape,
            jax.ShapeDtypeStruct((batch, 8, D_MODEL), F32),
        ],
        scratch_shapes=[pltpu.VMEM((tm, D_MODEL), BF16), pltpu.VMEM((nj, 8, TN), F32),
                        pltpu.VMEM((N_SEG_E, tm, TN), F32), pltpu.VMEM((N_SEG_E, tm, TN), F32)],
        compiler_params=_cparams(("arbitrary",)),
        name="even_front_prompt",
    )(x2d, mod.reshape(batch, 1, 3 * D_MODEL), g.reshape(1, D_MODEL), *([w_in] * N_SEG_E), cw, cb.reshape(1, D_MODEL))


ATTN_UNROLL = 16


def _residue_stream(src, r):
    n_out = src.shape[0] // 4
    halves = [pltpu.unpack_elementwise(src[pl.ds(r // 2 + off, n_out, stride=4), :], index=r % 2,
                                       packed_dtype=BF16, unpacked_dtype=F32) for off in (0, 2)]
    return pltpu.pack_elementwise(halves, packed_dtype=BF16)


def _attn_prompt_kernel(slope_ref, dist_ref, q_ref, k_ref, v_ref, sgb_ref, yb_ref,
                        q4_scr, q16_scr, kv4_scr, kv16_scr, acc_scr, m_scr, l_scr, bias_scr, *, seq):
    hp = pl.program_id(1)
    lane = lax.broadcasted_iota(jnp.int32, (1, LANES), 1)
    half0 = lane < HD_B
    dist = dist_ref[...]
    masked = jnp.full((NK, NK), MASK_DIST, F32)
    for p, d in enumerate(DILATIONS):
        for e in range(2):
            bias = dist * (slope_ref[2 * hp + e] * (float(d) * LOG2E))
            bias_scr[2 * p + e] = bias
            bias_scr[6 + 2 * p + e] = jnp.concatenate([bias[:, NK:], masked], axis=1)

    for a, ref in enumerate((q_ref, k_ref, v_ref)):
        for r in range(4):
            s = _residue_stream(ref, r)
            if a == 0:
                q4_scr[r] = s
            else:
                kv4_scr[a - 1, r] = s
        for r in range(16):
            s = _residue_stream(q4_scr.at[r % 4] if a == 0 else kv4_scr.at[a - 1, r % 4], r // 4)
            if a == 0:
                q16_scr[r] = s
            else:
                kv16_scr[a - 1, r] = s

    ones = jnp.ones((2 * NK, LANES), BF16)
    hw = NK // 2

    def tile(p, d, r, n):
        qw = pl.ds(pl.multiple_of(n * hw, hw), hw)
        kw = pl.ds(pl.multiple_of(jnp.maximum(n - 1, 0) * hw, hw), 2 * hw)
        if d == 1:
            qs, ks, vs = q_ref, k_ref, v_ref
        elif d == 4:
            qs, ks, vs = q4_scr.at[r], kv4_scr.at[0, r], kv4_scr.at[1, r]
        else:
            qs, ks, vs = q16_scr.at[r], kv16_scr.at[0, r], kv16_scr.at[1, r]
        q = pltpu.bitcast(qs[qw, :], BF16)
        kk = pltpu.bitcast(ks[kw, :], BF16)
        vv1 = jnp.concatenate([pltpu.bitcast(vs[kw, :], BF16), ones], axis=1)
        first = jnp.where(n == 0, 1, 0)
        res = []
        for e in range(2):
            qe = jnp.where(half0 if e == 0 else jnp.logical_not(half0), q, jnp.zeros_like(q))
            s = _dot_nt(qe, kk) - bias_scr[6 * first + 2 * p + e]
            m = jnp.max(s, axis=-1, keepdims=True)
            pe = jnp.exp2(s - m).astype(BF16)
            res.append((_dot(pe, vv1), m))
        (a0, m0), (a1, m1) = res
        rows = pl.ds(r + d * NK * n, NK, stride=d) if d > 1 else pl.ds(pl.multiple_of(n * NK, NK), NK)
        acc_scr[p, rows, :] = jnp.where(half0, a0[:, :LANES], a1[:, :LANES])
        l_scr[p, rows, :] = jnp.where(half0, a0[:, LANES:], a1[:, LANES:])
        m_scr[p, rows, :] = jnp.where(half0, m0, m1)

    for p, d in enumerate(DILATIONS):
        def body(it, c, p=p, d=d):
            for u in range(ATTN_UNROLL):
                t = it * ATTN_UNROLL + u
                if d == 1:
                    tile(p, d, 0, t)
                else:
                    tile(p, d, t % d, t // d)
            return c

        lax.fori_loop(0, seq // NK // ATTN_UNROLL, body, 0)

    rc = 512
    for c in range(seq // rc):
        rows = pl.ds(c * rc, rc)
        ms = [m_scr[p, rows, :] for p in range(3)]
        m = jnp.maximum(jnp.maximum(ms[0], ms[1]), ms[2])
        ws = [jnp.exp2(x - m) for x in ms]
        num = ws[0] * acc_scr[0, rows, :] + ws[1] * acc_scr[1, rows, :] + ws[2] * acc_scr[2, rows, :]
        den = ws[0] * l_scr[0, rows, :] + ws[1] * l_scr[1, rows, :] + ws[2] * l_scr[2, rows, :]
        yb_ref[rows, :] = (num * (1.0 / den) * sgb_ref[rows, :].astype(F32)).astype(BF16)


def _band_distance():
    qi = np.arange(NK)[:, None]
    kj = np.arange(2 * NK)[None, :]
    dist = NK + qi - kj
    return jnp.asarray(np.where((dist >= 0) & (dist <= NK), dist, MASK_DIST), dtype=F32)


def _alibi_slopes():
    return jnp.asarray(2.0 ** (-8.0 * np.arange(1, H_B + 1) / H_B), dtype=F32)


def _attn_prompt(q, k, v, sgb, batch, seq):
    word_spec = pl.BlockSpec((None, None, seq // 2, LANES), lambda b, hp: (b, hp, 0, 0))
    stat = pltpu.VMEM((len(DILATIONS), seq, LANES), F32)
    return pl.pallas_call(
        functools.partial(_attn_prompt_kernel, seq=seq),
        grid=(batch, N_PAIR),
        in_specs=[pl.BlockSpec(memory_space=pltpu.SMEM),
                  pl.BlockSpec((NK, 2 * NK), lambda b, hp: (0, 0)),
                  word_spec, word_spec, word_spec,
                  pl.BlockSpec((None, None, seq, LANES), lambda b, hp: (b, hp, 0, 0))],
        out_specs=pl.BlockSpec((None, seq, LANES), lambda b, hp: (b, 0, hp)),
        out_shape=jax.ShapeDtypeStruct((batch, seq, D_MODEL), BF16),
        scratch_shapes=[pltpu.VMEM((4, seq // 8, LANES), WORD),
                        pltpu.VMEM((16, seq // 32, LANES), WORD),
                        pltpu.VMEM((2, 4, seq // 8, LANES), WORD),
                        pltpu.VMEM((2, 16, seq // 32, LANES), WORD),
                        stat, stat, stat,
                        pltpu.VMEM((12, NK, 2 * NK), F32)],
        compiler_params=_cparams(("arbitrary", "arbitrary")),
        name="attn_prompt",
    )(_alibi_slopes(), _band_distance(), q, k, v, sgb)


def _outproj_mid_kernel(ya_ref, yb_ref, x_ref, mod_ref, w1_ref, w2_ref, g_ref, mod2_ref, x1_ref, h_ref):
    gate = mod_ref[:, 2 * D_MODEL:3 * D_MODEL]
    x1 = x_ref[...] + gate * (_dot(ya_ref[...], w1_ref[...]) + _dot(yb_ref[...], w2_ref[...]))
    x1_ref[...] = x1
    h_ref[...] = _norm_mod(x1, g_ref[...], mod2_ref[...]).astype(BF16)


def _outproj_final_kernel(ya_ref, yb_ref, x_ref, mod_ref, w1_ref, w2_ref, g_ref, y_ref):
    gate = mod_ref[:, 2 * D_MODEL:3 * D_MODEL]
    x1 = x_ref[...] + gate * (_dot(ya_ref[...], w1_ref[...]) + _dot(yb_ref[...], w2_ref[...]))
    y_ref[...] = _rms(x1) * g_ref[...]


def _outproj(ya, yb, x2d, mod, w_out, g, mod2, tm, rows_per_mod):
    rows = x2d.shape[0]
    half = w_out.shape[0] // 2
    row_spec = pl.BlockSpec((tm, D_MODEL), lambda i: (i, 0))
    if rows_per_mod == 1:
        mod_spec = pl.BlockSpec((tm, 3 * D_MODEL), lambda i: (i, 0))
        mods = (mod, mod2)
    else:
        tpb = rows_per_mod // tm
        mod_spec = pl.BlockSpec((None, 1, 3 * D_MODEL), lambda i: (i // tpb, 0, 0))
        mods = tuple(None if m is None else m.reshape(-1, 1, 3 * D_MODEL) for m in (mod, mod2))
    w1_spec = pl.BlockSpec((half, D_MODEL), lambda i: (0, 0))
    w2_spec = pl.BlockSpec((half, D_MODEL), lambda i: (1, 0))
    g_spec = pl.BlockSpec((1, D_MODEL), lambda i: (0, 0))
    common = dict(grid=(rows // tm,), compiler_params=_cparams(("arbitrary",)))
    if mod2 is None:
        return pl.pallas_call(
            _outproj_final_kernel,
            in_specs=[row_spec, row_spec, row_spec, mod_spec, w1_spec, w2_spec, g_spec],
            out_specs=row_spec,
            out_shape=jax.ShapeDtypeStruct((rows, D_MODEL), F32),
            name="outproj_final", **common,
        )(ya, yb, x2d, mods[0], w_out, w_out, g.reshape(1, D_MODEL))
    return pl.pallas_call(
        _outproj_mid_kernel,
        in_specs=[row_spec, row_spec, row_spec, mod_spec, w1_spec, w2_spec, g_spec, mod_spec],
        out_specs=[row_spec, row_spec],
        out_shape=[jax.ShapeDtypeStruct((rows, D_MODEL), F32), jax.ShapeDtypeStruct((rows, D_MODEL), BF16)],
        name="outproj_mid", **common,
    )(ya, yb, x2d, mods[0], w_out, w_out, g.reshape(1, D_MODEL), mods[1])


def _pick_by_chunk(j, vals):
    out = vals[-1]
    for idx in range(len(vals) - 2, -1, -1):
        out = jnp.where(j == idx, vals[idx], out)
    return out


N_SEG_O = 6


def _odd_front_kernel(h_ref, *refs, tiles_per_batch):
    ws = refs[:N_SEG_O]
    pw_ref, ps_ref = refs[N_SEG_O:N_SEG_O + 2]
    yc_ref, q_ref, kt_ref, v_ref, sgd_ref, pn_ref = refs[N_SEG_O + 2:N_SEG_O + 8]
    carry_scr, z_a, z_b = refs[N_SEG_O + 8:]
    nj = D_MODEL // TN
    s = pl.program_id(0)
    c_out = jnp.maximum(s - 1, 0)
    j_out = c_out % nj
    il_out = (c_out // nj) % tiles_per_batch
    tm = h_ref.shape[0]
    rb = tm // FRONT_PHASES
    seg_per_phase = N_SEG_O // FRONT_PHASES

    @pl.when(s == 0)
    def _():
        z_b[...] = jnp.zeros_like(z_b)
        carry_scr[...] = jnp.zeros_like(carry_scr)

    def phase(t, z_w, z_r):
        for seg in range(t * seg_per_phase, (t + 1) * seg_per_phase):
            z_w[seg] = _dot(h_ref[...], ws[seg][...])
        rows = slice(t * rb, (t + 1) * rb)
        u, gc, q, k, v, gd = (z_r[seg, rows, :] for seg in range(N_SEG_O))
        prev = carry_scr[j_out]
        if t == 0:
            prev = jnp.where(il_out == 0, 0.0, prev)
        tail = u[rb - POOL_PREV:rb, :]
        carry_scr[j_out] = tail
        if t == FRONT_PHASES - 1:
            pn_ref[...] = tail
        ext = jnp.concatenate([prev, u], axis=0)
        sums = []
        acc = ext
        for sh in (1, 2, 4, 8):
            acc = acc + pltpu.roll(acc, sh, 0)
            sums.append(acc[POOL_PREV:, :])
        win = _pick_by_chunk(j_out, sums)
        width = _pick_by_chunk(j_out, [float(w) for w in POOL_SIZES])
        pos = (il_out * tm + t * rb + lax.broadcasted_iota(jnp.int32, (rb, 1), 0)).astype(F32)
        pooled = win / jnp.minimum(width, pos + 1.0) - u
        mixed = _dot(pooled.astype(BF16), pw_ref[...]) * ps_ref[...]
        yc_ref[rows, :] = (mixed * _silu(gc)).astype(BF16)
        q_ref[rows, :] = q.astype(BF16)
        kt_ref[:, rows] = (k * (DK_D ** -0.5)).T.astype(BF16)
        v_ref[rows, :] = v.astype(BF16)
        sgd_ref[rows, :] = _silu(gd).astype(BF16)

    for parity, (z_w, z_r) in enumerate(((z_a, z_b), (z_b, z_a))):
        for t in range(FRONT_PHASES):
            pl.when(s % 2 == parity)(functools.partial(phase, t, z_w, z_r))


def _odd_front_prompt(h2d, w_in, pw, ps, batch, seq, tm):
    rows = batch * seq
    tpb = seq // tm
    nj = D_MODEL // TN
    n_chunks = (rows // tm) * nj

    def c_in(s):
        return jnp.minimum(s, n_chunks - 1)

    def out_ij(s):
        c = jnp.maximum(s - 1, 0)
        return c // nj, c % nj

    def head_map(s):
        i, j = out_ij(s)
        return (i // tpb, j, i % tpb, 0)

    def kt_map(s):
        i, j = out_ij(s)
        return (i // tpb, j, 0, i % tpb)

    def tail_map(s):
        i, j = out_ij(s)
        return (i // tpb, 0, jnp.where(i % tpb == tpb - 1, j, 0))

    head_spec = pl.BlockSpec((None, None, tm, TN), head_map)
    head_shape = jax.ShapeDtypeStruct((batch, H_D, seq, DK_D), BF16)
    tok_spec = pl.BlockSpec((tm, TN), out_ij)
    tok_shape = jax.ShapeDtypeStruct((rows, D_MODEL), BF16)
    zbuf = pltpu.VMEM((N_SEG_O, tm, TN), F32)
    return pl.pallas_call(
        functools.partial(_odd_front_kernel, tiles_per_batch=tpb),
        grid=(n_chunks + 1,),
        in_specs=[pl.BlockSpec((tm, D_MODEL), lambda s: (c_in(s) // nj, 0))] + [
                  pl.BlockSpec((D_MODEL, TN), lambda s, seg=seg: (0, seg * nj + c_in(s) % nj))
                  for seg in range(N_SEG_O)] + [
                  pl.BlockSpec((None, TN, TN), lambda s: (out_ij(s)[1], 0, 0)),
                  pl.BlockSpec((1, TN), lambda s: (0, out_ij(s)[1]))],
        out_specs=[tok_spec, head_spec, pl.BlockSpec((None, None, TN, tm), kt_map), head_spec, tok_spec,
                   pl.BlockSpec((None, POOL_PREV, TN), tail_map)],
        out_shape=[tok_shape, head_shape, jax.ShapeDtypeStruct((batch, H_D, DK_D, seq), BF16),
                   head_shape, tok_shape,
                   jax.ShapeDtypeStruct((batch, POOL_PREV, D_MODEL), F32)],
        scratch_shapes=[pltpu.VMEM((nj, POOL_PREV, TN), F32), zbuf, zbuf],
        compiler_params=_cparams(("arbitrary",)),
        name="odd_front_prompt",
    )(h2d, *([w_in] * N_SEG_O), pw, ps.reshape(1, D_MODEL))


RET_SBLK = 1024


def _ret_prompt_kernel(q_ref, kt_ref, v_ref, sgd_ref, dec_ref, cdec_ref, kdec_ref,
                       yd_ref, st_ref, state_scr, *, sblk, n_sblk):
    sb = pl.program_id(1)

    @pl.when(sb == 0)
    def _():
        state_scr[...] = jnp.zeros_like(state_scr)

    def chunk(c, carry):
        rows = pl.ds(pl.multiple_of(c * RET_CHUNK, RET_CHUNK), RET_CHUNK)
        for h in range(H_D):
            cols = slice(h * DK_D, (h + 1) * DK_D)
            q = q_ref[h, rows, :]
            kt = kt_ref[h, :, rows]
            v = v_ref[h, rows, :]
            state = state_scr[h]
            scores = _dot(q, kt) * dec_ref[h]
            o = _dot(scores.astype(BF16), v) + _dot(q, state.astype(BF16)) * cdec_ref[h]
            kd = (kt.astype(F32) * kdec_ref[h]).astype(BF16)
            state_scr[h] = float(np.exp(RET_CHUNK * RET_LOG_G[h])) * state + _dot(kd, v)
            yd_ref[rows, cols] = (_rms(o) * sgd_ref[rows, cols].astype(F32)).astype(BF16)
        return carry

    lax.fori_loop(0, sblk // RET_CHUNK, chunk, 0, unroll=4)

    @pl.when(sb == n_sblk - 1)
    def _():
        st_ref[...] = state_scr[...]


def _ret_consts():
    t = np.arange(RET_CHUNK, dtype=np.float64)
    diff = t[:, None] - t[None, :]
    lg = np.asarray(RET_LOG_G)[:, None, None]
    dec = np.where(diff >= 0, np.exp(np.maximum(diff, 0.0)[None] * lg), 0.0)
    cdec = np.broadcast_to(np.exp((t + 1.0)[None, :, None] * lg), (H_D, RET_CHUNK, DK_D))
    kdec = np.broadcast_to(np.exp((RET_CHUNK - 1.0 - t)[None, None, :] * lg), (H_D, DK_D, RET_CHUNK))
    return jnp.asarray(dec, F32), jnp.asarray(cdec, F32), jnp.asarray(kdec, F32)


def _ret_prompt(q, kt, v, sgd, batch, seq):
    dec, cdec, kdec = _ret_consts()
    n_sblk = seq // RET_SBLK
    head_spec = pl.BlockSpec((None, H_D, RET_SBLK, DK_D), lambda b, s: (b, 0, s, 0))
    tok_spec = pl.BlockSpec((None, RET_SBLK, D_MODEL), lambda b, s: (b, s, 0))

    def const_spec(shape):
        return pl.BlockSpec(shape, lambda b, s: (0, 0, 0))

    return pl.pallas_call(
        functools.partial(_ret_prompt_kernel, sblk=RET_SBLK, n_sblk=n_sblk),
        grid=(batch, n_sblk),
        in_specs=[head_spec, pl.BlockSpec((None, H_D, DK_D, RET_SBLK), lambda b, s: (b, 0, 0, s)), head_spec,
                  tok_spec, const_spec(dec.shape), const_spec(cdec.shape), const_spec(kdec.shape)],
        out_specs=[tok_spec, pl.BlockSpec((None, H_D, DK_D, DK_D), lambda b, s: (b, 0, 0, 0))],
        out_shape=[jax.ShapeDtypeStruct((batch, seq, D_MODEL), BF16),
                   jax.ShapeDtypeStruct((batch, H_D, DK_D, DK_D), F32)],
        scratch_shapes=[pltpu.VMEM((H_D, DK_D, DK_D), F32)],
        compiler_params=_cparams(("arbitrary", "arbitrary")),
        name="ret_prompt",
    )(q, kt, v, sgd.reshape(batch, seq, D_MODEL), dec, cdec, kdec)


def _even_front_sample_kernel(x_ref, mod_ref, g_ref, wbg, wcg, wxv, wga, wq, wk, wv, wgb, cw_ref, cb_ref,
                              prev_ref, ya_ref, q_ref, k_ref, v_ref, sgb_ref, cn_ref):
    h = _norm_mod(x_ref[...], g_ref[...], mod_ref[...]).astype(BF16)
    u = _dot(h, wcg[...]) * _dot(h, wxv[...])
    cw = cw_ref[...]
    conv = cb_ref[...] + cw[0:1, :] * prev_ref[0] + cw[1:2, :] * prev_ref[1] + cw[2:3, :] * u
    cn_ref[0] = prev_ref[1]
    cn_ref[1] = u
    ya_ref[...] = (_dot(h, wbg[...]) * conv * _silu(_dot(h, wga[...]))).astype(BF16)
    q_ref[...] = _dot(h, wq[...]) * (HD_B ** -0.5)
    k_ref[...] = _dot(h, wk[...])
    v_ref[...] = _dot(h, wv[...])
    sgb_ref[...] = _silu(_dot(h, wgb[...]))


def _even_front_sample(x2d, mod, g, w_in, cw, cb, prev_t):
    rows = x2d.shape[0]
    nj = D_MODEL // TN

    def wspec(s):
        return pl.BlockSpec((D_MODEL, TN), lambda j, s=s: (0, s * nj + j))

    full = pl.BlockSpec((rows, D_MODEL), lambda j: (0, 0))
    col = pl.BlockSpec((rows, TN), lambda j: (0, j))
    st = pl.BlockSpec((2, rows, TN), lambda j: (0, 0, j))
    colf = jax.ShapeDtypeStruct((rows, D_MODEL), F32)
    return pl.pallas_call(
        _even_front_sample_kernel,
        grid=(nj,),
        in_specs=[full, pl.BlockSpec((rows, 3 * D_MODEL), lambda j: (0, 0)),
                  pl.BlockSpec((1, D_MODEL), lambda j: (0, 0))] + [wspec(s) for s in range(8)] + [
            pl.BlockSpec((3, TN), lambda j: (0, j)), pl.BlockSpec((1, TN), lambda j: (0, j)), st],
        out_specs=[col, col, col, col, col, st],
        out_shape=[jax.ShapeDtypeStruct((rows, D_MODEL), BF16), colf, colf, colf, colf,
                   jax.ShapeDtypeStruct((2, rows, D_MODEL), F32)],
        compiler_params=_cparams(("arbitrary",)),
        name="even_front_sample",
    )(x2d, mod, g.reshape(1, D_MODEL), *([w_in] * 8), cw, cb.reshape(1, D_MODEL), prev_t)


def _attn_sample_kernel(slope_ref, pos_ref, q_ref, kn_ref, vn_ref, sgb_ref, kt_ref, vt_ref, yb_ref):
    gw = q_ref.shape[-1]
    row = lax.broadcasted_iota(jnp.int32, (8, gw), 0)
    col = lax.broadcasted_iota(jnp.int32, (8, gw), 1)
    own = (col >= row * HD_B) & (col < (row + 1) * HD_B)
    qm = jnp.where(own, q_ref[...], 0.0)
    s_self = jnp.sum(qm * kn_ref[...], axis=-1, keepdims=True)
    s_all = _dot(qm.astype(BF16), kt_ref[...].astype(BF16))
    s_all = s_all - slope_ref[:, 0:1] * pos_ref[0:1, :]
    v_self = vn_ref[...]
    probs, stats = [], []
    for p in range(len(DILATIONS)):
        s = s_all - pos_ref[p + 1:p + 2, :]
        m = jnp.maximum(jnp.max(s, axis=-1, keepdims=True), s_self)
        pe = jnp.exp(s - m)
        p_self = jnp.exp(s_self - m)
        probs.append(pe)
        stats.append((m, jnp.sum(pe, axis=-1, keepdims=True) + p_self, p_self))
    o_all = _dot_nt(jnp.concatenate(probs, axis=0).astype(BF16), vt_ref[...].astype(BF16))
    outs, lses = [], []
    for p, (m, l, p_self) in enumerate(stats):
        outs.append((o_all[8 * p:8 * p + 8, :] + p_self * v_self) * (1.0 / l))
        lses.append(m + jnp.log(l))
    m = jnp.maximum(jnp.maximum(lses[0], lses[1]), lses[2])
    ws = [jnp.exp(x - m) for x in lses]
    o = (ws[0] * outs[0] + ws[1] * outs[1] + ws[2] * outs[2]) * (1.0 / (ws[0] + ws[1] + ws[2]))
    o = jnp.sum(jnp.where(own, o, 0.0), axis=0, keepdims=True)
    yb_ref[...] = (o * sgb_ref[...]).astype(BF16)


def _attn_sample(q, kn, vn, sgb, cache_kt, cache_vt):
    rows, _, wb = cache_kt.shape
    gh = 4
    gw = gh * HD_B
    ng = H_B // gh
    back = wb - np.arange(wb, dtype=np.float64)
    pos = [back] + [np.where((back % d == 0) & (back <= NK * d), 0.0, MASK_ADD) for d in DILATIONS]
    slopes = (2.0 ** (-8.0 * np.arange(1, H_B + 1) / H_B)).reshape(ng, gh)
    slopes = np.concatenate([slopes, np.zeros((ng, 8 - gh))], axis=1)
    slope_arr = jnp.asarray(np.broadcast_to(slopes[:, :, None], (ng, 8, LANES)), F32)

    def r3(a):
        return a.reshape(rows, 1, D_MODEL)

    row_spec = pl.BlockSpec((None, 1, gw), lambda b, g: (b, 0, g))
    t_spec = pl.BlockSpec((None, gw, wb), lambda b, g: (b, g, 0))
    return pl.pallas_call(
        _attn_sample_kernel,
        grid=(rows, ng),
        in_specs=[pl.BlockSpec((None, 8, LANES), lambda b, g: (g, 0, 0)),
                  pl.BlockSpec((4, wb), lambda b, g: (0, 0)),
                  row_spec, row_spec, row_spec, row_spec, t_spec, t_spec],
        out_specs=row_spec,
        out_shape=jax.ShapeDtypeStruct((rows, 1, D_MODEL), BF16),
        compiler_params=_cparams(("arbitrary", "arbitrary")),
        name="attn_sample",
    )(slope_arr, jnp.asarray(np.stack(pos), F32), r3(q), r3(kn), r3(vn), r3(sgb), cache_kt, cache_vt
      ).reshape(rows, D_MODEL)


def _odd_front_sample_kernel(h_ref, wu, wgc, wq, wk, wv, wgd, pw_ref, ps_ref, prev_ref,
                             yc_ref, q_ref, k_ref, v_ref, sgd_ref, pn_ref):
    j = pl.program_id(0)
    h = h_ref[...]
    u = _dot(h, wu[...])
    n_prev = prev_ref.shape[0]
    sums = []
    s = jnp.zeros_like(u)
    back = 0
    for w in POOL_SIZES:
        while back < w - 1:
            s = s + prev_ref[n_prev - 1 - back]
            back += 1
        sums.append(s)
    win = _pick_by_chunk(j, sums) + u
    inv_w = _pick_by_chunk(j, [1.0 / w for w in POOL_SIZES])
    pooled = win * inv_w - u
    mixed = _dot(pooled.astype(BF16), pw_ref[...]) * ps_ref[...]
    yc_ref[...] = (mixed * _silu(_dot(h, wgc[...]))).astype(BF16)
    q_ref[...] = _dot(h, wq[...])
    k_ref[...] = _dot(h, wk[...]) * (DK_D ** -0.5)
    v_ref[...] = _dot(h, wv[...])
    sgd_ref[...] = _silu(_dot(h, wgd[...]))
    for t in range(n_prev - 1):
        pn_ref[t] = prev_ref[t + 1]
    pn_ref[n_prev - 1] = u


def _odd_front_sample(h2d, w_in, pw, ps, prev_t):
    rows = h2d.shape[0]
    n_prev = prev_t.shape[0]
    nj = D_MODEL // TN

    def wspec(s):
        return pl.BlockSpec((D_MODEL, TN), lambda j, s=s: (0, s * nj + j))

    col = pl.BlockSpec((rows, TN), lambda j: (0, j))
    st = pl.BlockSpec((n_prev, rows, TN), lambda j: (0, 0, j))
    colf = jax.ShapeDtypeStruct((rows, D_MODEL), F32)
    return pl.pallas_call(
        _odd_front_sample_kernel,
        grid=(nj,),
        in_specs=[pl.BlockSpec((rows, D_MODEL), lambda j: (0, 0))] + [wspec(s) for s in range(6)] + [
            pl.BlockSpec((None, TN, TN), lambda j: (j, 0, 0)), pl.BlockSpec((1, TN), lambda j: (0, j)), st],
        out_specs=[col, col, col, col, col, st],
        out_shape=[jax.ShapeDtypeStruct((rows, D_MODEL), BF16), colf, colf, colf, colf,
                   jax.ShapeDtypeStruct((n_prev, rows, D_MODEL), F32)],
        compiler_params=_cparams(("arbitrary",)),
        name="odd_front_sample",
    )(h2d, *([w_in] * 6), pw, ps.reshape(1, D_MODEL), prev_t)


def _ret_sample_kernel(q_ref, k_ref, v_ref, sgd_ref, st_ref, yd_ref, sn_ref):
    row = lax.broadcasted_iota(jnp.int32, (DK_D, DK_D), 0)
    col = lax.broadcasted_iota(jnp.int32, (DK_D, DK_D), 1)
    for h in range(H_D):
        g = float(np.exp(RET_LOG_G[h]))
        cols = slice(h * DK_D, (h + 1) * DK_D)
        q = q_ref[:, cols]
        k = k_ref[:, cols]
        v = v_ref[:, cols]
        state = st_ref[h]
        qk = jnp.sum(q * k, axis=-1, keepdims=True)
        q8 = jnp.broadcast_to(q, (8, DK_D)).astype(BF16)
        cross = _dot(q8, state.astype(BF16))[0:1, :]
        o = qk * v + g * cross
        k_diag = jnp.where(row == col, jnp.broadcast_to(k, (DK_D, DK_D)), 0.0).astype(BF16)
        v_rows = jnp.broadcast_to(v, (DK_D, DK_D)).astype(BF16)
        sn_ref[h] = g * state + _dot(k_diag, v_rows)
        yd_ref[:, cols] = (_rms(o) * sgd_ref[:, cols]).astype(BF16)


def _ret_sample(q, k, v, sgd, state):
    rows = q.shape[0]

    def r3(a):
        return a.reshape(rows, 1, D_MODEL)

    row_spec = pl.BlockSpec((None, 1, D_MODEL), lambda b: (b, 0, 0))
    st_spec = pl.BlockSpec((None, H_D, DK_D, DK_D), lambda b: (b, 0, 0, 0))
    yd, sn = pl.pallas_call(
        _ret_sample_kernel,
        grid=(rows,),
        in_specs=[row_spec, row_spec, row_spec, row_spec, st_spec],
        out_specs=[row_spec, st_spec],
        out_shape=[jax.ShapeDtypeStruct((rows, 1, D_MODEL), BF16),
                   jax.ShapeDtypeStruct((rows, H_D, DK_D, DK_D), F32)],
        compiler_params=_cparams(("arbitrary",)),
        name="ret_sample",
    )(r3(q), r3(k), r3(v), r3(sgd), state)
    return yd.reshape(rows, D_MODEL), sn


def kernel(x_prompt, x_sample, c_prompt, c_sample, state_conv, cache_win_k, cache_win_v, state_pool, state_ret,
           norm_e, ada_w_e, ada_b_e, w_in_e, conv_w, conv_b, w_out_e, norm_o, ada_w_o, ada_b_o, w_in_o,
           pool_w, pool_scale, w_out_o, norm_f):
    batch, seq, d = x_prompt.shape
    sb = x_sample.shape[0]
    assert d == D_MODEL and x_sample.shape[1] == 1
    assert norm_e.shape[0] == 1 and norm_o.shape[0] == 1
    wb = cache_win_k.shape[2]
    keep = min(wb, seq)
    assert wb == DILATIONS[-1] * NK and seq % (DILATIONS[-1] * NK) == 0
    tm = 512

    n_c = batch + sb
    pad = (-n_c) % 8
    c_all = jnp.concatenate([c_prompt, c_sample, jnp.zeros((pad, d), F32)], axis=0)
    mod_e, mod_o = _adaln(c_all, ada_w_e[0], ada_b_e[0], ada_w_o[0], ada_b_o[0])
    mod_e_p, mod_e_s = mod_e[:batch], mod_e[batch:n_c]
    mod_o_p, mod_o_s = mod_o[:batch], mod_o[batch:n_c]

    w_in_e16 = w_in_e[0].astype(BF16)
    w_out_e16 = w_out_e[0].astype(BF16)
    w_in_o16 = w_in_o[0].astype(BF16)
    w_out_o16 = w_out_o[0].astype(BF16)
    pool_w16 = pool_w[0].astype(BF16)

    xp2d = x_prompt.reshape(batch * seq, d)
    ya, q, k, v, sgb, k_new, v_new, conv_tail = _even_front_prompt(
        xp2d, mod_e_p, norm_e[0], w_in_e16, conv_w[0], conv_b[0], batch, seq, keep, tm)
    yb = _attn_prompt(q, k, v, sgb, batch, seq).reshape(batch * seq, d)
    x1, h1 = _outproj(ya, yb, xp2d, mod_e_p, w_out_e16, norm_o[0], mod_o_p, tm, seq)
    yc, rq, rk, rv, sgd, pool_tail = _odd_front_prompt(h1, w_in_o16, pool_w16, pool_scale[0], batch, seq, tm)
    yd, ret_p = _ret_prompt(rq, rk, rv, sgd, batch, seq)
    y_prompt = _outproj(yc, yd.reshape(batch * seq, d), x1, mod_o_p, w_out_o16, norm_f, None, tm, seq)

    xs2d = x_sample.reshape(sb, d)
    conv_prev_t = jnp.transpose(state_conv[0], (1, 0, 2))
    ya_s, q_s, k_s, v_s, sgb_s, conv_s_t = _even_front_sample(
        xs2d, mod_e_s, norm_e[0], w_in_e16, conv_w[0], conv_b[0], conv_prev_t)
    cache_kt = jnp.transpose(cache_win_k[0], (0, 2, 3, 1)).reshape(sb, d, wb)
    cache_vt = jnp.transpose(cache_win_v[0], (0, 2, 3, 1)).reshape(sb, d, wb)
    yb_s = _attn_sample(q_s, k_s, v_s, sgb_s, cache_kt, cache_vt)
    x1_s, h1_s = _outproj(ya_s, yb_s, xs2d, mod_e_s, w_out_e16, norm_o[0], mod_o_s, sb, 1)
    pool_prev_t = jnp.transpose(state_pool[0], (1, 0, 2))
    yc_s, rq_s, rk_s, rv_s, sgd_s, pool_s_t = _odd_front_sample(h1_s, w_in_o16, pool_w16, pool_scale[0], pool_prev_t)
    yd_s, ret_s = _ret_sample(rq_s, rk_s, rv_s, sgd_s, state_ret[0])
    y_sample = _outproj(yc_s, yd_s, x1_s, mod_o_s, w_out_o16, norm_f, None, sb, 1)

    return (
        y_prompt.reshape(batch, seq, d),
        y_sample.reshape(sb, 1, d),
        conv_tail[:, 6:8][None],
        jnp.transpose(conv_s_t, (1, 0, 2))[None],
        jnp.transpose(k_new.reshape(batch, H_B, HD_B, keep), (0, 3, 1, 2))[None],
        k_s.reshape(1, sb, 1, H_B, HD_B),
        jnp.transpose(v_new.reshape(batch, H_B, HD_B, keep), (0, 3, 1, 2))[None],
        v_s.reshape(1, sb, 1, H_B, HD_B),
        pool_tail[:, 1:][None],
        jnp.transpose(pool_s_t, (1, 0, 2))[None],
        ret_p[None],
        ret_s[None],
    )
```

```python
import functools

import numpy as np
import jax
import jax.numpy as jnp
from jax import lax
from jax.experimental import pallas as pl
from jax.experimental.pallas import tpu as pltpu

F32 = jnp.float32
BF16 = jnp.bfloat16
WORD = jnp.uint32

D_MODEL = 1024
EPS = 1e-6
H_B = 16
HD_B = 64
N_PAIR = H_B // 2
LANES = 128
NK = 128
DILATIONS = (1, 4, 16)
POOL_SIZES = (2, 4, 8, 16)
POOL_PREV = 16
H_D = 4
DK_D = 256
RET_CHUNK = 128
TN = 256
MASK_DIST = 1e9
LOG2E = float(np.log2(np.e))
MASK_ADD = 1e30
VMEM_LIMIT = 56 * 1024 * 1024

RET_LOG_G = [float(np.log(1.0 - 2.0 ** (-5.0 - h))) for h in range(H_D)]


def _cparams(sem):
    return pltpu.CompilerParams(dimension_semantics=sem, vmem_limit_bytes=VMEM_LIMIT)


def _silu(x):
    return x * (1.0 / (1.0 + jnp.exp(-x)))


def _dot(a, b):
    return jnp.dot(a, b, preferred_element_type=F32)


def _dot_nt(a, b):
    return lax.dot_general(a, b, (((1,), (1,)), ((), ())), preferred_element_type=F32)


def _rms(x):
    return x * lax.rsqrt(jnp.mean(x * x, axis=-1, keepdims=True) + EPS)


def _adaln_kernel(c_ref, we_ref, be_ref, wo_ref, bo_ref, me_ref, mo_ref):
    sc = _silu(c_ref[...]).astype(BF16)
    me_ref[...] = _dot(sc, we_ref[...].astype(BF16)) + be_ref[...]
    mo_ref[...] = _dot(sc, wo_ref[...].astype(BF16)) + bo_ref[...]


def _adaln(c_all, we, be, wo, bo):
    rows = c_all.shape[0]
    tn = 512
    n = 3 * D_MODEL
    wspec = pl.BlockSpec((D_MODEL, tn), lambda j: (0, j))
    bspec = pl.BlockSpec((1, tn), lambda j: (0, j))
    ospec = pl.BlockSpec((rows, tn), lambda j: (0, j))
    return pl.pallas_call(
        _adaln_kernel,
        grid=(n // tn,),
        in_specs=[pl.BlockSpec((rows, D_MODEL), lambda j: (0, 0)), wspec, bspec, wspec, bspec],
        out_specs=[ospec, ospec],
        out_shape=[jax.ShapeDtypeStruct((rows, n), F32)] * 2,
        compiler_params=_cparams(("arbitrary",)),
        name="adaln",
    )(c_all, we, be.reshape(1, n), wo, bo.reshape(1, n))


def _norm_mod(x, g, mod):
    shift = mod[:, 0:D_MODEL]
    scale = mod[:, D_MODEL:2 * D_MODEL]
    return _rms(x) * g * (1.0 + scale) + shift


def _shift_rows(u, k, prev_rows):
    row = lax.broadcasted_iota(jnp.int32, u.shape, 0)
    out = pltpu.roll(u, k, 0)
    for idx, pr in enumerate(prev_rows):
        out = jnp.where(row == idx, pr, out)
    return out


N_SEG_E = 8
PHASES_E = 8


def _phase_segments(t, n_seg, n_phases):
    return range(t * n_seg // n_phases, (t + 1) * n_seg // n_phases)


def _even_front_kernel(x_ref, mod_ref, g_ref, *refs, tiles_per_batch, n_chunks):
    ws = refs[:N_SEG_E]
    cw_ref, cb_ref = refs[N_SEG_E:N_SEG_E + 2]
    ya_ref, q_ref, k_ref, v_ref, sgb_ref, kn_ref, vn_ref, cn_ref = refs[N_SEG_E + 2:N_SEG_E + 10]
    h_scr, carry_scr, z_a, z_b = refs[N_SEG_E + 10:]
    nj = D_MODEL // TN
    s = pl.program_id(0)
    c_out = jnp.maximum(s - 1, 0)
    j_out = c_out % nj
    il_out = (c_out // nj) % tiles_per_batch

    @pl.when(s == 0)
    def _():
        z_b[...] = jnp.zeros_like(z_b)
        carry_scr[...] = jnp.zeros_like(carry_scr)

    @pl.when((s < n_chunks) & (s % nj == 0))
    def _():
        h_scr[...] = _norm_mod(x_ref[...], g_ref[...], mod_ref[...]).astype(BF16)

    tm = h_scr.shape[0]
    rb = tm // PHASES_E

    def phase(t, z_w, z_r):
        for seg in _phase_segments(t, N_SEG_E, PHASES_E):
            z_w[seg] = _dot(h_scr[...], ws[seg][...])
        rows = slice(t * rb, (t + 1) * rb)
        bg, cg, xv, ga, q, k, v, gb = (z_r[seg, rows, :] for seg in range(N_SEG_E))
        u = cg * xv
        prev = carry_scr[j_out]
        if t == 0:
            prev = jnp.where(il_out == 0, 0.0, prev)
        p2, p1 = prev[6:7, :], prev[7:8, :]
        u1 = _shift_rows(u, 1, [p1])
        u2 = _shift_rows(u, 2, [p2, p1])
        cw = cw_ref[...]
        conv = cb_ref[...] + cw[0:1, :] * u2 + cw[1:2, :] * u1 + cw[2:3, :] * u
        tail = u[rb - 8:rb, :]
        carry_scr[j_out] = tail
        if t == PHASES_E - 1:
            cn_ref[...] = tail
        ya_ref[rows, :] = (bg * conv * _silu(ga)).astype(BF16)
        kn_ref[:, rows] = k.T
        vn_ref[:, rows] = v.T
        wrows = slice(t * rb // 2, (t + 1) * rb // 2)
        for ref, val in ((q_ref, q * (HD_B ** -0.5 * LOG2E)),
                         (k_ref, k), (v_ref, v)):
            words = pltpu.bitcast(val.astype(BF16), WORD)
            for e in range(TN // LANES):
                ref[e, wrows, :] = words[:, e * LANES:(e + 1) * LANES]
        sgb = _silu(gb).astype(BF16)
        for e in range(TN // LANES):
            sgb_ref[e, rows, :] = sgb[:, e * LANES:(e + 1) * LANES]

    for parity, (z_w, z_r) in enumerate(((z_a, z_b), (z_b, z_a))):
        for t in range(PHASES_E):
            pl.when(s % 2 == parity)(functools.partial(phase, t, z_w, z_r))


def _even_front_prompt(x2d, mod, g, w_in, cw, cb, batch, seq, keep, tm):
    rows = batch * seq
    tpb = seq // tm
    nj = D_MODEL // TN
    n_chunks = (rows // tm) * nj
    off = (seq - keep) // tm
    ppc = TN // LANES

    def c_in(s):
        return jnp.minimum(s, n_chunks - 1)

    def c_out(s):
        return jnp.maximum(s - 1, 0)

    def out_ij(s):
        c = c_out(s)
        return c // nj, c % nj

    def pair_map(s):
        i, j = out_ij(s)
        return (i // tpb, j, i % tpb, 0)

    def keep_map(s):
        i, j = out_ij(s)
        il = i % tpb
        kept = il >= off
        return (i // tpb, jnp.where(kept, j, 0), jnp.where(kept, il - off, 0))

    def tail_map(s):
        i, j = out_ij(s)
        return (i // tpb, 0, jnp.where(i % tpb == tpb - 1, j, 0))

    pair_spec = pl.BlockSpec((None, ppc, tm, LANES), pair_map)
    pair_shape = jax.ShapeDtypeStruct((batch, N_PAIR, seq, LANES), BF16)
    word_spec = pl.BlockSpec((None, ppc, tm // 2, LANES), pair_map)
    word_shape = jax.ShapeDtypeStruct((batch, N_PAIR, seq // 2, LANES), WORD)
    keep_spec = pl.BlockSpec((None, TN, tm), keep_map)
    keep_shape = jax.ShapeDtypeStruct((batch, D_MODEL, keep), F32)
    return pl.pallas_call(
        functools.partial(_even_front_kernel, tiles_per_batch=tpb, n_chunks=n_chunks),
        grid=(n_chunks + 1,),
        in_specs=[
            pl.BlockSpec((tm, D_MODEL), lambda s: (c_in(s) // nj, 0)),
            pl.BlockSpec((None, 1, 3 * D_MODEL), lambda s: (c_in(s) // nj // tpb, 0, 0)),
            pl.BlockSpec((1, D_MODEL), lambda s: (0, 0)),
        ] + [pl.BlockSpec((D_MODEL, TN), lambda s, seg=seg: (0, seg * nj + c_in(s) % nj)) for seg in range(N_SEG_E)] + [
            pl.BlockSpec((3, TN), lambda s: (0, c_out(s) % nj)),
            pl.BlockSpec((1, TN), lambda s: (0, c_out(s) % nj)),
        ],
        out_specs=[
            pl.BlockSpec((tm, TN), lambda s: out_ij(s)),
            word_spec, word_spec, word_spec, pair_spec,
            keep_spec, keep_spec,
            pl.BlockSpec((None, 8, TN), tail_map),
        ],
        out_shape=[
            jax.ShapeDtypeStruct((rows, D_MODEL), BF16),
            word_shape, word_shape, word_shape, pair_shape,
            keep_shape, keep_shape,
            jax.ShapeDtypeStruct((batch, 8, D_MODEL), F32),
        ],
        scratch_shapes=[pltpu.VMEM((tm, D_MODEL), BF16), pltpu.VMEM((nj, 8, TN), F32),
                        pltpu.VMEM((N_SEG_E, tm, TN), F32), pltpu.VMEM((N_SEG_E, tm, TN), F32)],
        compiler_params=_cparams(("arbitrary",)),
        name="even_front_prompt",
    )(x2d, mod.reshape(batch, 1, 3 * D_MODEL), g.reshape(1, D_MODEL), *([w_in] * N_SEG_E), cw, cb.reshape(1, D_MODEL))


ATTN_UNROLL = 16


def _residue_stream(src, r):
    n_out = src.shape[0] // 4
    halves = [pltpu.unpack_elementwise(src[pl.ds(r // 2 + off, n_out, stride=4), :], index=r % 2,
                                       packed_dtype=BF16, unpacked_dtype=F32) for off in (0, 2)]
    return pltpu.pack_elementwise(halves, packed_dtype=BF16)


def _attn_prompt_kernel(slope_ref, dist_ref, q_ref, k_ref, v_ref, sgb_ref, yb_ref,
                        q4_scr, q16_scr, kv4_scr, kv16_scr, acc_scr, m_scr, l_scr, bias_scr, *, seq):
    hp = pl.program_id(1)
    lane = lax.broadcasted_iota(jnp.int32, (1, LANES), 1)
    half0 = lane < HD_B
    dist = dist_ref[...]
    masked = jnp.full((NK, NK), MASK_DIST, F32)
    for p, d in enumerate(DILATIONS):
        for e in range(2):
            bias = dist * (slope_ref[2 * hp + e] * (float(d) * LOG2E))
            bias_scr[2 * p + e] = bias
            bias_scr[6 + 2 * p + e] = jnp.concatenate([bias[:, NK:], masked], axis=1)

    for a, ref in enumerate((q_ref, k_ref, v_ref)):
        for r in range(4):
            s = _residue_stream(ref, r)
            if a == 0:
                q4_scr[r] = s
            else:
                kv4_scr[a - 1, r] = s
        for r in range(16):
            s = _residue_stream(q4_scr.at[r % 4] if a == 0 else kv4_scr.at[a - 1, r % 4], r // 4)
            if a == 0:
                q16_scr[r] = s
            else:
                kv16_scr[a - 1, r] = s

    ones = jnp.ones((2 * NK, LANES), BF16)
    hw = NK // 2

    def tile(p, d, r, n):
        qw = pl.ds(pl.multiple_of(n * hw, hw), hw)
        kw = pl.ds(pl.multiple_of(jnp.maximum(n - 1, 0) * hw, hw), 2 * hw)
        if d == 1:
            qs, ks, vs = q_ref, k_ref, v_ref
        elif d == 4:
            qs, ks, vs = q4_scr.at[r], kv4_scr.at[0, r], kv4_scr.at[1, r]
        else:
            qs, ks, vs = q16_scr.at[r], kv16_scr.at[0, r], kv16_scr.at[1, r]
        q = pltpu.bitcast(qs[qw, :], BF16)
        kk = pltpu.bitcast(ks[kw, :], BF16)
        vv1 = jnp.concatenate([pltpu.bitcast(vs[kw, :], BF16), ones], axis=1)
        first = jnp.where(n == 0, 1, 0)
        res = []
        for e in range(2):
            qe = jnp.where(half0 if e == 0 else jnp.logical_not(half0), q, jnp.zeros_like(q))
            s = _dot_nt(qe, kk) - bias_scr[6 * first + 2 * p + e]
            m = jnp.max(s, axis=-1, keepdims=True)
            pe = jnp.exp2(s - m).astype(BF16)
            res.append((_dot(pe, vv1), m))
        (a0, m0), (a1, m1) = res
        rows = pl.ds(r + d * NK * n, NK, stride=d) if d > 1 else pl.ds(pl.multiple_of(n * NK, NK), NK)
        acc_scr[p, rows, :] = jnp.where(half0, a0[:, :LANES], a1[:, :LANES])
        l_scr[p, rows, :] = jnp.where(half0, a0[:, LANES:], a1[:, LANES:])
        m_scr[p, rows, :] = jnp.where(half0, m0, m1)

    for p, d in enumerate(DILATIONS):
        def body(it, c, p=p, d=d):
            for u in range(ATTN_UNROLL):
                t = it * ATTN_UNROLL + u
                if d == 1:
                    tile(p, d, 0, t)
                else:
                    tile(p, d, t % d, t // d)
            return c

        lax.fori_loop(0, seq // NK // ATTN_UNROLL, body, 0)

    rc = 512
    for c in range(seq // rc):
        rows = pl.ds(c * rc, rc)
        ms = [m_scr[p, rows, :] for p in range(3)]
        m = jnp.maximum(jnp.maximum(ms[0], ms[1]), ms[2])
        ws = [jnp.exp2(x - m) for x in ms]
        num = ws[0] * acc_scr[0, rows, :] + ws[1] * acc_scr[1, rows, :] + ws[2] * acc_scr[2, rows, :]
        den = ws[0] * l_scr[0, rows, :] + ws[1] * l_scr[1, rows, :] + ws[2] * l_scr[2, rows, :]
        yb_ref[rows, :] = (num * (1.0 / den) * sgb_ref[rows, :].astype(F32)).astype(BF16)


def _band_distance():
    qi = np.arange(NK)[:, None]
    kj = np.arange(2 * NK)[None, :]
    dist = NK + qi - kj
    return jnp.asarray(np.where((dist >= 0) & (dist <= NK), dist, MASK_DIST), dtype=F32)


def _alibi_slopes():
    return jnp.asarray(2.0 ** (-8.0 * np.arange(1, H_B + 1) / H_B), dtype=F32)


def _attn_prompt(q, k, v, sgb, batch, seq):
    word_spec = pl.BlockSpec((None, None, seq // 2, LANES), lambda b, hp: (b, hp, 0, 0))
    stat = pltpu.VMEM((len(DILATIONS), seq, LANES), F32)
    return pl.pallas_call(
        functools.partial(_attn_prompt_kernel, seq=seq),
        grid=(batch, N_PAIR),
        in_specs=[pl.BlockSpec(memory_space=pltpu.SMEM),
                  pl.BlockSpec((NK, 2 * NK), lambda b, hp: (0, 0)),
                  word_spec, word_spec, word_spec,
                  pl.BlockSpec((None, None, seq, LANES), lambda b, hp: (b, hp, 0, 0))],
        out_specs=pl.BlockSpec((None, seq, LANES), lambda b, hp: (b, 0, hp)),
        out_shape=jax.ShapeDtypeStruct((batch, seq, D_MODEL), BF16),
        scratch_shapes=[pltpu.VMEM((4, seq // 8, LANES), WORD),
                        pltpu.VMEM((16, seq // 32, LANES), WORD),
                        pltpu.VMEM((2, 4, seq // 8, LANES), WORD),
                        pltpu.VMEM((2, 16, seq // 32, LANES), WORD),
                        stat, stat, stat,
                        pltpu.VMEM((12, NK, 2 * NK), F32)],
        compiler_params=_cparams(("arbitrary", "arbitrary")),
        name="attn_prompt",
    )(_alibi_slopes(), _band_distance(), q, k, v, sgb)


def _outproj_mid_kernel(ya_ref, yb_ref, x_ref, mod_ref, w1_ref, w2_ref, g_ref, mod2_ref, x1_ref, h_ref):
    gate = mod_ref[:, 2 * D_MODEL:3 * D_MODEL]
    x1 = x_ref[...] + gate * (_dot(ya_ref[...], w1_ref[...]) + _dot(yb_ref[...], w2_ref[...]))
    x1_ref[...] = x1
    h_ref[...] = _norm_mod(x1, g_ref[...], mod2_ref[...]).astype(BF16)


def _outproj_final_kernel(ya_ref, yb_ref, x_ref, mod_ref, w1_ref, w2_ref, g_ref, y_ref):
    gate = mod_ref[:, 2 * D_MODEL:3 * D_MODEL]
    x1 = x_ref[...] + gate * (_dot(ya_ref[...], w1_ref[...]) + _dot(yb_ref[...], w2_ref[...]))
    y_ref[...] = _rms(x1) * g_ref[...]


def _outproj(ya, yb, x2d, mod, w_out, g, mod2, tm, rows_per_mod):
    rows = x2d.shape[0]
    half = w_out.shape[0] // 2
    row_spec = pl.BlockSpec((tm, D_MODEL), lambda i: (i, 0))
    if rows_per_mod == 1:
        mod_spec = pl.BlockSpec((tm, 3 * D_MODEL), lambda i: (i, 0))
        mods = (mod, mod2)
    else:
        tpb = rows_per_mod // tm
        mod_spec = pl.BlockSpec((None, 1, 3 * D_MODEL), lambda i: (i // tpb, 0, 0))
        mods = tuple(None if m is None else m.reshape(-1, 1, 3 * D_MODEL) for m in (mod, mod2))
    w1_spec = pl.BlockSpec((half, D_MODEL), lambda i: (0, 0))
    w2_spec = pl.BlockSpec((half, D_MODEL), lambda i: (1, 0))
    g_spec = pl.BlockSpec((1, D_MODEL), lambda i: (0, 0))
    common = dict(grid=(rows // tm,), compiler_params=_cparams(("arbitrary",)))
    if mod2 is None:
        return pl.pallas_call(
            _outproj_final_kernel,
            in_specs=[row_spec, row_spec, row_spec, mod_spec, w1_spec, w2_spec, g_spec],
            out_specs=row_spec,
            out_shape=jax.ShapeDtypeStruct((rows, D_MODEL), F32),
            name="outproj_final", **common,
        )(ya, yb, x2d, mods[0], w_out, w_out, g.reshape(1, D_MODEL))
    return pl.pallas_call(
        _outproj_mid_kernel,
        in_specs=[row_spec, row_spec, row_spec, mod_spec, w1_spec, w2_spec, g_spec, mod_spec],
        out_specs=[row_spec, row_spec],
        out_shape=[jax.ShapeDtypeStruct((rows, D_MODEL), F32), jax.ShapeDtypeStruct((rows, D_MODEL), BF16)],
        name="outproj_mid", **common,
    )(ya, yb, x2d, mods[0], w_out, w_out, g.reshape(1, D_MODEL), mods[1])


def _pick_by_chunk(j, vals):
    out = vals[-1]
    for idx in range(len(vals) - 2, -1, -1):
        out = jnp.where(j == idx, vals[idx], out)
    return out


N_SEG_O = 6
PHASES_O = 4


def _odd_front_kernel(h_ref, *refs, tiles_per_batch):
    ws = refs[:N_SEG_O]
    pw_ref, ps_ref = refs[N_SEG_O:N_SEG_O + 2]
    yc_ref, q_ref, kt_ref, v_ref, sgd_ref, pn_ref = refs[N_SEG_O + 2:N_SEG_O + 8]
    carry_scr, z_a, z_b = refs[N_SEG_O + 8:]
    nj = D_MODEL // TN
    s = pl.program_id(0)
    c_out = jnp.maximum(s - 1, 0)
    j_out = c_out % nj
    il_out = (c_out // nj) % tiles_per_batch
    tm = h_ref.shape[0]
    rb = tm // PHASES_O

    @pl.when(s == 0)
    def _():
        z_b[...] = jnp.zeros_like(z_b)
        carry_scr[...] = jnp.zeros_like(carry_scr)

    def phase(t, z_w, z_r):
        for seg in _phase_segments(t, N_SEG_O, PHASES_O):
            z_w[seg] = _dot(h_ref[...], ws[seg][...])
        rows = slice(t * rb, (t + 1) * rb)
        u, gc, q, k, v, gd = (z_r[seg, rows, :] for seg in range(N_SEG_O))
        prev = carry_scr[j_out]
        if t == 0:
            prev = jnp.where(il_out == 0, 0.0, prev)
        tail = u[rb - POOL_PREV:rb, :]
        carry_scr[j_out] = tail
        if t == PHASES_O - 1:
            pn_ref[...] = tail
        ext = jnp.concatenate([prev, u], axis=0)
        sums = []
        acc = ext
        for sh in (1, 2, 4, 8):
            acc = acc + pltpu.roll(acc, sh, 0)
            sums.append(acc[POOL_PREV:, :])
        win = _pick_by_chunk(j_out, sums)
        width = _pick_by_chunk(j_out, [float(w) for w in POOL_SIZES])
        pos = (il_out * tm + t * rb + lax.broadcasted_iota(jnp.int32, (rb, 1), 0)).astype(F32)
        pooled = win / jnp.minimum(width, pos + 1.0) - u
        mixed = _dot(pooled.astype(BF16), pw_ref[...]) * ps_ref[...]
        yc_ref[rows, :] = (mixed * _silu(gc)).astype(BF16)
        q_ref[rows, :] = q.astype(BF16)
        kt_ref[:, rows] = (k * (DK_D ** -0.5)).T.astype(BF16)
        v_ref[rows, :] = v.astype(BF16)
        sgd_ref[rows, :] = _silu(gd).astype(BF16)

    for parity, (z_w, z_r) in enumerate(((z_a, z_b), (z_b, z_a))):
        for t in range(PHASES_O):
            pl.when(s % 2 == parity)(functools.partial(phase, t, z_w, z_r))


def _odd_front_prompt(h2d, w_in, pw, ps, batch, seq, tm):
    rows = batch * seq
    tpb = seq // tm
    nj = D_MODEL // TN
    n_chunks = (rows // tm) * nj

    def c_in(s):
        return jnp.minimum(s, n_chunks - 1)

    def out_ij(s):
        c = jnp.maximum(s - 1, 0)
        return c // nj, c % nj

    def head_map(s):
        i, j = out_ij(s)
        return (i // tpb, j, i % tpb, 0)

    def kt_map(s):
        i, j = out_ij(s)
        return (i // tpb, j, 0, i % tpb)

    def tail_map(s):
        i, j = out_ij(s)
        return (i // tpb, 0, jnp.where(i % tpb == tpb - 1, j, 0))

    head_spec = pl.BlockSpec((None, None, tm, TN), head_map)
    head_shape = jax.ShapeDtypeStruct((batch, H_D, seq, DK_D), BF16)
    tok_spec = pl.BlockSpec((tm, TN), out_ij)
    tok_shape = jax.ShapeDtypeStruct((rows, D_MODEL), BF16)
    zbuf = pltpu.VMEM((N_SEG_O, tm, TN), F32)
    return pl.pallas_call(
        functools.partial(_odd_front_kernel, tiles_per_batch=tpb),
        grid=(n_chunks + 1,),
        in_specs=[pl.BlockSpec((tm, D_MODEL), lambda s: (c_in(s) // nj, 0))] + [
                  pl.BlockSpec((D_MODEL, TN), lambda s, seg=seg: (0, seg * nj + c_in(s) % nj))
                  for seg in range(N_SEG_O)] + [
                  pl.BlockSpec((None, TN, TN), lambda s: (out_ij(s)[1], 0, 0)),
                  pl.BlockSpec((1, TN), lambda s: (0, out_ij(s)[1]))],
        out_specs=[tok_spec, head_spec, pl.BlockSpec((None, None, TN, tm), kt_map), head_spec, tok_spec,
                   pl.BlockSpec((None, POOL_PREV, TN), tail_map)],
        out_shape=[tok_shape, head_shape, jax.ShapeDtypeStruct((batch, H_D, DK_D, seq), BF16),
                   head_shape, tok_shape,
                   jax.ShapeDtypeStruct((batch, POOL_PREV, D_MODEL), F32)],
        scratch_shapes=[pltpu.VMEM((nj, POOL_PREV, TN), F32), zbuf, zbuf],
        compiler_params=_cparams(("arbitrary",)),
        name="odd_front_prompt",
    )(h2d, *([w_in] * N_SEG_O), pw, ps.reshape(1, D_MODEL))


RET_SBLK = 1024


def _ret_prompt_kernel(q_ref, kt_ref, v_ref, sgd_ref, dec_ref, cdec_ref, kdec_ref,
                       yd_ref, st_ref, state_scr, *, sblk, n_sblk):
    sb = pl.program_id(1)

    @pl.when(sb == 0)
    def _():
        state_scr[...] = jnp.zeros_like(state_scr)

    def chunk(c, carry):
        rows = pl.ds(pl.multiple_of(c * RET_CHUNK, RET_CHUNK), RET_CHUNK)
        for h in range(H_D):
            cols = slice(h * DK_D, (h + 1) * DK_D)
            q = q_ref[h, rows, :]
            kt = kt_ref[h, :, rows]
            v = v_ref[h, rows, :]
            state = state_scr[h]
            scores = _dot(q, kt) * dec_ref[h]
            o = _dot(scores.astype(BF16), v) + _dot(q, state.astype(BF16)) * cdec_ref[h]
            kd = (kt.astype(F32) * kdec_ref[h]).astype(BF16)
            state_scr[h] = float(np.exp(RET_CHUNK * RET_LOG_G[h])) * state + _dot(kd, v)
            yd_ref[rows, cols] = (_rms(o) * sgd_ref[rows, cols].astype(F32)).astype(BF16)
        return carry

    lax.fori_loop(0, sblk // RET_CHUNK, chunk, 0, unroll=4)

    @pl.when(sb == n_sblk - 1)
    def _():
        st_ref[...] = state_scr[...]


def _ret_consts():
    t = np.arange(RET_CHUNK, dtype=np.float64)
    diff = t[:, None] - t[None, :]
    lg = np.asarray(RET_LOG_G)[:, None, None]
    dec = np.where(diff >= 0, np.exp(np.maximum(diff, 0.0)[None] * lg), 0.0)
    cdec = np.broadcast_to(np.exp((t + 1.0)[None, :, None] * lg), (H_D, RET_CHUNK, DK_D))
    kdec = np.broadcast_to(np.exp((RET_CHUNK - 1.0 - t)[None, None, :] * lg), (H_D, DK_D, RET_CHUNK))
    return jnp.asarray(dec, F32), jnp.asarray(cdec, F32), jnp.asarray(kdec, F32)


def _ret_prompt(q, kt, v, sgd, batch, seq):
    dec, cdec, kdec = _ret_consts()
    n_sblk = seq // RET_SBLK
    head_spec = pl.BlockSpec((None, H_D, RET_SBLK, DK_D), lambda b, s: (b, 0, s, 0))
    tok_spec = pl.BlockSpec((None, RET_SBLK, D_MODEL), lambda b, s: (b, s, 0))

    def const_spec(shape):
        return pl.BlockSpec(shape, lambda b, s: (0, 0, 0))

    return pl.pallas_call(
        functools.partial(_ret_prompt_kernel, sblk=RET_SBLK, n_sblk=n_sblk),
        grid=(batch, n_sblk),
        in_specs=[head_spec, pl.BlockSpec((None, H_D, DK_D, RET_SBLK), lambda b, s: (b, 0, 0, s)), head_spec,
                  tok_spec, const_spec(dec.shape), const_spec(cdec.shape), const_spec(kdec.shape)],
        out_specs=[tok_spec, pl.BlockSpec((None, H_D, DK_D, DK_D), lambda b, s: (b, 0, 0, 0))],
        out_shape=[jax.ShapeDtypeStruct((batch, seq, D_MODEL), BF16),
                   jax.ShapeDtypeStruct((batch, H_D, DK_D, DK_D), F32)],
        scratch_shapes=[pltpu.VMEM((H_D, DK_D, DK_D), F32)],
        compiler_params=_cparams(("arbitrary", "arbitrary")),
        name="ret_prompt",
    )(q, kt, v, sgd.reshape(batch, seq, D_MODEL), dec, cdec, kdec)


def _even_front_sample_kernel(x_ref, mod_ref, g_ref, wbg, wcg, wxv, wga, wq, wk, wv, wgb, cw_ref, cb_ref,
                              prev_ref, ya_ref, q_ref, k_ref, v_ref, sgb_ref, cn_ref):
    h = _norm_mod(x_ref[...], g_ref[...], mod_ref[...]).astype(BF16)
    u = _dot(h, wcg[...]) * _dot(h, wxv[...])
    cw = cw_ref[...]
    conv = cb_ref[...] + cw[0:1, :] * prev_ref[0] + cw[1:2, :] * prev_ref[1] + cw[2:3, :] * u
    cn_ref[0] = prev_ref[1]
    cn_ref[1] = u
    ya_ref[...] = (_dot(h, wbg[...]) * conv * _silu(_dot(h, wga[...]))).astype(BF16)
    q_ref[...] = _dot(h, wq[...]) * (HD_B ** -0.5)
    k_ref[...] = _dot(h, wk[...])
    v_ref[...] = _dot(h, wv[...])
    sgb_ref[...] = _silu(_dot(h, wgb[...]))


def _even_front_sample(x2d, mod, g, w_in, cw, cb, prev_t):
    rows = x2d.shape[0]
    nj = D_MODEL // TN

    def wspec(s):
        return pl.BlockSpec((D_MODEL, TN), lambda j, s=s: (0, s * nj + j))

    full = pl.BlockSpec((rows, D_MODEL), lambda j: (0, 0))
    col = pl.BlockSpec((rows, TN), lambda j: (0, j))
    st = pl.BlockSpec((2, rows, TN), lambda j: (0, 0, j))
    colf = jax.ShapeDtypeStruct((rows, D_MODEL), F32)
    return pl.pallas_call(
        _even_front_sample_kernel,
        grid=(nj,),
        in_specs=[full, pl.BlockSpec((rows, 3 * D_MODEL), lambda j: (0, 0)),
                  pl.BlockSpec((1, D_MODEL), lambda j: (0, 0))] + [wspec(s) for s in range(8)] + [
            pl.BlockSpec((3, TN), lambda j: (0, j)), pl.BlockSpec((1, TN), lambda j: (0, j)), st],
        out_specs=[col, col, col, col, col, st],
        out_shape=[jax.ShapeDtypeStruct((rows, D_MODEL), BF16), colf, colf, colf, colf,
                   jax.ShapeDtypeStruct((2, rows, D_MODEL), F32)],
        compiler_params=_cparams(("arbitrary",)),
        name="even_front_sample",
    )(x2d, mod, g.reshape(1, D_MODEL), *([w_in] * 8), cw, cb.reshape(1, D_MODEL), prev_t)


def _attn_sample_kernel(slope_ref, pos_ref, q_ref, kn_ref, vn_ref, sgb_ref, kt_ref, vt_ref, yb_ref):
    gw = q_ref.shape[-1]
    row = lax.broadcasted_iota(jnp.int32, (8, gw), 0)
    col = lax.broadcasted_iota(jnp.int32, (8, gw), 1)
    own = (col >= row * HD_B) & (col < (row + 1) * HD_B)
    qm = jnp.where(own, q_ref[...], 0.0)
    s_self = jnp.sum(qm * kn_ref[...], axis=-1, keepdims=True)
    s_all = _dot(qm.astype(BF16), kt_ref[...].astype(BF16))
    s_all = s_all - slope_ref[:, 0:1] * pos_ref[0:1, :]
    v_self = vn_ref[...]
    probs, stats = [], []
    for p in range(len(DILATIONS)):
        s = s_all - pos_ref[p + 1:p + 2, :]
        m = jnp.maximum(jnp.max(s, axis=-1, keepdims=True), s_self)
        pe = jnp.exp(s - m)
        p_self = jnp.exp(s_self - m)
        probs.append(pe)
        stats.append((m, jnp.sum(pe, axis=-1, keepdims=True) + p_self, p_self))
    o_all = _dot_nt(jnp.concatenate(probs, axis=0).astype(BF16), vt_ref[...].astype(BF16))
    outs, lses = [], []
    for p, (m, l, p_self) in enumerate(stats):
        outs.append((o_all[8 * p:8 * p + 8, :] + p_self * v_self) * (1.0 / l))
        lses.append(m + jnp.log(l))
    m = jnp.maximum(jnp.maximum(lses[0], lses[1]), lses[2])
    ws = [jnp.exp(x - m) for x in lses]
    o = (ws[0] * outs[0] + ws[1] * outs[1] + ws[2] * outs[2]) * (1.0 / (ws[0] + ws[1] + ws[2]))
    o = jnp.sum(jnp.where(own, o, 0.0), axis=0, keepdims=True)
    yb_ref[...] = (o * sgb_ref[...]).astype(BF16)


def _attn_sample(q, kn, vn, sgb, cache_kt, cache_vt):
    rows, _, wb = cache_kt.shape
    gh = 4
    gw = gh * HD_B
    ng = H_B // gh
    back = wb - np.arange(wb, dtype=np.float64)
    pos = [back] + [np.where((back % d == 0) & (back <= NK * d), 0.0, MASK_ADD) for d in DILATIONS]
    slopes = (2.0 ** (-8.0 * np.arange(1, H_B + 1) / H_B)).reshape(ng, gh)
    slopes = np.concatenate([slopes, np.zeros((ng, 8 - gh))], axis=1)
    slope_arr = jnp.asarray(np.broadcast_to(slopes[:, :, None], (ng, 8, LANES)), F32)

    def r3(a):
        return a.reshape(rows, 1, D_MODEL)

    row_spec = pl.BlockSpec((None, 1, gw), lambda b, g: (b, 0, g))
    t_spec = pl.BlockSpec((None, gw, wb), lambda b, g: (b, g, 0))
    return pl.pallas_call(
        _attn_sample_kernel,
        grid=(rows, ng),
        in_specs=[pl.BlockSpec((None, 8, LANES), lambda b, g: (g, 0, 0)),
                  pl.BlockSpec((4, wb), lambda b, g: (0, 0)),
                  row_spec, row_spec, row_spec, row_spec, t_spec, t_spec],
        out_specs=row_spec,
        out_shape=jax.ShapeDtypeStruct((rows, 1, D_MODEL), BF16),
        compiler_params=_cparams(("arbitrary", "arbitrary")),
        name="attn_sample",
    )(slope_arr, jnp.asarray(np.stack(pos), F32), r3(q), r3(kn), r3(vn), r3(sgb), cache_kt, cache_vt
      ).reshape(rows, D_MODEL)


def _odd_front_sample_kernel(h_ref, wu, wgc, wq, wk, wv, wgd, pw_ref, ps_ref, prev_ref,
                             yc_ref, q_ref, k_ref, v_ref, sgd_ref, pn_ref):
    j = pl.program_id(0)
    h = h_ref[...]
    u = _dot(h, wu[...])
    n_prev = prev_ref.shape[0]
    sums = []
    s = jnp.zeros_like(u)
    back = 0
    for w in POOL_SIZES:
        while back < w - 1:
            s = s + prev_ref[n_prev - 1 - back]
            back += 1
        sums.append(s)
    win = _pick_by_chunk(j, sums) + u
    inv_w = _pick_by_chunk(j, [1.0 / w for w in POOL_SIZES])
    pooled = win * inv_w - u
    mixed = _dot(pooled.astype(BF16), pw_ref[...]) * ps_ref[...]
    yc_ref[...] = (mixed * _silu(_dot(h, wgc[...]))).astype(BF16)
    q_ref[...] = _dot(h, wq[...])
    k_ref[...] = _dot(h, wk[...]) * (DK_D ** -0.5)
    v_ref[...] = _dot(h, wv[...])
    sgd_ref[...] = _silu(_dot(h, wgd[...]))
    for t in range(n_prev - 1):
        pn_ref[t] = prev_ref[t + 1]
    pn_ref[n_prev - 1] = u


def _odd_front_sample(h2d, w_in, pw, ps, prev_t):
    rows = h2d.shape[0]
    n_prev = prev_t.shape[0]
    nj = D_MODEL // TN

    def wspec(s):
        return pl.BlockSpec((D_MODEL, TN), lambda j, s=s: (0, s * nj + j))

    col = pl.BlockSpec((rows, TN), lambda j: (0, j))
    st = pl.BlockSpec((n_prev, rows, TN), lambda j: (0, 0, j))
    colf = jax.ShapeDtypeStruct((rows, D_MODEL), F32)
    return pl.pallas_call(
        _odd_front_sample_kernel,
        grid=(nj,),
        in_specs=[pl.BlockSpec((rows, D_MODEL), lambda j: (0, 0))] + [wspec(s) for s in range(6)] + [
            pl.BlockSpec((None, TN, TN), lambda j: (j, 0, 0)), pl.BlockSpec((1, TN), lambda j: (0, j)), st],
        out_specs=[col, col, col, col, col, st],
        out_shape=[jax.ShapeDtypeStruct((rows, D_MODEL), BF16), colf, colf, colf, colf,
                   jax.ShapeDtypeStruct((n_prev, rows, D_MODEL), F32)],
        compiler_params=_cparams(("arbitrary",)),
        name="odd_front_sample",
    )(h2d, *([w_in] * 6), pw, ps.reshape(1, D_MODEL), prev_t)


def _ret_sample_kernel(q_ref, k_ref, v_ref, sgd_ref, st_ref, yd_ref, sn_ref):
    row = lax.broadcasted_iota(jnp.int32, (DK_D, DK_D), 0)
    col = lax.broadcasted_iota(jnp.int32, (DK_D, DK_D), 1)
    for h in range(H_D):
        g = float(np.exp(RET_LOG_G[h]))
        cols = slice(h * DK_D, (h + 1) * DK_D)
        q = q_ref[:, cols]
        k = k_ref[:, cols]
        v = v_ref[:, cols]
        state = st_ref[h]
        qk = jnp.sum(q * k, axis=-1, keepdims=True)
        q8 = jnp.broadcast_to(q, (8, DK_D)).astype(BF16)
        cross = _dot(q8, state.astype(BF16))[0:1, :]
        o = qk * v + g * cross
        k_diag = jnp.where(row == col, jnp.broadcast_to(k, (DK_D, DK_D)), 0.0).astype(BF16)
        v_rows = jnp.broadcast_to(v, (DK_D, DK_D)).astype(BF16)
        sn_ref[h] = g * state + _dot(k_diag, v_rows)
        yd_ref[:, cols] = (_rms(o) * sgd_ref[:, cols]).astype(BF16)


def _ret_sample(q, k, v, sgd, state):
    rows = q.shape[0]

    def r3(a):
        return a.reshape(rows, 1, D_MODEL)

    row_spec = pl.BlockSpec((None, 1, D_MODEL), lambda b: (b, 0, 0))
    st_spec = pl.BlockSpec((None, H_D, DK_D, DK_D), lambda b: (b, 0, 0, 0))
    yd, sn = pl.pallas_call(
        _ret_sample_kernel,
        grid=(rows,),
        in_specs=[row_spec, row_spec, row_spec, row_spec, st_spec],
        out_specs=[row_spec, st_spec],
        out_shape=[jax.ShapeDtypeStruct((rows, 1, D_MODEL), BF16),
                   jax.ShapeDtypeStruct((rows, H_D, DK_D, DK_D), F32)],
        compiler_params=_cparams(("arbitrary",)),
        name="ret_sample",
    )(r3(q), r3(k), r3(v), r3(sgd), state)
    return yd.reshape(rows, D_MODEL), sn


def kernel(x_prompt, x_sample, c_prompt, c_sample, state_conv, cache_win_k, cache_win_v, state_pool, state_ret,
           norm_e, ada_w_e, ada_b_e, w_in_e, conv_w, conv_b, w_out_e, norm_o, ada_w_o, ada_b_o, w_in_o,
           pool_w, pool_scale, w_out_o, norm_f):
    batch, seq, d = x_prompt.shape
    sb = x_sample.shape[0]
    assert d == D_MODEL and x_sample.shape[1] == 1
    assert norm_e.shape[0] == 1 and norm_o.shape[0] == 1
    wb = cache_win_k.shape[2]
    keep = min(wb, seq)
    assert wb == DILATIONS[-1] * NK and seq % (DILATIONS[-1] * NK) == 0
    tm = 512

    n_c = batch + sb
    pad = (-n_c) % 8
    c_all = jnp.concatenate([c_prompt, c_sample, jnp.zeros((pad, d), F32)], axis=0)
    mod_e, mod_o = _adaln(c_all, ada_w_e[0], ada_b_e[0], ada_w_o[0], ada_b_o[0])
    mod_e_p, mod_e_s = mod_e[:batch], mod_e[batch:n_c]
    mod_o_p, mod_o_s = mod_o[:batch], mod_o[batch:n_c]

    w_in_e16 = w_in_e[0].astype(BF16)
    w_out_e16 = w_out_e[0].astype(BF16)
    w_in_o16 = w_in_o[0].astype(BF16)
    w_out_o16 = w_out_o[0].astype(BF16)
    pool_w16 = pool_w[0].astype(BF16)

    xp2d = x_prompt.reshape(batch * seq, d)
    ya, q, k, v, sgb, k_new, v_new, conv_tail = _even_front_prompt(
        xp2d, mod_e_p, norm_e[0], w_in_e16, conv_w[0], conv_b[0], batch, seq, keep, tm)
    yb = _attn_prompt(q, k, v, sgb, batch, seq).reshape(batch * seq, d)
    x1, h1 = _outproj(ya, yb, xp2d, mod_e_p, w_out_e16, norm_o[0], mod_o_p, tm, seq)
    yc, rq, rk, rv, sgd, pool_tail = _odd_front_prompt(h1, w_in_o16, pool_w16, pool_scale[0], batch, seq, tm)
    yd, ret_p = _ret_prompt(rq, rk, rv, sgd, batch, seq)
    y_prompt = _outproj(yc, yd.reshape(batch * seq, d), x1, mod_o_p, w_out_o16, norm_f, None, tm, seq)

    xs2d = x_sample.reshape(sb, d)
    conv_prev_t = jnp.transpose(state_conv[0], (1, 0, 2))
    ya_s, q_s, k_s, v_s, sgb_s, conv_s_t = _even_front_sample(
        xs2d, mod_e_s, norm_e[0], w_in_e16, conv_w[0], conv_b[0], conv_prev_t)
    cache_kt = jnp.transpose(cache_win_k[0], (0, 2, 3, 1)).reshape(sb, d, wb)
    cache_vt = jnp.transpose(cache_win_v[0], (0, 2, 3, 1)).reshape(sb, d, wb)
    yb_s = _attn_sample(q_s, k_s, v_s, sgb_s, cache_kt, cache_vt)
    x1_s, h1_s = _outproj(ya_s, yb_s, xs2d, mod_e_s, w_out_e16, norm_o[0], mod_o_s, sb, 1)
    pool_prev_t = jnp.transpose(state_pool[0], (1, 0, 2))
    yc_s, rq_s, rk_s, rv_s, sgd_s, pool_s_t = _odd_front_sample(h1_s, w_in_o16, pool_w16, pool_scale[0], pool_prev_t)
    yd_s, ret_s = _ret_sample(rq_s, rk_s, rv_s, sgd_s, state_ret[0])
    y_sample = _outproj(yc_s, yd_s, x1_s, mod_o_s, w_out_o16, norm_f, None, sb, 1)

    return (
        y_prompt.reshape(batch, seq, d),
        y_sample.reshape(sb, 1, d),
        conv_tail[:, 6:8][None],
        jnp.transpose(conv_s_t, (1, 0, 2))[None],
        jnp.transpose(k_new.reshape(batch, H_B, HD_B, keep), (0, 3, 1, 2))[None],
        k_s.reshape(1, sb, 1, H_B, HD_B),
        jnp.transpose(v_new.reshape(batch, H_B, HD_B, keep), (0, 3, 1, 2))[None],
        v_s.reshape(1, sb, 1, H_B, HD_B),
        pool_tail[:, 1:][None],
        jnp.transpose(pool_s_t, (1, 0, 2))[None],
        ret_p[None],
        ret_s[None],
    )
```

```python
import functools

import numpy as np
import jax
import jax.numpy as jnp
from jax import lax
from jax.experimental import pallas as pl
from jax.experimental.pallas import tpu as pltpu

F32 = jnp.float32
BF16 = jnp.bfloat16
WORD = jnp.uint32

D_MODEL = 1024
EPS = 1e-6
H_B = 16
HD_B = 64
N_PAIR = H_B // 2
LANES = 128
NK = 128
DILATIONS = (1, 4, 16)
POOL_SIZES = (2, 4, 8, 16)
POOL_PREV = 16
H_D = 4
DK_D = 256
RET_CHUNK = 256
TN = 256
TM_FRONT = 1024
TM_OUT = 512
MASK_DIST = 1e9
LOG2E = float(np.log2(np.e))
MASK_ADD = 1e30
VMEM_LIMIT = 56 * 1024 * 1024

RET_LOG_G = [float(np.log(1.0 - 2.0 ** (-5.0 - h))) for h in range(H_D)]


def _cparams(sem):
    return pltpu.CompilerParams(dimension_semantics=sem, vmem_limit_bytes=VMEM_LIMIT)


def _silu(x):
    return x * (1.0 / (1.0 + jnp.exp(-x)))


def _dot(a, b):
    return jnp.dot(a, b, preferred_element_type=F32)


def _dot_nt(a, b):
    return lax.dot_general(a, b, (((1,), (1,)), ((), ())), preferred_element_type=F32)


def _rms(x):
    return x * lax.rsqrt(jnp.mean(x * x, axis=-1, keepdims=True) + EPS)


def _adaln_kernel(c_ref, we_ref, be_ref, wo_ref, bo_ref, me_ref, mo_ref):
    sc = _silu(c_ref[...]).astype(BF16)
    me_ref[...] = _dot(sc, we_ref[...].astype(BF16)) + be_ref[...]
    mo_ref[...] = _dot(sc, wo_ref[...].astype(BF16)) + bo_ref[...]


def _adaln(c_all, we, be, wo, bo):
    rows = c_all.shape[0]
    tn = 512
    n = 3 * D_MODEL
    wspec = pl.BlockSpec((D_MODEL, tn), lambda j: (0, j))
    bspec = pl.BlockSpec((1, tn), lambda j: (0, j))
    ospec = pl.BlockSpec((rows, tn), lambda j: (0, j))
    return pl.pallas_call(
        _adaln_kernel,
        grid=(n // tn,),
        in_specs=[pl.BlockSpec((rows, D_MODEL), lambda j: (0, 0)), wspec, bspec, wspec, bspec],
        out_specs=[ospec, ospec],
        out_shape=[jax.ShapeDtypeStruct((rows, n), F32)] * 2,
        compiler_params=_cparams(("arbitrary",)),
        name="adaln",
    )(c_all, we, be.reshape(1, n), wo, bo.reshape(1, n))


def _norm_mod(x, g, mod):
    shift = mod[:, 0:D_MODEL]
    scale = mod[:, D_MODEL:2 * D_MODEL]
    return _rms(x) * g * (1.0 + scale) + shift


def _shift_rows(u, k, prev_rows):
    row = lax.broadcasted_iota(jnp.int32, u.shape, 0)
    out = pltpu.roll(u, k, 0)
    for idx, pr in enumerate(prev_rows):
        out = jnp.where(row == idx, pr, out)
    return out


N_SEG_E = 8
PHASES_E = 8


def _phase_segments(t, n_seg, n_phases):
    return range(t * n_seg // n_phases, (t + 1) * n_seg // n_phases)


def _even_front_kernel(x_ref, mod_ref, g_ref, *refs, tiles_per_batch, n_chunks):
    ws = refs[:N_SEG_E]
    cw_ref, cb_ref = refs[N_SEG_E:N_SEG_E + 2]
    ya_ref, q_ref, k_ref, v_ref, sgb_ref, kn_ref, vn_ref, cn_ref = refs[N_SEG_E + 2:N_SEG_E + 10]
    h_scr, carry_scr, z_a, z_b = refs[N_SEG_E + 10:]
    nj = D_MODEL // TN
    s = pl.program_id(0)
    c_out = jnp.maximum(s - 1, 0)
    j_out = c_out % nj
    il_out = (c_out // nj) % tiles_per_batch

    @pl.when(s == 0)
    def _():
        z_b[...] = jnp.zeros_like(z_b)
        carry_scr[...] = jnp.zeros_like(carry_scr)

    @pl.when((s < n_chunks) & (s % nj == 0))
    def _():
        h_scr[...] = _norm_mod(x_ref[...], g_ref[...], mod_ref[...]).astype(BF16)

    tm = h_scr.shape[0]
    rb = tm // PHASES_E

    def phase(t, z_w, z_r):
        for seg in _phase_segments(t, N_SEG_E, PHASES_E):
            z_w[seg] = _dot(h_scr[...], ws[seg][...])
        rows = slice(t * rb, (t + 1) * rb)
        bg, cg, xv, ga, q, k, v, gb = (z_r[seg, rows, :] for seg in range(N_SEG_E))
        u = cg * xv
        prev = carry_scr[j_out]
        if t == 0:
            prev = jnp.where(il_out == 0, 0.0, prev)
        p2, p1 = prev[6:7, :], prev[7:8, :]
        u1 = _shift_rows(u, 1, [p1])
        u2 = _shift_rows(u, 2, [p2, p1])
        cw = cw_ref[...]
        conv = cb_ref[...] + cw[0:1, :] * u2 + cw[1:2, :] * u1 + cw[2:3, :] * u
        tail = u[rb - 8:rb, :]
        carry_scr[j_out] = tail
        if t == PHASES_E - 1:
            cn_ref[...] = tail
        ya_ref[rows, :] = (bg * conv * _silu(ga)).astype(BF16)
        kn_ref[:, rows] = k.T
        vn_ref[:, rows] = v.T
        wrows = slice(t * rb // 2, (t + 1) * rb // 2)
        for ref, val in ((q_ref, q * (HD_B ** -0.5 * LOG2E)),
                         (k_ref, k), (v_ref, v)):
            words = pltpu.bitcast(val.astype(BF16), WORD)
            for e in range(TN // LANES):
                ref[e, wrows, :] = words[:, e * LANES:(e + 1) * LANES]
        sgb = _silu(gb).astype(BF16)
        for e in range(TN // LANES):
            sgb_ref[e, rows, :] = sgb[:, e * LANES:(e + 1) * LANES]

    for parity, (z_w, z_r) in enumerate(((z_a, z_b), (z_b, z_a))):
        for t in range(PHASES_E):
            pl.when(s % 2 == parity)(functools.partial(phase, t, z_w, z_r))


def _even_front_prompt(x2d, mod, g, w_in, cw, cb, batch, seq, keep, tm):
    rows = batch * seq
    tpb = seq // tm
    nj = D_MODEL // TN
    n_chunks = (rows // tm) * nj
    off = (seq - keep) // tm
    ppc = TN // LANES

    def c_in(s):
        return jnp.minimum(s, n_chunks - 1)

    def c_out(s):
        return jnp.maximum(s - 1, 0)

    def out_ij(s):
        c = c_out(s)
        return c // nj, c % nj

    def pair_map(s):
        i, j = out_ij(s)
        return (i // tpb, j, i % tpb, 0)

    def keep_map(s):
        i, j = out_ij(s)
        il = i % tpb
        kept = il >= off
        return (i // tpb, jnp.where(kept, j, 0), jnp.where(kept, il - off, 0))

    def tail_map(s):
        i, j = out_ij(s)
        return (i // tpb, 0, jnp.where(i % tpb == tpb - 1, j, 0))

    pair_spec = pl.BlockSpec((None, ppc, tm, LANES), pair_map)
    pair_shape = jax.ShapeDtypeStruct((batch, N_PAIR, seq, LANES), BF16)
    word_spec = pl.BlockSpec((None, ppc, tm // 2, LANES), pair_map)
    word_shape = jax.ShapeDtypeStruct((batch, N_PAIR, seq // 2, LANES), WORD)
    keep_spec = pl.BlockSpec((None, TN, tm), keep_map)
    keep_shape = jax.ShapeDtypeStruct((batch, D_MODEL, keep), F32)
    return pl.pallas_call(
        functools.partial(_even_front_kernel, tiles_per_batch=tpb, n_chunks=n_chunks),
        grid=(n_chunks + 1,),
        in_specs=[
            pl.BlockSpec((tm, D_MODEL), lambda s: (c_in(s) // nj, 0)),
            pl.BlockSpec((None, 1, 3 * D_MODEL), lambda s: (c_in(s) // nj // tpb, 0, 0)),
            pl.BlockSpec((1, D_MODEL), lambda s: (0, 0)),
        ] + [pl.BlockSpec((D_MODEL, TN), lambda s, seg=seg: (0, seg * nj + c_in(s) % nj)) for seg in range(N_SEG_E)] + [
            pl.BlockSpec((3, TN), lambda s: (0, c_out(s) % nj)),
            pl.BlockSpec((1, TN), lambda s: (0, c_out(s) % nj)),
        ],
        out_specs=[
            pl.BlockSpec((tm, TN), lambda s: out_ij(s)),
            word_spec, word_spec, word_spec, pair_spec,
            keep_spec, keep_spec,
            pl.BlockSpec((None, 8, TN), tail_map),
        ],
        out_shape=[
            jax.ShapeDtypeStruct((rows, D_MODEL), BF16),
            word_shape, word_shape, word_shape, pair_shape,
            keep_shape, keep_shape,
            jax.ShapeDtypeStruct((batch, 8, D_MODEL), F32),
        ],
        scratch_shapes=[pltpu.VMEM((tm, D_MODEL), BF16), pltpu.VMEM((nj, 8, TN), F32),
                        pltpu.VMEM((N_SEG_E, tm, TN), F32), pltpu.VMEM((N_SEG_E, tm, TN), F32)],
        compiler_params=_cparams(("arbitrary",)),
        name="even_front_prompt",
    )(x2d, mod.reshape(batch, 1, 3 * D_MODEL), g.reshape(1, D_MODEL), *([w_in] * N_SEG_E), cw, cb.reshape(1, D_MODEL))


ATTN_UNROLL = 16


def _residue_stream(src, r):
    n_out = src.shape[0] // 4
    halves = [pltpu.unpack_elementwise(src[pl.ds(r // 2 + off, n_out, stride=4), :], index=r % 2,
                                       packed_dtype=BF16, unpacked_dtype=F32) for off in (0, 2)]
    return pltpu.pack_elementwise(halves, packed_dtype=BF16)


def _attn_prompt_kernel(slope_ref, dist_ref, q_ref, k_ref, v_ref, sgb_ref, yb_ref,
                        q4_scr, q16_scr, kv4_scr, kv16_scr, acc_scr, m_scr, l_scr, bias_scr, *, seq):
    hp = pl.program_id(1)
    lane = lax.broadcasted_iota(jnp.int32, (1, LANES), 1)
    half0 = lane < HD_B
    dist = dist_ref[...]
    masked = jnp.full((NK, NK), MASK_DIST, F32)
    for p, d in enumerate(DILATIONS):
        for e in range(2):
            bias = dist * (slope_ref[2 * hp + e] * (float(d) * LOG2E))
            bias_scr[2 * p + e] = bias
            bias_scr[6 + 2 * p + e] = jnp.concatenate([bias[:, NK:], masked], axis=1)

    for a, ref in enumerate((q_ref, k_ref, v_ref)):
        for r in range(4):
            s = _residue_stream(ref, r)
            if a == 0:
                q4_scr[r] = s
            else:
                kv4_scr[a - 1, r] = s
        for r in range(16):
            s = _residue_stream(q4_scr.at[r % 4] if a == 0 else kv4_scr.at[a - 1, r % 4], r // 4)
            if a == 0:
                q16_scr[r] = s
            else:
                kv16_scr[a - 1, r] = s

    ones = jnp.ones((2 * NK, LANES), BF16)
    hw = NK // 2

    def tile(p, d, r, n):
        qw = pl.ds(pl.multiple_of(n * hw, hw), hw)
        kw = pl.ds(pl.multiple_of(jnp.maximum(n - 1, 0) * hw, hw), 2 * hw)
        if d == 1:
            qs, ks, vs = q_ref, k_ref, v_ref
        elif d == 4:
            qs, ks, vs = q4_scr.at[r], kv4_scr.at[0, r], kv4_scr.at[1, r]
        else:
            qs, ks, vs = q16_scr.at[r], kv16_scr.at[0, r], kv16_scr.at[1, r]
        q = pltpu.bitcast(qs[qw, :], BF16)
        kk = pltpu.bitcast(ks[kw, :], BF16)
        vv1 = jnp.concatenate([pltpu.bitcast(vs[kw, :], BF16), ones], axis=1)
        first = jnp.where(n == 0, 1, 0)
        res = []
        for e in range(2):
            qe = jnp.where(half0 if e == 0 else jnp.logical_not(half0), q, jnp.zeros_like(q))
            s = _dot_nt(qe, kk) - bias_scr[6 * first + 2 * p + e]
            m = jnp.max(s, axis=-1, keepdims=True)
            pe = jnp.exp2(s - m).astype(BF16)
            res.append((_dot(pe, vv1), m))
        (a0, m0), (a1, m1) = res
        rows = pl.ds(r + d * NK * n, NK, stride=d) if d > 1 else pl.ds(pl.multiple_of(n * NK, NK), NK)
        acc_scr[p, rows, :] = jnp.where(half0, a0[:, :LANES], a1[:, :LANES])
        l_scr[p, rows, :] = jnp.where(half0, a0[:, LANES:], a1[:, LANES:])
        m_scr[p, rows, :] = jnp.where(half0, m0, m1)

    for p, d in enumerate(DILATIONS):
        def body(it, c, p=p, d=d):
            for u in range(ATTN_UNROLL):
                t = it * ATTN_UNROLL + u
                if d == 1:
                    tile(p, d, 0, t)
                else:
                    tile(p, d, t % d, t // d)
            return c

        lax.fori_loop(0, seq // NK // ATTN_UNROLL, body, 0)

    rc = 512
    for c in range(seq // rc):
        rows = pl.ds(c * rc, rc)
        ms = [m_scr[p, rows, :] for p in range(3)]
        m = jnp.maximum(jnp.maximum(ms[0], ms[1]), ms[2])
        ws = [jnp.exp2(x - m) for x in ms]
        num = ws[0] * acc_scr[0, rows, :] + ws[1] * acc_scr[1, rows, :] + ws[2] * acc_scr[2, rows, :]
        den = ws[0] * l_scr[0, rows, :] + ws[1] * l_scr[1, rows, :] + ws[2] * l_scr[2, rows, :]
        yb_ref[rows, :] = (num * (1.0 / den) * sgb_ref[rows, :].astype(F32)).astype(BF16)


def _band_distance():
    qi = np.arange(NK)[:, None]
    kj = np.arange(2 * NK)[None, :]
    dist = NK + qi - kj
    return jnp.asarray(np.where((dist >= 0) & (dist <= NK), dist, MASK_DIST), dtype=F32)


def _alibi_slopes():
    return jnp.asarray(2.0 ** (-8.0 * np.arange(1, H_B + 1) / H_B), dtype=F32)


def _attn_prompt(q, k, v, sgb, batch, seq):
    word_spec = pl.BlockSpec((None, None, seq // 2, LANES), lambda b, hp: (b, hp, 0, 0))
    stat = pltpu.VMEM((len(DILATIONS), seq, LANES), F32)
    return pl.pallas_call(
        functools.partial(_attn_prompt_kernel, seq=seq),
        grid=(batch, N_PAIR),
        in_specs=[pl.BlockSpec(memory_space=pltpu.SMEM),
                  pl.BlockSpec((NK, 2 * NK), lambda b, hp: (0, 0)),
                  word_spec, word_spec, word_spec,
                  pl.BlockSpec((None, None, seq, LANES), lambda b, hp: (b, hp, 0, 0))],
        out_specs=pl.BlockSpec((None, seq, LANES), lambda b, hp: (b, 0, hp)),
        out_shape=jax.ShapeDtypeStruct((batch, seq, D_MODEL), BF16),
        scratch_shapes=[pltpu.VMEM((4, seq // 8, LANES), WORD),
                        pltpu.VMEM((16, seq // 32, LANES), WORD),
                        pltpu.VMEM((2, 4, seq // 8, LANES), WORD),
                        pltpu.VMEM((2, 16, seq // 32, LANES), WORD),
                        stat, stat, stat,
                        pltpu.VMEM((12, NK, 2 * NK), F32)],
        compiler_params=_cparams(("arbitrary", "arbitrary")),
        name="attn_prompt",
    )(_alibi_slopes(), _band_distance(), q, k, v, sgb)


def _outproj_mid_kernel(ya_ref, yb_ref, x_ref, mod_ref, w1_ref, w2_ref, g_ref, mod2_ref, x1_ref, h_ref):
    gate = mod_ref[:, 2 * D_MODEL:3 * D_MODEL]
    x1 = x_ref[...] + gate * (_dot(ya_ref[...], w1_ref[...]) + _dot(yb_ref[...], w2_ref[...]))
    x1_ref[...] = x1
    h_ref[...] = _norm_mod(x1, g_ref[...], mod2_ref[...]).astype(BF16)


def _outproj_final_kernel(ya_ref, yb_ref, x_ref, mod_ref, w1_ref, w2_ref, g_ref, y_ref):
    gate = mod_ref[:, 2 * D_MODEL:3 * D_MODEL]
    x1 = x_ref[...] + gate * (_dot(ya_ref[...], w1_ref[...]) + _dot(yb_ref[...], w2_ref[...]))
    y_ref[...] = _rms(x1) * g_ref[...]


def _outproj(ya, yb, x2d, mod, w_out, g, mod2, tm, rows_per_mod):
    rows = x2d.shape[0]
    half = w_out.shape[0] // 2
    row_spec = pl.BlockSpec((tm, D_MODEL), lambda i: (i, 0))
    if rows_per_mod == 1:
        mod_spec = pl.BlockSpec((tm, 3 * D_MODEL), lambda i: (i, 0))
        mods = (mod, mod2)
    else:
        tpb = rows_per_mod // tm
        mod_spec = pl.BlockSpec((None, 1, 3 * D_MODEL), lambda i: (i // tpb, 0, 0))
        mods = tuple(None if m is None else m.reshape(-1, 1, 3 * D_MODEL) for m in (mod, mod2))
    w1_spec = pl.BlockSpec((half, D_MODEL), lambda i: (0, 0))
    w2_spec = pl.BlockSpec((half, D_MODEL), lambda i: (1, 0))
    g_spec = pl.BlockSpec((1, D_MODEL), lambda i: (0, 0))
    common = dict(grid=(rows // tm,), compiler_params=_cparams(("arbitrary",)))
    if mod2 is None:
        return pl.pallas_call(
            _outproj_final_kernel,
            in_specs=[row_spec, row_spec, row_spec, mod_spec, w1_spec, w2_spec, g_spec],
            out_specs=row_spec,
            out_shape=jax.ShapeDtypeStruct((rows, D_MODEL), F32),
            name="outproj_final", **common,
        )(ya, yb, x2d, mods[0], w_out, w_out, g.reshape(1, D_MODEL))
    return pl.pallas_call(
        _outproj_mid_kernel,
        in_specs=[row_spec, row_spec, row_spec, mod_spec, w1_spec, w2_spec, g_spec, mod_spec],
        out_specs=[row_spec, row_spec],
        out_shape=[jax.ShapeDtypeStruct((rows, D_MODEL), F32), jax.ShapeDtypeStruct((rows, D_MODEL), BF16)],
        name="outproj_mid", **common,
    )(ya, yb, x2d, mods[0], w_out, w_out, g.reshape(1, D_MODEL), mods[1])


def _pick_by_chunk(j, vals):
    out = vals[-1]
    for idx in range(len(vals) - 2, -1, -1):
        out = jnp.where(j == idx, vals[idx], out)
    return out


N_SEG_O = 6
PHASES_O = 4


def _odd_front_kernel(h_ref, *refs, tiles_per_batch):
    ws = refs[:N_SEG_O]
    pw_ref, ps_ref = refs[N_SEG_O:N_SEG_O + 2]
    yc_ref, q_ref, kt_ref, v_ref, sgd_ref, pn_ref = refs[N_SEG_O + 2:N_SEG_O + 8]
    carry_scr, z_a, z_b = refs[N_SEG_O + 8:]
    nj = D_MODEL // TN
    s = pl.program_id(0)
    c_out = jnp.maximum(s - 1, 0)
    j_out = c_out % nj
    il_out = (c_out // nj) % tiles_per_batch
    tm = h_ref.shape[0]
    rb = tm // PHASES_O

    @pl.when(s == 0)
    def _():
        z_b[...] = jnp.zeros_like(z_b)
        carry_scr[...] = jnp.zeros_like(carry_scr)

    def phase(t, z_w, z_r):
        for seg in _phase_segments(t, N_SEG_O, PHASES_O):
            z_w[seg] = _dot(h_ref[...], ws[seg][...])
        rows = slice(t * rb, (t + 1) * rb)
        u, gc, q, k, v, gd = (z_r[seg, rows, :] for seg in range(N_SEG_O))
        prev = carry_scr[j_out]
        if t == 0:
            prev = jnp.where(il_out == 0, 0.0, prev)
        tail = u[rb - POOL_PREV:rb, :]
        carry_scr[j_out] = tail
        if t == PHASES_O - 1:
            pn_ref[...] = tail
        ext = jnp.concatenate([prev, u], axis=0)
        sums = []
        acc = ext
        for sh in (1, 2, 4, 8):
            acc = acc + pltpu.roll(acc, sh, 0)
            sums.append(acc[POOL_PREV:, :])
        win = _pick_by_chunk(j_out, sums)
        width = _pick_by_chunk(j_out, [float(w) for w in POOL_SIZES])
        pos = (il_out * tm + t * rb + lax.broadcasted_iota(jnp.int32, (rb, 1), 0)).astype(F32)
        pooled = win / jnp.minimum(width, pos + 1.0) - u
        mixed = _dot(pooled.astype(BF16), pw_ref[...]) * ps_ref[...]
        yc_ref[rows, :] = (mixed * _silu(gc)).astype(BF16)
        q_ref[rows, :] = q.astype(BF16)
        kt_ref[:, rows] = (k * (DK_D ** -0.5)).T.astype(BF16)
        v_ref[rows, :] = v.astype(BF16)
        sgd_ref[rows, :] = _silu(gd).astype(BF16)

    for parity, (z_w, z_r) in enumerate(((z_a, z_b), (z_b, z_a))):
        for t in range(PHASES_O):
            pl.when(s % 2 == parity)(functools.partial(phase, t, z_w, z_r))


def _odd_front_prompt(h2d, w_in, pw, ps, batch, seq, tm):
    rows = batch * seq
    tpb = seq // tm
    nj = D_MODEL // TN
    n_chunks = (rows // tm) * nj

    def c_in(s):
        return jnp.minimum(s, n_chunks - 1)

    def out_ij(s):
        c = jnp.maximum(s - 1, 0)
        return c // nj, c % nj

    def head_map(s):
        i, j = out_ij(s)
        return (i // tpb, j, i % tpb, 0)

    def kt_map(s):
        i, j = out_ij(s)
        return (i // tpb, j, 0, i % tpb)

    def tail_map(s):
        i, j = out_ij(s)
        return (i // tpb, 0, jnp.where(i % tpb == tpb - 1, j, 0))

    head_spec = pl.BlockSpec((None, None, tm, TN), head_map)
    head_shape = jax.ShapeDtypeStruct((batch, H_D, seq, DK_D), BF16)
    tok_spec = pl.BlockSpec((tm, TN), out_ij)
    tok_shape = jax.ShapeDtypeStruct((rows, D_MODEL), BF16)
    zbuf = pltpu.VMEM((N_SEG_O, tm, TN), F32)
    return pl.pallas_call(
        functools.partial(_odd_front_kernel, tiles_per_batch=tpb),
        grid=(n_chunks + 1,),
        in_specs=[pl.BlockSpec((tm, D_MODEL), lambda s: (c_in(s) // nj, 0))] + [
                  pl.BlockSpec((D_MODEL, TN), lambda s, seg=seg: (0, seg * nj + c_in(s) % nj))
                  for seg in range(N_SEG_O)] + [
                  pl.BlockSpec((None, TN, TN), lambda s: (out_ij(s)[1], 0, 0)),
                  pl.BlockSpec((1, TN), lambda s: (0, out_ij(s)[1]))],
        out_specs=[tok_spec, head_spec, pl.BlockSpec((None, None, TN, tm), kt_map), head_spec, tok_spec,
                   pl.BlockSpec((None, POOL_PREV, TN), tail_map)],
        out_shape=[tok_shape, head_shape, jax.ShapeDtypeStruct((batch, H_D, DK_D, seq), BF16),
                   head_shape, tok_shape,
                   jax.ShapeDtypeStruct((batch, POOL_PREV, D_MODEL), F32)],
        scratch_shapes=[pltpu.VMEM((nj, POOL_PREV, TN), F32), zbuf, zbuf],
        compiler_params=_cparams(("arbitrary",)),
        name="odd_front_prompt",
    )(h2d, *([w_in] * N_SEG_O), pw, ps.reshape(1, D_MODEL))


RET_SBLK = 1024


def _ret_prompt_kernel(q_ref, kt_ref, v_ref, sgd_ref, dec_ref, cdec_ref, kdec_ref,
                       yd_ref, st_ref, state_scr, *, sblk, n_sblk):
    sb = pl.program_id(1)

    @pl.when(sb == 0)
    def _():
        state_scr[...] = jnp.zeros_like(state_scr)

    def chunk(c, carry):
        rows = pl.ds(pl.multiple_of(c * RET_CHUNK, RET_CHUNK), RET_CHUNK)
        for h in range(H_D):
            cols = slice(h * DK_D, (h + 1) * DK_D)
            q = q_ref[h, rows, :]
            kt = kt_ref[h, :, rows]
            v = v_ref[h, rows, :]
            state = state_scr[h]
            scores = _dot(q, kt) * dec_ref[h]
            o = _dot(scores.astype(BF16), v) + _dot(q, state.astype(BF16)) * cdec_ref[h]
            kd = (kt.astype(F32) * kdec_ref[h]).astype(BF16)
            state_scr[h] = float(np.exp(RET_CHUNK * RET_LOG_G[h])) * state + _dot(kd, v)
            yd_ref[rows, cols] = (_rms(o) * sgd_ref[rows, cols].astype(F32)).astype(BF16)
        return carry

    lax.fori_loop(0, sblk // RET_CHUNK, chunk, 0, unroll=4)

    @pl.when(sb == n_sblk - 1)
    def _():
        st_ref[...] = state_scr[...]


def _ret_consts():
    t = np.arange(RET_CHUNK, dtype=np.float64)
    diff = t[:, None] - t[None, :]
    lg = np.asarray(RET_LOG_G)[:, None, None]
    dec = np.where(diff >= 0, np.exp(np.maximum(diff, 0.0)[None] * lg), 0.0)
    cdec = np.broadcast_to(np.exp((t + 1.0)[None, :, None] * lg), (H_D, RET_CHUNK, DK_D))
    kdec = np.broadcast_to(np.exp((RET_CHUNK - 1.0 - t)[None, None, :] * lg), (H_D, DK_D, RET_CHUNK))
    return jnp.asarray(dec, F32), jnp.asarray(cdec, F32), jnp.asarray(kdec, F32)


def _ret_prompt(q, kt, v, sgd, batch, seq):
    dec, cdec, kdec = _ret_consts()
    n_sblk = seq // RET_SBLK
    head_spec = pl.BlockSpec((None, H_D, RET_SBLK, DK_D), lambda b, s: (b, 0, s, 0))
    tok_spec = pl.BlockSpec((None, RET_SBLK, D_MODEL), lambda b, s: (b, s, 0))

    def const_spec(shape):
        return pl.BlockSpec(shape, lambda b, s: (0, 0, 0))

    return pl.pallas_call(
        functools.partial(_ret_prompt_kernel, sblk=RET_SBLK, n_sblk=n_sblk),
        grid=(batch, n_sblk),
        in_specs=[head_spec, pl.BlockSpec((None, H_D, DK_D, RET_SBLK), lambda b, s: (b, 0, 0, s)), head_spec,
                  tok_spec, const_spec(dec.shape), const_spec(cdec.shape), const_spec(kdec.shape)],
        out_specs=[tok_spec, pl.BlockSpec((None, H_D, DK_D, DK_D), lambda b, s: (b, 0, 0, 0))],
        out_shape=[jax.ShapeDtypeStruct((batch, seq, D_MODEL), BF16),
                   jax.ShapeDtypeStruct((batch, H_D, DK_D, DK_D), F32)],
        scratch_shapes=[pltpu.VMEM((H_D, DK_D, DK_D), F32)],
        compiler_params=_cparams(("arbitrary", "arbitrary")),
        name="ret_prompt",
    )(q, kt, v, sgd.reshape(batch, seq, D_MODEL), dec, cdec, kdec)


def _even_front_sample_kernel(x_ref, mod_ref, g_ref, wbg, wcg, wxv, wga, wq, wk, wv, wgb, cw_ref, cb_ref,
                              prev_ref, ya_ref, q_ref, k_ref, v_ref, sgb_ref, cn_ref):
    h = _norm_mod(x_ref[...], g_ref[...], mod_ref[...]).astype(BF16)
    u = _dot(h, wcg[...]) * _dot(h, wxv[...])
    cw = cw_ref[...]
    conv = cb_ref[...] + cw[0:1, :] * prev_ref[0] + cw[1:2, :] * prev_ref[1] + cw[2:3, :] * u
    cn_ref[0] = prev_ref[1]
    cn_ref[1] = u
    ya_ref[...] = (_dot(h, wbg[...]) * conv * _silu(_dot(h, wga[...]))).astype(BF16)
    q_ref[...] = _dot(h, wq[...]) * (HD_B ** -0.5)
    k_ref[...] = _dot(h, wk[...])
    v_ref[...] = _dot(h, wv[...])
    sgb_ref[...] = _silu(_dot(h, wgb[...]))


def _even_front_sample(x2d, mod, g, w_in, cw, cb, prev_t):
    rows = x2d.shape[0]
    nj = D_MODEL // TN

    def wspec(s):
        return pl.BlockSpec((D_MODEL, TN), lambda j, s=s: (0, s * nj + j))

    full = pl.BlockSpec((rows, D_MODEL), lambda j: (0, 0))
    col = pl.BlockSpec((rows, TN), lambda j: (0, j))
    st = pl.BlockSpec((2, rows, TN), lambda j: (0, 0, j))
    colf = jax.ShapeDtypeStruct((rows, D_MODEL), F32)
    return pl.pallas_call(
        _even_front_sample_kernel,
        grid=(nj,),
        in_specs=[full, pl.BlockSpec((rows, 3 * D_MODEL), lambda j: (0, 0)),
                  pl.BlockSpec((1, D_MODEL), lambda j: (0, 0))] + [wspec(s) for s in range(8)] + [
            pl.BlockSpec((3, TN), lambda j: (0, j)), pl.BlockSpec((1, TN), lambda j: (0, j)), st],
        out_specs=[col, col, col, col, col, st],
        out_shape=[jax.ShapeDtypeStruct((rows, D_MODEL), BF16), colf, colf, colf, colf,
                   jax.ShapeDtypeStruct((2, rows, D_MODEL), F32)],
        compiler_params=_cparams(("arbitrary",)),
        name="even_front_sample",
    )(x2d, mod, g.reshape(1, D_MODEL), *([w_in] * 8), cw, cb.reshape(1, D_MODEL), prev_t)


def _attn_sample_kernel(slope_ref, pos_ref, q_ref, kn_ref, vn_ref, sgb_ref, kt_ref, vt_ref, yb_ref):
    gw = q_ref.shape[-1]
    row = lax.broadcasted_iota(jnp.int32, (8, gw), 0)
    col = lax.broadcasted_iota(jnp.int32, (8, gw), 1)
    own = (col >= row * HD_B) & (col < (row + 1) * HD_B)
    qm = jnp.where(own, q_ref[...], 0.0)
    s_self = jnp.sum(qm * kn_ref[...], axis=-1, keepdims=True)
    s_all = _dot(qm.astype(BF16), kt_ref[...].astype(BF16))
    s_all = s_all - slope_ref[:, 0:1] * pos_ref[0:1, :]
    v_self = vn_ref[...]
    probs, stats = [], []
    for p in range(len(DILATIONS)):
        s = s_all - pos_ref[p + 1:p + 2, :]
        m = jnp.maximum(jnp.max(s, axis=-1, keepdims=True), s_self)
        pe = jnp.exp(s - m)
        p_self = jnp.exp(s_self - m)
        probs.append(pe)
        stats.append((m, jnp.sum(pe, axis=-1, keepdims=True) + p_self, p_self))
    o_all = _dot_nt(jnp.concatenate(probs, axis=0).astype(BF16), vt_ref[...].astype(BF16))
    outs, lses = [], []
    for p, (m, l, p_self) in enumerate(stats):
        outs.append((o_all[8 * p:8 * p + 8, :] + p_self * v_self) * (1.0 / l))
        lses.append(m + jnp.log(l))
    m = jnp.maximum(jnp.maximum(lses[0], lses[1]), lses[2])
    ws = [jnp.exp(x - m) for x in lses]
    o = (ws[0] * outs[0] + ws[1] * outs[1] + ws[2] * outs[2]) * (1.0 / (ws[0] + ws[1] + ws[2]))
    o = jnp.sum(jnp.where(own, o, 0.0), axis=0, keepdims=True)
    yb_ref[...] = (o * sgb_ref[...]).astype(BF16)


def _attn_sample(q, kn, vn, sgb, cache_kt, cache_vt):
    rows, _, wb = cache_kt.shape
    gh = 8
    gw = gh * HD_B
    ng = H_B // gh
    back = wb - np.arange(wb, dtype=np.float64)
    pos = [back] + [np.where((back % d == 0) & (back <= NK * d), 0.0, MASK_ADD) for d in DILATIONS]
    slopes = (2.0 ** (-8.0 * np.arange(1, H_B + 1) / H_B)).reshape(ng, gh)
    slopes = np.concatenate([slopes, np.zeros((ng, 8 - gh))], axis=1)
    slope_arr = jnp.asarray(np.broadcast_to(slopes[:, :, None], (ng, 8, LANES)), F32)

    def r3(a):
        return a.reshape(rows, 1, D_MODEL)

    row_spec = pl.BlockSpec((None, 1, gw), lambda b, g: (b, 0, g))
    t_spec = pl.BlockSpec((None, gw, wb), lambda b, g: (b, g, 0))
    return pl.pallas_call(
        _attn_sample_kernel,
        grid=(rows, ng),
        in_specs=[pl.BlockSpec((None, 8, LANES), lambda b, g: (g, 0, 0)),
                  pl.BlockSpec((4, wb), lambda b, g: (0, 0)),
                  row_spec, row_spec, row_spec, row_spec, t_spec, t_spec],
        out_specs=row_spec,
        out_shape=jax.ShapeDtypeStruct((rows, 1, D_MODEL), BF16),
        compiler_params=_cparams(("arbitrary", "arbitrary")),
        name="attn_sample",
    )(slope_arr, jnp.asarray(np.stack(pos), F32), r3(q), r3(kn), r3(vn), r3(sgb), cache_kt, cache_vt
      ).reshape(rows, D_MODEL)


def _odd_front_sample_kernel(h_ref, wu, wgc, wq, wk, wv, wgd, pw_ref, ps_ref, prev_ref,
                             yc_ref, q_ref, k_ref, v_ref, sgd_ref, pn_ref):
    j = pl.program_id(0)
    h = h_ref[...]
    u = _dot(h, wu[...])
    n_prev = prev_ref.shape[0]
    sums = []
    s = jnp.zeros_like(u)
    back = 0
    for w in POOL_SIZES:
        while back < w - 1:
            s = s + prev_ref[n_prev - 1 - back]
            back += 1
        sums.append(s)
    win = _pick_by_chunk(j, sums) + u
    inv_w = _pick_by_chunk(j, [1.0 / w for w in POOL_SIZES])
    pooled = win * inv_w - u
    mixed = _dot(pooled.astype(BF16), pw_ref[...]) * ps_ref[...]
    yc_ref[...] = (mixed * _silu(_dot(h, wgc[...]))).astype(BF16)
    q_ref[...] = _dot(h, wq[...])
    k_ref[...] = _dot(h, wk[...]) * (DK_D ** -0.5)
    v_ref[...] = _dot(h, wv[...])
    sgd_ref[...] = _silu(_dot(h, wgd[...]))
    for t in range(n_prev - 1):
        pn_ref[t] = prev_ref[t + 1]
    pn_ref[n_prev - 1] = u


def _odd_front_sample(h2d, w_in, pw, ps, prev_t):
    rows = h2d.shape[0]
    n_prev = prev_t.shape[0]
    nj = D_MODEL // TN

    def wspec(s):
        return pl.BlockSpec((D_MODEL, TN), lambda j, s=s: (0, s * nj + j))

    col = pl.BlockSpec((rows, TN), lambda j: (0, j))
    st = pl.BlockSpec((n_prev, rows, TN), lambda j: (0, 0, j))
    colf = jax.ShapeDtypeStruct((rows, D_MODEL), F32)
    return pl.pallas_call(
        _odd_front_sample_kernel,
        grid=(nj,),
        in_specs=[pl.BlockSpec((rows, D_MODEL), lambda j: (0, 0))] + [wspec(s) for s in range(6)] + [
            pl.BlockSpec((None, TN, TN), lambda j: (j, 0, 0)), pl.BlockSpec((1, TN), lambda j: (0, j)), st],
        out_specs=[col, col, col, col, col, st],
        out_shape=[jax.ShapeDtypeStruct((rows, D_MODEL), BF16), colf, colf, colf, colf,
                   jax.ShapeDtypeStruct((n_prev, rows, D_MODEL), F32)],
        compiler_params=_cparams(("arbitrary",)),
        name="odd_front_sample",
    )(h2d, *([w_in] * 6), pw, ps.reshape(1, D_MODEL), prev_t)


def _ret_sample_kernel(q_ref, k_ref, v_ref, sgd_ref, st_ref, yd_ref, sn_ref):
    row = lax.broadcasted_iota(jnp.int32, (DK_D, DK_D), 0)
    col = lax.broadcasted_iota(jnp.int32, (DK_D, DK_D), 1)
    for h in range(H_D):
        g = float(np.exp(RET_LOG_G[h]))
        cols = slice(h * DK_D, (h + 1) * DK_D)
        q = q_ref[:, cols]
        k = k_ref[:, cols]
        v = v_ref[:, cols]
        state = st_ref[h]
        qk = jnp.sum(q * k, axis=-1, keepdims=True)
        q8 = jnp.broadcast_to(q, (8, DK_D)).astype(BF16)
        cross = _dot(q8, state.astype(BF16))[0:1, :]
        o = qk * v + g * cross
        k_diag = jnp.where(row == col, jnp.broadcast_to(k, (DK_D, DK_D)), 0.0).astype(BF16)
        v_rows = jnp.broadcast_to(v, (DK_D, DK_D)).astype(BF16)
        sn_ref[h] = g * state + _dot(k_diag, v_rows)
        yd_ref[:, cols] = (_rms(o) * sgd_ref[:, cols]).astype(BF16)


def _ret_sample(q, k, v, sgd, state):
    rows = q.shape[0]

    def r3(a):
        return a.reshape(rows, 1, D_MODEL)

    row_spec = pl.BlockSpec((None, 1, D_MODEL), lambda b: (b, 0, 0))
    st_spec = pl.BlockSpec((None, H_D, DK_D, DK_D), lambda b: (b, 0, 0, 0))
    yd, sn = pl.pallas_call(
        _ret_sample_kernel,
        grid=(rows,),
        in_specs=[row_spec, row_spec, row_spec, row_spec, st_spec],
        out_specs=[row_spec, st_spec],
        out_shape=[jax.ShapeDtypeStruct((rows, 1, D_MODEL), BF16),
                   jax.ShapeDtypeStruct((rows, H_D, DK_D, DK_D), F32)],
        compiler_params=_cparams(("arbitrary",)),
        name="ret_sample",
    )(r3(q), r3(k), r3(v), r3(sgd), state)
    return yd.reshape(rows, D_MODEL), sn


def kernel(x_prompt, x_sample, c_prompt, c_sample, state_conv, cache_win_k, cache_win_v, state_pool, state_ret,
           norm_e, ada_w_e, ada_b_e, w_in_e, conv_w, conv_b, w_out_e, norm_o, ada_w_o, ada_b_o, w_in_o,
           pool_w, pool_scale, w_out_o, norm_f):
    batch, seq, d = x_prompt.shape
    sb = x_sample.shape[0]
    assert d == D_MODEL and x_sample.shape[1] == 1
    assert norm_e.shape[0] == 1 and norm_o.shape[0] == 1
    wb = cache_win_k.shape[2]
    keep = min(wb, seq)
    assert wb == DILATIONS[-1] * NK and seq % (DILATIONS[-1] * NK) == 0
    tm = TM_OUT

    n_c = batch + sb
    pad = (-n_c) % 8
    c_all = jnp.concatenate([c_prompt, c_sample, jnp.zeros((pad, d), F32)], axis=0)
    mod_e, mod_o = _adaln(c_all, ada_w_e[0], ada_b_e[0], ada_w_o[0], ada_b_o[0])
    mod_e_p, mod_e_s = mod_e[:batch], mod_e[batch:n_c]
    mod_o_p, mod_o_s = mod_o[:batch], mod_o[batch:n_c]

    w_in_e16 = w_in_e[0].astype(BF16)
    w_out_e16 = w_out_e[0].astype(BF16)
    w_in_o16 = w_in_o[0].astype(BF16)
    w_out_o16 = w_out_o[0].astype(BF16)
    pool_w16 = pool_w[0].astype(BF16)

    xp2d = x_prompt.reshape(batch * seq, d)
    ya, q, k, v, sgb, k_new, v_new, conv_tail = _even_front_prompt(
        xp2d, mod_e_p, norm_e[0], w_in_e16, conv_w[0], conv_b[0], batch, seq, keep, TM_FRONT)
    yb = _attn_prompt(q, k, v, sgb, batch, seq).reshape(batch * seq, d)
    x1, h1 = _outproj(ya, yb, xp2d, mod_e_p, w_out_e16, norm_o[0], mod_o_p, tm, seq)
    yc, rq, rk, rv, sgd, pool_tail = _odd_front_prompt(h1, w_in_o16, pool_w16, pool_scale[0], batch, seq, TM_FRONT)
    yd, ret_p = _ret_prompt(rq, rk, rv, sgd, batch, seq)
    y_prompt = _outproj(yc, yd.reshape(batch * seq, d), x1, mod_o_p, w_out_o16, norm_f, None, tm, seq)

    xs2d = x_sample.reshape(sb, d)
    conv_prev_t = jnp.transpose(state_conv[0], (1, 0, 2))
    ya_s, q_s, k_s, v_s, sgb_s, conv_s_t = _even_front_sample(
        xs2d, mod_e_s, norm_e[0], w_in_e16, conv_w[0], conv_b[0], conv_prev_t)
    cache_kt = jnp.transpose(cache_win_k[0], (0, 2, 3, 1)).reshape(sb, d, wb)
    cache_vt = jnp.transpose(cache_win_v[0], (0, 2, 3, 1)).reshape(sb, d, wb)
    yb_s = _attn_sample(q_s, k_s, v_s, sgb_s, cache_kt, cache_vt)
    x1_s, h1_s = _outproj(ya_s, yb_s, xs2d, mod_e_s, w_out_e16, norm_o[0], mod_o_s, sb, 1)
    pool_prev_t = jnp.transpose(state_pool[0], (1, 0, 2))
    yc_s, rq_s, rk_s, rv_s, sgd_s, pool_s_t = _odd_front_sample(h1_s, w_in_o16, pool_w16, pool_scale[0], pool_prev_t)
    yd_s, ret_s = _ret_sample(rq_s, rk_s, rv_s, sgd_s, state_ret[0])
    y_sample = _outproj(yc_s, yd_s, x1_s, mod_o_s, w_out_o16, norm_f, None, sb, 1)

    return (
        y_prompt.reshape(batch, seq, d),
        y_sample.reshape(sb, 1, d),
        conv_tail[:, 6:8][None],
        jnp.transpose(conv_s_t, (1, 0, 2))[None],
        jnp.transpose(k_new.reshape(batch, H_B, HD_B, keep), (0, 3, 1, 2))[None],
        k_s.reshape(1, sb, 1, H_B, HD_B),
        jnp.transpose(v_new.reshape(batch, H_B, HD_B, keep), (0, 3, 1, 2))[None],
        v_s.reshape(1, sb, 1, H_B, HD_B),
        pool_tail[:, 1:][None],
        jnp.transpose(pool_s_t, (1, 0, 2))[None],
        ret_p[None],
        ret_s[None],
    )
```

```python
import functools

import numpy as np
import jax
import jax.numpy as jnp
from jax import lax
from jax.experimental import pallas as pl
from jax.experimental.pallas import tpu as pltpu

F32 = jnp.float32
BF16 = jnp.bfloat16
WORD = jnp.uint32

D_MODEL = 1024
EPS = 1e-6
H_B = 16
HD_B = 64
N_PAIR = H_B // 2
LANES = 128
NK = 128
DILATIONS = (1, 4, 16)
POOL_SIZES = (2, 4, 8, 16)
POOL_PREV = 16
H_D = 4
DK_D = 256
RET_CHUNK = 256
TN = 256
TM_FRONT = 1024
TM_OUT = 1024
MASK_DIST = 1e9
LOG2E = float(np.log2(np.e))
MASK_ADD = 1e30
VMEM_LIMIT = 56 * 1024 * 1024

RET_LOG_G = [float(np.log(1.0 - 2.0 ** (-5.0 - h))) for h in range(H_D)]


def _cparams(sem):
    return pltpu.CompilerParams(dimension_semantics=sem, vmem_limit_bytes=VMEM_LIMIT)


def _silu(x):
    return x * (1.0 / (1.0 + jnp.exp(-x)))


def _dot(a, b):
    return jnp.dot(a, b, preferred_element_type=F32)


def _dot_nt(a, b):
    return lax.dot_general(a, b, (((1,), (1,)), ((), ())), preferred_element_type=F32)


def _rms(x):
    return x * lax.rsqrt(jnp.mean(x * x, axis=-1, keepdims=True) + EPS)


def _adaln_kernel(c_ref, we_ref, be_ref, wo_ref, bo_ref, me_ref, mo_ref):
    sc = _silu(c_ref[...]).astype(BF16)
    me_ref[...] = _dot(sc, we_ref[...].astype(BF16)) + be_ref[...]
    mo_ref[...] = _dot(sc, wo_ref[...].astype(BF16)) + bo_ref[...]


def _adaln(c_all, we, be, wo, bo):
    rows = c_all.shape[0]
    tn = 512
    n = 3 * D_MODEL
    wspec = pl.BlockSpec((D_MODEL, tn), lambda j: (0, j))
    bspec = pl.BlockSpec((1, tn), lambda j: (0, j))
    ospec = pl.BlockSpec((rows, tn), lambda j: (0, j))
    return pl.pallas_call(
        _adaln_kernel,
        grid=(n // tn,),
        in_specs=[pl.BlockSpec((rows, D_MODEL), lambda j: (0, 0)), wspec, bspec, wspec, bspec],
        out_specs=[ospec, ospec],
        out_shape=[jax.ShapeDtypeStruct((rows, n), F32)] * 2,
        compiler_params=_cparams(("arbitrary",)),
        name="adaln",
    )(c_all, we, be.reshape(1, n), wo, bo.reshape(1, n))


def _norm_mod(x, g, mod):
    shift = mod[:, 0:D_MODEL]
    scale = mod[:, D_MODEL:2 * D_MODEL]
    return _rms(x) * g * (1.0 + scale) + shift


def _shift_rows(u, k, prev_rows):
    row = lax.broadcasted_iota(jnp.int32, u.shape, 0)
    out = pltpu.roll(u, k, 0)
    for idx, pr in enumerate(prev_rows):
        out = jnp.where(row == idx, pr, out)
    return out


N_SEG_E = 8
PHASES_E = 8


def _phase_segments(t, n_seg, n_phases):
    return range(t * n_seg // n_phases, (t + 1) * n_seg // n_phases)


def _even_front_kernel(x_ref, mod_ref, g_ref, *refs, tiles_per_batch, n_chunks):
    ws = refs[:N_SEG_E]
    cw_ref, cb_ref = refs[N_SEG_E:N_SEG_E + 2]
    ya_ref, q_ref, k_ref, v_ref, sgb_ref, kn_ref, vn_ref, cn_ref = refs[N_SEG_E + 2:N_SEG_E + 10]
    h_scr, carry_scr, z_a, z_b = refs[N_SEG_E + 10:]
    nj = D_MODEL // TN
    s = pl.program_id(0)
    c_out = jnp.maximum(s - 1, 0)
    j_out = c_out % nj
    il_out = (c_out // nj) % tiles_per_batch

    @pl.when(s == 0)
    def _():
        z_b[...] = jnp.zeros_like(z_b)
        carry_scr[...] = jnp.zeros_like(carry_scr)

    @pl.when((s < n_chunks) & (s % nj == 0))
    def _():
        h_scr[...] = _norm_mod(x_ref[...], g_ref[...], mod_ref[...]).astype(BF16)

    tm = h_scr.shape[0]
    rb = tm // PHASES_E

    def phase(t, z_w, z_r):
        for seg in _phase_segments(t, N_SEG_E, PHASES_E):
            z_w[seg] = _dot(h_scr[...], ws[seg][...])
        rows = slice(t * rb, (t + 1) * rb)
        bg, cg, xv, ga, q, k, v, gb = (z_r[seg, rows, :] for seg in range(N_SEG_E))
        u = cg * xv
        prev = carry_scr[j_out]
        if t == 0:
            prev = jnp.where(il_out == 0, 0.0, prev)
        p2, p1 = prev[6:7, :], prev[7:8, :]
        u1 = _shift_rows(u, 1, [p1])
        u2 = _shift_rows(u, 2, [p2, p1])
        cw = cw_ref[...]
        conv = cb_ref[...] + cw[0:1, :] * u2 + cw[1:2, :] * u1 + cw[2:3, :] * u
        tail = u[rb - 8:rb, :]
        carry_scr[j_out] = tail
        if t == PHASES_E - 1:
            cn_ref[...] = tail
        ya_ref[rows, :] = (bg * conv * _silu(ga)).astype(BF16)
        kn_ref[:, rows] = k.T
        vn_ref[:, rows] = v.T
        wrows = slice(t * rb // 2, (t + 1) * rb // 2)
        for ref, val in ((q_ref, q * (HD_B ** -0.5 * LOG2E)),
                         (k_ref, k), (v_ref, v)):
            words = pltpu.bitcast(val.astype(BF16), WORD)
            for e in range(TN // LANES):
                ref[e, wrows, :] = words[:, e * LANES:(e + 1) * LANES]
        sgb = _silu(gb).astype(BF16)
        for e in range(TN // LANES):
            sgb_ref[e, rows, :] = sgb[:, e * LANES:(e + 1) * LANES]

    for parity, (z_w, z_r) in enumerate(((z_a, z_b), (z_b, z_a))):
        for t in range(PHASES_E):
            pl.when(s % 2 == parity)(functools.partial(phase, t, z_w, z_r))


def _even_front_prompt(x2d, mod, g, w_in, cw, cb, batch, seq, keep, tm):
    rows = batch * seq
    tpb = seq // tm
    nj = D_MODEL // TN
    n_chunks = (rows // tm) * nj
    off = (seq - keep) // tm
    ppc = TN // LANES

    def c_in(s):
        return jnp.minimum(s, n_chunks - 1)

    def c_out(s):
        return jnp.maximum(s - 1, 0)

    def out_ij(s):
        c = c_out(s)
        return c // nj, c % nj

    def pair_map(s):
        i, j = out_ij(s)
        return (i // tpb, j, i % tpb, 0)

    def keep_map(s):
        i, j = out_ij(s)
        il = i % tpb
        kept = il >= off
        return (i // tpb, jnp.where(kept, j, 0), jnp.where(kept, il - off, 0))

    def tail_map(s):
        i, j = out_ij(s)
        return (i // tpb, 0, jnp.where(i % tpb == tpb - 1, j, 0))

    pair_spec = pl.BlockSpec((None, ppc, tm, LANES), pair_map)
    pair_shape = jax.ShapeDtypeStruct((batch, N_PAIR, seq, LANES), BF16)
    word_spec = pl.BlockSpec((None, ppc, tm // 2, LANES), pair_map)
    word_shape = jax.ShapeDtypeStruct((batch, N_PAIR, seq // 2, LANES), WORD)
    keep_spec = pl.BlockSpec((None, TN, tm), keep_map)
    keep_shape = jax.ShapeDtypeStruct((batch, D_MODEL, keep), F32)
    return pl.pallas_call(
        functools.partial(_even_front_kernel, tiles_per_batch=tpb, n_chunks=n_chunks),
        grid=(n_chunks + 1,),
        in_specs=[
            pl.BlockSpec((tm, D_MODEL), lambda s: (c_in(s) // nj, 0)),
            pl.BlockSpec((None, 1, 3 * D_MODEL), lambda s: (c_in(s) // nj // tpb, 0, 0)),
            pl.BlockSpec((1, D_MODEL), lambda s: (0, 0)),
        ] + [pl.BlockSpec((D_MODEL, TN), lambda s, seg=seg: (0, seg * nj + c_in(s) % nj)) for seg in range(N_SEG_E)] + [
            pl.BlockSpec((3, TN), lambda s: (0, c_out(s) % nj)),
            pl.BlockSpec((1, TN), lambda s: (0, c_out(s) % nj)),
        ],
        out_specs=[
            pl.BlockSpec((tm, TN), lambda s: out_ij(s)),
            word_spec, word_spec, word_spec, pair_spec,
            keep_spec, keep_spec,
            pl.BlockSpec((None, 8, TN), tail_map),
        ],
        out_shape=[
            jax.ShapeDtypeStruct((rows, D_MODEL), BF16),
            word_shape, word_shape, word_shape, pair_shape,
            keep_shape, keep_shape,
            jax.ShapeDtypeStruct((batch, 8, D_MODEL), F32),
        ],
        scratch_shapes=[pltpu.VMEM((tm, D_MODEL), BF16), pltpu.VMEM((nj, 8, TN), F32),
                        pltpu.VMEM((N_SEG_E, tm, TN), F32), pltpu.VMEM((N_SEG_E, tm, TN), F32)],
        compiler_params=_cparams(("arbitrary",)),
        name="even_front_prompt",
    )(x2d, mod.reshape(batch, 1, 3 * D_MODEL), g.reshape(1, D_MODEL), *([w_in] * N_SEG_E), cw, cb.reshape(1, D_MODEL))


STAT_PITCH = {1: 1, 4: 4, 16: 24}
ATTN_UNROLL = 16


def _residue_stream(src, r):
    n_out = src.shape[0] // 4
    halves = [pltpu.unpack_elementwise(src[pl.ds(r // 2 + off, n_out, stride=4), :], index=r % 2,
                                       packed_dtype=BF16, unpacked_dtype=F32) for off in (0, 2)]
    return pltpu.pack_elementwise(halves, packed_dtype=BF16)


def _attn_prompt_kernel(slope_ref, dist_ref, q_ref, k_ref, v_ref, sgb_ref, yb_ref,
                        q4_scr, q16_scr, kv4_scr, kv16_scr, acc_scr, m_scr, l_scr, bias_scr, *, seq):
    hp = pl.program_id(1)
    lane = lax.broadcasted_iota(jnp.int32, (1, LANES), 1)
    half0 = lane < HD_B
    dist = dist_ref[...]
    masked = jnp.full((NK, NK), MASK_DIST, F32)
    for p, d in enumerate(DILATIONS):
        for e in range(2):
            bias = dist * (slope_ref[2 * hp + e] * (float(d) * LOG2E))
            bias_scr[2 * p + e] = bias
            bias_scr[6 + 2 * p + e] = jnp.concatenate([bias[:, NK:], masked], axis=1)

    for a, ref in enumerate((q_ref, k_ref, v_ref)):
        for r in range(4):
            s = _residue_stream(ref, r)
            if a == 0:
                q4_scr[r] = s
            else:
                kv4_scr[a - 1, r] = s
        for r in range(16):
            s = _residue_stream(q4_scr.at[r % 4] if a == 0 else kv4_scr.at[a - 1, r % 4], r // 4)
            if a == 0:
                q16_scr[r] = s
            else:
                kv16_scr[a - 1, r] = s

    ones = jnp.ones((2 * NK, LANES), BF16)
    hw = NK // 2

    def tile(p, d, r, n):
        qw = pl.ds(pl.multiple_of(n * hw, hw), hw)
        kw = pl.ds(pl.multiple_of(jnp.maximum(n - 1, 0) * hw, hw), 2 * hw)
        if d == 1:
            qs, ks, vs = q_ref, k_ref, v_ref
        elif d == 4:
            qs, ks, vs = q4_scr.at[r], kv4_scr.at[0, r], kv4_scr.at[1, r]
        else:
            qs, ks, vs = q16_scr.at[r], kv16_scr.at[0, r], kv16_scr.at[1, r]
        q = pltpu.bitcast(qs[qw, :], BF16)
        kk = pltpu.bitcast(ks[kw, :], BF16)
        vv1 = jnp.concatenate([pltpu.bitcast(vs[kw, :], BF16), ones], axis=1)
        first = jnp.where(n == 0, 1, 0)
        res = []
        for e in range(2):
            qe = jnp.where(half0 if e == 0 else jnp.logical_not(half0), q, jnp.zeros_like(q))
            s = _dot_nt(qe, kk) - bias_scr[6 * first + 2 * p + e]
            m = jnp.max(s, axis=-1, keepdims=True)
            pe = jnp.exp2(s - m).astype(BF16)
            res.append((_dot(pe, vv1), m))
        (a0, m0), (a1, m1) = res
        acc = jnp.where(half0, a0[:, :LANES], a1[:, :LANES])
        l = jnp.where(half0, a0[:, LANES:], a1[:, LANES:])
        m = jnp.where(half0, m0, m1)
        if d > 1:
            pitch = STAT_PITCH[d]
            rows = pl.ds(r + pitch * NK * n, NK, stride=pitch)
            acc_scr[p - 1, rows, :] = acc
            l_scr[p - 1, rows, :] = l
            m_scr[p - 1, rows, :] = m
            return
        accs, ls, ms = [acc], [l], [m]
        for pp in range(1, len(DILATIONS)):
            dd = DILATIONS[pp]
            for ref, vals in ((acc_scr, accs), (l_scr, ls), (m_scr, ms)):
                vals.append(jnp.concatenate(
                    [ref[pp - 1, pl.ds(pl.multiple_of((n * (NK // dd) + g) * STAT_PITCH[dd], 8), dd), :]
                     for g in range(NK // dd)], axis=0) if STAT_PITCH[dd] != dd
                    else ref[pp - 1, pl.ds(pl.multiple_of(n * NK, NK), NK), :])
        top = jnp.maximum(jnp.maximum(ms[0], ms[1]), ms[2])
        ws = [jnp.exp2(x - top) for x in ms]
        num = ws[0] * accs[0] + ws[1] * accs[1] + ws[2] * accs[2]
        den = ws[0] * ls[0] + ws[1] * ls[1] + ws[2] * ls[2]
        rows = pl.ds(pl.multiple_of(n * NK, NK), NK)
        yb_ref[rows, :] = (num * (1.0 / den) * sgb_ref[rows, :].astype(F32)).astype(BF16)

    for p, d in reversed(list(enumerate(DILATIONS))):
        def body(it, c, p=p, d=d):
            for u in range(ATTN_UNROLL):
                t = it * ATTN_UNROLL + u
                if d == 1:
                    tile(p, d, 0, t)
                else:
                    tile(p, d, t % d, t // d)
            return c

        lax.fori_loop(0, seq // NK // ATTN_UNROLL, body, 0)


def _band_distance():
    qi = np.arange(NK)[:, None]
    kj = np.arange(2 * NK)[None, :]
    dist = NK + qi - kj
    return jnp.asarray(np.where((dist >= 0) & (dist <= NK), dist, MASK_DIST), dtype=F32)


def _alibi_slopes():
    return jnp.asarray(2.0 ** (-8.0 * np.arange(1, H_B + 1) / H_B), dtype=F32)


def _attn_prompt(q, k, v, sgb, batch, seq):
    word_spec = pl.BlockSpec((None, None, seq // 2, LANES), lambda b, hp: (b, hp, 0, 0))
    stat_rows = max(seq // d * STAT_PITCH[d] for d in DILATIONS)
    stat = pltpu.VMEM((len(DILATIONS) - 1, stat_rows, LANES), F32)
    return pl.pallas_call(
        functools.partial(_attn_prompt_kernel, seq=seq),
        grid=(batch, N_PAIR),
        in_specs=[pl.BlockSpec(memory_space=pltpu.SMEM),
                  pl.BlockSpec((NK, 2 * NK), lambda b, hp: (0, 0)),
                  word_spec, word_spec, word_spec,
                  pl.BlockSpec((None, None, seq, LANES), lambda b, hp: (b, hp, 0, 0))],
        out_specs=pl.BlockSpec((None, seq, LANES), lambda b, hp: (b, 0, hp)),
        out_shape=jax.ShapeDtypeStruct((batch, seq, D_MODEL), BF16),
        scratch_shapes=[pltpu.VMEM((4, seq // 8, LANES), WORD),
                        pltpu.VMEM((16, seq // 32, LANES), WORD),
                        pltpu.VMEM((2, 4, seq // 8, LANES), WORD),
                        pltpu.VMEM((2, 16, seq // 32, LANES), WORD),
                        stat, stat, stat,
                        pltpu.VMEM((12, NK, 2 * NK), F32)],
        compiler_params=_cparams(("arbitrary", "arbitrary")),
        name="attn_prompt",
    )(_alibi_slopes(), _band_distance(), q, k, v, sgb)


def _outproj_mid_kernel(ya_ref, yb_ref, x_ref, mod_ref, w1_ref, w2_ref, g_ref, mod2_ref, x1_ref, h_ref):
    gate = mod_ref[:, 2 * D_MODEL:3 * D_MODEL]
    x1 = x_ref[...] + gate * (_dot(ya_ref[...], w1_ref[...]) + _dot(yb_ref[...], w2_ref[...]))
    x1_ref[...] = x1
    h_ref[...] = _norm_mod(x1, g_ref[...], mod2_ref[...]).astype(BF16)


def _outproj_final_kernel(ya_ref, yb_ref, x_ref, mod_ref, w1_ref, w2_ref, g_ref, y_ref):
    gate = mod_ref[:, 2 * D_MODEL:3 * D_MODEL]
    x1 = x_ref[...] + gate * (_dot(ya_ref[...], w1_ref[...]) + _dot(yb_ref[...], w2_ref[...]))
    y_ref[...] = _rms(x1) * g_ref[...]


def _outproj(ya, yb, x2d, mod, w_out, g, mod2, tm, rows_per_mod):
    rows = x2d.shape[0]
    half = w_out.shape[0] // 2
    row_spec = pl.BlockSpec((tm, D_MODEL), lambda i: (i, 0))
    if rows_per_mod == 1:
        mod_spec = pl.BlockSpec((tm, 3 * D_MODEL), lambda i: (i, 0))
        mods = (mod, mod2)
    else:
        tpb = rows_per_mod // tm
        mod_spec = pl.BlockSpec((None, 1, 3 * D_MODEL), lambda i: (i // tpb, 0, 0))
        mods = tuple(None if m is None else m.reshape(-1, 1, 3 * D_MODEL) for m in (mod, mod2))
    w1_spec = pl.BlockSpec((half, D_MODEL), lambda i: (0, 0))
    w2_spec = pl.BlockSpec((half, D_MODEL), lambda i: (1, 0))
    g_spec = pl.BlockSpec((1, D_MODEL), lambda i: (0, 0))
    common = dict(grid=(rows // tm,), compiler_params=_cparams(("arbitrary",)))
    if mod2 is None:
        return pl.pallas_call(
            _outproj_final_kernel,
            in_specs=[row_spec, row_spec, row_spec, mod_spec, w1_spec, w2_spec, g_spec],
            out_specs=row_spec,
            out_shape=jax.ShapeDtypeStruct((rows, D_MODEL), F32),
            name="outproj_final", **common,
        )(ya, yb, x2d, mods[0], w_out, w_out, g.reshape(1, D_MODEL))
    return pl.pallas_call(
        _outproj_mid_kernel,
        in_specs=[row_spec, row_spec, row_spec, mod_spec, w1_spec, w2_spec, g_spec, mod_spec],
        out_specs=[row_spec, row_spec],
        out_shape=[jax.ShapeDtypeStruct((rows, D_MODEL), F32), jax.ShapeDtypeStruct((rows, D_MODEL), BF16)],
        name="outproj_mid", **common,
    )(ya, yb, x2d, mods[0], w_out, w_out, g.reshape(1, D_MODEL), mods[1])


def _pick_by_chunk(j, vals):
    out = vals[-1]
    for idx in range(len(vals) - 2, -1, -1):
        out = jnp.where(j == idx, vals[idx], out)
    return out


N_SEG_O = 6
PHASES_O = 4


def _odd_front_kernel(h_ref, *refs, tiles_per_batch):
    ws = refs[:N_SEG_O]
    pw_ref, ps_ref = refs[N_SEG_O:N_SEG_O + 2]
    yc_ref, q_ref, kt_ref, v_ref, sgd_ref, pn_ref = refs[N_SEG_O + 2:N_SEG_O + 8]
    carry_scr, z_a, z_b = refs[N_SEG_O + 8:]
    nj = D_MODEL // TN
    s = pl.program_id(0)
    c_out = jnp.maximum(s - 1, 0)
    j_out = c_out % nj
    il_out = (c_out // nj) % tiles_per_batch
    tm = h_ref.shape[0]
    rb = tm // PHASES_O

    @pl.when(s == 0)
    def _():
        z_b[...] = jnp.zeros_like(z_b)
        carry_scr[...] = jnp.zeros_like(carry_scr)

    def phase(t, z_w, z_r):
        for seg in _phase_segments(t, N_SEG_O, PHASES_O):
            z_w[seg] = _dot(h_ref[...], ws[seg][...])
        rows = slice(t * rb, (t + 1) * rb)
        u, gc, q, k, v, gd = (z_r[seg, rows, :] for seg in range(N_SEG_O))
        prev = carry_scr[j_out]
        if t == 0:
            prev = jnp.where(il_out == 0, 0.0, prev)
        tail = u[rb - POOL_PREV:rb, :]
        carry_scr[j_out] = tail
        if t == PHASES_O - 1:
            pn_ref[...] = tail
        ext = jnp.concatenate([prev, u], axis=0)
        sums = []
        acc = ext
        for sh in (1, 2, 4, 8):
            acc = acc + pltpu.roll(acc, sh, 0)
            sums.append(acc[POOL_PREV:, :])
        win = _pick_by_chunk(j_out, sums)
        width = _pick_by_chunk(j_out, [float(w) for w in POOL_SIZES])
        pos = (il_out * tm + t * rb + lax.broadcasted_iota(jnp.int32, (rb, 1), 0)).astype(F32)
        pooled = win / jnp.minimum(width, pos + 1.0) - u
        mixed = _dot(pooled.astype(BF16), pw_ref[...]) * ps_ref[...]
        yc_ref[rows, :] = (mixed * _silu(gc)).astype(BF16)
        q_ref[rows, :] = q.astype(BF16)
        kt_ref[:, rows] = (k * (DK_D ** -0.5)).T.astype(BF16)
        v_ref[rows, :] = v.astype(BF16)
        sgd_ref[rows, :] = _silu(gd).astype(BF16)

    for parity, (z_w, z_r) in enumerate(((z_a, z_b), (z_b, z_a))):
        for t in range(PHASES_O):
            pl.when(s % 2 == parity)(functools.partial(phase, t, z_w, z_r))


def _odd_front_prompt(h2d, w_in, pw, ps, batch, seq, tm):
    rows = batch * seq
    tpb = seq // tm
    nj = D_MODEL // TN
    n_chunks = (rows // tm) * nj

    def c_in(s):
        return jnp.minimum(s, n_chunks - 1)

    def out_ij(s):
        c = jnp.maximum(s - 1, 0)
        return c // nj, c % nj

    def head_map(s):
        i, j = out_ij(s)
        return (i // tpb, j, i % tpb, 0)

    def kt_map(s):
        i, j = out_ij(s)
        return (i // tpb, j, 0, i % tpb)

    def tail_map(s):
        i, j = out_ij(s)
        return (i // tpb, 0, jnp.where(i % tpb == tpb - 1, j, 0))

    head_spec = pl.BlockSpec((None, None, tm, TN), head_map)
    head_shape = jax.ShapeDtypeStruct((batch, H_D, seq, DK_D), BF16)
    tok_spec = pl.BlockSpec((tm, TN), out_ij)
    tok_shape = jax.ShapeDtypeStruct((rows, D_MODEL), BF16)
    zbuf = pltpu.VMEM((N_SEG_O, tm, TN), F32)
    return pl.pallas_call(
        functools.partial(_odd_front_kernel, tiles_per_batch=tpb),
        grid=(n_chunks + 1,),
        in_specs=[pl.BlockSpec((tm, D_MODEL), lambda s: (c_in(s) // nj, 0))] + [
                  pl.BlockSpec((D_MODEL, TN), lambda s, seg=seg: (0, seg * nj + c_in(s) % nj))
                  for seg in range(N_SEG_O)] + [
                  pl.BlockSpec((None, TN, TN), lambda s: (out_ij(s)[1], 0, 0)),
                  pl.BlockSpec((1, TN), lambda s: (0, out_ij(s)[1]))],
        out_specs=[tok_spec, head_spec, pl.BlockSpec((None, None, TN, tm), kt_map), head_spec, tok_spec,
                   pl.BlockSpec((None, POOL_PREV, TN), tail_map)],
        out_shape=[tok_shape, head_shape, jax.ShapeDtypeStruct((batch, H_D, DK_D, seq), BF16),
                   head_shape, tok_shape,
                   jax.ShapeDtypeStruct((batch, POOL_PREV, D_MODEL), F32)],
        scratch_shapes=[pltpu.VMEM((nj, POOL_PREV, TN), F32), zbuf, zbuf],
        compiler_params=_cparams(("arbitrary",)),
        name="odd_front_prompt",
    )(h2d, *([w_in] * N_SEG_O), pw, ps.reshape(1, D_MODEL))


RET_SBLK = 1024


def _ret_prompt_kernel(q_ref, kt_ref, v_ref, sgd_ref, dec_ref, cdec_ref, kdec_ref,
                       yd_ref, st_ref, state_scr, *, sblk, n_sblk):
    sb = pl.program_id(1)

    @pl.when(sb == 0)
    def _():
        state_scr[...] = jnp.zeros_like(state_scr)

    def chunk(c, carry):
        rows = pl.ds(pl.multiple_of(c * RET_CHUNK, RET_CHUNK), RET_CHUNK)
        for h in range(H_D):
            cols = slice(h * DK_D, (h + 1) * DK_D)
            q = q_ref[h, rows, :]
            kt = kt_ref[h, :, rows]
            v = v_ref[h, rows, :]
            state = state_scr[h]
            scores = _dot(q, kt) * dec_ref[h]
            o = _dot(scores.astype(BF16), v) + _dot(q, state.astype(BF16)) * cdec_ref[h]
            kd = (kt.astype(F32) * kdec_ref[h]).astype(BF16)
            state_scr[h] = float(np.exp(RET_CHUNK * RET_LOG_G[h])) * state + _dot(kd, v)
            yd_ref[rows, cols] = (_rms(o) * sgd_ref[rows, cols].astype(F32)).astype(BF16)
        return carry

    lax.fori_loop(0, sblk // RET_CHUNK, chunk, 0, unroll=4)

    @pl.when(sb == n_sblk - 1)
    def _():
        st_ref[...] = state_scr[...]


def _ret_consts():
    t = np.arange(RET_CHUNK, dtype=np.float64)
    diff = t[:, None] - t[None, :]
    lg = np.asarray(RET_LOG_G)[:, None, None]
    dec = np.where(diff >= 0, np.exp(np.maximum(diff, 0.0)[None] * lg), 0.0)
    cdec = np.broadcast_to(np.exp((t + 1.0)[None, :, None] * lg), (H_D, RET_CHUNK, DK_D))
    kdec = np.broadcast_to(np.exp((RET_CHUNK - 1.0 - t)[None, None, :] * lg), (H_D, DK_D, RET_CHUNK))
    return jnp.asarray(dec, F32), jnp.asarray(cdec, F32), jnp.asarray(kdec, F32)


def _ret_prompt(q, kt, v, sgd, batch, seq):
    dec, cdec, kdec = _ret_consts()
    n_sblk = seq // RET_SBLK
    head_spec = pl.BlockSpec((None, H_D, RET_SBLK, DK_D), lambda b, s: (b, 0, s, 0))
    tok_spec = pl.BlockSpec((None, RET_SBLK, D_MODEL), lambda b, s: (b, s, 0))

    def const_spec(shape):
        return pl.BlockSpec(shape, lambda b, s: (0, 0, 0))

    return pl.pallas_call(
        functools.partial(_ret_prompt_kernel, sblk=RET_SBLK, n_sblk=n_sblk),
        grid=(batch, n_sblk),
        in_specs=[head_spec, pl.BlockSpec((None, H_D, DK_D, RET_SBLK), lambda b, s: (b, 0, 0, s)), head_spec,
                  tok_spec, const_spec(dec.shape), const_spec(cdec.shape), const_spec(kdec.shape)],
        out_specs=[tok_spec, pl.BlockSpec((None, H_D, DK_D, DK_D), lambda b, s: (b, 0, 0, 0))],
        out_shape=[jax.ShapeDtypeStruct((batch, seq, D_MODEL), BF16),
                   jax.ShapeDtypeStruct((batch, H_D, DK_D, DK_D), F32)],
        scratch_shapes=[pltpu.VMEM((H_D, DK_D, DK_D), F32)],
        compiler_params=_cparams(("arbitrary", "arbitrary")),
        name="ret_prompt",
    )(q, kt, v, sgd.reshape(batch, seq, D_MODEL), dec, cdec, kdec)


def _even_front_sample_kernel(x_ref, mod_ref, g_ref, wbg, wcg, wxv, wga, wq, wk, wv, wgb, cw_ref, cb_ref,
                              prev_ref, ya_ref, q_ref, k_ref, v_ref, sgb_ref, cn_ref):
    h = _norm_mod(x_ref[...], g_ref[...], mod_ref[...]).astype(BF16)
    u = _dot(h, wcg[...]) * _dot(h, wxv[...])
    cw = cw_ref[...]
    conv = cb_ref[...] + cw[0:1, :] * prev_ref[0] + cw[1:2, :] * prev_ref[1] + cw[2:3, :] * u
    cn_ref[0] = prev_ref[1]
    cn_ref[1] = u
    ya_ref[...] = (_dot(h, wbg[...]) * conv * _silu(_dot(h, wga[...]))).astype(BF16)
    q_ref[...] = _dot(h, wq[...]) * (HD_B ** -0.5)
    k_ref[...] = _dot(h, wk[...])
    v_ref[...] = _dot(h, wv[...])
    sgb_ref[...] = _silu(_dot(h, wgb[...]))


def _even_front_sample(x2d, mod, g, w_in, cw, cb, prev_t):
    rows = x2d.shape[0]
    nj = D_MODEL // TN

    def wspec(s):
        return pl.BlockSpec((D_MODEL, TN), lambda j, s=s: (0, s * nj + j))

    full = pl.BlockSpec((rows, D_MODEL), lambda j: (0, 0))
    col = pl.BlockSpec((rows, TN), lambda j: (0, j))
    st = pl.BlockSpec((2, rows, TN), lambda j: (0, 0, j))
    colf = jax.ShapeDtypeStruct((rows, D_MODEL), F32)
    return pl.pallas_call(
        _even_front_sample_kernel,
        grid=(nj,),
        in_specs=[full, pl.BlockSpec((rows, 3 * D_MODEL), lambda j: (0, 0)),
                  pl.BlockSpec((1, D_MODEL), lambda j: (0, 0))] + [wspec(s) for s in range(8)] + [
            pl.BlockSpec((3, TN), lambda j: (0, j)), pl.BlockSpec((1, TN), lambda j: (0, j)), st],
        out_specs=[col, col, col, col, col, st],
        out_shape=[jax.ShapeDtypeStruct((rows, D_MODEL), BF16), colf, colf, colf, colf,
                   jax.ShapeDtypeStruct((2, rows, D_MODEL), F32)],
        compiler_params=_cparams(("arbitrary",)),
        name="even_front_sample",
    )(x2d, mod, g.reshape(1, D_MODEL), *([w_in] * 8), cw, cb.reshape(1, D_MODEL), prev_t)


def _attn_sample_kernel(slope_ref, pos_ref, q_ref, kn_ref, vn_ref, sgb_ref, kt_ref, vt_ref, yb_ref):
    gw = q_ref.shape[-1]
    row = lax.broadcasted_iota(jnp.int32, (8, gw), 0)
    col = lax.broadcasted_iota(jnp.int32, (8, gw), 1)
    own = (col >= row * HD_B) & (col < (row + 1) * HD_B)
    qm = jnp.where(own, q_ref[...], 0.0)
    s_self = jnp.sum(qm * kn_ref[...], axis=-1, keepdims=True)
    s_all = _dot(qm.astype(BF16), kt_ref[...].astype(BF16))
    s_all = s_all - slope_ref[:, 0:1] * pos_ref[0:1, :]
    v_self = vn_ref[...]
    probs, stats = [], []
    for p in range(len(DILATIONS)):
        s = s_all - pos_ref[p + 1:p + 2, :]
        m = jnp.maximum(jnp.max(s, axis=-1, keepdims=True), s_self)
        pe = jnp.exp(s - m)
        p_self = jnp.exp(s_self - m)
        probs.append(pe)
        stats.append((m, jnp.sum(pe, axis=-1, keepdims=True) + p_self, p_self))
    o_all = _dot_nt(jnp.concatenate(probs, axis=0).astype(BF16), vt_ref[...].astype(BF16))
    outs, lses = [], []
    for p, (m, l, p_self) in enumerate(stats):
        outs.append((o_all[8 * p:8 * p + 8, :] + p_self * v_self) * (1.0 / l))
        lses.append(m + jnp.log(l))
    m = jnp.maximum(jnp.maximum(lses[0], lses[1]), lses[2])
    ws = [jnp.exp(x - m) for x in lses]
    o = (ws[0] * outs[0] + ws[1] * outs[1] + ws[2] * outs[2]) * (1.0 / (ws[0] + ws[1] + ws[2]))
    o = jnp.sum(jnp.where(own, o, 0.0), axis=0, keepdims=True)
    yb_ref[...] = (o * sgb_ref[...]).astype(BF16)


def _attn_sample(q, kn, vn, sgb, cache_kt, cache_vt):
    rows, _, wb = cache_kt.shape
    gh = 8
    gw = gh * HD_B
    ng = H_B // gh
    back = wb - np.arange(wb, dtype=np.float64)
    pos = [back] + [np.where((back % d == 0) & (back <= NK * d), 0.0, MASK_ADD) for d in DILATIONS]
    slopes = (2.0 ** (-8.0 * np.arange(1, H_B + 1) / H_B)).reshape(ng, gh)
    slopes = np.concatenate([slopes, np.zeros((ng, 8 - gh))], axis=1)
    slope_arr = jnp.asarray(np.broadcast_to(slopes[:, :, None], (ng, 8, LANES)), F32)

    def r3(a):
        return a.reshape(rows, 1, D_MODEL)

    row_spec = pl.BlockSpec((None, 1, gw), lambda b, g: (b, 0, g))
    t_spec = pl.BlockSpec((None, gw, wb), lambda b, g: (b, g, 0))
    return pl.pallas_call(
        _attn_sample_kernel,
        grid=(rows, ng),
        in_specs=[pl.BlockSpec((None, 8, LANES), lambda b, g: (g, 0, 0)),
                  pl.BlockSpec((4, wb), lambda b, g: (0, 0)),
                  row_spec, row_spec, row_spec, row_spec, t_spec, t_spec],
        out_specs=row_spec,
        out_shape=jax.ShapeDtypeStruct((rows, 1, D_MODEL), BF16),
        compiler_params=_cparams(("arbitrary", "arbitrary")),
        name="attn_sample",
    )(slope_arr, jnp.asarray(np.stack(pos), F32), r3(q), r3(kn), r3(vn), r3(sgb), cache_kt, cache_vt
      ).reshape(rows, D_MODEL)


def _odd_front_sample_kernel(h_ref, wu, wgc, wq, wk, wv, wgd, pw_ref, ps_ref, prev_ref,
                             yc_ref, q_ref, k_ref, v_ref, sgd_ref, pn_ref):
    j = pl.program_id(0)
    h = h_ref[...]
    u = _dot(h, wu[...])
    n_prev = prev_ref.shape[0]
    sums = []
    s = jnp.zeros_like(u)
    back = 0
    for w in POOL_SIZES:
        while back < w - 1:
            s = s + prev_ref[n_prev - 1 - back]
            back += 1
        sums.append(s)
    win = _pick_by_chunk(j, sums) + u
    inv_w = _pick_by_chunk(j, [1.0 / w for w in POOL_SIZES])
    pooled = win * inv_w - u
    mixed = _dot(pooled.astype(BF16), pw_ref[...]) * ps_ref[...]
    yc_ref[...] = (mixed * _silu(_dot(h, wgc[...]))).astype(BF16)
    q_ref[...] = _dot(h, wq[...])
    k_ref[...] = _dot(h, wk[...]) * (DK_D ** -0.5)
    v_ref[...] = _dot(h, wv[...])
    sgd_ref[...] = _silu(_dot(h, wgd[...]))
    for t in range(n_prev - 1):
        pn_ref[t] = prev_ref[t + 1]
    pn_ref[n_prev - 1] = u


def _odd_front_sample(h2d, w_in, pw, ps, prev_t):
    rows = h2d.shape[0]
    n_prev = prev_t.shape[0]
    nj = D_MODEL // TN

    def wspec(s):
        return pl.BlockSpec((D_MODEL, TN), lambda j, s=s: (0, s * nj + j))

    col = pl.BlockSpec((rows, TN), lambda j: (0, j))
    st = pl.BlockSpec((n_prev, rows, TN), lambda j: (0, 0, j))
    colf = jax.ShapeDtypeStruct((rows, D_MODEL), F32)
    return pl.pallas_call(
        _odd_front_sample_kernel,
        grid=(nj,),
        in_specs=[pl.BlockSpec((rows, D_MODEL), lambda j: (0, 0))] + [wspec(s) for s in range(6)] + [
            pl.BlockSpec((None, TN, TN), lambda j: (j, 0, 0)), pl.BlockSpec((1, TN), lambda j: (0, j)), st],
        out_specs=[col, col, col, col, col, st],
        out_shape=[jax.ShapeDtypeStruct((rows, D_MODEL), BF16), colf, colf, colf, colf,
                   jax.ShapeDtypeStruct((n_prev, rows, D_MODEL), F32)],
        compiler_params=_cparams(("arbitrary",)),
        name="odd_front_sample",
    )(h2d, *([w_in] * 6), pw, ps.reshape(1, D_MODEL), prev_t)


def _ret_sample_kernel(q_ref, k_ref, v_ref, sgd_ref, st_ref, yd_ref, sn_ref):
    row = lax.broadcasted_iota(jnp.int32, (DK_D, DK_D), 0)
    col = lax.broadcasted_iota(jnp.int32, (DK_D, DK_D), 1)
    for h in range(H_D):
        g = float(np.exp(RET_LOG_G[h]))
        cols = slice(h * DK_D, (h + 1) * DK_D)
        q = q_ref[:, cols]
        k = k_ref[:, cols]
        v = v_ref[:, cols]
        state = st_ref[h]
        qk = jnp.sum(q * k, axis=-1, keepdims=True)
        q8 = jnp.broadcast_to(q, (8, DK_D)).astype(BF16)
        cross = _dot(q8, state.astype(BF16))[0:1, :]
        o = qk * v + g * cross
        k_diag = jnp.where(row == col, jnp.broadcast_to(k, (DK_D, DK_D)), 0.0).astype(BF16)
        v_rows = jnp.broadcast_to(v, (DK_D, DK_D)).astype(BF16)
        sn_ref[h] = g * state + _dot(k_diag, v_rows)
        yd_ref[:, cols] = (_rms(o) * sgd_ref[:, cols]).astype(BF16)


def _ret_sample(q, k, v, sgd, state):
    rows = q.shape[0]

    def r3(a):
        return a.reshape(rows, 1, D_MODEL)

    row_spec = pl.BlockSpec((None, 1, D_MODEL), lambda b: (b, 0, 0))
    st_spec = pl.BlockSpec((None, H_D, DK_D, DK_D), lambda b: (b, 0, 0, 0))
    yd, sn = pl.pallas_call(
        _ret_sample_kernel,
        grid=(rows,),
        in_specs=[row_spec, row_spec, row_spec, row_spec, st_spec],
        out_specs=[row_spec, st_spec],
        out_shape=[jax.ShapeDtypeStruct((rows, 1, D_MODEL), BF16),
                   jax.ShapeDtypeStruct((rows, H_D, DK_D, DK_D), F32)],
        compiler_params=_cparams(("arbitrary",)),
        name="ret_sample",
    )(r3(q), r3(k), r3(v), r3(sgd), state)
    return yd.reshape(rows, D_MODEL), sn


def kernel(x_prompt, x_sample, c_prompt, c_sample, state_conv, cache_win_k, cache_win_v, state_pool, state_ret,
           norm_e, ada_w_e, ada_b_e, w_in_e, conv_w, conv_b, w_out_e, norm_o, ada_w_o, ada_b_o, w_in_o,
           pool_w, pool_scale, w_out_o, norm_f):
    batch, seq, d = x_prompt.shape
    sb = x_sample.shape[0]
    assert d == D_MODEL and x_sample.shape[1] == 1
    assert norm_e.shape[0] == 1 and norm_o.shape[0] == 1
    wb = cache_win_k.shape[2]
    keep = min(wb, seq)
    assert wb == DILATIONS[-1] * NK and seq % (DILATIONS[-1] * NK) == 0
    tm = TM_OUT

    n_c = batch + sb
    pad = (-n_c) % 8
    c_all = jnp.concatenate([c_prompt, c_sample, jnp.zeros((pad, d), F32)], axis=0)
    mod_e, mod_o = _adaln(c_all, ada_w_e[0], ada_b_e[0], ada_w_o[0], ada_b_o[0])
    mod_e_p, mod_e_s = mod_e[:batch], mod_e[batch:n_c]
    mod_o_p, mod_o_s = mod_o[:batch], mod_o[batch:n_c]

    w_in_e16 = w_in_e[0].astype(BF16)
    w_out_e16 = w_out_e[0].astype(BF16)
    w_in_o16 = w_in_o[0].astype(BF16)
    w_out_o16 = w_out_o[0].astype(BF16)
    pool_w16 = pool_w[0].astype(BF16)

    xp2d = x_prompt.reshape(batch * seq, d)
    ya, q, k, v, sgb, k_new, v_new, conv_tail = _even_front_prompt(
        xp2d, mod_e_p, norm_e[0], w_in_e16, conv_w[0], conv_b[0], batch, seq, keep, TM_FRONT)
    yb = _attn_prompt(q, k, v, sgb, batch, seq).reshape(batch * seq, d)
    x1, h1 = _outproj(ya, yb, xp2d, mod_e_p, w_out_e16, norm_o[0], mod_o_p, tm, seq)
    yc, rq, rk, rv, sgd, pool_tail = _odd_front_prompt(h1, w_in_o16, pool_w16, pool_scale[0], batch, seq, TM_FRONT)
    yd, ret_p = _ret_prompt(rq, rk, rv, sgd, batch, seq)
    y_prompt = _outproj(yc, yd.reshape(batch * seq, d), x1, mod_o_p, w_out_o16, norm_f, None, tm, seq)

    xs2d = x_sample.reshape(sb, d)
    conv_prev_t = jnp.transpose(state_conv[0], (1, 0, 2))
    ya_s, q_s, k_s, v_s, sgb_s, conv_s_t = _even_front_sample(
        xs2d, mod_e_s, norm_e[0], w_in_e16, conv_w[0], conv_b[0], conv_prev_t)
    cache_kt = jnp.transpose(cache_win_k[0], (0, 2, 3, 1)).reshape(sb, d, wb)
    cache_vt = jnp.transpose(cache_win_v[0], (0, 2, 3, 1)).reshape(sb, d, wb)
    yb_s = _attn_sample(q_s, k_s, v_s, sgb_s, cache_kt, cache_vt)
    x1_s, h1_s = _outproj(ya_s, yb_s, xs2d, mod_e_s, w_out_e16, norm_o[0], mod_o_s, sb, 1)
    pool_prev_t = jnp.transpose(state_pool[0], (1, 0, 2))
    yc_s, rq_s, rk_s, rv_s, sgd_s, pool_s_t = _odd_front_sample(h1_s, w_in_o16, pool_w16, pool_scale[0], pool_prev_t)
    yd_s, ret_s = _ret_sample(rq_s, rk_s, rv_s, sgd_s, state_ret[0])
    y_sample = _outproj(yc_s, yd_s, x1_s, mod_o_s, w_out_o16, norm_f, None, sb, 1)

    return (
        y_prompt.reshape(batch, seq, d),
        y_sample.reshape(sb, 1, d),
        conv_tail[:, 6:8][None],
        jnp.transpose(conv_s_t, (1, 0, 2))[None],
        jnp.transpose(k_new.reshape(batch, H_B, HD_B, keep), (0, 3, 1, 2))[None],
        k_s.reshape(1, sb, 1, H_B, HD_B),
        jnp.transpose(v_new.reshape(batch, H_B, HD_B, keep), (0, 3, 1, 2))[None],
        v_s.reshape(1, sb, 1, H_B, HD_B),
        pool_tail[:, 1:][None],
        jnp.transpose(pool_s_t, (1, 0, 2))[None],
        ret_p[None],
        ret_s[None],
    )
```

```python
import functools

import numpy as np
import jax
import jax.numpy as jnp
from jax import lax
from jax.experimental import pallas as pl
from jax.experimental.pallas import tpu as pltpu

F32 = jnp.float32
BF16 = jnp.bfloat16
WORD = jnp.uint32

D_MODEL = 1024
EPS = 1e-6
H_B = 16
HD_B = 64
N_PAIR = H_B // 2
LANES = 128
NK = 128
DILATIONS = (1, 4, 16)
POOL_SIZES = (2, 4, 8, 16)
POOL_PREV = 16
H_D = 4
DK_D = 256
RET_CHUNK = 256
TN = 256
TM_FRONT = 1024
TM_OUT = 1024
MASK_DIST = 1e9
LOG2E = float(np.log2(np.e))
MASK_ADD = 1e30
VMEM_LIMIT = 56 * 1024 * 1024

RET_LOG_G = [float(np.log(1.0 - 2.0 ** (-5.0 - h))) for h in range(H_D)]


def _cparams(sem):
    return pltpu.CompilerParams(dimension_semantics=sem, vmem_limit_bytes=VMEM_LIMIT)


def _silu(x):
    return x * (1.0 / (1.0 + jnp.exp(-x)))


def _dot(a, b):
    return jnp.dot(a, b, preferred_element_type=F32)


def _dot_nt(a, b):
    return lax.dot_general(a, b, (((1,), (1,)), ((), ())), preferred_element_type=F32)


def _rms(x):
    return x * lax.rsqrt(jnp.mean(x * x, axis=-1, keepdims=True) + EPS)


def _adaln_kernel(c_ref, we_ref, be_ref, wo_ref, bo_ref, me_ref, mo_ref):
    sc = _silu(c_ref[...]).astype(BF16)
    me_ref[...] = _dot(sc, we_ref[...].astype(BF16)) + be_ref[...]
    mo_ref[...] = _dot(sc, wo_ref[...].astype(BF16)) + bo_ref[...]


def _adaln(c_all, we, be, wo, bo):
    rows = c_all.shape[0]
    tn = 512
    n = 3 * D_MODEL
    wspec = pl.BlockSpec((D_MODEL, tn), lambda j: (0, j))
    bspec = pl.BlockSpec((1, tn), lambda j: (0, j))
    ospec = pl.BlockSpec((rows, tn), lambda j: (0, j))
    return pl.pallas_call(
        _adaln_kernel,
        grid=(n // tn,),
        in_specs=[pl.BlockSpec((rows, D_MODEL), lambda j: (0, 0)), wspec, bspec, wspec, bspec],
        out_specs=[ospec, ospec],
        out_shape=[jax.ShapeDtypeStruct((rows, n), F32)] * 2,
        compiler_params=_cparams(("arbitrary",)),
        name="adaln",
    )(c_all, we, be.reshape(1, n), wo, bo.reshape(1, n))


def _norm_mod(x, g, mod):
    shift = mod[:, 0:D_MODEL]
    scale = mod[:, D_MODEL:2 * D_MODEL]
    return _rms(x) * g * (1.0 + scale) + shift


def _shift_rows(u, k, prev_rows):
    row = lax.broadcasted_iota(jnp.int32, u.shape, 0)
    out = pltpu.roll(u, k, 0)
    for idx, pr in enumerate(prev_rows):
        out = jnp.where(row == idx, pr, out)
    return out


N_SEG_E = 8
PHASES_E = 8


def _phase_segments(t, n_seg, n_phases):
    return range(t * n_seg // n_phases, (t + 1) * n_seg // n_phases)


def _even_front_kernel(x_ref, mod_ref, g_ref, *refs, tiles_per_batch, n_chunks):
    ws = refs[:N_SEG_E]
    cw_ref, cb_ref = refs[N_SEG_E:N_SEG_E + 2]
    ya_ref, q_ref, k_ref, v_ref, sgb_ref, kn_ref, vn_ref, cn_ref = refs[N_SEG_E + 2:N_SEG_E + 10]
    h_scr, carry_scr, z_a, z_b = refs[N_SEG_E + 10:]
    nj = D_MODEL // TN
    s = pl.program_id(0)
    c_out = jnp.maximum(s - 1, 0)
    j_out = c_out % nj
    il_out = (c_out // nj) % tiles_per_batch

    @pl.when(s == 0)
    def _():
        z_b[...] = jnp.zeros_like(z_b)
        carry_scr[...] = jnp.zeros_like(carry_scr)

    @pl.when((s < n_chunks) & (s % nj == 0))
    def _():
        h_scr[...] = _norm_mod(x_ref[...], g_ref[...], mod_ref[...]).astype(BF16)

    tm = h_scr.shape[0]
    rb = tm // PHASES_E

    def phase(t, z_w, z_r):
        for seg in _phase_segments(t, N_SEG_E, PHASES_E):
            z_w[seg] = _dot(h_scr[...], ws[seg][...])
        rows = slice(t * rb, (t + 1) * rb)
        bg, cg, xv, ga, q, k, v, gb = (z_r[seg, rows, :] for seg in range(N_SEG_E))
        u = cg * xv
        prev = carry_scr[j_out]
        if t == 0:
            prev = jnp.where(il_out == 0, 0.0, prev)
        p2, p1 = prev[6:7, :], prev[7:8, :]
        u1 = _shift_rows(u, 1, [p1])
        u2 = _shift_rows(u, 2, [p2, p1])
        cw = cw_ref[...]
        conv = cb_ref[...] + cw[0:1, :] * u2 + cw[1:2, :] * u1 + cw[2:3, :] * u
        tail = u[rb - 8:rb, :]
        carry_scr[j_out] = tail
        if t == PHASES_E - 1:
            cn_ref[...] = tail
        ya_ref[rows, :] = (bg * conv * _silu(ga)).astype(BF16)
        kn_ref[:, rows] = k.T
        vn_ref[:, rows] = v.T
        wrows = slice(t * rb // 2, (t + 1) * rb // 2)
        for ref, val in ((q_ref, q * (HD_B ** -0.5 * LOG2E)),
                         (k_ref, k), (v_ref, v)):
            words = pltpu.bitcast(val.astype(BF16), WORD)
            for e in range(TN // LANES):
                ref[e, wrows, :] = words[:, e * LANES:(e + 1) * LANES]
        sgb = _silu(gb).astype(BF16)
        for e in range(TN // LANES):
            sgb_ref[e, rows, :] = sgb[:, e * LANES:(e + 1) * LANES]

    for parity, (z_w, z_r) in enumerate(((z_a, z_b), (z_b, z_a))):
        for t in range(PHASES_E):
            pl.when(s % 2 == parity)(functools.partial(phase, t, z_w, z_r))


def _even_front_prompt(x2d, mod, g, w_in, cw, cb, batch, seq, keep, tm):
    rows = batch * seq
    tpb = seq // tm
    nj = D_MODEL // TN
    n_chunks = (rows // tm) * nj
    off = (seq - keep) // tm
    ppc = TN // LANES

    def c_in(s):
        return jnp.minimum(s, n_chunks - 1)

    def c_out(s):
        return jnp.maximum(s - 1, 0)

    def out_ij(s):
        c = c_out(s)
        return c // nj, c % nj

    def pair_map(s):
        i, j = out_ij(s)
        return (i // tpb, j, i % tpb, 0)

    def keep_map(s):
        i, j = out_ij(s)
        il = i % tpb
        kept = il >= off
        return (i // tpb, jnp.where(kept, j, 0), jnp.where(kept, il - off, 0))

    def tail_map(s):
        i, j = out_ij(s)
        return (i // tpb, 0, jnp.where(i % tpb == tpb - 1, j, 0))

    pair_spec = pl.BlockSpec((None, ppc, tm, LANES), pair_map)
    pair_shape = jax.ShapeDtypeStruct((batch, N_PAIR, seq, LANES), BF16)
    word_spec = pl.BlockSpec((None, ppc, tm // 2, LANES), pair_map)
    word_shape = jax.ShapeDtypeStruct((batch, N_PAIR, seq // 2, LANES), WORD)
    keep_spec = pl.BlockSpec((None, TN, tm), keep_map)
    keep_shape = jax.ShapeDtypeStruct((batch, D_MODEL, keep), F32)
    return pl.pallas_call(
        functools.partial(_even_front_kernel, tiles_per_batch=tpb, n_chunks=n_chunks),
        grid=(n_chunks + 1,),
        in_specs=[
            pl.BlockSpec((tm, D_MODEL), lambda s: (c_in(s) // nj, 0)),
            pl.BlockSpec((None, 1, 3 * D_MODEL), lambda s: (c_in(s) // nj // tpb, 0, 0)),
            pl.BlockSpec((1, D_MODEL), lambda s: (0, 0)),
        ] + [pl.BlockSpec((D_MODEL, TN), lambda s, seg=seg: (0, seg * nj + c_in(s) % nj)) for seg in range(N_SEG_E)] + [
            pl.BlockSpec((3, TN), lambda s: (0, c_out(s) % nj)),
            pl.BlockSpec((1, TN), lambda s: (0, c_out(s) % nj)),
        ],
        out_specs=[
            pl.BlockSpec((tm, TN), lambda s: out_ij(s)),
            word_spec, word_spec, word_spec, pair_spec,
            keep_spec, keep_spec,
            pl.BlockSpec((None, 8, TN), tail_map),
        ],
        out_shape=[
            jax.ShapeDtypeStruct((rows, D_MODEL), BF16),
            word_shape, word_shape, word_shape, pair_shape,
            keep_shape, keep_shape,
            jax.ShapeDtypeStruct((batch, 8, D_MODEL), F32),
        ],
        scratch_shapes=[pltpu.VMEM((tm, D_MODEL), BF16), pltpu.VMEM((nj, 8, TN), F32),
                        pltpu.VMEM((N_SEG_E, tm, TN), F32), pltpu.VMEM((N_SEG_E, tm, TN), F32)],
        compiler_params=_cparams(("arbitrary",)),
        name="even_front_prompt",
    )(x2d, mod.reshape(batch, 1, 3 * D_MODEL), g.reshape(1, D_MODEL), *([w_in] * N_SEG_E), cw, cb.reshape(1, D_MODEL))


STAT_PITCH = {1: 1, 4: 4, 16: 24}
ATTN_UNROLL = 16


def _residue_stream(src, r):
    n_out = src.shape[0] // 4
    halves = [pltpu.unpack_elementwise(src[pl.ds(r // 2 + off, n_out, stride=4), :], index=r % 2,
                                       packed_dtype=BF16, unpacked_dtype=F32) for off in (0, 2)]
    return pltpu.pack_elementwise(halves, packed_dtype=BF16)


def _attn_prompt_kernel(slope_ref, dist_ref, q_ref, k_ref, v_ref, sgb_ref, yb_ref,
                        q4_scr, q16_scr, kv4_scr, kv16_scr, acc_scr, m_scr, l_scr, bias_scr, *, seq):
    hp = pl.program_id(1)
    lane = lax.broadcasted_iota(jnp.int32, (1, LANES), 1)
    half0 = lane < HD_B
    dist = dist_ref[...]
    masked = jnp.full((NK, NK), MASK_DIST, F32)
    for p, d in enumerate(DILATIONS):
        for e in range(2):
            bias = dist * (slope_ref[2 * hp + e] * (float(d) * LOG2E))
            bias_scr[2 * p + e] = bias
            bias_scr[6 + 2 * p + e] = jnp.concatenate([bias[:, NK:], masked], axis=1)

    for a, ref in enumerate((q_ref, k_ref, v_ref)):
        for r in range(4):
            s = _residue_stream(ref, r)
            if a == 0:
                q4_scr[r] = s
            else:
                kv4_scr[a - 1, r] = s
        for r in range(16):
            s = _residue_stream(q4_scr.at[r % 4] if a == 0 else kv4_scr.at[a - 1, r % 4], r // 4)
            if a == 0:
                q16_scr[r] = s
            else:
                kv16_scr[a - 1, r] = s

    ones = jnp.ones((2 * NK, LANES), BF16)
    hw = NK // 2

    def tile(p, d, r, n):
        qw = pl.ds(pl.multiple_of(n * hw, hw), hw)
        kw = pl.ds(pl.multiple_of(jnp.maximum(n - 1, 0) * hw, hw), 2 * hw)
        if d == 1:
            qs, ks, vs = q_ref, k_ref, v_ref
        elif d == 4:
            qs, ks, vs = q4_scr.at[r], kv4_scr.at[0, r], kv4_scr.at[1, r]
        else:
            qs, ks, vs = q16_scr.at[r], kv16_scr.at[0, r], kv16_scr.at[1, r]
        q = pltpu.bitcast(qs[qw, :], BF16)
        kk = pltpu.bitcast(ks[kw, :], BF16)
        vv1 = jnp.concatenate([pltpu.bitcast(vs[kw, :], BF16), ones], axis=1)
        first = jnp.where(n == 0, 1, 0)
        res = []
        for e in range(2):
            qe = jnp.where(half0 if e == 0 else jnp.logical_not(half0), q, jnp.zeros_like(q))
            s = _dot_nt(qe, kk) - bias_scr[6 * first + 2 * p + e]
            m = jnp.max(s, axis=-1, keepdims=True)
            pe = jnp.exp2(s - m).astype(BF16)
            res.append((_dot(pe, vv1), m))
        (a0, m0), (a1, m1) = res
        acc = jnp.where(half0, a0[:, :LANES], a1[:, :LANES])
        l = jnp.where(half0, a0[:, LANES:], a1[:, LANES:])
        m = jnp.where(half0, m0, m1)
        if d > 1:
            pitch = STAT_PITCH[d]
            rows = pl.ds(r + pitch * NK * n, NK, stride=pitch)
            acc_scr[p - 1, rows, :] = acc
            l_scr[p - 1, rows, :] = l
            m_scr[p - 1, rows, :] = m
            return
        accs, ls, ms = [acc], [l], [m]
        for pp in range(1, len(DILATIONS)):
            dd = DILATIONS[pp]
            for ref, vals in ((acc_scr, accs), (l_scr, ls), (m_scr, ms)):
                vals.append(jnp.concatenate(
                    [ref[pp - 1, pl.ds(pl.multiple_of((n * (NK // dd) + g) * STAT_PITCH[dd], 8), dd), :]
                     for g in range(NK // dd)], axis=0) if STAT_PITCH[dd] != dd
                    else ref[pp - 1, pl.ds(pl.multiple_of(n * NK, NK), NK), :])
        top = jnp.maximum(jnp.maximum(ms[0], ms[1]), ms[2])
        ws = [jnp.exp2(x - top) for x in ms]
        num = ws[0] * accs[0] + ws[1] * accs[1] + ws[2] * accs[2]
        den = ws[0] * ls[0] + ws[1] * ls[1] + ws[2] * ls[2]
        rows = pl.ds(pl.multiple_of(n * NK, NK), NK)
        yb_ref[rows, :] = (num * (1.0 / den) * sgb_ref[rows, :].astype(F32)).astype(BF16)

    for p, d in reversed(list(enumerate(DILATIONS))):
        def body(it, c, p=p, d=d):
            for u in range(ATTN_UNROLL):
                t = it * ATTN_UNROLL + u
                if d == 1:
                    tile(p, d, 0, t)
                else:
                    tile(p, d, t % d, t // d)
            return c

        lax.fori_loop(0, seq // NK // ATTN_UNROLL, body, 0)


def _band_distance():
    qi = np.arange(NK)[:, None]
    kj = np.arange(2 * NK)[None, :]
    dist = NK + qi - kj
    return jnp.asarray(np.where((dist >= 0) & (dist <= NK), dist, MASK_DIST), dtype=F32)


def _alibi_slopes():
    return jnp.asarray(2.0 ** (-8.0 * np.arange(1, H_B + 1) / H_B), dtype=F32)


def _attn_prompt(q, k, v, sgb, batch, seq):
    word_spec = pl.BlockSpec((None, None, seq // 2, LANES), lambda b, hp: (b, hp, 0, 0))
    stat_rows = max(seq // d * STAT_PITCH[d] for d in DILATIONS)
    stat = pltpu.VMEM((len(DILATIONS) - 1, stat_rows, LANES), F32)
    return pl.pallas_call(
        functools.partial(_attn_prompt_kernel, seq=seq),
        grid=(batch, N_PAIR),
        in_specs=[pl.BlockSpec(memory_space=pltpu.SMEM),
                  pl.BlockSpec((NK, 2 * NK), lambda b, hp: (0, 0)),
                  word_spec, word_spec, word_spec,
                  pl.BlockSpec((None, None, seq, LANES), lambda b, hp: (b, hp, 0, 0))],
        out_specs=pl.BlockSpec((None, seq, LANES), lambda b, hp: (b, 0, hp)),
        out_shape=jax.ShapeDtypeStruct((batch, seq, D_MODEL), BF16),
        scratch_shapes=[pltpu.VMEM((4, seq // 8, LANES), WORD),
                        pltpu.VMEM((16, seq // 32, LANES), WORD),
                        pltpu.VMEM((2, 4, seq // 8, LANES), WORD),
                        pltpu.VMEM((2, 16, seq // 32, LANES), WORD),
                        stat, stat, stat,
                        pltpu.VMEM((12, NK, 2 * NK), F32)],
        compiler_params=_cparams(("arbitrary", "arbitrary")),
        name="attn_prompt",
    )(_alibi_slopes(), _band_distance(), q, k, v, sgb)


def _outproj_mid_kernel(ya_ref, yb_ref, x_ref, mod_ref, w1_ref, w2_ref, g_ref, mod2_ref, x1_ref, h_ref):
    gate = mod_ref[:, 2 * D_MODEL:3 * D_MODEL]
    x1 = x_ref[...] + gate * (_dot(ya_ref[...], w1_ref[...]) + _dot(yb_ref[...].astype(BF16), w2_ref[...]))
    x1_ref[...] = x1
    h_ref[...] = _norm_mod(x1, g_ref[...], mod2_ref[...]).astype(BF16)


def _outproj_final_kernel(ya_ref, yb_ref, x_ref, mod_ref, w1_ref, w2_ref, g_ref, y_ref):
    gate = mod_ref[:, 2 * D_MODEL:3 * D_MODEL]
    x1 = x_ref[...] + gate * (_dot(ya_ref[...], w1_ref[...]) + _dot(yb_ref[...].astype(BF16), w2_ref[...]))
    y_ref[...] = _rms(x1) * g_ref[...]


def _outproj(ya, yb, x2d, mod, w_out, g, mod2, tm, rows_per_mod):
    rows = x2d.shape[0]
    half = w_out.shape[0] // 2
    row_spec = pl.BlockSpec((tm, D_MODEL), lambda i: (i, 0))
    if rows_per_mod == 1:
        mod_spec = pl.BlockSpec((tm, 3 * D_MODEL), lambda i: (i, 0))
        mods = (mod, mod2)
    else:
        tpb = rows_per_mod // tm
        mod_spec = pl.BlockSpec((None, 1, 3 * D_MODEL), lambda i: (i // tpb, 0, 0))
        mods = tuple(None if m is None else m.reshape(-1, 1, 3 * D_MODEL) for m in (mod, mod2))
    w1_spec = pl.BlockSpec((half, D_MODEL), lambda i: (0, 0))
    w2_spec = pl.BlockSpec((half, D_MODEL), lambda i: (1, 0))
    g_spec = pl.BlockSpec((1, D_MODEL), lambda i: (0, 0))
    common = dict(grid=(rows // tm,), compiler_params=_cparams(("arbitrary",)))
    if mod2 is None:
        return pl.pallas_call(
            _outproj_final_kernel,
            in_specs=[row_spec, row_spec, row_spec, mod_spec, w1_spec, w2_spec, g_spec],
            out_specs=row_spec,
            out_shape=jax.ShapeDtypeStruct((rows, D_MODEL), F32),
            name="outproj_final", **common,
        )(ya, yb, x2d, mods[0], w_out, w_out, g.reshape(1, D_MODEL))
    return pl.pallas_call(
        _outproj_mid_kernel,
        in_specs=[row_spec, row_spec, row_spec, mod_spec, w1_spec, w2_spec, g_spec, mod_spec],
        out_specs=[row_spec, row_spec],
        out_shape=[jax.ShapeDtypeStruct((rows, D_MODEL), F32), jax.ShapeDtypeStruct((rows, D_MODEL), BF16)],
        name="outproj_mid", **common,
    )(ya, yb, x2d, mods[0], w_out, w_out, g.reshape(1, D_MODEL), mods[1])


def _pick_by_chunk(j, vals):
    out = vals[-1]
    for idx in range(len(vals) - 2, -1, -1):
        out = jnp.where(j == idx, vals[idx], out)
    return out


N_SEG_O = 6
PHASES_O = 4


def _odd_front_kernel(h_ref, *refs, tiles_per_batch):
    ws = refs[:N_SEG_O]
    pw_ref, ps_ref = refs[N_SEG_O:N_SEG_O + 2]
    yc_ref, q_ref, kt_ref, v_ref, sgd_ref, pn_ref = refs[N_SEG_O + 2:N_SEG_O + 8]
    carry_scr, z_a, z_b = refs[N_SEG_O + 8:]
    nj = D_MODEL // TN
    s = pl.program_id(0)
    c_out = jnp.maximum(s - 1, 0)
    j_out = c_out % nj
    il_out = (c_out // nj) % tiles_per_batch
    tm = h_ref.shape[0]
    rb = tm // PHASES_O

    @pl.when(s == 0)
    def _():
        z_b[...] = jnp.zeros_like(z_b)
        carry_scr[...] = jnp.zeros_like(carry_scr)

    def phase(t, z_w, z_r):
        for seg in _phase_segments(t, N_SEG_O, PHASES_O):
            z_w[seg] = _dot(h_ref[...], ws[seg][...])
        rows = slice(t * rb, (t + 1) * rb)
        u, gc, q, k, v, gd = (z_r[seg, rows, :] for seg in range(N_SEG_O))
        prev = carry_scr[j_out]
        if t == 0:
            prev = jnp.where(il_out == 0, 0.0, prev)
        tail = u[rb - POOL_PREV:rb, :]
        carry_scr[j_out] = tail
        if t == PHASES_O - 1:
            pn_ref[...] = tail
        ext = jnp.concatenate([prev, u], axis=0)
        sums = []
        acc = ext
        for sh in (1, 2, 4, 8):
            acc = acc + pltpu.roll(acc, sh, 0)
            sums.append(acc[POOL_PREV:, :])
        win = _pick_by_chunk(j_out, sums)
        width = _pick_by_chunk(j_out, [float(w) for w in POOL_SIZES])
        pos = (il_out * tm + t * rb + lax.broadcasted_iota(jnp.int32, (rb, 1), 0)).astype(F32)
        pooled = win / jnp.minimum(width, pos + 1.0) - u
        mixed = _dot(pooled.astype(BF16), pw_ref[...]) * ps_ref[...]
        yc_ref[rows, :] = (mixed * _silu(gc)).astype(BF16)
        q_ref[rows, :] = q.astype(BF16)
        kt_ref[:, rows] = (k * (DK_D ** -0.5)).T.astype(BF16)
        v_ref[rows, :] = v.astype(BF16)
        sgd_ref[rows, :] = _silu(gd).astype(BF16)

    for parity, (z_w, z_r) in enumerate(((z_a, z_b), (z_b, z_a))):
        for t in range(PHASES_O):
            pl.when(s % 2 == parity)(functools.partial(phase, t, z_w, z_r))


def _odd_front_prompt(h2d, w_in, pw, ps, batch, seq, tm):
    rows = batch * seq
    tpb = seq // tm
    nj = D_MODEL // TN
    n_chunks = (rows // tm) * nj

    def c_in(s):
        return jnp.minimum(s, n_chunks - 1)

    def out_ij(s):
        c = jnp.maximum(s - 1, 0)
        return c // nj, c % nj

    def head_map(s):
        i, j = out_ij(s)
        return (i // tpb, j, i % tpb, 0)

    def kt_map(s):
        i, j = out_ij(s)
        return (i // tpb, j, 0, i % tpb)

    def tail_map(s):
        i, j = out_ij(s)
        return (i // tpb, 0, jnp.where(i % tpb == tpb - 1, j, 0))

    head_spec = pl.BlockSpec((None, None, tm, TN), head_map)
    head_shape = jax.ShapeDtypeStruct((batch, H_D, seq, DK_D), BF16)
    tok_spec = pl.BlockSpec((tm, TN), out_ij)
    tok_shape = jax.ShapeDtypeStruct((rows, D_MODEL), BF16)
    zbuf = pltpu.VMEM((N_SEG_O, tm, TN), F32)
    return pl.pallas_call(
        functools.partial(_odd_front_kernel, tiles_per_batch=tpb),
        grid=(n_chunks + 1,),
        in_specs=[pl.BlockSpec((tm, D_MODEL), lambda s: (c_in(s) // nj, 0))] + [
                  pl.BlockSpec((D_MODEL, TN), lambda s, seg=seg: (0, seg * nj + c_in(s) % nj))
                  for seg in range(N_SEG_O)] + [
                  pl.BlockSpec((None, TN, TN), lambda s: (out_ij(s)[1], 0, 0)),
                  pl.BlockSpec((1, TN), lambda s: (0, out_ij(s)[1]))],
        out_specs=[tok_spec, head_spec, pl.BlockSpec((None, None, TN, tm), kt_map), head_spec, tok_spec,
                   pl.BlockSpec((None, POOL_PREV, TN), tail_map)],
        out_shape=[tok_shape, head_shape, jax.ShapeDtypeStruct((batch, H_D, DK_D, seq), BF16),
                   head_shape, tok_shape,
                   jax.ShapeDtypeStruct((batch, POOL_PREV, D_MODEL), F32)],
        scratch_shapes=[pltpu.VMEM((nj, POOL_PREV, TN), F32), zbuf, zbuf],
        compiler_params=_cparams(("arbitrary",)),
        name="odd_front_prompt",
    )(h2d, *([w_in] * N_SEG_O), pw, ps.reshape(1, D_MODEL))


RET_SBLK = 1024


def _ret_prompt_kernel(q_ref, kt_ref, v_ref, sgd_ref, dec_ref, cdec_ref, kdec_ref,
                       yd_ref, st_ref, state_scr, *, sblk, n_sblk):
    sb = pl.program_id(1)

    @pl.when(sb == 0)
    def _():
        state_scr[...] = jnp.zeros_like(state_scr)

    def chunk(c, carry):
        rows = pl.ds(pl.multiple_of(c * RET_CHUNK, RET_CHUNK), RET_CHUNK)
        for h in range(H_D):
            cols = slice(h * DK_D, (h + 1) * DK_D)
            q = q_ref[h, rows, :]
            kt = kt_ref[h, :, rows]
            v = v_ref[h, rows, :]
            state = state_scr[h]
            scores = _dot(q, kt) * dec_ref[h]
            o = _dot(scores.astype(BF16), v) + _dot(q, state.astype(BF16)) * cdec_ref[h]
            kd = (kt.astype(F32) * kdec_ref[h]).astype(BF16)
            state_scr[h] = float(np.exp(RET_CHUNK * RET_LOG_G[h])) * state + _dot(kd, v)
            yd_ref[rows, cols] = (_rms(o) * sgd_ref[rows, cols].astype(F32)).astype(BF16)
        return carry

    lax.fori_loop(0, sblk // RET_CHUNK, chunk, 0, unroll=4)

    @pl.when(sb == n_sblk - 1)
    def _():
        st_ref[...] = state_scr[...]


def _ret_consts():
    t = np.arange(RET_CHUNK, dtype=np.float64)
    diff = t[:, None] - t[None, :]
    lg = np.asarray(RET_LOG_G)[:, None, None]
    dec = np.where(diff >= 0, np.exp(np.maximum(diff, 0.0)[None] * lg), 0.0)
    cdec = np.broadcast_to(np.exp((t + 1.0)[None, :, None] * lg), (H_D, RET_CHUNK, DK_D))
    kdec = np.broadcast_to(np.exp((RET_CHUNK - 1.0 - t)[None, None, :] * lg), (H_D, DK_D, RET_CHUNK))
    return jnp.asarray(dec, F32), jnp.asarray(cdec, F32), jnp.asarray(kdec, F32)


def _ret_prompt(q, kt, v, sgd, batch, seq):
    dec, cdec, kdec = _ret_consts()
    n_sblk = seq // RET_SBLK
    head_spec = pl.BlockSpec((None, H_D, RET_SBLK, DK_D), lambda b, s: (b, 0, s, 0))
    tok_spec = pl.BlockSpec((None, RET_SBLK, D_MODEL), lambda b, s: (b, s, 0))

    def const_spec(shape):
        return pl.BlockSpec(shape, lambda b, s: (0, 0, 0))

    return pl.pallas_call(
        functools.partial(_ret_prompt_kernel, sblk=RET_SBLK, n_sblk=n_sblk),
        grid=(batch, n_sblk),
        in_specs=[head_spec, pl.BlockSpec((None, H_D, DK_D, RET_SBLK), lambda b, s: (b, 0, 0, s)), head_spec,
                  tok_spec, const_spec(dec.shape), const_spec(cdec.shape), const_spec(kdec.shape)],
        out_specs=[tok_spec, pl.BlockSpec((None, H_D, DK_D, DK_D), lambda b, s: (b, 0, 0, 0))],
        out_shape=[jax.ShapeDtypeStruct((batch, seq, D_MODEL), BF16),
                   jax.ShapeDtypeStruct((batch, H_D, DK_D, DK_D), F32)],
        scratch_shapes=[pltpu.VMEM((H_D, DK_D, DK_D), F32)],
        compiler_params=_cparams(("arbitrary", "arbitrary")),
        name="ret_prompt",
    )(q, kt, v, sgd.reshape(batch, seq, D_MODEL), dec, cdec, kdec)


def _even_front_sample_kernel(x_ref, mod_ref, g_ref, wbg, wcg, wxv, wga, wq, wk, wv, wgb, cw_ref, cb_ref,
                              prev_ref, ya_ref, q_ref, k_ref, v_ref, sgb_ref, cn_ref):
    h = _norm_mod(x_ref[...], g_ref[...], mod_ref[...]).astype(BF16)
    u = _dot(h, wcg[...]) * _dot(h, wxv[...])
    cw = cw_ref[...]
    conv = cb_ref[...] + cw[0:1, :] * prev_ref[0] + cw[1:2, :] * prev_ref[1] + cw[2:3, :] * u
    cn_ref[0] = prev_ref[1]
    cn_ref[1] = u
    ya_ref[...] = (_dot(h, wbg[...]) * conv * _silu(_dot(h, wga[...]))).astype(BF16)
    q_ref[...] = _dot(h, wq[...]) * (HD_B ** -0.5)
    k_ref[...] = _dot(h, wk[...])
    v_ref[...] = _dot(h, wv[...])
    sgb_ref[...] = _silu(_dot(h, wgb[...]))


def _even_front_sample(x2d, mod, g, w_in, cw, cb, prev_t):
    rows = x2d.shape[0]
    nj = D_MODEL // TN

    def wspec(s):
        return pl.BlockSpec((D_MODEL, TN), lambda j, s=s: (0, s * nj + j))

    full = pl.BlockSpec((rows, D_MODEL), lambda j: (0, 0))
    col = pl.BlockSpec((rows, TN), lambda j: (0, j))
    st = pl.BlockSpec((2, rows, TN), lambda j: (0, 0, j))
    colf = jax.ShapeDtypeStruct((rows, D_MODEL), F32)
    return pl.pallas_call(
        _even_front_sample_kernel,
        grid=(nj,),
        in_specs=[full, pl.BlockSpec((rows, 3 * D_MODEL), lambda j: (0, 0)),
                  pl.BlockSpec((1, D_MODEL), lambda j: (0, 0))] + [wspec(s) for s in range(8)] + [
            pl.BlockSpec((3, TN), lambda j: (0, j)), pl.BlockSpec((1, TN), lambda j: (0, j)), st],
        out_specs=[col, col, col, col, col, st],
        out_shape=[jax.ShapeDtypeStruct((rows, D_MODEL), BF16), colf, colf, colf, colf,
                   jax.ShapeDtypeStruct((2, rows, D_MODEL), F32)],
        compiler_params=_cparams(("arbitrary",)),
        name="even_front_sample",
    )(x2d, mod, g.reshape(1, D_MODEL), *([w_in] * 8), cw, cb.reshape(1, D_MODEL), prev_t)


def _attn_sample_kernel(slope_ref, pos_ref, q_ref, kn_ref, vn_ref, sgb_ref, kt_ref, vt_ref, yb_ref):
    gw = q_ref.shape[-1]
    me = pl.ds(pl.program_id(1) % 8, 1)
    row = lax.broadcasted_iota(jnp.int32, (8, gw), 0)
    col = lax.broadcasted_iota(jnp.int32, (8, gw), 1)
    own = (col >= row * HD_B) & (col < (row + 1) * HD_B)
    qm = jnp.where(own, q_ref[me, :], 0.0)
    s_self = jnp.sum(qm * kn_ref[me, :], axis=-1, keepdims=True)
    s_all = _dot(qm.astype(BF16), kt_ref[...].astype(BF16))
    s_all = s_all - slope_ref[:, 0:1] * pos_ref[0:1, :]
    v_self = vn_ref[me, :]
    probs, stats = [], []
    for p in range(len(DILATIONS)):
        s = s_all - pos_ref[p + 1:p + 2, :]
        m = jnp.maximum(jnp.max(s, axis=-1, keepdims=True), s_self)
        pe = jnp.exp(s - m)
        p_self = jnp.exp(s_self - m)
        probs.append(pe)
        stats.append((m, jnp.sum(pe, axis=-1, keepdims=True) + p_self, p_self))
    o_all = _dot_nt(jnp.concatenate(probs, axis=0).astype(BF16), vt_ref[...].astype(BF16))
    outs, lses = [], []
    for p, (m, l, p_self) in enumerate(stats):
        outs.append((o_all[8 * p:8 * p + 8, :] + p_self * v_self) * (1.0 / l))
        lses.append(m + jnp.log(l))
    m = jnp.maximum(jnp.maximum(lses[0], lses[1]), lses[2])
    ws = [jnp.exp(x - m) for x in lses]
    o = (ws[0] * outs[0] + ws[1] * outs[1] + ws[2] * outs[2]) * (1.0 / (ws[0] + ws[1] + ws[2]))
    o = jnp.sum(jnp.where(own, o, 0.0), axis=0, keepdims=True)
    yb_ref[me, :] = o * sgb_ref[me, :]


def _attn_sample(q, kn, vn, sgb, cache_kt, cache_vt):
    rows, _, wb = cache_kt.shape
    gh = 8
    gw = gh * HD_B
    ng = H_B // gh
    back = wb - np.arange(wb, dtype=np.float64)
    pos = [back] + [np.where((back % d == 0) & (back <= NK * d), 0.0, MASK_ADD) for d in DILATIONS]
    slopes = (2.0 ** (-8.0 * np.arange(1, H_B + 1) / H_B)).reshape(ng, gh)
    slope_arr = jnp.asarray(np.broadcast_to(slopes[:, :, None], (ng, 8, LANES)), F32)
    row_spec = pl.BlockSpec((8, gw), lambda g, b: (b // 8, g))
    t_spec = pl.BlockSpec((None, gw, wb), lambda g, b: (b, g, 0))
    return pl.pallas_call(
        _attn_sample_kernel,
        grid=(ng, rows),
        in_specs=[pl.BlockSpec((None, 8, LANES), lambda g, b: (g, 0, 0)),
                  pl.BlockSpec((4, wb), lambda g, b: (0, 0)),
                  row_spec, row_spec, row_spec, row_spec, t_spec, t_spec],
        out_specs=row_spec,
        out_shape=jax.ShapeDtypeStruct((rows, D_MODEL), F32),
        compiler_params=_cparams(("arbitrary", "arbitrary")),
        name="attn_sample",
    )(slope_arr, jnp.asarray(np.stack(pos), F32), q, kn, vn, sgb, cache_kt, cache_vt)


def _odd_front_sample_kernel(h_ref, wu, wgc, wq, wk, wv, wgd, pw_ref, ps_ref, prev_ref,
                             yc_ref, q_ref, k_ref, v_ref, sgd_ref, pn_ref):
    j = pl.program_id(0)
    h = h_ref[...]
    u = _dot(h, wu[...])
    n_prev = prev_ref.shape[0]
    sums = []
    s = jnp.zeros_like(u)
    back = 0
    for w in POOL_SIZES:
        while back < w - 1:
            s = s + prev_ref[n_prev - 1 - back]
            back += 1
        sums.append(s)
    win = _pick_by_chunk(j, sums) + u
    inv_w = _pick_by_chunk(j, [1.0 / w for w in POOL_SIZES])
    pooled = win * inv_w - u
    mixed = _dot(pooled.astype(BF16), pw_ref[...]) * ps_ref[...]
    yc_ref[...] = (mixed * _silu(_dot(h, wgc[...]))).astype(BF16)
    q_ref[...] = _dot(h, wq[...])
    k_ref[...] = _dot(h, wk[...]) * (DK_D ** -0.5)
    v_ref[...] = _dot(h, wv[...])
    sgd_ref[...] = _silu(_dot(h, wgd[...]))
    for t in range(n_prev - 1):
        pn_ref[t] = prev_ref[t + 1]
    pn_ref[n_prev - 1] = u


def _odd_front_sample(h2d, w_in, pw, ps, prev_t):
    rows = h2d.shape[0]
    n_prev = prev_t.shape[0]
    nj = D_MODEL // TN

    def wspec(s):
        return pl.BlockSpec((D_MODEL, TN), lambda j, s=s: (0, s * nj + j))

    col = pl.BlockSpec((rows, TN), lambda j: (0, j))
    st = pl.BlockSpec((n_prev, rows, TN), lambda j: (0, 0, j))
    colf = jax.ShapeDtypeStruct((rows, D_MODEL), F32)
    return pl.pallas_call(
        _odd_front_sample_kernel,
        grid=(nj,),
        in_specs=[pl.BlockSpec((rows, D_MODEL), lambda j: (0, 0))] + [wspec(s) for s in range(6)] + [
            pl.BlockSpec((None, TN, TN), lambda j: (j, 0, 0)), pl.BlockSpec((1, TN), lambda j: (0, j)), st],
        out_specs=[col, col, col, col, col, st],
        out_shape=[jax.ShapeDtypeStruct((rows, D_MODEL), BF16), colf, colf, colf, colf,
                   jax.ShapeDtypeStruct((n_prev, rows, D_MODEL), F32)],
        compiler_params=_cparams(("arbitrary",)),
        name="odd_front_sample",
    )(h2d, *([w_in] * 6), pw, ps.reshape(1, D_MODEL), prev_t)


RET_SAMPLE_ROWS = 8


def _ret_sample_kernel(q_ref, k_ref, v_ref, sgd_ref, st_ref, yd_ref, sn_ref):
    row = lax.broadcasted_iota(jnp.int32, (DK_D, DK_D), 0)
    col = lax.broadcasted_iota(jnp.int32, (DK_D, DK_D), 1)
    qs, ks, vs, gs = q_ref[...], k_ref[...], v_ref[...], sgd_ref[...]
    out_rows = []
    for i in range(RET_SAMPLE_ROWS):
        out_heads = []
        for h in range(H_D):
            g = float(np.exp(RET_LOG_G[h]))
            cols = slice(h * DK_D, (h + 1) * DK_D)
            q, k, v = qs[i:i + 1, cols], ks[i:i + 1, cols], vs[i:i + 1, cols]
            state = st_ref[i, h]
            qk = jnp.sum(q * k, axis=-1, keepdims=True)
            q8 = jnp.broadcast_to(q, (8, DK_D)).astype(BF16)
            cross = _dot(q8, state.astype(BF16))[0:1, :]
            o = qk * v + g * cross
            k_diag = jnp.where(row == col, jnp.broadcast_to(k, (DK_D, DK_D)), 0.0).astype(BF16)
            v_rows = jnp.broadcast_to(v, (DK_D, DK_D)).astype(BF16)
            sn_ref[i, h] = g * state + _dot(k_diag, v_rows)
            out_heads.append(_rms(o) * gs[i:i + 1, cols])
        out_rows.append(jnp.concatenate(out_heads, axis=1))
    yd_ref[...] = jnp.concatenate(out_rows, axis=0)


def _ret_sample(q, k, v, sgd, state):
    rows = q.shape[0]
    rs = RET_SAMPLE_ROWS
    row_spec = pl.BlockSpec((rs, D_MODEL), lambda i: (i, 0))
    st_spec = pl.BlockSpec((rs, H_D, DK_D, DK_D), lambda i: (i, 0, 0, 0))
    return pl.pallas_call(
        _ret_sample_kernel,
        grid=(rows // rs,),
        in_specs=[row_spec, row_spec, row_spec, row_spec, st_spec],
        out_specs=[row_spec, st_spec],
        out_shape=[jax.ShapeDtypeStruct((rows, D_MODEL), F32),
                   jax.ShapeDtypeStruct((rows, H_D, DK_D, DK_D), F32)],
        compiler_params=_cparams(("arbitrary",)),
        name="ret_sample",
    )(q, k, v, sgd, state)


def kernel(x_prompt, x_sample, c_prompt, c_sample, state_conv, cache_win_k, cache_win_v, state_pool, state_ret,
           norm_e, ada_w_e, ada_b_e, w_in_e, conv_w, conv_b, w_out_e, norm_o, ada_w_o, ada_b_o, w_in_o,
           pool_w, pool_scale, w_out_o, norm_f):
    batch, seq, d = x_prompt.shape
    sb = x_sample.shape[0]
    assert d == D_MODEL and x_sample.shape[1] == 1
    assert norm_e.shape[0] == 1 and norm_o.shape[0] == 1
    wb = cache_win_k.shape[2]
    keep = min(wb, seq)
    assert wb == DILATIONS[-1] * NK and seq % (DILATIONS[-1] * NK) == 0
    tm = TM_OUT

    n_c = batch + sb
    pad = (-n_c) % 8
    c_all = jnp.concatenate([c_prompt, c_sample, jnp.zeros((pad, d), F32)], axis=0)
    mod_e, mod_o = _adaln(c_all, ada_w_e[0], ada_b_e[0], ada_w_o[0], ada_b_o[0])
    mod_e_p, mod_e_s = mod_e[:batch], mod_e[batch:n_c]
    mod_o_p, mod_o_s = mod_o[:batch], mod_o[batch:n_c]

    w_in_e16 = w_in_e[0].astype(BF16)
    w_out_e16 = w_out_e[0].astype(BF16)
    w_in_o16 = w_in_o[0].astype(BF16)
    w_out_o16 = w_out_o[0].astype(BF16)
    pool_w16 = pool_w[0].astype(BF16)

    xp2d = x_prompt.reshape(batch * seq, d)
    ya, q, k, v, sgb, k_new, v_new, conv_tail = _even_front_prompt(
        xp2d, mod_e_p, norm_e[0], w_in_e16, conv_w[0], conv_b[0], batch, seq, keep, TM_FRONT)
    yb = _attn_prompt(q, k, v, sgb, batch, seq).reshape(batch * seq, d)
    x1, h1 = _outproj(ya, yb, xp2d, mod_e_p, w_out_e16, norm_o[0], mod_o_p, tm, seq)
    yc, rq, rk, rv, sgd, pool_tail = _odd_front_prompt(h1, w_in_o16, pool_w16, pool_scale[0], batch, seq, TM_FRONT)
    yd, ret_p = _ret_prompt(rq, rk, rv, sgd, batch, seq)
    y_prompt = _outproj(yc, yd.reshape(batch * seq, d), x1, mod_o_p, w_out_o16, norm_f, None, tm, seq)

    xs2d = x_sample.reshape(sb, d)
    conv_prev_t = jnp.transpose(state_conv[0], (1, 0, 2))
    ya_s, q_s, k_s, v_s, sgb_s, conv_s_t = _even_front_sample(
        xs2d, mod_e_s, norm_e[0], w_in_e16, conv_w[0], conv_b[0], conv_prev_t)
    cache_kt = jnp.transpose(cache_win_k[0], (0, 2, 3, 1)).reshape(sb, d, wb)
    cache_vt = jnp.transpose(cache_win_v[0], (0, 2, 3, 1)).reshape(sb, d, wb)
    yb_s = _attn_sample(q_s, k_s, v_s, sgb_s, cache_kt, cache_vt)
    x1_s, h1_s = _outproj(ya_s, yb_s, xs2d, mod_e_s, w_out_e16, norm_o[0], mod_o_s, sb, 1)
    pool_prev_t = jnp.transpose(state_pool[0], (1, 0, 2))
    yc_s, rq_s, rk_s, rv_s, sgd_s, pool_s_t = _odd_front_sample(h1_s, w_in_o16, pool_w16, pool_scale[0], pool_prev_t)
    yd_s, ret_s = _ret_sample(rq_s, rk_s, rv_s, sgd_s, state_ret[0])
    y_sample = _outproj(yc_s, yd_s, x1_s, mod_o_s, w_out_o16, norm_f, None, sb, 1)

    return (
        y_prompt.reshape(batch, seq, d),
        y_sample.reshape(sb, 1, d),
        conv_tail[:, 6:8][None],
        jnp.transpose(conv_s_t, (1, 0, 2))[None],
        jnp.transpose(k_new.reshape(batch, H_B, HD_B, keep), (0, 3, 1, 2))[None],
        k_s.reshape(1, sb, 1, H_B, HD_B),
        jnp.transpose(v_new.reshape(batch, H_B, HD_B, keep), (0, 3, 1, 2))[None],
        v_s.reshape(1, sb, 1, H_B, HD_B),
        pool_tail[:, 1:][None],
        jnp.transpose(pool_s_t, (1, 0, 2))[None],
        ret_p[None],
        ret_s[None],
    )
```

```python
import functools

import numpy as np
import jax
import jax.numpy as jnp
from jax import lax
from jax.experimental import pallas as pl
from jax.experimental.pallas import tpu as pltpu

F32 = jnp.float32
BF16 = jnp.bfloat16
WORD = jnp.uint32

D_MODEL = 1024
EPS = 1e-6
H_B = 16
HD_B = 64
N_PAIR = H_B // 2
LANES = 128
NK = 128
DILATIONS = (1, 4, 16)
POOL_SIZES = (2, 4, 8, 16)
POOL_PREV = 16
H_D = 4
DK_D = 256
RET_CHUNK = 256
TN = 256
TM_FRONT = 1024
TM_OUT = 1024
MASK_DIST = 1e9
LOG2E = float(np.log2(np.e))
MASK_ADD = 1e30
VMEM_LIMIT = 56 * 1024 * 1024

RET_LOG_G = [float(np.log(1.0 - 2.0 ** (-5.0 - h))) for h in range(H_D)]


def _cparams(sem):
    return pltpu.CompilerParams(dimension_semantics=sem, vmem_limit_bytes=VMEM_LIMIT)


def _silu(x):
    return x * (1.0 / (1.0 + jnp.exp(-x)))


def _dot(a, b):
    return jnp.dot(a, b, preferred_element_type=F32)


def _dot_nt(a, b):
    return lax.dot_general(a, b, (((1,), (1,)), ((), ())), preferred_element_type=F32)


def _rms(x):
    return x * lax.rsqrt(jnp.mean(x * x, axis=-1, keepdims=True) + EPS)


def _adaln_kernel(c_ref, we_ref, be_ref, wo_ref, bo_ref, me_ref, mo_ref):
    sc = _silu(c_ref[...]).astype(BF16)
    me_ref[...] = _dot(sc, we_ref[...].astype(BF16)) + be_ref[...]
    mo_ref[...] = _dot(sc, wo_ref[...].astype(BF16)) + bo_ref[...]


def _adaln(c_all, we, be, wo, bo):
    rows = c_all.shape[0]
    tn = 512
    n = 3 * D_MODEL
    wspec = pl.BlockSpec((D_MODEL, tn), lambda j: (0, j))
    bspec = pl.BlockSpec((1, tn), lambda j: (0, j))
    ospec = pl.BlockSpec((rows, tn), lambda j: (0, j))
    return pl.pallas_call(
        _adaln_kernel,
        grid=(n // tn,),
        in_specs=[pl.BlockSpec((rows, D_MODEL), lambda j: (0, 0)), wspec, bspec, wspec, bspec],
        out_specs=[ospec, ospec],
        out_shape=[jax.ShapeDtypeStruct((rows, n), F32)] * 2,
        compiler_params=_cparams(("arbitrary",)),
        name="adaln",
    )(c_all, we, be.reshape(1, n), wo, bo.reshape(1, n))


def _norm_mod(x, g, mod):
    shift = mod[:, 0:D_MODEL]
    scale = mod[:, D_MODEL:2 * D_MODEL]
    return _rms(x) * g * (1.0 + scale) + shift


def _shift_rows(u, k, prev_rows):
    row = lax.broadcasted_iota(jnp.int32, u.shape, 0)
    out = pltpu.roll(u, k, 0)
    for idx, pr in enumerate(prev_rows):
        out = jnp.where(row == idx, pr, out)
    return out


N_SEG_E = 8
PHASES_E = 8


def _phase_segments(t, n_seg, n_phases):
    return range(t * n_seg // n_phases, (t + 1) * n_seg // n_phases)


def _even_front_kernel(x_ref, mod_ref, g_ref, *refs, tiles_per_batch, n_chunks):
    ws = refs[:N_SEG_E]
    cw_ref, cb_ref = refs[N_SEG_E:N_SEG_E + 2]
    ya_ref, q_ref, k_ref, v_ref, sgb_ref, kn_ref, vn_ref, cn_ref = refs[N_SEG_E + 2:N_SEG_E + 10]
    h_scr, carry_scr, z_a, z_b = refs[N_SEG_E + 10:]
    nj = D_MODEL // TN
    s = pl.program_id(0)
    c_out = jnp.maximum(s - 1, 0)
    j_out = c_out % nj
    il_out = (c_out // nj) % tiles_per_batch

    @pl.when(s == 0)
    def _():
        z_b[...] = jnp.zeros_like(z_b)
        carry_scr[...] = jnp.zeros_like(carry_scr)

    @pl.when((s < n_chunks) & (s % nj == 0))
    def _():
        h_scr[...] = _norm_mod(x_ref[...], g_ref[...], mod_ref[...]).astype(BF16)

    tm = h_scr.shape[0]
    rb = tm // PHASES_E

    def phase(t, z_w, z_r):
        for seg in _phase_segments(t, N_SEG_E, PHASES_E):
            z_w[seg] = _dot(h_scr[...], ws[seg][...])
        rows = slice(t * rb, (t + 1) * rb)
        bg, cg, xv, ga, q, k, v, gb = (z_r[seg, rows, :] for seg in range(N_SEG_E))
        u = cg * xv
        prev = carry_scr[j_out]
        if t == 0:
            prev = jnp.where(il_out == 0, 0.0, prev)
        p2, p1 = prev[6:7, :], prev[7:8, :]
        u1 = _shift_rows(u, 1, [p1])
        u2 = _shift_rows(u, 2, [p2, p1])
        cw = cw_ref[...]
        conv = cb_ref[...] + cw[0:1, :] * u2 + cw[1:2, :] * u1 + cw[2:3, :] * u
        tail = u[rb - 8:rb, :]
        carry_scr[j_out] = tail
        if t == PHASES_E - 1:
            cn_ref[...] = tail
        ya_ref[rows, :] = (bg * conv * _silu(ga)).astype(BF16)
        kn_ref[rows, :] = k
        vn_ref[rows, :] = v
        wrows = slice(t * rb // 2, (t + 1) * rb // 2)
        for ref, val in ((q_ref, q * (HD_B ** -0.5 * LOG2E)),
                         (k_ref, k), (v_ref, v)):
            words = pltpu.bitcast(val.astype(BF16), WORD)
            for e in range(TN // LANES):
                ref[e, wrows, :] = words[:, e * LANES:(e + 1) * LANES]
        sgb = _silu(gb).astype(BF16)
        for e in range(TN // LANES):
            sgb_ref[e, rows, :] = sgb[:, e * LANES:(e + 1) * LANES]

    for parity, (z_w, z_r) in enumerate(((z_a, z_b), (z_b, z_a))):
        for t in range(PHASES_E):
            pl.when(s % 2 == parity)(functools.partial(phase, t, z_w, z_r))


def _even_front_prompt(x2d, mod, g, w_in, cw, cb, batch, seq, keep, tm):
    rows = batch * seq
    tpb = seq // tm
    nj = D_MODEL // TN
    n_chunks = (rows // tm) * nj
    off = (seq - keep) // tm
    ppc = TN // LANES

    def c_in(s):
        return jnp.minimum(s, n_chunks - 1)

    def c_out(s):
        return jnp.maximum(s - 1, 0)

    def out_ij(s):
        c = c_out(s)
        return c // nj, c % nj

    def pair_map(s):
        i, j = out_ij(s)
        return (i // tpb, j, i % tpb, 0)

    def keep_map(s):
        i, j = out_ij(s)
        il = i % tpb
        kept = il >= off
        return (i // tpb, jnp.where(kept, il - off, 0), jnp.where(kept, j, 0))

    def tail_map(s):
        i, j = out_ij(s)
        return (i // tpb, 0, jnp.where(i % tpb == tpb - 1, j, 0))

    pair_spec = pl.BlockSpec((None, ppc, tm, LANES), pair_map)
    pair_shape = jax.ShapeDtypeStruct((batch, N_PAIR, seq, LANES), BF16)
    word_spec = pl.BlockSpec((None, ppc, tm // 2, LANES), pair_map)
    word_shape = jax.ShapeDtypeStruct((batch, N_PAIR, seq // 2, LANES), WORD)
    keep_spec = pl.BlockSpec((None, tm, TN), keep_map)
    keep_shape = jax.ShapeDtypeStruct((batch, keep, D_MODEL), F32)
    return pl.pallas_call(
        functools.partial(_even_front_kernel, tiles_per_batch=tpb, n_chunks=n_chunks),
        grid=(n_chunks + 1,),
        in_specs=[
            pl.BlockSpec((tm, D_MODEL), lambda s: (c_in(s) // nj, 0)),
            pl.BlockSpec((None, 1, 3 * D_MODEL), lambda s: (c_in(s) // nj // tpb, 0, 0)),
            pl.BlockSpec((1, D_MODEL), lambda s: (0, 0)),
        ] + [pl.BlockSpec((D_MODEL, TN), lambda s, seg=seg: (0, seg * nj + c_in(s) % nj)) for seg in range(N_SEG_E)] + [
            pl.BlockSpec((3, TN), lambda s: (0, c_out(s) % nj)),
            pl.BlockSpec((1, TN), lambda s: (0, c_out(s) % nj)),
        ],
        out_specs=[
            pl.BlockSpec((tm, TN), lambda s: out_ij(s)),
            word_spec, word_spec, word_spec, pair_spec,
            keep_spec, keep_spec,
            pl.BlockSpec((None, 8, TN), tail_map),
        ],
        out_shape=[
            jax.ShapeDtypeStruct((rows, D_MODEL), BF16),
            word_shape, word_shape, word_shape, pair_shape,
            keep_shape, keep_shape,
            jax.ShapeDtypeStruct((batch, 8, D_MODEL), F32),
        ],
        scratch_shapes=[pltpu.VMEM((tm, D_MODEL), BF16), pltpu.VMEM((nj, 8, TN), F32),
                        pltpu.VMEM((N_SEG_E, tm, TN), F32), pltpu.VMEM((N_SEG_E, tm, TN), F32)],
        compiler_params=_cparams(("arbitrary",)),
        name="even_front_prompt",
    )(x2d, mod.reshape(batch, 1, 3 * D_MODEL), g.reshape(1, D_MODEL), *([w_in] * N_SEG_E), cw, cb.reshape(1, D_MODEL))


STAT_PITCH = {1: 1, 4: 4, 16: 24}
ATTN_UNROLL = 16


def _residue_stream(src, r):
    n_out = src.shape[0] // 4
    halves = [pltpu.unpack_elementwise(src[pl.ds(r // 2 + off, n_out, stride=4), :], index=r % 2,
                                       packed_dtype=BF16, unpacked_dtype=F32) for off in (0, 2)]
    return pltpu.pack_elementwise(halves, packed_dtype=BF16)


def _attn_prompt_kernel(bias_scr, q_ref, k_ref, v_ref, sgb_ref, yb_ref,
                        q4_scr, q16_scr, kv4_scr, kv16_scr, o_scr, lse_scr, *, seq):
    lane = lax.broadcasted_iota(jnp.int32, (1, LANES), 1)
    half0 = lane < HD_B

    for a, ref in enumerate((q_ref, k_ref, v_ref)):
        for r in range(4):
            s = _residue_stream(ref, r)
            if a == 0:
                q4_scr[r] = s
            else:
                kv4_scr[a - 1, r] = s
        for r in range(16):
            s = _residue_stream(q4_scr.at[r % 4] if a == 0 else kv4_scr.at[a - 1, r % 4], r // 4)
            if a == 0:
                q16_scr[r] = s
            else:
                kv16_scr[a - 1, r] = s

    ones = jnp.ones((2 * NK, LANES), BF16)
    hw = NK // 2

    def tile(p, d, r, n):
        qw = pl.ds(pl.multiple_of(n * hw, hw), hw)
        kw = pl.ds(pl.multiple_of(jnp.maximum(n - 1, 0) * hw, hw), 2 * hw)
        if d == 1:
            qs, ks, vs = q_ref, k_ref, v_ref
        elif d == 4:
            qs, ks, vs = q4_scr.at[r], kv4_scr.at[0, r], kv4_scr.at[1, r]
        else:
            qs, ks, vs = q16_scr.at[r], kv16_scr.at[0, r], kv16_scr.at[1, r]
        q = pltpu.bitcast(qs[qw, :], BF16)
        kk = pltpu.bitcast(ks[kw, :], BF16)
        vv1 = jnp.concatenate([pltpu.bitcast(vs[kw, :], BF16), ones], axis=1)
        first = jnp.where(n == 0, 1, 0)
        res = []
        for e in range(2):
            qe = jnp.where(half0 if e == 0 else jnp.logical_not(half0), q, jnp.zeros_like(q))
            s = _dot_nt(qe, kk) - bias_scr[6 * first + 2 * p + e]
            m = jnp.max(s, axis=-1, keepdims=True)
            pe = jnp.exp2(s - m).astype(BF16)
            res.append((_dot(pe, vv1), m))
        (a0, m0), (a1, m1) = res
        acc = jnp.where(half0, a0[:, :LANES], a1[:, :LANES])
        l = jnp.where(half0, a0[:, LANES:], a1[:, LANES:])
        m = jnp.where(half0, m0, m1)
        if d > 1:
            pitch = STAT_PITCH[d]
            rows = pl.ds(r + pitch * NK * n, NK, stride=pitch)
            o_scr[p - 1, rows, :] = acc * (1.0 / l)
            lse_scr[p - 1, rows, :] = m + jnp.log2(l)
            return
        outs, lses = [], []
        for pp in range(1, len(DILATIONS)):
            dd = DILATIONS[pp]
            for ref, vals in ((o_scr, outs), (lse_scr, lses)):
                vals.append(jnp.concatenate(
                    [ref[pp - 1, pl.ds(pl.multiple_of((n * (NK // dd) + g) * STAT_PITCH[dd], 8), dd), :]
                     for g in range(NK // dd)], axis=0) if STAT_PITCH[dd] != dd
                    else ref[pp - 1, pl.ds(pl.multiple_of(n * NK, NK), NK), :])
        top = jnp.maximum(jnp.maximum(m, lses[0]), lses[1])
        w0 = jnp.exp2(m - top)
        ws = [jnp.exp2(x - top) for x in lses]
        num = w0 * acc + ws[0] * outs[0] + ws[1] * outs[1]
        den = w0 * l + ws[0] + ws[1]
        rows = pl.ds(pl.multiple_of(n * NK, NK), NK)
        yb_ref[rows, :] = (num * (1.0 / den) * sgb_ref[rows, :].astype(F32)).astype(BF16)

    for p, d in reversed(list(enumerate(DILATIONS))):
        def body(it, c, p=p, d=d):
            for u in range(ATTN_UNROLL):
                t = it * ATTN_UNROLL + u
                if d == 1:
                    tile(p, d, 0, t)
                else:
                    tile(p, d, t % d, t // d)
            return c

        lax.fori_loop(0, seq // NK // ATTN_UNROLL, body, 0)


def _alibi_tables():
    qi = np.arange(NK)[:, None]
    kj = np.arange(2 * NK)[None, :]
    dist = NK + qi - kj
    dist = np.where((dist >= 0) & (dist <= NK), dist, MASK_DIST)
    first = np.concatenate([dist[:, NK:], np.full((NK, NK), MASK_DIST)], axis=1)
    slopes = (2.0 ** (-8.0 * np.arange(1, H_B + 1) / H_B)).reshape(N_PAIR, 1, 2)
    scale = slopes * (np.asarray(DILATIONS, np.float64).reshape(1, -1, 1) * LOG2E)
    scale = jnp.asarray(np.concatenate([scale.reshape(N_PAIR, -1)] * 2, axis=1), F32)
    base = jnp.asarray(np.stack([dist] * 6 + [first] * 6), F32)
    return scale[:, :, None, None] * base[None]


def _attn_prompt(q, k, v, sgb, batch, seq):
    word_spec = pl.BlockSpec((None, None, seq // 2, LANES), lambda b, hp: (b, hp, 0, 0))
    stat_rows = max(seq // d * STAT_PITCH[d] for d in DILATIONS)
    stat = pltpu.VMEM((len(DILATIONS) - 1, stat_rows, LANES), F32)
    return pl.pallas_call(
        functools.partial(_attn_prompt_kernel, seq=seq),
        grid=(batch, N_PAIR),
        in_specs=[pl.BlockSpec((None, 12, NK, 2 * NK), lambda b, hp: (hp, 0, 0, 0)),
                  word_spec, word_spec, word_spec,
                  pl.BlockSpec((None, None, seq, LANES), lambda b, hp: (b, hp, 0, 0))],
        out_specs=pl.BlockSpec((None, seq, LANES), lambda b, hp: (b, 0, hp)),
        out_shape=jax.ShapeDtypeStruct((batch, seq, D_MODEL), BF16),
        scratch_shapes=[pltpu.VMEM((4, seq // 8, LANES), WORD),
                        pltpu.VMEM((16, seq // 32, LANES), WORD),
                        pltpu.VMEM((2, 4, seq // 8, LANES), WORD),
                        pltpu.VMEM((2, 16, seq // 32, LANES), WORD),
                        stat, stat],
        compiler_params=_cparams(("arbitrary", "arbitrary")),
        name="attn_prompt",
    )(_alibi_tables(), q, k, v, sgb)


def _outproj_mid_kernel(ya_ref, yb_ref, x_ref, mod_ref, w1_ref, w2_ref, g_ref, mod2_ref, x1_ref, h_ref):
    gate = mod_ref[:, 2 * D_MODEL:3 * D_MODEL]
    x1 = x_ref[...] + gate * (_dot(ya_ref[...], w1_ref[...]) + _dot(yb_ref[...].astype(BF16), w2_ref[...]))
    x1_ref[...] = x1
    h_ref[...] = _norm_mod(x1, g_ref[...], mod2_ref[...]).astype(BF16)


def _outproj_final_kernel(ya_ref, yb_ref, x_ref, mod_ref, w1_ref, w2_ref, g_ref, y_ref):
    gate = mod_ref[:, 2 * D_MODEL:3 * D_MODEL]
    x1 = x_ref[...] + gate * (_dot(ya_ref[...], w1_ref[...]) + _dot(yb_ref[...].astype(BF16), w2_ref[...]))
    y_ref[...] = _rms(x1) * g_ref[...]


def _outproj(ya, yb, x2d, mod, w_out, g, mod2, tm, rows_per_mod):
    rows = x2d.shape[0]
    half = w_out.shape[0] // 2
    row_spec = pl.BlockSpec((tm, D_MODEL), lambda i: (i, 0))
    if rows_per_mod == 1:
        mod_spec = pl.BlockSpec((tm, 3 * D_MODEL), lambda i: (i, 0))
        mods = (mod, mod2)
    else:
        tpb = rows_per_mod // tm
        mod_spec = pl.BlockSpec((None, 1, 3 * D_MODEL), lambda i: (i // tpb, 0, 0))
        mods = tuple(None if m is None else m.reshape(-1, 1, 3 * D_MODEL) for m in (mod, mod2))
    w1_spec = pl.BlockSpec((half, D_MODEL), lambda i: (0, 0))
    w2_spec = pl.BlockSpec((half, D_MODEL), lambda i: (1, 0))
    g_spec = pl.BlockSpec((1, D_MODEL), lambda i: (0, 0))
    common = dict(grid=(rows // tm,), compiler_params=_cparams(("arbitrary",)))
    if mod2 is None:
        return pl.pallas_call(
            _outproj_final_kernel,
            in_specs=[row_spec, row_spec, row_spec, mod_spec, w1_spec, w2_spec, g_spec],
            out_specs=row_spec,
            out_shape=jax.ShapeDtypeStruct((rows, D_MODEL), F32),
            name="outproj_final", **common,
        )(ya, yb, x2d, mods[0], w_out, w_out, g.reshape(1, D_MODEL))
    return pl.pallas_call(
        _outproj_mid_kernel,
        in_specs=[row_spec, row_spec, row_spec, mod_spec, w1_spec, w2_spec, g_spec, mod_spec],
        out_specs=[row_spec, row_spec],
        out_shape=[jax.ShapeDtypeStruct((rows, D_MODEL), F32), jax.ShapeDtypeStruct((rows, D_MODEL), BF16)],
        name="outproj_mid", **common,
    )(ya, yb, x2d, mods[0], w_out, w_out, g.reshape(1, D_MODEL), mods[1])


def _pick_by_chunk(j, vals):
    out = vals[-1]
    for idx in range(len(vals) - 2, -1, -1):
        out = jnp.where(j == idx, vals[idx], out)
    return out


N_SEG_O = 6
PHASES_O = 4


def _odd_front_kernel(h_ref, *refs, tiles_per_batch):
    ws = refs[:N_SEG_O]
    pw_ref, ps_ref = refs[N_SEG_O:N_SEG_O + 2]
    yc_ref, q_ref, kt_ref, v_ref, sgd_ref, pn_ref = refs[N_SEG_O + 2:N_SEG_O + 8]
    carry_scr, z_a, z_b = refs[N_SEG_O + 8:]
    nj = D_MODEL // TN
    s = pl.program_id(0)
    c_out = jnp.maximum(s - 1, 0)
    j_out = c_out % nj
    il_out = (c_out // nj) % tiles_per_batch
    tm = h_ref.shape[0]
    rb = tm // PHASES_O

    @pl.when(s == 0)
    def _():
        z_b[...] = jnp.zeros_like(z_b)
        carry_scr[...] = jnp.zeros_like(carry_scr)

    def phase(t, z_w, z_r):
        for seg in _phase_segments(t, N_SEG_O, PHASES_O):
            z_w[seg] = _dot(h_ref[...], ws[seg][...])
        rows = slice(t * rb, (t + 1) * rb)
        u, gc, q, k, v, gd = (z_r[seg, rows, :] for seg in range(N_SEG_O))
        prev = carry_scr[j_out]
        if t == 0:
            prev = jnp.where(il_out == 0, 0.0, prev)
        tail = u[rb - POOL_PREV:rb, :]
        carry_scr[j_out] = tail
        if t == PHASES_O - 1:
            pn_ref[...] = tail
        ext = jnp.concatenate([prev, u], axis=0)
        sums = []
        acc = ext
        for sh in (1, 2, 4, 8):
            acc = acc + pltpu.roll(acc, sh, 0)
            sums.append(acc[POOL_PREV:, :])
        win = _pick_by_chunk(j_out, sums)
        width = _pick_by_chunk(j_out, [float(w) for w in POOL_SIZES])
        pos = (il_out * tm + t * rb + lax.broadcasted_iota(jnp.int32, (rb, 1), 0)).astype(F32)
        pooled = win / jnp.minimum(width, pos + 1.0) - u
        mixed = _dot(pooled.astype(BF16), pw_ref[...]) * ps_ref[...]
        yc_ref[rows, :] = (mixed * _silu(gc)).astype(BF16)
        q_ref[rows, :] = q.astype(BF16)
        kt_ref[:, rows] = (k * (DK_D ** -0.5)).T.astype(BF16)
        v_ref[rows, :] = v.astype(BF16)
        sgd_ref[rows, :] = _silu(gd).astype(BF16)

    for parity, (z_w, z_r) in enumerate(((z_a, z_b), (z_b, z_a))):
        for t in range(PHASES_O):
            pl.when(s % 2 == parity)(functools.partial(phase, t, z_w, z_r))


def _odd_front_prompt(h2d, w_in, pw, ps, batch, seq, tm):
    rows = batch * seq
    tpb = seq // tm
    nj = D_MODEL // TN
    n_chunks = (rows // tm) * nj

    def c_in(s):
        return jnp.minimum(s, n_chunks - 1)

    def out_ij(s):
        c = jnp.maximum(s - 1, 0)
        return c // nj, c % nj

    def head_map(s):
        i, j = out_ij(s)
        return (i // tpb, j, i % tpb, 0)

    def kt_map(s):
        i, j = out_ij(s)
        return (i // tpb, j, 0, i % tpb)

    def tail_map(s):
        i, j = out_ij(s)
        return (i // tpb, 0, jnp.where(i % tpb == tpb - 1, j, 0))

    head_spec = pl.BlockSpec((None, None, tm, TN), head_map)
    head_shape = jax.ShapeDtypeStruct((batch, H_D, seq, DK_D), BF16)
    tok_spec = pl.BlockSpec((tm, TN), out_ij)
    tok_shape = jax.ShapeDtypeStruct((rows, D_MODEL), BF16)
    zbuf = pltpu.VMEM((N_SEG_O, tm, TN), F32)
    return pl.pallas_call(
        functools.partial(_odd_front_kernel, tiles_per_batch=tpb),
        grid=(n_chunks + 1,),
        in_specs=[pl.BlockSpec((tm, D_MODEL), lambda s: (c_in(s) // nj, 0))] + [
                  pl.BlockSpec((D_MODEL, TN), lambda s, seg=seg: (0, seg * nj + c_in(s) % nj))
                  for seg in range(N_SEG_O)] + [
                  pl.BlockSpec((None, TN, TN), lambda s: (out_ij(s)[1], 0, 0)),
                  pl.BlockSpec((1, TN), lambda s: (0, out_ij(s)[1]))],
        out_specs=[tok_spec, head_spec, pl.BlockSpec((None, None, TN, tm), kt_map), head_spec, tok_spec,
                   pl.BlockSpec((None, POOL_PREV, TN), tail_map)],
        out_shape=[tok_shape, head_shape, jax.ShapeDtypeStruct((batch, H_D, DK_D, seq), BF16),
                   head_shape, tok_shape,
                   jax.ShapeDtypeStruct((batch, POOL_PREV, D_MODEL), F32)],
        scratch_shapes=[pltpu.VMEM((nj, POOL_PREV, TN), F32), zbuf, zbuf],
        compiler_params=_cparams(("arbitrary",)),
        name="odd_front_prompt",
    )(h2d, *([w_in] * N_SEG_O), pw, ps.reshape(1, D_MODEL))


RET_SBLK = 1024


def _ret_prompt_kernel(q_ref, kt_ref, v_ref, sgd_ref, dec_ref, cdec_ref, kdec_ref,
                       yd_ref, st_ref, state_scr, *, sblk, n_sblk):
    sb = pl.program_id(1)

    @pl.when(sb == 0)
    def _():
        state_scr[...] = jnp.zeros_like(state_scr)

    def chunk(c, carry):
        rows = pl.ds(pl.multiple_of(c * RET_CHUNK, RET_CHUNK), RET_CHUNK)
        for h in range(H_D):
            cols = slice(h * DK_D, (h + 1) * DK_D)
            q = q_ref[h, rows, :]
            kt = kt_ref[h, :, rows]
            v = v_ref[h, rows, :]
            state = state_scr[h]
            scores = _dot(q, kt) * dec_ref[h]
            o = _dot(scores.astype(BF16), v) + _dot(q, state.astype(BF16)) * cdec_ref[h]
            kd = (kt.astype(F32) * kdec_ref[h]).astype(BF16)
            state_scr[h] = float(np.exp(RET_CHUNK * RET_LOG_G[h])) * state + _dot(kd, v)
            yd_ref[rows, cols] = (_rms(o) * sgd_ref[rows, cols].astype(F32)).astype(BF16)
        return carry

    lax.fori_loop(0, sblk // RET_CHUNK, chunk, 0, unroll=4)

    @pl.when(sb == n_sblk - 1)
    def _():
        st_ref[...] = state_scr[...]


def _ret_consts():
    t = np.arange(RET_CHUNK, dtype=np.float64)
    diff = t[:, None] - t[None, :]
    lg = np.asarray(RET_LOG_G)[:, None, None]
    dec = np.where(diff >= 0, np.exp(np.maximum(diff, 0.0)[None] * lg), 0.0)
    cdec = np.broadcast_to(np.exp((t + 1.0)[None, :, None] * lg), (H_D, RET_CHUNK, DK_D))
    kdec = np.broadcast_to(np.exp((RET_CHUNK - 1.0 - t)[None, None, :] * lg), (H_D, DK_D, RET_CHUNK))
    return jnp.asarray(dec, F32), jnp.asarray(cdec, F32), jnp.asarray(kdec, F32)


def _ret_prompt(q, kt, v, sgd, batch, seq):
    dec, cdec, kdec = _ret_consts()
    n_sblk = seq // RET_SBLK
    head_spec = pl.BlockSpec((None, H_D, RET_SBLK, DK_D), lambda b, s: (b, 0, s, 0))
    tok_spec = pl.BlockSpec((None, RET_SBLK, D_MODEL), lambda b, s: (b, s, 0))

    def const_spec(shape):
        return pl.BlockSpec(shape, lambda b, s: (0, 0, 0))

    return pl.pallas_call(
        functools.partial(_ret_prompt_kernel, sblk=RET_SBLK, n_sblk=n_sblk),
        grid=(batch, n_sblk),
        in_specs=[head_spec, pl.BlockSpec((None, H_D, DK_D, RET_SBLK), lambda b, s: (b, 0, 0, s)), head_spec,
                  tok_spec, const_spec(dec.shape), const_spec(cdec.shape), const_spec(kdec.shape)],
        out_specs=[tok_spec, pl.BlockSpec((None, H_D, DK_D, DK_D), lambda b, s: (b, 0, 0, 0))],
        out_shape=[jax.ShapeDtypeStruct((batch, seq, D_MODEL), BF16),
                   jax.ShapeDtypeStruct((batch, H_D, DK_D, DK_D), F32)],
        scratch_shapes=[pltpu.VMEM((H_D, DK_D, DK_D), F32)],
        compiler_params=_cparams(("arbitrary", "arbitrary")),
        name="ret_prompt",
    )(q, kt, v, sgd.reshape(batch, seq, D_MODEL), dec, cdec, kdec)


def _even_front_sample_kernel(x_ref, mod_ref, g_ref, wbg, wcg, wxv, wga, wq, wk, wv, wgb, cw_ref, cb_ref,
                              prev_ref, ya_ref, q_ref, k_ref, v_ref, sgb_ref, cn_ref):
    h = _norm_mod(x_ref[...], g_ref[...], mod_ref[...]).astype(BF16)
    u = _dot(h, wcg[...]) * _dot(h, wxv[...])
    cw = cw_ref[...]
    conv = cb_ref[...] + cw[0:1, :] * prev_ref[0] + cw[1:2, :] * prev_ref[1] + cw[2:3, :] * u
    cn_ref[0] = prev_ref[1]
    cn_ref[1] = u
    ya_ref[...] = (_dot(h, wbg[...]) * conv * _silu(_dot(h, wga[...]))).astype(BF16)
    q_ref[...] = _dot(h, wq[...]) * (HD_B ** -0.5)
    k_ref[...] = _dot(h, wk[...])
    v_ref[...] = _dot(h, wv[...])
    sgb_ref[...] = _silu(_dot(h, wgb[...]))


def _even_front_sample(x2d, mod, g, w_in, cw, cb, prev_t):
    rows = x2d.shape[0]
    nj = D_MODEL // TN

    def wspec(s):
        return pl.BlockSpec((D_MODEL, TN), lambda j, s=s: (0, s * nj + j))

    full = pl.BlockSpec((rows, D_MODEL), lambda j: (0, 0))
    col = pl.BlockSpec((rows, TN), lambda j: (0, j))
    st = pl.BlockSpec((2, rows, TN), lambda j: (0, 0, j))
    colf = jax.ShapeDtypeStruct((rows, D_MODEL), F32)
    return pl.pallas_call(
        _even_front_sample_kernel,
        grid=(nj,),
        in_specs=[full, pl.BlockSpec((rows, 3 * D_MODEL), lambda j: (0, 0)),
                  pl.BlockSpec((1, D_MODEL), lambda j: (0, 0))] + [wspec(s) for s in range(8)] + [
            pl.BlockSpec((3, TN), lambda j: (0, j)), pl.BlockSpec((1, TN), lambda j: (0, j)), st],
        out_specs=[col, col, col, col, col, st],
        out_shape=[jax.ShapeDtypeStruct((rows, D_MODEL), BF16), colf, colf, colf, colf,
                   jax.ShapeDtypeStruct((2, rows, D_MODEL), F32)],
        compiler_params=_cparams(("arbitrary",)),
        name="even_front_sample",
    )(x2d, mod, g.reshape(1, D_MODEL), *([w_in] * 8), cw, cb.reshape(1, D_MODEL), prev_t)


def _attn_sample_kernel(slope_ref, pos_ref, q_ref, kn_ref, vn_ref, sgb_ref, kt_ref, vt_ref, yb_ref):
    gw = q_ref.shape[-1]
    me = pl.ds(pl.program_id(1) % 8, 1)
    row = lax.broadcasted_iota(jnp.int32, (8, gw), 0)
    col = lax.broadcasted_iota(jnp.int32, (8, gw), 1)
    own = (col >= row * HD_B) & (col < (row + 1) * HD_B)
    qm = jnp.where(own, q_ref[me, :], 0.0)
    s_self = jnp.sum(qm * kn_ref[me, :], axis=-1, keepdims=True)
    s_all = _dot(qm.astype(BF16), kt_ref[...].astype(BF16))
    s_all = s_all - slope_ref[:, 0:1] * pos_ref[0:1, :]
    v_self = vn_ref[me, :]
    probs, stats = [], []
    for p in range(len(DILATIONS)):
        s = s_all - pos_ref[p + 1:p + 2, :]
        m = jnp.maximum(jnp.max(s, axis=-1, keepdims=True), s_self)
        pe = jnp.exp(s - m)
        p_self = jnp.exp(s_self - m)
        probs.append(pe)
        stats.append((m, jnp.sum(pe, axis=-1, keepdims=True) + p_self, p_self))
    o_all = _dot_nt(jnp.concatenate(probs, axis=0).astype(BF16), vt_ref[...].astype(BF16))
    outs, lses = [], []
    for p, (m, l, p_self) in enumerate(stats):
        outs.append((o_all[8 * p:8 * p + 8, :] + p_self * v_self) * (1.0 / l))
        lses.append(m + jnp.log(l))
    m = jnp.maximum(jnp.maximum(lses[0], lses[1]), lses[2])
    ws = [jnp.exp(x - m) for x in lses]
    o = (ws[0] * outs[0] + ws[1] * outs[1] + ws[2] * outs[2]) * (1.0 / (ws[0] + ws[1] + ws[2]))
    o = jnp.sum(jnp.where(own, o, 0.0), axis=0, keepdims=True)
    yb_ref[me, :] = o * sgb_ref[me, :]


def _attn_sample(q, kn, vn, sgb, cache_kt, cache_vt):
    rows, _, wb = cache_kt.shape
    gh = 8
    gw = gh * HD_B
    ng = H_B // gh
    back = wb - np.arange(wb, dtype=np.float64)
    pos = [back] + [np.where((back % d == 0) & (back <= NK * d), 0.0, MASK_ADD) for d in DILATIONS]
    slopes = (2.0 ** (-8.0 * np.arange(1, H_B + 1) / H_B)).reshape(ng, gh)
    slope_arr = jnp.asarray(np.broadcast_to(slopes[:, :, None], (ng, 8, LANES)), F32)
    row_spec = pl.BlockSpec((8, gw), lambda g, b: (b // 8, g))
    t_spec = pl.BlockSpec((None, gw, wb), lambda g, b: (b, g, 0))
    return pl.pallas_call(
        _attn_sample_kernel,
        grid=(ng, rows),
        in_specs=[pl.BlockSpec((None, 8, LANES), lambda g, b: (g, 0, 0)),
                  pl.BlockSpec((4, wb), lambda g, b: (0, 0)),
                  row_spec, row_spec, row_spec, row_spec, t_spec, t_spec],
        out_specs=row_spec,
        out_shape=jax.ShapeDtypeStruct((rows, D_MODEL), F32),
        compiler_params=_cparams(("arbitrary", "arbitrary")),
        name="attn_sample",
    )(slope_arr, jnp.asarray(np.stack(pos), F32), q, kn, vn, sgb, cache_kt, cache_vt)


def _odd_front_sample_kernel(h_ref, wu, wgc, wq, wk, wv, wgd, pw_ref, ps_ref, prev_ref,
                             yc_ref, q_ref, k_ref, v_ref, sgd_ref, pn_ref):
    j = pl.program_id(0)
    h = h_ref[...]
    u = _dot(h, wu[...])
    n_prev = prev_ref.shape[0]
    sums = []
    s = jnp.zeros_like(u)
    back = 0
    for w in POOL_SIZES:
        while back < w - 1:
            s = s + prev_ref[n_prev - 1 - back]
            back += 1
        sums.append(s)
    win = _pick_by_chunk(j, sums) + u
    inv_w = _pick_by_chunk(j, [1.0 / w for w in POOL_SIZES])
    pooled = win * inv_w - u
    mixed = _dot(pooled.astype(BF16), pw_ref[...]) * ps_ref[...]
    yc_ref[...] = (mixed * _silu(_dot(h, wgc[...]))).astype(BF16)
    q_ref[...] = _dot(h, wq[...])
    k_ref[...] = _dot(h, wk[...]) * (DK_D ** -0.5)
    v_ref[...] = _dot(h, wv[...])
    sgd_ref[...] = _silu(_dot(h, wgd[...]))
    for t in range(n_prev - 1):
        pn_ref[t] = prev_ref[t + 1]
    pn_ref[n_prev - 1] = u


def _odd_front_sample(h2d, w_in, pw, ps, prev_t):
    rows = h2d.shape[0]
    n_prev = prev_t.shape[0]
    nj = D_MODEL // TN

    def wspec(s):
        return pl.BlockSpec((D_MODEL, TN), lambda j, s=s: (0, s * nj + j))

    col = pl.BlockSpec((rows, TN), lambda j: (0, j))
    st = pl.BlockSpec((n_prev, rows, TN), lambda j: (0, 0, j))
    colf = jax.ShapeDtypeStruct((rows, D_MODEL), F32)
    return pl.pallas_call(
        _odd_front_sample_kernel,
        grid=(nj,),
        in_specs=[pl.BlockSpec((rows, D_MODEL), lambda j: (0, 0))] + [wspec(s) for s in range(6)] + [
            pl.BlockSpec((None, TN, TN), lambda j: (j, 0, 0)), pl.BlockSpec((1, TN), lambda j: (0, j)), st],
        out_specs=[col, col, col, col, col, st],
        out_shape=[jax.ShapeDtypeStruct((rows, D_MODEL), BF16), colf, colf, colf, colf,
                   jax.ShapeDtypeStruct((n_prev, rows, D_MODEL), F32)],
        compiler_params=_cparams(("arbitrary",)),
        name="odd_front_sample",
    )(h2d, *([w_in] * 6), pw, ps.reshape(1, D_MODEL), prev_t)


RET_SAMPLE_ROWS = 8


def _ret_sample_kernel(q_ref, k_ref, v_ref, sgd_ref, st_ref, yd_ref, sn_ref):
    row = lax.broadcasted_iota(jnp.int32, (DK_D, DK_D), 0)
    col = lax.broadcasted_iota(jnp.int32, (DK_D, DK_D), 1)
    qs, ks, vs, gs = q_ref[...], k_ref[...], v_ref[...], sgd_ref[...]
    out_rows = []
    for i in range(RET_SAMPLE_ROWS):
        out_heads = []
        for h in range(H_D):
            g = float(np.exp(RET_LOG_G[h]))
            cols = slice(h * DK_D, (h + 1) * DK_D)
            q, k, v = qs[i:i + 1, cols], ks[i:i + 1, cols], vs[i:i + 1, cols]
            state = st_ref[i, h]
            qk = jnp.sum(q * k, axis=-1, keepdims=True)
            q8 = jnp.broadcast_to(q, (8, DK_D)).astype(BF16)
            cross = _dot(q8, state.astype(BF16))[0:1, :]
            o = qk * v + g * cross
            k_diag = jnp.where(row == col, jnp.broadcast_to(k, (DK_D, DK_D)), 0.0).astype(BF16)
            v_rows = jnp.broadcast_to(v, (DK_D, DK_D)).astype(BF16)
            sn_ref[i, h] = g * state + _dot(k_diag, v_rows)
            out_heads.append(_rms(o) * gs[i:i + 1, cols])
        out_rows.append(jnp.concatenate(out_heads, axis=1))
    yd_ref[...] = jnp.concatenate(out_rows, axis=0)


def _ret_sample(q, k, v, sgd, state):
    rows = q.shape[0]
    rs = RET_SAMPLE_ROWS
    row_spec = pl.BlockSpec((rs, D_MODEL), lambda i: (i, 0))
    st_spec = pl.BlockSpec((rs, H_D, DK_D, DK_D), lambda i: (i, 0, 0, 0))
    return pl.pallas_call(
        _ret_sample_kernel,
        grid=(rows // rs,),
        in_specs=[row_spec, row_spec, row_spec, row_spec, st_spec],
        out_specs=[row_spec, st_spec],
        out_shape=[jax.ShapeDtypeStruct((rows, D_MODEL), F32),
                   jax.ShapeDtypeStruct((rows, H_D, DK_D, DK_D), F32)],
        compiler_params=_cparams(("arbitrary",)),
        name="ret_sample",
    )(q, k, v, sgd, state)


def kernel(x_prompt, x_sample, c_prompt, c_sample, state_conv, cache_win_k, cache_win_v, state_pool, state_ret,
           norm_e, ada_w_e, ada_b_e, w_in_e, conv_w, conv_b, w_out_e, norm_o, ada_w_o, ada_b_o, w_in_o,
           pool_w, pool_scale, w_out_o, norm_f):
    batch, seq, d = x_prompt.shape
    sb = x_sample.shape[0]
    assert d == D_MODEL and x_sample.shape[1] == 1
    assert norm_e.shape[0] == 1 and norm_o.shape[0] == 1
    wb = cache_win_k.shape[2]
    keep = min(wb, seq)
    assert wb == DILATIONS[-1] * NK and seq % (DILATIONS[-1] * NK) == 0
    tm = TM_OUT

    n_c = batch + sb
    pad = (-n_c) % 8
    c_all = jnp.concatenate([c_prompt, c_sample, jnp.zeros((pad, d), F32)], axis=0)
    mod_e, mod_o = _adaln(c_all, ada_w_e[0], ada_b_e[0], ada_w_o[0], ada_b_o[0])
    mod_e_p, mod_e_s = mod_e[:batch], mod_e[batch:n_c]
    mod_o_p, mod_o_s = mod_o[:batch], mod_o[batch:n_c]

    w_in_e16 = w_in_e[0].astype(BF16)
    w_out_e16 = w_out_e[0].astype(BF16)
    w_in_o16 = w_in_o[0].astype(BF16)
    w_out_o16 = w_out_o[0].astype(BF16)
    pool_w16 = pool_w[0].astype(BF16)

    xp2d = x_prompt.reshape(batch * seq, d)
    ya, q, k, v, sgb, k_new, v_new, conv_tail = _even_front_prompt(
        xp2d, mod_e_p, norm_e[0], w_in_e16, conv_w[0], conv_b[0], batch, seq, keep, TM_FRONT)
    yb = _attn_prompt(q, k, v, sgb, batch, seq).reshape(batch * seq, d)
    x1, h1 = _outproj(ya, yb, xp2d, mod_e_p, w_out_e16, norm_o[0], mod_o_p, tm, seq)
    yc, rq, rk, rv, sgd, pool_tail = _odd_front_prompt(h1, w_in_o16, pool_w16, pool_scale[0], batch, seq, TM_FRONT)
    yd, ret_p = _ret_prompt(rq, rk, rv, sgd, batch, seq)
    y_prompt = _outproj(yc, yd.reshape(batch * seq, d), x1, mod_o_p, w_out_o16, norm_f, None, tm, seq)

    xs2d = x_sample.reshape(sb, d)
    conv_prev_t = jnp.transpose(state_conv[0], (1, 0, 2))
    ya_s, q_s, k_s, v_s, sgb_s, conv_s_t = _even_front_sample(
        xs2d, mod_e_s, norm_e[0], w_in_e16, conv_w[0], conv_b[0], conv_prev_t)
    cache_kt = jnp.transpose(cache_win_k[0], (0, 2, 3, 1)).reshape(sb, d, wb)
    cache_vt = jnp.transpose(cache_win_v[0], (0, 2, 3, 1)).reshape(sb, d, wb)
    yb_s = _attn_sample(q_s, k_s, v_s, sgb_s, cache_kt, cache_vt)
    x1_s, h1_s = _outproj(ya_s, yb_s, xs2d, mod_e_s, w_out_e16, norm_o[0], mod_o_s, sb, 1)
    pool_prev_t = jnp.transpose(state_pool[0], (1, 0, 2))
    yc_s, rq_s, rk_s, rv_s, sgd_s, pool_s_t = _odd_front_sample(h1_s, w_in_o16, pool_w16, pool_scale[0], pool_prev_t)
    yd_s, ret_s = _ret_sample(rq_s, rk_s, rv_s, sgd_s, state_ret[0])
    y_sample = _outproj(yc_s, yd_s, x1_s, mod_o_s, w_out_o16, norm_f, None, sb, 1)

    return (
        y_prompt.reshape(batch, seq, d),
        y_sample.reshape(sb, 1, d),
        conv_tail[:, 6:8][None],
        jnp.transpose(conv_s_t, (1, 0, 2))[None],
        k_new.reshape(1, batch, keep, H_B, HD_B),
        k_s.reshape(1, sb, 1, H_B, HD_B),
        v_new.reshape(1, batch, keep, H_B, HD_B),
        v_s.reshape(1, sb, 1, H_B, HD_B),
        pool_tail[:, 1:][None],
        jnp.transpose(pool_s_t, (1, 0, 2))[None],
        ret_p[None],
        ret_s[None],
    )
```

```python
import functools

import numpy as np
import jax
import jax.numpy as jnp
from jax import lax
from jax.experimental import pallas as pl
from jax.experimental.pallas import tpu as pltpu

F32 = jnp.float32
BF16 = jnp.bfloat16
WORD = jnp.uint32

D_MODEL = 1024
EPS = 1e-6
H_B = 16
HD_B = 64
N_PAIR = H_B // 2
LANES = 128
NK = 128
DILATIONS = (1, 4, 16)
POOL_SIZES = (2, 4, 8, 16)
POOL_PREV = 16
H_D = 4
DK_D = 256
RET_CHUNK = 256
TN = 256
TM_FRONT = 1024
TM_OUT = 1024
MASK_DIST = 1e9
LOG2E = float(np.log2(np.e))
MASK_ADD = 1e30
VMEM_LIMIT = 56 * 1024 * 1024

RET_LOG_G = [float(np.log(1.0 - 2.0 ** (-5.0 - h))) for h in range(H_D)]


def _cparams(sem):
    return pltpu.CompilerParams(dimension_semantics=sem, vmem_limit_bytes=VMEM_LIMIT)


def _silu(x):
    return x * (1.0 / (1.0 + jnp.exp(-x)))


def _dot(a, b):
    return jnp.dot(a, b, preferred_element_type=F32)


def _dot_nt(a, b):
    return lax.dot_general(a, b, (((1,), (1,)), ((), ())), preferred_element_type=F32)


def _rms(x):
    return x * lax.rsqrt(jnp.mean(x * x, axis=-1, keepdims=True) + EPS)


def _adaln_kernel(c_ref, we_ref, be_ref, wo_ref, bo_ref, me_ref, mo_ref):
    sc = _silu(c_ref[...]).astype(BF16)
    me_ref[...] = _dot(sc, we_ref[...].astype(BF16)) + be_ref[...]
    mo_ref[...] = _dot(sc, wo_ref[...].astype(BF16)) + bo_ref[...]


def _adaln(c_all, we, be, wo, bo):
    rows = c_all.shape[0]
    tn = 512
    n = 3 * D_MODEL
    wspec = pl.BlockSpec((D_MODEL, tn), lambda j: (0, j))
    bspec = pl.BlockSpec((1, tn), lambda j: (0, j))
    ospec = pl.BlockSpec((rows, tn), lambda j: (0, j))
    return pl.pallas_call(
        _adaln_kernel,
        grid=(n // tn,),
        in_specs=[pl.BlockSpec((rows, D_MODEL), lambda j: (0, 0)), wspec, bspec, wspec, bspec],
        out_specs=[ospec, ospec],
        out_shape=[jax.ShapeDtypeStruct((rows, n), F32)] * 2,
        compiler_params=_cparams(("arbitrary",)),
        name="adaln",
    )(c_all, we, be.reshape(1, n), wo, bo.reshape(1, n))


def _norm_mod(x, g, mod):
    shift = mod[:, 0:D_MODEL]
    scale = mod[:, D_MODEL:2 * D_MODEL]
    return _rms(x) * g * (1.0 + scale) + shift


def _shift_rows(u, k, prev_rows):
    row = lax.broadcasted_iota(jnp.int32, u.shape, 0)
    out = pltpu.roll(u, k, 0)
    for idx, pr in enumerate(prev_rows):
        out = jnp.where(row == idx, pr, out)
    return out


N_SEG_E = 8
PHASES_E = 8


def _phase_segments(t, n_seg, n_phases):
    return range(t * n_seg // n_phases, (t + 1) * n_seg // n_phases)


def _even_front_kernel(xq_ref, modq_ref, x0_ref, mod0_ref, g_ref, *refs, tiles_per_batch):
    ws = refs[:N_SEG_E]
    cw_ref, cb_ref = refs[N_SEG_E:N_SEG_E + 2]
    ya_ref, q_ref, k_ref, v_ref, sgb_ref, kn_ref, vn_ref, cn_ref = refs[N_SEG_E + 2:N_SEG_E + 10]
    h_a, h_b, carry_scr, z_a, z_b = refs[N_SEG_E + 10:]
    nj = D_MODEL // TN
    s = pl.program_id(0)
    c_out = jnp.maximum(s - 1, 0)
    j_out = c_out % nj
    il_out = (c_out // nj) % tiles_per_batch

    @pl.when(s == 0)
    def _():
        z_b[...] = jnp.zeros_like(z_b)
        carry_scr[...] = jnp.zeros_like(carry_scr)
        h_a[...] = _norm_mod(x0_ref[...], g_ref[...], mod0_ref[...]).astype(BF16)

    tm = h_a.shape[0]
    rb = tm // PHASES_E
    nb = tm // nj // PHASES_E

    def phase(t, z_w, z_r, h_cur, h_next):
        dst = pl.ds(pl.multiple_of((s % nj) * (tm // nj) + t * nb, nb), nb)
        h_next[dst, :] = _norm_mod(xq_ref[t * nb:(t + 1) * nb, :], g_ref[...], modq_ref[...]).astype(BF16)
        for seg in _phase_segments(t, N_SEG_E, PHASES_E):
            z_w[seg] = _dot(h_cur[...], ws[seg][...])
        rows = slice(t * rb, (t + 1) * rb)
        bg, cg, xv, ga, q, k, v, gb = (z_r[seg, rows, :] for seg in range(N_SEG_E))
        u = cg * xv
        prev = carry_scr[j_out]
        if t == 0:
            prev = jnp.where(il_out == 0, 0.0, prev)
        p2, p1 = prev[6:7, :], prev[7:8, :]
        u1 = _shift_rows(u, 1, [p1])
        u2 = _shift_rows(u, 2, [p2, p1])
        cw = cw_ref[...]
        conv = cb_ref[...] + cw[0:1, :] * u2 + cw[1:2, :] * u1 + cw[2:3, :] * u
        tail = u[rb - 8:rb, :]
        carry_scr[j_out] = tail
        if t == PHASES_E - 1:
            cn_ref[...] = tail
        ya_ref[rows, :] = (bg * conv * _silu(ga)).astype(BF16)
        kn_ref[:, rows] = k.T
        vn_ref[:, rows] = v.T
        wrows = slice(t * rb // 2, (t + 1) * rb // 2)
        for ref, val in ((q_ref, q * (HD_B ** -0.5 * LOG2E)),
                         (k_ref, k), (v_ref, v)):
            words = pltpu.bitcast(val.astype(BF16), WORD)
            for e in range(TN // LANES):
                ref[e, wrows, :] = words[:, e * LANES:(e + 1) * LANES]
        sgb = _silu(gb).astype(BF16)
        for e in range(TN // LANES):
            sgb_ref[e, rows, :] = sgb[:, e * LANES:(e + 1) * LANES]

    for parity, (z_w, z_r) in enumerate(((z_a, z_b), (z_b, z_a))):
        for tile_parity, (h_cur, h_next) in enumerate(((h_a, h_b), (h_b, h_a))):
            for t in range(PHASES_E):
                pl.when((s % 2 == parity) & ((s // nj) % 2 == tile_parity))(
                    functools.partial(phase, t, z_w, z_r, h_cur, h_next))


def _even_front_prompt(x2d, mod, g, w_in, cw, cb, batch, seq, keep, tm):
    rows = batch * seq
    tpb = seq // tm
    nj = D_MODEL // TN
    n_chunks = (rows // tm) * nj
    off = (seq - keep) // tm
    ppc = TN // LANES

    def c_in(s):
        return jnp.minimum(s, n_chunks - 1)

    def c_out(s):
        return jnp.maximum(s - 1, 0)

    def next_quarter(s):
        return jnp.minimum(s + nj, n_chunks - 1)

    mod3 = mod.reshape(batch, 1, 3 * D_MODEL)

    def out_ij(s):
        c = c_out(s)
        return c // nj, c % nj

    def pair_map(s):
        i, j = out_ij(s)
        return (i // tpb, j, i % tpb, 0)

    def keep_map(s):
        i, j = out_ij(s)
        il = i % tpb
        kept = il >= off
        return (i // tpb, jnp.where(kept, j, 0), jnp.where(kept, il - off, 0))

    def tail_map(s):
        i, j = out_ij(s)
        return (i // tpb, 0, jnp.where(i % tpb == tpb - 1, j, 0))

    pair_spec = pl.BlockSpec((None, ppc, tm, LANES), pair_map)
    pair_shape = jax.ShapeDtypeStruct((batch, N_PAIR, seq, LANES), BF16)
    word_spec = pl.BlockSpec((None, ppc, tm // 2, LANES), pair_map)
    word_shape = jax.ShapeDtypeStruct((batch, N_PAIR, seq // 2, LANES), WORD)
    keep_spec = pl.BlockSpec((None, TN, tm), keep_map)
    keep_shape = jax.ShapeDtypeStruct((batch, D_MODEL, keep), F32)
    return pl.pallas_call(
        functools.partial(_even_front_kernel, tiles_per_batch=tpb),
        grid=(n_chunks + 1,),
        in_specs=[
            pl.BlockSpec((tm // nj, D_MODEL), lambda s: (next_quarter(s), 0)),
            pl.BlockSpec((None, 1, 3 * D_MODEL), lambda s: (next_quarter(s) // nj // tpb, 0, 0)),
            pl.BlockSpec((tm, D_MODEL), lambda s: (0, 0)),
            pl.BlockSpec((None, 1, 3 * D_MODEL), lambda s: (0, 0, 0)),
            pl.BlockSpec((1, D_MODEL), lambda s: (0, 0)),
        ] + [pl.BlockSpec((D_MODEL, TN), lambda s, seg=seg: (0, seg * nj + c_in(s) % nj)) for seg in range(N_SEG_E)] + [
            pl.BlockSpec((3, TN), lambda s: (0, c_out(s) % nj)),
            pl.BlockSpec((1, TN), lambda s: (0, c_out(s) % nj)),
        ],
        out_specs=[
            pl.BlockSpec((tm, TN), lambda s: out_ij(s)),
            word_spec, word_spec, word_spec, pair_spec,
            keep_spec, keep_spec,
            pl.BlockSpec((None, 8, TN), tail_map),
        ],
        out_shape=[
            jax.ShapeDtypeStruct((rows, D_MODEL), BF16),
            word_shape, word_shape, word_shape, pair_shape,
            keep_shape, keep_shape,
            jax.ShapeDtypeStruct((batch, 8, D_MODEL), F32),
        ],
        scratch_shapes=[pltpu.VMEM((tm, D_MODEL), BF16), pltpu.VMEM((tm, D_MODEL), BF16),
                        pltpu.VMEM((nj, 8, TN), F32),
                        pltpu.VMEM((N_SEG_E, tm, TN), F32), pltpu.VMEM((N_SEG_E, tm, TN), F32)],
        compiler_params=_cparams(("arbitrary",)),
        name="even_front_prompt",
    )(x2d, mod3, x2d, mod3, g.reshape(1, D_MODEL), *([w_in] * N_SEG_E), cw, cb.reshape(1, D_MODEL))


STAT_PITCH = {1: 1, 4: 4, 16: 24}
ATTN_UNROLL = 16


def _residue_stream(src, r):
    n_out = src.shape[0] // 4
    halves = [pltpu.unpack_elementwise(src[pl.ds(r // 2 + off, n_out, stride=4), :], index=r % 2,
                                       packed_dtype=BF16, unpacked_dtype=F32) for off in (0, 2)]
    return pltpu.pack_elementwise(halves, packed_dtype=BF16)


def _attn_prompt_kernel(slope_ref, dist_ref, q_ref, k_ref, v_ref, sgb_ref, yb_ref,
                        q4_scr, q16_scr, kv4_scr, kv16_scr, acc_scr, m_scr, l_scr, bias_scr, *, seq):
    hp = pl.program_id(1)
    lane = lax.broadcasted_iota(jnp.int32, (1, LANES), 1)
    half0 = lane < HD_B
    dist = dist_ref[...]
    masked = jnp.full((NK, NK), MASK_DIST, F32)
    for p, d in enumerate(DILATIONS):
        for e in range(2):
            bias = dist * (slope_ref[2 * hp + e] * (float(d) * LOG2E))
            bias_scr[2 * p + e] = bias
            bias_scr[6 + 2 * p + e] = jnp.concatenate([bias[:, NK:], masked], axis=1)

    for a, ref in enumerate((q_ref, k_ref, v_ref)):
        for r in range(4):
            s = _residue_stream(ref, r)
            if a == 0:
                q4_scr[r] = s
            else:
                kv4_scr[a - 1, r] = s
        for r in range(16):
            s = _residue_stream(q4_scr.at[r % 4] if a == 0 else kv4_scr.at[a - 1, r % 4], r // 4)
            if a == 0:
                q16_scr[r] = s
            else:
                kv16_scr[a - 1, r] = s

    ones = jnp.ones((2 * NK, LANES), BF16)
    hw = NK // 2

    def tile(p, d, r, n):
        qw = pl.ds(pl.multiple_of(n * hw, hw), hw)
        kw = pl.ds(pl.multiple_of(jnp.maximum(n - 1, 0) * hw, hw), 2 * hw)
        if d == 1:
            qs, ks, vs = q_ref, k_ref, v_ref
        elif d == 4:
            qs, ks, vs = q4_scr.at[r], kv4_scr.at[0, r], kv4_scr.at[1, r]
        else:
            qs, ks, vs = q16_scr.at[r], kv16_scr.at[0, r], kv16_scr.at[1, r]
        q = pltpu.bitcast(qs[qw, :], BF16)
        kk = pltpu.bitcast(ks[kw, :], BF16)
        vv1 = jnp.concatenate([pltpu.bitcast(vs[kw, :], BF16), ones], axis=1)
        first = jnp.where(n == 0, 1, 0)
        res = []
        for e in range(2):
            qe = jnp.where(half0 if e == 0 else jnp.logical_not(half0), q, jnp.zeros_like(q))
            s = _dot_nt(qe, kk) - bias_scr[6 * first + 2 * p + e]
            m = jnp.max(s, axis=-1, keepdims=True)
            pe = jnp.exp2(s - m).astype(BF16)
            res.append((_dot(pe, vv1), m))
        (a0, m0), (a1, m1) = res
        acc = jnp.where(half0, a0[:, :LANES], a1[:, :LANES])
        l = jnp.where(half0, a0[:, LANES:], a1[:, LANES:])
        m = jnp.where(half0, m0, m1)
        if d > 1:
            pitch = STAT_PITCH[d]
            rows = pl.ds(r + pitch * NK * n, NK, stride=pitch)
            acc_scr[p - 1, rows, :] = acc
            l_scr[p - 1, rows, :] = l
            m_scr[p - 1, rows, :] = m
            return
        accs, ls, ms = [acc], [l], [m]
        for pp in range(1, len(DILATIONS)):
            dd = DILATIONS[pp]
            for ref, vals in ((acc_scr, accs), (l_scr, ls), (m_scr, ms)):
                vals.append(jnp.concatenate(
                    [ref[pp - 1, pl.ds(pl.multiple_of((n * (NK // dd) + g) * STAT_PITCH[dd], 8), dd), :]
                     for g in range(NK // dd)], axis=0) if STAT_PITCH[dd] != dd
                    else ref[pp - 1, pl.ds(pl.multiple_of(n * NK, NK), NK), :])
        top = jnp.maximum(jnp.maximum(ms[0], ms[1]), ms[2])
        ws = [jnp.exp2(x - top) for x in ms]
        num = ws[0] * accs[0] + ws[1] * accs[1] + ws[2] * accs[2]
        den = ws[0] * ls[0] + ws[1] * ls[1] + ws[2] * ls[2]
        rows = pl.ds(pl.multiple_of(n * NK, NK), NK)
        yb_ref[rows, :] = (num * (1.0 / den) * sgb_ref[rows, :].astype(F32)).astype(BF16)

    for p, d in reversed(list(enumerate(DILATIONS))):
        def body(it, c, p=p, d=d):
            for u in range(ATTN_UNROLL):
                t = it * ATTN_UNROLL + u
                if d == 1:
                    tile(p, d, 0, t)
                else:
                    tile(p, d, t % d, t // d)
            return c

        lax.fori_loop(0, seq // NK // ATTN_UNROLL, body, 0)


def _band_distance():
    qi = np.arange(NK)[:, None]
    kj = np.arange(2 * NK)[None, :]
    dist = NK + qi - kj
    return jnp.asarray(np.where((dist >= 0) & (dist <= NK), dist, MASK_DIST), dtype=F32)


def _alibi_slopes():
    return jnp.asarray(2.0 ** (-8.0 * np.arange(1, H_B + 1) / H_B), dtype=F32)


def _attn_prompt(q, k, v, sgb, batch, seq):
    word_spec = pl.BlockSpec((None, None, seq // 2, LANES), lambda b, hp: (b, hp, 0, 0))
    stat_rows = max(seq // d * STAT_PITCH[d] for d in DILATIONS)
    stat = pltpu.VMEM((len(DILATIONS) - 1, stat_rows, LANES), F32)
    return pl.pallas_call(
        functools.partial(_attn_prompt_kernel, seq=seq),
        grid=(batch, N_PAIR),
        in_specs=[pl.BlockSpec(memory_space=pltpu.SMEM),
                  pl.BlockSpec((NK, 2 * NK), lambda b, hp: (0, 0)),
                  word_spec, word_spec, word_spec,
                  pl.BlockSpec((None, None, seq, LANES), lambda b, hp: (b, hp, 0, 0))],
        out_specs=pl.BlockSpec((None, seq, LANES), lambda b, hp: (b, 0, hp)),
        out_shape=jax.ShapeDtypeStruct((batch, seq, D_MODEL), BF16),
        scratch_shapes=[pltpu.VMEM((4, seq // 8, LANES), WORD),
                        pltpu.VMEM((16, seq // 32, LANES), WORD),
                        pltpu.VMEM((2, 4, seq // 8, LANES), WORD),
                        pltpu.VMEM((2, 16, seq // 32, LANES), WORD),
                        stat, stat, stat,
                        pltpu.VMEM((12, NK, 2 * NK), F32)],
        compiler_params=_cparams(("arbitrary", "arbitrary")),
        name="attn_prompt",
    )(_alibi_slopes(), _band_distance(), q, k, v, sgb)


def _outproj_mid_kernel(ya_ref, yb_ref, x_ref, mod_ref, w1_ref, w2_ref, g_ref, mod2_ref, x1_ref, h_ref):
    gate = mod_ref[:, 2 * D_MODEL:3 * D_MODEL]
    x1 = x_ref[...] + gate * (_dot(ya_ref[...], w1_ref[...]) + _dot(yb_ref[...].astype(BF16), w2_ref[...]))
    x1_ref[...] = x1
    h_ref[...] = _norm_mod(x1, g_ref[...], mod2_ref[...]).astype(BF16)


def _outproj_final_kernel(ya_ref, yb_ref, x_ref, mod_ref, w1_ref, w2_ref, g_ref, y_ref):
    gate = mod_ref[:, 2 * D_MODEL:3 * D_MODEL]
    x1 = x_ref[...] + gate * (_dot(ya_ref[...], w1_ref[...]) + _dot(yb_ref[...].astype(BF16), w2_ref[...]))
    y_ref[...] = _rms(x1) * g_ref[...]


def _outproj(ya, yb, x2d, mod, w_out, g, mod2, tm, rows_per_mod):
    rows = x2d.shape[0]
    half = w_out.shape[0] // 2
    row_spec = pl.BlockSpec((tm, D_MODEL), lambda i: (i, 0))
    if rows_per_mod == 1:
        mod_spec = pl.BlockSpec((tm, 3 * D_MODEL), lambda i: (i, 0))
        mods = (mod, mod2)
    else:
        tpb = rows_per_mod // tm
        mod_spec = pl.BlockSpec((None, 1, 3 * D_MODEL), lambda i: (i // tpb, 0, 0))
        mods = tuple(None if m is None else m.reshape(-1, 1, 3 * D_MODEL) for m in (mod, mod2))
    w1_spec = pl.BlockSpec((half, D_MODEL), lambda i: (0, 0))
    w2_spec = pl.BlockSpec((half, D_MODEL), lambda i: (1, 0))
    g_spec = pl.BlockSpec((1, D_MODEL), lambda i: (0, 0))
    common = dict(grid=(rows // tm,), compiler_params=_cparams(("arbitrary",)))
    if mod2 is None:
        return pl.pallas_call(
            _outproj_final_kernel,
            in_specs=[row_spec, row_spec, row_spec, mod_spec, w1_spec, w2_spec, g_spec],
            out_specs=row_spec,
            out_shape=jax.ShapeDtypeStruct((rows, D_MODEL), F32),
            name="outproj_final", **common,
        )(ya, yb, x2d, mods[0], w_out, w_out, g.reshape(1, D_MODEL))
    return pl.pallas_call(
        _outproj_mid_kernel,
        in_specs=[row_spec, row_spec, row_spec, mod_spec, w1_spec, w2_spec, g_spec, mod_spec],
        out_specs=[row_spec, row_spec],
        out_shape=[jax.ShapeDtypeStruct((rows, D_MODEL), F32), jax.ShapeDtypeStruct((rows, D_MODEL), BF16)],
        name="outproj_mid", **common,
    )(ya, yb, x2d, mods[0], w_out, w_out, g.reshape(1, D_MODEL), mods[1])


def _pick_by_chunk(j, vals):
    out = vals[-1]
    for idx in range(len(vals) - 2, -1, -1):
        out = jnp.where(j == idx, vals[idx], out)
    return out


N_SEG_O = 6
PHASES_O = 4


def _odd_front_kernel(h_ref, *refs, tiles_per_batch):
    ws = refs[:N_SEG_O]
    pw_ref, ps_ref = refs[N_SEG_O:N_SEG_O + 2]
    yc_ref, q_ref, kt_ref, v_ref, sgd_ref, pn_ref = refs[N_SEG_O + 2:N_SEG_O + 8]
    carry_scr, z_a, z_b = refs[N_SEG_O + 8:]
    nj = D_MODEL // TN
    s = pl.program_id(0)
    c_out = jnp.maximum(s - 1, 0)
    j_out = c_out % nj
    il_out = (c_out // nj) % tiles_per_batch
    tm = h_ref.shape[0]
    rb = tm // PHASES_O

    @pl.when(s == 0)
    def _():
        z_b[...] = jnp.zeros_like(z_b)
        carry_scr[...] = jnp.zeros_like(carry_scr)

    def phase(t, z_w, z_r):
        for seg in _phase_segments(t, N_SEG_O, PHASES_O):
            z_w[seg] = _dot(h_ref[...], ws[seg][...])
        rows = slice(t * rb, (t + 1) * rb)
        u, gc, q, k, v, gd = (z_r[seg, rows, :] for seg in range(N_SEG_O))
        prev = carry_scr[j_out]
        if t == 0:
            prev = jnp.where(il_out == 0, 0.0, prev)
        tail = u[rb - POOL_PREV:rb, :]
        carry_scr[j_out] = tail
        if t == PHASES_O - 1:
            pn_ref[...] = tail
        ext = jnp.concatenate([prev, u], axis=0)
        sums = []
        acc = ext
        for sh in (1, 2, 4, 8):
            acc = acc + pltpu.roll(acc, sh, 0)
            sums.append(acc[POOL_PREV:, :])
        win = _pick_by_chunk(j_out, sums)
        width = _pick_by_chunk(j_out, [float(w) for w in POOL_SIZES])
        pos = (il_out * tm + t * rb + lax.broadcasted_iota(jnp.int32, (rb, 1), 0)).astype(F32)
        pooled = win / jnp.minimum(width, pos + 1.0) - u
        mixed = _dot(pooled.astype(BF16), pw_ref[...]) * ps_ref[...]
        yc_ref[rows, :] = (mixed * _silu(gc)).astype(BF16)
        q_ref[rows, :] = q.astype(BF16)
        kt_ref[:, rows] = (k * (DK_D ** -0.5)).T.astype(BF16)
        v_ref[rows, :] = v.astype(BF16)
        sgd_ref[rows, :] = _silu(gd).astype(BF16)

    for parity, (z_w, z_r) in enumerate(((z_a, z_b), (z_b, z_a))):
        for t in range(PHASES_O):
            pl.when(s % 2 == parity)(functools.partial(phase, t, z_w, z_r))


def _odd_front_prompt(h2d, w_in, pw, ps, batch, seq, tm):
    rows = batch * seq
    tpb = seq // tm
    nj = D_MODEL // TN
    n_chunks = (rows // tm) * nj

    def c_in(s):
        return jnp.minimum(s, n_chunks - 1)

    def out_ij(s):
        c = jnp.maximum(s - 1, 0)
        return c // nj, c % nj

    def head_map(s):
        i, j = out_ij(s)
        return (i // tpb, j, i % tpb, 0)

    def kt_map(s):
        i, j = out_ij(s)
        return (i // tpb, j, 0, i % tpb)

    def tail_map(s):
        i, j = out_ij(s)
        return (i // tpb, 0, jnp.where(i % tpb == tpb - 1, j, 0))

    head_spec = pl.BlockSpec((None, None, tm, TN), head_map)
    head_shape = jax.ShapeDtypeStruct((batch, H_D, seq, DK_D), BF16)
    tok_spec = pl.BlockSpec((tm, TN), out_ij)
    tok_shape = jax.ShapeDtypeStruct((rows, D_MODEL), BF16)
    zbuf = pltpu.VMEM((N_SEG_O, tm, TN), F32)
    return pl.pallas_call(
        functools.partial(_odd_front_kernel, tiles_per_batch=tpb),
        grid=(n_chunks + 1,),
        in_specs=[pl.BlockSpec((tm, D_MODEL), lambda s: (c_in(s) // nj, 0))] + [
                  pl.BlockSpec((D_MODEL, TN), lambda s, seg=seg: (0, seg * nj + c_in(s) % nj))
                  for seg in range(N_SEG_O)] + [
                  pl.BlockSpec((None, TN, TN), lambda s: (out_ij(s)[1], 0, 0)),
                  pl.BlockSpec((1, TN), lambda s: (0, out_ij(s)[1]))],
        out_specs=[tok_spec, head_spec, pl.BlockSpec((None, None, TN, tm), kt_map), head_spec, tok_spec,
                   pl.BlockSpec((None, POOL_PREV, TN), tail_map)],
        out_shape=[tok_shape, head_shape, jax.ShapeDtypeStruct((batch, H_D, DK_D, seq), BF16),
                   head_shape, tok_shape,
                   jax.ShapeDtypeStruct((batch, POOL_PREV, D_MODEL), F32)],
        scratch_shapes=[pltpu.VMEM((nj, POOL_PREV, TN), F32), zbuf, zbuf],
        compiler_params=_cparams(("arbitrary",)),
        name="odd_front_prompt",
    )(h2d, *([w_in] * N_SEG_O), pw, ps.reshape(1, D_MODEL))


RET_SBLK = 1024


def _ret_prompt_kernel(q_ref, kt_ref, v_ref, sgd_ref, dec_ref, cdec_ref, kdec_ref,
                       yd_ref, st_ref, state_scr, *, sblk, n_sblk):
    sb = pl.program_id(1)

    @pl.when(sb == 0)
    def _():
        state_scr[...] = jnp.zeros_like(state_scr)

    def chunk(c, carry):
        rows = pl.ds(pl.multiple_of(c * RET_CHUNK, RET_CHUNK), RET_CHUNK)
        for h in range(H_D):
            cols = slice(h * DK_D, (h + 1) * DK_D)
            q = q_ref[h, rows, :]
            kt = kt_ref[h, :, rows]
            v = v_ref[h, rows, :]
            state = state_scr[h]
            scores = _dot(q, kt) * dec_ref[h]
            o = _dot(scores.astype(BF16), v) + _dot(q, state.astype(BF16)) * cdec_ref[h]
            kd = (kt.astype(F32) * kdec_ref[h]).astype(BF16)
            state_scr[h] = float(np.exp(RET_CHUNK * RET_LOG_G[h])) * state + _dot(kd, v)
            yd_ref[rows, cols] = (_rms(o) * sgd_ref[rows, cols].astype(F32)).astype(BF16)
        return carry

    lax.fori_loop(0, sblk // RET_CHUNK, chunk, 0, unroll=4)

    @pl.when(sb == n_sblk - 1)
    def _():
        st_ref[...] = state_scr[...]


def _ret_consts():
    t = np.arange(RET_CHUNK, dtype=np.float64)
    diff = t[:, None] - t[None, :]
    lg = np.asarray(RET_LOG_G)[:, None, None]
    dec = np.where(diff >= 0, np.exp(np.maximum(diff, 0.0)[None] * lg), 0.0)
    cdec = np.broadcast_to(np.exp((t + 1.0)[None, :, None] * lg), (H_D, RET_CHUNK, DK_D))
    kdec = np.broadcast_to(np.exp((RET_CHUNK - 1.0 - t)[None, None, :] * lg), (H_D, DK_D, RET_CHUNK))
    return jnp.asarray(dec, F32), jnp.asarray(cdec, F32), jnp.asarray(kdec, F32)


def _ret_prompt(q, kt, v, sgd, batch, seq):
    dec, cdec, kdec = _ret_consts()
    n_sblk = seq // RET_SBLK
    head_spec = pl.BlockSpec((None, H_D, RET_SBLK, DK_D), lambda b, s: (b, 0, s, 0))
    tok_spec = pl.BlockSpec((None, RET_SBLK, D_MODEL), lambda b, s: (b, s, 0))

    def const_spec(shape):
        return pl.BlockSpec(shape, lambda b, s: (0, 0, 0))

    return pl.pallas_call(
        functools.partial(_ret_prompt_kernel, sblk=RET_SBLK, n_sblk=n_sblk),
        grid=(batch, n_sblk),
        in_specs=[head_spec, pl.BlockSpec((None, H_D, DK_D, RET_SBLK), lambda b, s: (b, 0, 0, s)), head_spec,
                  tok_spec, const_spec(dec.shape), const_spec(cdec.shape), const_spec(kdec.shape)],
        out_specs=[tok_spec, pl.BlockSpec((None, H_D, DK_D, DK_D), lambda b, s: (b, 0, 0, 0))],
        out_shape=[jax.ShapeDtypeStruct((batch, seq, D_MODEL), BF16),
                   jax.ShapeDtypeStruct((batch, H_D, DK_D, DK_D), F32)],
        scratch_shapes=[pltpu.VMEM((H_D, DK_D, DK_D), F32)],
        compiler_params=_cparams(("arbitrary", "arbitrary")),
        name="ret_prompt",
    )(q, kt, v, sgd.reshape(batch, seq, D_MODEL), dec, cdec, kdec)


def _even_front_sample_kernel(x_ref, mod_ref, g_ref, wbg, wcg, wxv, wga, wq, wk, wv, wgb, cw_ref, cb_ref,
                              prev_ref, ya_ref, q_ref, k_ref, v_ref, sgb_ref, cn_ref):
    h = _norm_mod(x_ref[...], g_ref[...], mod_ref[...]).astype(BF16)
    u = _dot(h, wcg[...]) * _dot(h, wxv[...])
    cw = cw_ref[...]
    conv = cb_ref[...] + cw[0:1, :] * prev_ref[0] + cw[1:2, :] * prev_ref[1] + cw[2:3, :] * u
    cn_ref[0] = prev_ref[1]
    cn_ref[1] = u
    ya_ref[...] = (_dot(h, wbg[...]) * conv * _silu(_dot(h, wga[...]))).astype(BF16)
    q_ref[...] = _dot(h, wq[...]) * (HD_B ** -0.5)
    k_ref[...] = _dot(h, wk[...])
    v_ref[...] = _dot(h, wv[...])
    sgb_ref[...] = _silu(_dot(h, wgb[...]))


def _even_front_sample(x2d, mod, g, w_in, cw, cb, prev_t):
    rows = x2d.shape[0]
    nj = D_MODEL // TN

    def wspec(s):
        return pl.BlockSpec((D_MODEL, TN), lambda j, s=s: (0, s * nj + j))

    full = pl.BlockSpec((rows, D_MODEL), lambda j: (0, 0))
    col = pl.BlockSpec((rows, TN), lambda j: (0, j))
    st = pl.BlockSpec((2, rows, TN), lambda j: (0, 0, j))
    colf = jax.ShapeDtypeStruct((rows, D_MODEL), F32)
    return pl.pallas_call(
        _even_front_sample_kernel,
        grid=(nj,),
        in_specs=[full, pl.BlockSpec((rows, 3 * D_MODEL), lambda j: (0, 0)),
                  pl.BlockSpec((1, D_MODEL), lambda j: (0, 0))] + [wspec(s) for s in range(8)] + [
            pl.BlockSpec((3, TN), lambda j: (0, j)), pl.BlockSpec((1, TN), lambda j: (0, j)), st],
        out_specs=[col, col, col, col, col, st],
        out_shape=[jax.ShapeDtypeStruct((rows, D_MODEL), BF16), colf, colf, colf, colf,
                   jax.ShapeDtypeStruct((2, rows, D_MODEL), F32)],
        compiler_params=_cparams(("arbitrary",)),
        name="even_front_sample",
    )(x2d, mod, g.reshape(1, D_MODEL), *([w_in] * 8), cw, cb.reshape(1, D_MODEL), prev_t)


def _attn_sample_kernel(slope_ref, pos_ref, q_ref, kn_ref, vn_ref, sgb_ref, kt_ref, vt_ref, yb_ref):
    gw = q_ref.shape[-1]
    me = pl.ds(pl.program_id(1) % 8, 1)
    row = lax.broadcasted_iota(jnp.int32, (8, gw), 0)
    col = lax.broadcasted_iota(jnp.int32, (8, gw), 1)
    own = (col >= row * HD_B) & (col < (row + 1) * HD_B)
    qm = jnp.where(own, q_ref[me, :], 0.0)
    s_self = jnp.sum(qm * kn_ref[me, :], axis=-1, keepdims=True)
    s_all = _dot(qm.astype(BF16), kt_ref[...].astype(BF16))
    s_all = s_all - slope_ref[:, 0:1] * pos_ref[0:1, :]
    v_self = vn_ref[me, :]
    probs, stats = [], []
    for p in range(len(DILATIONS)):
        s = s_all - pos_ref[p + 1:p + 2, :]
        m = jnp.maximum(jnp.max(s, axis=-1, keepdims=True), s_self)
        pe = jnp.exp(s - m)
        p_self = jnp.exp(s_self - m)
        probs.append(pe)
        stats.append((m, jnp.sum(pe, axis=-1, keepdims=True) + p_self, p_self))
    o_all = _dot_nt(jnp.concatenate(probs, axis=0).astype(BF16), vt_ref[...].astype(BF16))
    outs, lses = [], []
    for p, (m, l, p_self) in enumerate(stats):
        outs.append((o_all[8 * p:8 * p + 8, :] + p_self * v_self) * (1.0 / l))
        lses.append(m + jnp.log(l))
    m = jnp.maximum(jnp.maximum(lses[0], lses[1]), lses[2])
    ws = [jnp.exp(x - m) for x in lses]
    o = (ws[0] * outs[0] + ws[1] * outs[1] + ws[2] * outs[2]) * (1.0 / (ws[0] + ws[1] + ws[2]))
    o = jnp.sum(jnp.where(own, o, 0.0), axis=0, keepdims=True)
    yb_ref[me, :] = o * sgb_ref[me, :]


def _attn_sample(q, kn, vn, sgb, cache_kt, cache_vt):
    rows, _, wb = cache_kt.shape
    gh = 8
    gw = gh * HD_B
    ng = H_B // gh
    back = wb - np.arange(wb, dtype=np.float64)
    pos = [back] + [np.where((back % d == 0) & (back <= NK * d), 0.0, MASK_ADD) for d in DILATIONS]
    slopes = (2.0 ** (-8.0 * np.arange(1, H_B + 1) / H_B)).reshape(ng, gh)
    slope_arr = jnp.asarray(np.broadcast_to(slopes[:, :, None], (ng, 8, LANES)), F32)
    row_spec = pl.BlockSpec((8, gw), lambda g, b: (b // 8, g))
    t_spec = pl.BlockSpec((None, gw, wb), lambda g, b: (b, g, 0))
    return pl.pallas_call(
        _attn_sample_kernel,
        grid=(ng, rows),
        in_specs=[pl.BlockSpec((None, 8, LANES), lambda g, b: (g, 0, 0)),
                  pl.BlockSpec((4, wb), lambda g, b: (0, 0)),
                  row_spec, row_spec, row_spec, row_spec, t_spec, t_spec],
        out_specs=row_spec,
        out_shape=jax.ShapeDtypeStruct((rows, D_MODEL), F32),
        compiler_params=_cparams(("arbitrary", "arbitrary")),
        name="attn_sample",
    )(slope_arr, jnp.asarray(np.stack(pos), F32), q, kn, vn, sgb, cache_kt, cache_vt)


def _odd_front_sample_kernel(h_ref, wu, wgc, wq, wk, wv, wgd, pw_ref, ps_ref, prev_ref,
                             yc_ref, q_ref, k_ref, v_ref, sgd_ref, pn_ref):
    j = pl.program_id(0)
    h = h_ref[...]
    u = _dot(h, wu[...])
    n_prev = prev_ref.shape[0]
    sums = []
    s = jnp.zeros_like(u)
    back = 0
    for w in POOL_SIZES:
        while back < w - 1:
            s = s + prev_ref[n_prev - 1 - back]
            back += 1
        sums.append(s)
    win = _pick_by_chunk(j, sums) + u
    inv_w = _pick_by_chunk(j, [1.0 / w for w in POOL_SIZES])
    pooled = win * inv_w - u
    mixed = _dot(pooled.astype(BF16), pw_ref[...]) * ps_ref[...]
    yc_ref[...] = (mixed * _silu(_dot(h, wgc[...]))).astype(BF16)
    q_ref[...] = _dot(h, wq[...])
    k_ref[...] = _dot(h, wk[...]) * (DK_D ** -0.5)
    v_ref[...] = _dot(h, wv[...])
    sgd_ref[...] = _silu(_dot(h, wgd[...]))
    for t in range(n_prev - 1):
        pn_ref[t] = prev_ref[t + 1]
    pn_ref[n_prev - 1] = u


def _odd_front_sample(h2d, w_in, pw, ps, prev_t):
    rows = h2d.shape[0]
    n_prev = prev_t.shape[0]
    nj = D_MODEL // TN

    def wspec(s):
        return pl.BlockSpec((D_MODEL, TN), lambda j, s=s: (0, s * nj + j))

    col = pl.BlockSpec((rows, TN), lambda j: (0, j))
    st = pl.BlockSpec((n_prev, rows, TN), lambda j: (0, 0, j))
    colf = jax.ShapeDtypeStruct((rows, D_MODEL), F32)
    return pl.pallas_call(
        _odd_front_sample_kernel,
        grid=(nj,),
        in_specs=[pl.BlockSpec((rows, D_MODEL), lambda j: (0, 0))] + [wspec(s) for s in range(6)] + [
            pl.BlockSpec((None, TN, TN), lambda j: (j, 0, 0)), pl.BlockSpec((1, TN), lambda j: (0, j)), st],
        out_specs=[col, col, col, col, col, st],
        out_shape=[jax.ShapeDtypeStruct((rows, D_MODEL), BF16), colf, colf, colf, colf,
                   jax.ShapeDtypeStruct((n_prev, rows, D_MODEL), F32)],
        compiler_params=_cparams(("arbitrary",)),
        name="odd_front_sample",
    )(h2d, *([w_in] * 6), pw, ps.reshape(1, D_MODEL), prev_t)


RET_SAMPLE_ROWS = 8


def _ret_sample_kernel(q_ref, k_ref, v_ref, sgd_ref, st_ref, yd_ref, sn_ref):
    row = lax.broadcasted_iota(jnp.int32, (DK_D, DK_D), 0)
    col = lax.broadcasted_iota(jnp.int32, (DK_D, DK_D), 1)
    qs, ks, vs, gs = q_ref[...], k_ref[...], v_ref[...], sgd_ref[...]
    out_rows = []
    for i in range(RET_SAMPLE_ROWS):
        out_heads = []
        for h in range(H_D):
            g = float(np.exp(RET_LOG_G[h]))
            cols = slice(h * DK_D, (h + 1) * DK_D)
            q, k, v = qs[i:i + 1, cols], ks[i:i + 1, cols], vs[i:i + 1, cols]
            state = st_ref[i, h]
            qk = jnp.sum(q * k, axis=-1, keepdims=True)
            q8 = jnp.broadcast_to(q, (8, DK_D)).astype(BF16)
            cross = _dot(q8, state.astype(BF16))[0:1, :]
            o = qk * v + g * cross
            k_diag = jnp.where(row == col, jnp.broadcast_to(k, (DK_D, DK_D)), 0.0).astype(BF16)
            v_rows = jnp.broadcast_to(v, (DK_D, DK_D)).astype(BF16)
            sn_ref[i, h] = g * state + _dot(k_diag, v_rows)
            out_heads.append(_rms(o) * gs[i:i + 1, cols])
        out_rows.append(jnp.concatenate(out_heads, axis=1))
    yd_ref[...] = jnp.concatenate(out_rows, axis=0)


def _ret_sample(q, k, v, sgd, state):
    rows = q.shape[0]
    rs = RET_SAMPLE_ROWS
    row_spec = pl.BlockSpec((rs, D_MODEL), lambda i: (i, 0))
    st_spec = pl.BlockSpec((rs, H_D, DK_D, DK_D), lambda i: (i, 0, 0, 0))
    return pl.pallas_call(
        _ret_sample_kernel,
        grid=(rows // rs,),
        in_specs=[row_spec, row_spec, row_spec, row_spec, st_spec],
        out_specs=[row_spec, st_spec],
        out_shape=[jax.ShapeDtypeStruct((rows, D_MODEL), F32),
                   jax.ShapeDtypeStruct((rows, H_D, DK_D, DK_D), F32)],
        compiler_params=_cparams(("arbitrary",)),
        name="ret_sample",
    )(q, k, v, sgd, state)


def kernel(x_prompt, x_sample, c_prompt, c_sample, state_conv, cache_win_k, cache_win_v, state_pool, state_ret,
           norm_e, ada_w_e, ada_b_e, w_in_e, conv_w, conv_b, w_out_e, norm_o, ada_w_o, ada_b_o, w_in_o,
           pool_w, pool_scale, w_out_o, norm_f):
    batch, seq, d = x_prompt.shape
    sb = x_sample.shape[0]
    assert d == D_MODEL and x_sample.shape[1] == 1
    assert norm_e.shape[0] == 1 and norm_o.shape[0] == 1
    wb = cache_win_k.shape[2]
    keep = min(wb, seq)
    assert wb == DILATIONS[-1] * NK and seq % (DILATIONS[-1] * NK) == 0
    tm = TM_OUT

    n_c = batch + sb
    pad = (-n_c) % 8
    c_all = jnp.concatenate([c_prompt, c_sample, jnp.zeros((pad, d), F32)], axis=0)
    mod_e, mod_o = _adaln(c_all, ada_w_e[0], ada_b_e[0], ada_w_o[0], ada_b_o[0])
    mod_e_p, mod_e_s = mod_e[:batch], mod_e[batch:n_c]
    mod_o_p, mod_o_s = mod_o[:batch], mod_o[batch:n_c]

    w_in_e16 = w_in_e[0].astype(BF16)
    w_out_e16 = w_out_e[0].astype(BF16)
    w_in_o16 = w_in_o[0].astype(BF16)
    w_out_o16 = w_out_o[0].astype(BF16)
    pool_w16 = pool_w[0].astype(BF16)

    xp2d = x_prompt.reshape(batch * seq, d)
    ya, q, k, v, sgb, k_new, v_new, conv_tail = _even_front_prompt(
        xp2d, mod_e_p, norm_e[0], w_in_e16, conv_w[0], conv_b[0], batch, seq, keep, TM_FRONT)
    yb = _attn_prompt(q, k, v, sgb, batch, seq).reshape(batch * seq, d)
    x1, h1 = _outproj(ya, yb, xp2d, mod_e_p, w_out_e16, norm_o[0], mod_o_p, tm, seq)
    yc, rq, rk, rv, sgd, pool_tail = _odd_front_prompt(h1, w_in_o16, pool_w16, pool_scale[0], batch, seq, TM_FRONT)
    yd, ret_p = _ret_prompt(rq, rk, rv, sgd, batch, seq)
    y_prompt = _outproj(yc, yd.reshape(batch * seq, d), x1, mod_o_p, w_out_o16, norm_f, None, tm, seq)

    xs2d = x_sample.reshape(sb, d)
    conv_prev_t = jnp.transpose(state_conv[0], (1, 0, 2))
    ya_s, q_s, k_s, v_s, sgb_s, conv_s_t = _even_front_sample(
        xs2d, mod_e_s, norm_e[0], w_in_e16, conv_w[0], conv_b[0], conv_prev_t)
    cache_kt = jnp.transpose(cache_win_k[0], (0, 2, 3, 1)).reshape(sb, d, wb)
    cache_vt = jnp.transpose(cache_win_v[0], (0, 2, 3, 1)).reshape(sb, d, wb)
    yb_s = _attn_sample(q_s, k_s, v_s, sgb_s, cache_kt, cache_vt)
    x1_s, h1_s = _outproj(ya_s, yb_s, xs2d, mod_e_s, w_out_e16, norm_o[0], mod_o_s, sb, 1)
    pool_prev_t = jnp.transpose(state_pool[0], (1, 0, 2))
    yc_s, rq_s, rk_s, rv_s, sgd_s, pool_s_t = _odd_front_sample(h1_s, w_in_o16, pool_w16, pool_scale[0], pool_prev_t)
    yd_s, ret_s = _ret_sample(rq_s, rk_s, rv_s, sgd_s, state_ret[0])
    y_sample = _outproj(yc_s, yd_s, x1_s, mod_o_s, w_out_o16, norm_f, None, sb, 1)

    return (
        y_prompt.reshape(batch, seq, d),
        y_sample.reshape(sb, 1, d),
        conv_tail[:, 6:8][None],
        jnp.transpose(conv_s_t, (1, 0, 2))[None],
        jnp.transpose(k_new.reshape(batch, H_B, HD_B, keep), (0, 3, 1, 2))[None],
        k_s.reshape(1, sb, 1, H_B, HD_B),
        jnp.transpose(v_new.reshape(batch, H_B, HD_B, keep), (0, 3, 1, 2))[None],
        v_s.reshape(1, sb, 1, H_B, HD_B),
        pool_tail[:, 1:][None],
        jnp.transpose(pool_s_t, (1, 0, 2))[None],
        ret_p[None],
        ret_s[None],
    )
```

```python
import functools

import numpy as np
import jax
import jax.numpy as jnp
from jax import lax
from jax.experimental import pallas as pl
from jax.experimental.pallas import tpu as pltpu

F32 = jnp.float32
BF16 = jnp.bfloat16
WORD = jnp.uint32

D_MODEL = 1024
EPS = 1e-6
H_B = 16
HD_B = 64
N_PAIR = H_B // 2
LANES = 128
NK = 128
DILATIONS = (1, 4, 16)
POOL_SIZES = (2, 4, 8, 16)
POOL_PREV = 16
H_D = 4
DK_D = 256
RET_CHUNK = 256
TN = 256
TM_FRONT = 1024
TM_OUT = 1024
MASK_DIST = 1e9
LOG2E = float(np.log2(np.e))
MASK_ADD = 1e30
VMEM_LIMIT = 56 * 1024 * 1024

RET_LOG_G = [float(np.log(1.0 - 2.0 ** (-5.0 - h))) for h in range(H_D)]


def _cparams(sem):
    return pltpu.CompilerParams(dimension_semantics=sem, vmem_limit_bytes=VMEM_LIMIT)


def _silu(x):
    return x * (1.0 / (1.0 + jnp.exp(-x)))


def _dot(a, b):
    return jnp.dot(a, b, preferred_element_type=F32)


def _dot_nt(a, b):
    return lax.dot_general(a, b, (((1,), (1,)), ((), ())), preferred_element_type=F32)


def _rms(x):
    return x * lax.rsqrt(jnp.mean(x * x, axis=-1, keepdims=True) + EPS)


def _adaln_kernel(c_ref, we_ref, be_ref, wo_ref, bo_ref, me_ref, mo_ref):
    sc = _silu(c_ref[...]).astype(BF16)
    me_ref[...] = _dot(sc, we_ref[...].astype(BF16)) + be_ref[...]
    mo_ref[...] = _dot(sc, wo_ref[...].astype(BF16)) + bo_ref[...]


def _adaln(c_all, we, be, wo, bo):
    rows = c_all.shape[0]
    tn = 512
    n = 3 * D_MODEL
    wspec = pl.BlockSpec((D_MODEL, tn), lambda j: (0, j))
    bspec = pl.BlockSpec((1, tn), lambda j: (0, j))
    ospec = pl.BlockSpec((rows, tn), lambda j: (0, j))
    return pl.pallas_call(
        _adaln_kernel,
        grid=(n // tn,),
        in_specs=[pl.BlockSpec((rows, D_MODEL), lambda j: (0, 0)), wspec, bspec, wspec, bspec],
        out_specs=[ospec, ospec],
        out_shape=[jax.ShapeDtypeStruct((rows, n), F32)] * 2,
        compiler_params=_cparams(("arbitrary",)),
        name="adaln",
    )(c_all, we, be.reshape(1, n), wo, bo.reshape(1, n))


def _norm_mod(x, g, mod):
    shift = mod[:, 0:D_MODEL]
    scale = mod[:, D_MODEL:2 * D_MODEL]
    return _rms(x) * g * (1.0 + scale) + shift


def _shift_rows(u, k, prev_rows):
    row = lax.broadcasted_iota(jnp.int32, u.shape, 0)
    out = pltpu.roll(u, k, 0)
    for idx, pr in enumerate(prev_rows):
        out = jnp.where(row == idx, pr, out)
    return out


N_SEG_E = 8
PHASES_E = 8


def _phase_segments(t, n_seg, n_phases):
    return range(t * n_seg // n_phases, (t + 1) * n_seg // n_phases)


def _even_front_kernel(x_ref, mod_ref, g_ref, *refs, tiles_per_batch, n_chunks):
    ws = refs[:N_SEG_E]
    cw_ref, cb_ref = refs[N_SEG_E:N_SEG_E + 2]
    ya_ref, q_ref, k_ref, v_ref, sgb_ref, kn_ref, vn_ref, cn_ref = refs[N_SEG_E + 2:N_SEG_E + 10]
    h_scr, carry_scr, z_a, z_b = refs[N_SEG_E + 10:]
    nj = D_MODEL // TN
    s = pl.program_id(0)
    c_out = jnp.maximum(s - 1, 0)
    j_out = c_out % nj
    il_out = (c_out // nj) % tiles_per_batch

    @pl.when(s == 0)
    def _():
        z_b[...] = jnp.zeros_like(z_b)
        carry_scr[...] = jnp.zeros_like(carry_scr)

    @pl.when((s < n_chunks) & (s % nj == 0))
    def _():
        h_scr[...] = _norm_mod(x_ref[...], g_ref[...], mod_ref[...]).astype(BF16)

    tm = h_scr.shape[0]
    rb = tm // PHASES_E

    def phase(t, z_w, z_r):
        for seg in _phase_segments(t, N_SEG_E, PHASES_E):
            z_w[seg] = _dot(h_scr[...], ws[seg][...])
        rows = slice(t * rb, (t + 1) * rb)
        bg, cg, xv, ga, q, k, v, gb = (z_r[seg, rows, :] for seg in range(N_SEG_E))
        u = cg * xv
        prev = carry_scr[j_out]
        if t == 0:
            prev = jnp.where(il_out == 0, 0.0, prev)
        p2, p1 = prev[6:7, :], prev[7:8, :]
        u1 = _shift_rows(u, 1, [p1])
        u2 = _shift_rows(u, 2, [p2, p1])
        cw = cw_ref[...]
        conv = cb_ref[...] + cw[0:1, :] * u2 + cw[1:2, :] * u1 + cw[2:3, :] * u
        tail = u[rb - 8:rb, :]
        carry_scr[j_out] = tail
        if t == PHASES_E - 1:
            cn_ref[...] = tail
        ya_ref[rows, :] = (bg * conv * _silu(ga)).astype(BF16)
        kn_ref[:, rows] = k.T
        vn_ref[:, rows] = v.T
        wrows = slice(t * rb // 2, (t + 1) * rb // 2)
        for ref, val in ((q_ref, q * (HD_B ** -0.5 * LOG2E)),
                         (k_ref, k), (v_ref, v)):
            words = pltpu.bitcast(val.astype(BF16), WORD)
            for e in range(TN // LANES):
                ref[e, wrows, :] = words[:, e * LANES:(e + 1) * LANES]
        sgb = _silu(gb).astype(BF16)
        for e in range(TN // LANES):
            sgb_ref[e, rows, :] = sgb[:, e * LANES:(e + 1) * LANES]

    for parity, (z_w, z_r) in enumerate(((z_a, z_b), (z_b, z_a))):
        for t in range(PHASES_E):
            pl.when(s % 2 == parity)(functools.partial(phase, t, z_w, z_r))


def _even_front_prompt(x2d, mod, g, w_in, cw, cb, batch, seq, keep, tm):
    rows = batch * seq
    tpb = seq // tm
    nj = D_MODEL // TN
    n_chunks = (rows // tm) * nj
    off = (seq - keep) // tm
    ppc = TN // LANES

    def c_in(s):
        return jnp.minimum(s, n_chunks - 1)

    def c_out(s):
        return jnp.maximum(s - 1, 0)

    def out_ij(s):
        c = c_out(s)
        return c // nj, c % nj

    def pair_map(s):
        i, j = out_ij(s)
        return (i // tpb, j, i % tpb, 0)

    def keep_map(s):
        i, j = out_ij(s)
        il = i % tpb
        kept = il >= off
        return (i // tpb, jnp.where(kept, j, 0), jnp.where(kept, il - off, 0))

    def tail_map(s):
        i, j = out_ij(s)
        return (i // tpb, 0, jnp.where(i % tpb == tpb - 1, j, 0))

    pair_spec = pl.BlockSpec((None, ppc, tm, LANES), pair_map)
    pair_shape = jax.ShapeDtypeStruct((batch, N_PAIR, seq, LANES), BF16)
    word_spec = pl.BlockSpec((None, ppc, tm // 2, LANES), pair_map)
    word_shape = jax.ShapeDtypeStruct((batch, N_PAIR, seq // 2, LANES), WORD)
    keep_spec = pl.BlockSpec((None, TN, tm), keep_map)
    keep_shape = jax.ShapeDtypeStruct((batch, D_MODEL, keep), F32)
    return pl.pallas_call(
        functools.partial(_even_front_kernel, tiles_per_batch=tpb, n_chunks=n_chunks),
        grid=(n_chunks + 1,),
        in_specs=[
            pl.BlockSpec((tm, D_MODEL), lambda s: (c_in(s) // nj, 0)),
            pl.BlockSpec((None, 1, 3 * D_MODEL), lambda s: (c_in(s) // nj // tpb, 0, 0)),
            pl.BlockSpec((1, D_MODEL), lambda s: (0, 0)),
        ] + [pl.BlockSpec((D_MODEL, TN), lambda s, seg=seg: (0, seg * nj + c_in(s) % nj)) for seg in range(N_SEG_E)] + [
            pl.BlockSpec((3, TN), lambda s: (0, c_out(s) % nj)),
            pl.BlockSpec((1, TN), lambda s: (0, c_out(s) % nj)),
        ],
        out_specs=[
            pl.BlockSpec((tm, TN), lambda s: out_ij(s)),
            word_spec, word_spec, word_spec, pair_spec,
            keep_spec, keep_spec,
            pl.BlockSpec((None, 8, TN), tail_map),
        ],
        out_shape=[
            jax.ShapeDtypeStruct((rows, D_MODEL), BF16),
            word_shape, word_shape, word_shape, pair_shape,
            keep_shape, keep_shape,
            jax.ShapeDtypeStruct((batch, 8, D_MODEL), F32),
        ],
        scratch_shapes=[pltpu.VMEM((tm, D_MODEL), BF16), pltpu.VMEM((nj, 8, TN), F32),
                        pltpu.VMEM((N_SEG_E, tm, TN), F32), pltpu.VMEM((N_SEG_E, tm, TN), F32)],
        compiler_params=_cparams(("arbitrary",)),
        name="even_front_prompt",
    )(x2d, mod.reshape(batch, 1, 3 * D_MODEL), g.reshape(1, D_MODEL), *([w_in] * N_SEG_E), cw, cb.reshape(1, D_MODEL))


STAT_PITCH = {1: 1, 4: 4, 16: 24}
ATTN_UNROLL = 32


def _residue_stream(src, r):
    n_out = src.shape[0] // 4
    halves = [pltpu.unpack_elementwise(src[pl.ds(r // 2 + off, n_out, stride=4), :], index=r % 2,
                                       packed_dtype=BF16, unpacked_dtype=F32) for off in (0, 2)]
    return pltpu.pack_elementwise(halves, packed_dtype=BF16)


def _attn_prompt_kernel(slope_ref, dist_ref, q_ref, k_ref, v_ref, sgb_ref, yb_ref,
                        q4_scr, q16_scr, kv4_scr, kv16_scr, acc_scr, m_scr, l_scr, bias_scr, *, seq):
    hp = pl.program_id(1)
    lane = lax.broadcasted_iota(jnp.int32, (1, LANES), 1)
    half0 = lane < HD_B
    dist = dist_ref[...]
    masked = jnp.full((NK, NK), MASK_DIST, F32)
    for p, d in enumerate(DILATIONS):
        for e in range(2):
            bias = dist * (slope_ref[2 * hp + e] * (float(d) * LOG2E))
            bias_scr[2 * p + e] = bias
            bias_scr[6 + 2 * p + e] = jnp.concatenate([bias[:, NK:], masked], axis=1)

    for a, ref in enumerate((q_ref, k_ref, v_ref)):
        for r in range(4):
            s = _residue_stream(ref, r)
            if a == 0:
                q4_scr[r] = s
            else:
                kv4_scr[a - 1, r] = s
        for r in range(16):
            s = _residue_stream(q4_scr.at[r % 4] if a == 0 else kv4_scr.at[a - 1, r % 4], r // 4)
            if a == 0:
                q16_scr[r] = s
            else:
                kv16_scr[a - 1, r] = s

    ones = jnp.ones((2 * NK, LANES), BF16)
    hw = NK // 2

    def tile(p, d, r, n):
        qw = pl.ds(pl.multiple_of(n * hw, hw), hw)
        kw = pl.ds(pl.multiple_of(jnp.maximum(n - 1, 0) * hw, hw), 2 * hw)
        if d == 1:
            qs, ks, vs = q_ref, k_ref, v_ref
        elif d == 4:
            qs, ks, vs = q4_scr.at[r], kv4_scr.at[0, r], kv4_scr.at[1, r]
        else:
            qs, ks, vs = q16_scr.at[r], kv16_scr.at[0, r], kv16_scr.at[1, r]
        q = pltpu.bitcast(qs[qw, :], BF16)
        kk = pltpu.bitcast(ks[kw, :], BF16)
        vv1 = jnp.concatenate([pltpu.bitcast(vs[kw, :], BF16), ones], axis=1)
        first = jnp.where(n == 0, 1, 0)
        res = []
        for e in range(2):
            qe = jnp.where(half0 if e == 0 else jnp.logical_not(half0), q, jnp.zeros_like(q))
            s = _dot_nt(qe, kk) - bias_scr[6 * first + 2 * p + e]
            m = jnp.max(s, axis=-1, keepdims=True)
            pe = jnp.exp2(s - m).astype(BF16)
            res.append((_dot(pe, vv1), m))
        (a0, m0), (a1, m1) = res
        acc = jnp.where(half0, a0[:, :LANES], a1[:, :LANES])
        l = jnp.where(half0, a0[:, LANES:], a1[:, LANES:])
        m = jnp.where(half0, m0, m1)
        if d > 1:
            pitch = STAT_PITCH[d]
            rows = pl.ds(r + pitch * NK * n, NK, stride=pitch)
            acc_scr[p - 1, rows, :] = acc
            l_scr[p - 1, rows, :] = l
            m_scr[p - 1, rows, :] = m
            return
        accs, ls, ms = [acc], [l], [m]
        for pp in range(1, len(DILATIONS)):
            dd = DILATIONS[pp]
            for ref, vals in ((acc_scr, accs), (l_scr, ls), (m_scr, ms)):
                vals.append(jnp.concatenate(
                    [ref[pp - 1, pl.ds(pl.multiple_of((n * (NK // dd) + g) * STAT_PITCH[dd], 8), dd), :]
                     for g in range(NK // dd)], axis=0) if STAT_PITCH[dd] != dd
                    else ref[pp - 1, pl.ds(pl.multiple_of(n * NK, NK), NK), :])
        top = jnp.maximum(jnp.maximum(ms[0], ms[1]), ms[2])
        ws = [jnp.exp2(x - top) for x in ms]
        num = ws[0] * accs[0] + ws[1] * accs[1] + ws[2] * accs[2]
        den = ws[0] * ls[0] + ws[1] * ls[1] + ws[2] * ls[2]
        rows = pl.ds(pl.multiple_of(n * NK, NK), NK)
        yb_ref[rows, :] = (num * (1.0 / den) * sgb_ref[rows, :].astype(F32)).astype(BF16)

    for p, d in reversed(list(enumerate(DILATIONS))):
        def body(it, c, p=p, d=d):
            for u in range(ATTN_UNROLL):
                t = it * ATTN_UNROLL + u
                if d == 1:
                    tile(p, d, 0, t)
                else:
                    tile(p, d, t % d, t // d)
            return c

        lax.fori_loop(0, seq // NK // ATTN_UNROLL, body, 0)


def _band_distance():
    qi = np.arange(NK)[:, None]
    kj = np.arange(2 * NK)[None, :]
    dist = NK + qi - kj
    return jnp.asarray(np.where((dist >= 0) & (dist <= NK), dist, MASK_DIST), dtype=F32)


def _alibi_slopes():
    return jnp.asarray(2.0 ** (-8.0 * np.arange(1, H_B + 1) / H_B), dtype=F32)


def _attn_prompt(q, k, v, sgb, batch, seq):
    word_spec = pl.BlockSpec((None, None, seq // 2, LANES), lambda b, hp: (b, hp, 0, 0))
    stat_rows = max(seq // d * STAT_PITCH[d] for d in DILATIONS)
    stat = pltpu.VMEM((len(DILATIONS) - 1, stat_rows, LANES), F32)
    return pl.pallas_call(
        functools.partial(_attn_prompt_kernel, seq=seq),
        grid=(batch, N_PAIR),
        in_specs=[pl.BlockSpec(memory_space=pltpu.SMEM),
                  pl.BlockSpec((NK, 2 * NK), lambda b, hp: (0, 0)),
                  word_spec, word_spec, word_spec,
                  pl.BlockSpec((None, None, seq, LANES), lambda b, hp: (b, hp, 0, 0))],
        out_specs=pl.BlockSpec((None, seq, LANES), lambda b, hp: (b, 0, hp)),
        out_shape=jax.ShapeDtypeStruct((batch, seq, D_MODEL), BF16),
        scratch_shapes=[pltpu.VMEM((4, seq // 8, LANES), WORD),
                        pltpu.VMEM((16, seq // 32, LANES), WORD),
                        pltpu.VMEM((2, 4, seq // 8, LANES), WORD),
                        pltpu.VMEM((2, 16, seq // 32, LANES), WORD),
                        stat, stat, stat,
                        pltpu.VMEM((12, NK, 2 * NK), F32)],
        compiler_params=_cparams(("arbitrary", "arbitrary")),
        name="attn_prompt",
    )(_alibi_slopes(), _band_distance(), q, k, v, sgb)


def _outproj_mid_kernel(ya_ref, yb_ref, x_ref, mod_ref, w1_ref, w2_ref, g_ref, mod2_ref, x1_ref, h_ref):
    gate = mod_ref[:, 2 * D_MODEL:3 * D_MODEL]
    x1 = x_ref[...] + gate * (_dot(ya_ref[...], w1_ref[...]) + _dot(yb_ref[...].astype(BF16), w2_ref[...]))
    x1_ref[...] = x1
    h_ref[...] = _norm_mod(x1, g_ref[...], mod2_ref[...]).astype(BF16)


def _outproj_final_kernel(ya_ref, yb_ref, x_ref, mod_ref, w1_ref, w2_ref, g_ref, y_ref):
    gate = mod_ref[:, 2 * D_MODEL:3 * D_MODEL]
    x1 = x_ref[...] + gate * (_dot(ya_ref[...], w1_ref[...]) + _dot(yb_ref[...].astype(BF16), w2_ref[...]))
    y_ref[...] = _rms(x1) * g_ref[...]


def _outproj(ya, yb, x2d, mod, w_out, g, mod2, tm, rows_per_mod):
    rows = x2d.shape[0]
    half = w_out.shape[0] // 2
    row_spec = pl.BlockSpec((tm, D_MODEL), lambda i: (i, 0))
    if rows_per_mod == 1:
        mod_spec = pl.BlockSpec((tm, 3 * D_MODEL), lambda i: (i, 0))
        mods = (mod, mod2)
    else:
        tpb = rows_per_mod // tm
        mod_spec = pl.BlockSpec((None, 1, 3 * D_MODEL), lambda i: (i // tpb, 0, 0))
        mods = tuple(None if m is None else m.reshape(-1, 1, 3 * D_MODEL) for m in (mod, mod2))
    w1_spec = pl.BlockSpec((half, D_MODEL), lambda i: (0, 0))
    w2_spec = pl.BlockSpec((half, D_MODEL), lambda i: (1, 0))
    g_spec = pl.BlockSpec((1, D_MODEL), lambda i: (0, 0))
    common = dict(grid=(rows // tm,), compiler_params=_cparams(("arbitrary",)))
    if mod2 is None:
        return pl.pallas_call(
            _outproj_final_kernel,
            in_specs=[row_spec, row_spec, row_spec, mod_spec, w1_spec, w2_spec, g_spec],
            out_specs=row_spec,
            out_shape=jax.ShapeDtypeStruct((rows, D_MODEL), F32),
            name="outproj_final", **common,
        )(ya, yb, x2d, mods[0], w_out, w_out, g.reshape(1, D_MODEL))
    return pl.pallas_call(
        _outproj_mid_kernel,
        in_specs=[row_spec, row_spec, row_spec, mod_spec, w1_spec, w2_spec, g_spec, mod_spec],
        out_specs=[row_spec, row_spec],
        out_shape=[jax.ShapeDtypeStruct((rows, D_MODEL), F32), jax.ShapeDtypeStruct((rows, D_MODEL), BF16)],
        name="outproj_mid", **common,
    )(ya, yb, x2d, mods[0], w_out, w_out, g.reshape(1, D_MODEL), mods[1])


def _pick_by_chunk(j, vals):
    out = vals[-1]
    for idx in range(len(vals) - 2, -1, -1):
        out = jnp.where(j == idx, vals[idx], out)
    return out


N_SEG_O = 6
PHASES_O = 4


def _odd_front_kernel(h_ref, *refs, tiles_per_batch):
    ws = refs[:N_SEG_O]
    pw_ref, ps_ref = refs[N_SEG_O:N_SEG_O + 2]
    yc_ref, q_ref, kt_ref, v_ref, sgd_ref, pn_ref = refs[N_SEG_O + 2:N_SEG_O + 8]
    carry_scr, z_a, z_b = refs[N_SEG_O + 8:]
    nj = D_MODEL // TN
    s = pl.program_id(0)
    c_out = jnp.maximum(s - 1, 0)
    j_out = c_out % nj
    il_out = (c_out // nj) % tiles_per_batch
    tm = h_ref.shape[0]
    rb = tm // PHASES_O

    @pl.when(s == 0)
    def _():
        z_b[...] = jnp.zeros_like(z_b)
        carry_scr[...] = jnp.zeros_like(carry_scr)

    def phase(t, z_w, z_r):
        for seg in _phase_segments(t, N_SEG_O, PHASES_O):
            z_w[seg] = _dot(h_ref[...], ws[seg][...])
        rows = slice(t * rb, (t + 1) * rb)
        u, gc, q, k, v, gd = (z_r[seg, rows, :] for seg in range(N_SEG_O))
        prev = carry_scr[j_out]
        if t == 0:
            prev = jnp.where(il_out == 0, 0.0, prev)
        tail = u[rb - POOL_PREV:rb, :]
        carry_scr[j_out] = tail
        if t == PHASES_O - 1:
            pn_ref[...] = tail
        ext = jnp.concatenate([prev, u], axis=0)
        sums = []
        acc = ext
        for sh in (1, 2, 4, 8):
            acc = acc + pltpu.roll(acc, sh, 0)
            sums.append(acc[POOL_PREV:, :])
        win = _pick_by_chunk(j_out, sums)
        width = _pick_by_chunk(j_out, [float(w) for w in POOL_SIZES])
        pos = (il_out * tm + t * rb + lax.broadcasted_iota(jnp.int32, (rb, 1), 0)).astype(F32)
        pooled = win / jnp.minimum(width, pos + 1.0) - u
        mixed = _dot(pooled.astype(BF16), pw_ref[...]) * ps_ref[...]
        yc_ref[rows, :] = (mixed * _silu(gc)).astype(BF16)
        q_ref[rows, :] = q.astype(BF16)
        kt_ref[:, rows] = (k * (DK_D ** -0.5)).T.astype(BF16)
        v_ref[rows, :] = v.astype(BF16)
        sgd_ref[rows, :] = _silu(gd).astype(BF16)

    for parity, (z_w, z_r) in enumerate(((z_a, z_b), (z_b, z_a))):
        for t in range(PHASES_O):
            pl.when(s % 2 == parity)(functools.partial(phase, t, z_w, z_r))


def _odd_front_prompt(h2d, w_in, pw, ps, batch, seq, tm):
    rows = batch * seq
    tpb = seq // tm
    nj = D_MODEL // TN
    n_chunks = (rows // tm) * nj

    def c_in(s):
        return jnp.minimum(s, n_chunks - 1)

    def out_ij(s):
        c = jnp.maximum(s - 1, 0)
        return c // nj, c % nj

    def head_map(s):
        i, j = out_ij(s)
        return (i // tpb, j, i % tpb, 0)

    def kt_map(s):
        i, j = out_ij(s)
        return (i // tpb, j, 0, i % tpb)

    def tail_map(s):
        i, j = out_ij(s)
        return (i // tpb, 0, jnp.where(i % tpb == tpb - 1, j, 0))

    head_spec = pl.BlockSpec((None, None, tm, TN), head_map)
    head_shape = jax.ShapeDtypeStruct((batch, H_D, seq, DK_D), BF16)
    tok_spec = pl.BlockSpec((tm, TN), out_ij)
    tok_shape = jax.ShapeDtypeStruct((rows, D_MODEL), BF16)
    zbuf = pltpu.VMEM((N_SEG_O, tm, TN), F32)
    return pl.pallas_call(
        functools.partial(_odd_front_kernel, tiles_per_batch=tpb),
        grid=(n_chunks + 1,),
        in_specs=[pl.BlockSpec((tm, D_MODEL), lambda s: (c_in(s) // nj, 0))] + [
                  pl.BlockSpec((D_MODEL, TN), lambda s, seg=seg: (0, seg * nj + c_in(s) % nj))
                  for seg in range(N_SEG_O)] + [
                  pl.BlockSpec((None, TN, TN), lambda s: (out_ij(s)[1], 0, 0)),
                  pl.BlockSpec((1, TN), lambda s: (0, out_ij(s)[1]))],
        out_specs=[tok_spec, head_spec, pl.BlockSpec((None, None, TN, tm), kt_map), head_spec, tok_spec,
                   pl.BlockSpec((None, POOL_PREV, TN), tail_map)],
        out_shape=[tok_shape, head_shape, jax.ShapeDtypeStruct((batch, H_D, DK_D, seq), BF16),
                   head_shape, tok_shape,
                   jax.ShapeDtypeStruct((batch, POOL_PREV, D_MODEL), F32)],
        scratch_shapes=[pltpu.VMEM((nj, POOL_PREV, TN), F32), zbuf, zbuf],
        compiler_params=_cparams(("arbitrary",)),
        name="odd_front_prompt",
    )(h2d, *([w_in] * N_SEG_O), pw, ps.reshape(1, D_MODEL))


RET_SBLK = 1024


def _ret_prompt_kernel(q_ref, kt_ref, v_ref, sgd_ref, dec_ref, cdec_ref, kdec_ref,
                       yd_ref, st_ref, state_scr, *, sblk, n_sblk):
    sb = pl.program_id(1)

    @pl.when(sb == 0)
    def _():
        state_scr[...] = jnp.zeros_like(state_scr)

    def chunk(c, carry):
        rows = pl.ds(pl.multiple_of(c * RET_CHUNK, RET_CHUNK), RET_CHUNK)
        for h in range(H_D):
            cols = slice(h * DK_D, (h + 1) * DK_D)
            q = q_ref[h, rows, :]
            kt = kt_ref[h, :, rows]
            v = v_ref[h, rows, :]
            state = state_scr[h]
            scores = _dot(q, kt) * dec_ref[h]
            o = _dot(scores.astype(BF16), v) + _dot(q, state.astype(BF16)) * cdec_ref[h]
            kd = (kt.astype(F32) * kdec_ref[h]).astype(BF16)
            state_scr[h] = float(np.exp(RET_CHUNK * RET_LOG_G[h])) * state + _dot(kd, v)
            yd_ref[rows, cols] = (_rms(o) * sgd_ref[rows, cols].astype(F32)).astype(BF16)
        return carry

    lax.fori_loop(0, sblk // RET_CHUNK, chunk, 0, unroll=4)

    @pl.when(sb == n_sblk - 1)
    def _():
        st_ref[...] = state_scr[...]


def _ret_consts():
    t = np.arange(RET_CHUNK, dtype=np.float64)
    diff = t[:, None] - t[None, :]
    lg = np.asarray(RET_LOG_G)[:, None, None]
    dec = np.where(diff >= 0, np.exp(np.maximum(diff, 0.0)[None] * lg), 0.0)
    cdec = np.broadcast_to(np.exp((t + 1.0)[None, :, None] * lg), (H_D, RET_CHUNK, DK_D))
    kdec = np.broadcast_to(np.exp((RET_CHUNK - 1.0 - t)[None, None, :] * lg), (H_D, DK_D, RET_CHUNK))
    return jnp.asarray(dec, F32), jnp.asarray(cdec, F32), jnp.asarray(kdec, F32)


def _ret_prompt(q, kt, v, sgd, batch, seq):
    dec, cdec, kdec = _ret_consts()
    n_sblk = seq // RET_SBLK
    head_spec = pl.BlockSpec((None, H_D, RET_SBLK, DK_D), lambda b, s: (b, 0, s, 0))
    tok_spec = pl.BlockSpec((None, RET_SBLK, D_MODEL), lambda b, s: (b, s, 0))

    def const_spec(shape):
        return pl.BlockSpec(shape, lambda b, s: (0, 0, 0))

    return pl.pallas_call(
        functools.partial(_ret_prompt_kernel, sblk=RET_SBLK, n_sblk=n_sblk),
        grid=(batch, n_sblk),
        in_specs=[head_spec, pl.BlockSpec((None, H_D, DK_D, RET_SBLK), lambda b, s: (b, 0, 0, s)), head_spec,
                  tok_spec, const_spec(dec.shape), const_spec(cdec.shape), const_spec(kdec.shape)],
        out_specs=[tok_spec, pl.BlockSpec((None, H_D, DK_D, DK_D), lambda b, s: (b, 0, 0, 0))],
        out_shape=[jax.ShapeDtypeStruct((batch, seq, D_MODEL), BF16),
                   jax.ShapeDtypeStruct((batch, H_D, DK_D, DK_D), F32)],
        scratch_shapes=[pltpu.VMEM((H_D, DK_D, DK_D), F32)],
        compiler_params=_cparams(("arbitrary", "arbitrary")),
        name="ret_prompt",
    )(q, kt, v, sgd.reshape(batch, seq, D_MODEL), dec, cdec, kdec)


def _even_front_sample_kernel(x_ref, mod_ref, g_ref, wbg, wcg, wxv, wga, wq, wk, wv, wgb, cw_ref, cb_ref,
                              prev_ref, ya_ref, q_ref, k_ref, v_ref, sgb_ref, cn_ref):
    h = _norm_mod(x_ref[...], g_ref[...], mod_ref[...]).astype(BF16)
    u = _dot(h, wcg[...]) * _dot(h, wxv[...])
    cw = cw_ref[...]
    conv = cb_ref[...] + cw[0:1, :] * prev_ref[0] + cw[1:2, :] * prev_ref[1] + cw[2:3, :] * u
    cn_ref[0] = prev_ref[1]
    cn_ref[1] = u
    ya_ref[...] = (_dot(h, wbg[...]) * conv * _silu(_dot(h, wga[...]))).astype(BF16)
    q_ref[...] = _dot(h, wq[...]) * (HD_B ** -0.5)
    k_ref[...] = _dot(h, wk[...])
    v_ref[...] = _dot(h, wv[...])
    sgb_ref[...] = _silu(_dot(h, wgb[...]))


def _even_front_sample(x2d, mod, g, w_in, cw, cb, prev_t):
    rows = x2d.shape[0]
    nj = D_MODEL // TN

    def wspec(s):
        return pl.BlockSpec((D_MODEL, TN), lambda j, s=s: (0, s * nj + j))

    full = pl.BlockSpec((rows, D_MODEL), lambda j: (0, 0))
    col = pl.BlockSpec((rows, TN), lambda j: (0, j))
    st = pl.BlockSpec((2, rows, TN), lambda j: (0, 0, j))
    colf = jax.ShapeDtypeStruct((rows, D_MODEL), F32)
    return pl.pallas_call(
        _even_front_sample_kernel,
        grid=(nj,),
        in_specs=[full, pl.BlockSpec((rows, 3 * D_MODEL), lambda j: (0, 0)),
                  pl.BlockSpec((1, D_MODEL), lambda j: (0, 0))] + [wspec(s) for s in range(8)] + [
            pl.BlockSpec((3, TN), lambda j: (0, j)), pl.BlockSpec((1, TN), lambda j: (0, j)), st],
        out_specs=[col, col, col, col, col, st],
        out_shape=[jax.ShapeDtypeStruct((rows, D_MODEL), BF16), colf, colf, colf, colf,
                   jax.ShapeDtypeStruct((2, rows, D_MODEL), F32)],
        compiler_params=_cparams(("arbitrary",)),
        name="even_front_sample",
    )(x2d, mod, g.reshape(1, D_MODEL), *([w_in] * 8), cw, cb.reshape(1, D_MODEL), prev_t)


def _attn_sample_kernel(slope_ref, pos_ref, q_ref, kn_ref, vn_ref, sgb_ref, kt_ref, vt_ref, yb_ref):
    gw = q_ref.shape[-1]
    me = pl.ds(pl.program_id(1) % 8, 1)
    row = lax.broadcasted_iota(jnp.int32, (8, gw), 0)
    col = lax.broadcasted_iota(jnp.int32, (8, gw), 1)
    own = (col >= row * HD_B) & (col < (row + 1) * HD_B)
    qm = jnp.where(own, q_ref[me, :], 0.0)
    s_self = jnp.sum(qm * kn_ref[me, :], axis=-1, keepdims=True)
    s_all = _dot(qm.astype(BF16), kt_ref[...].astype(BF16))
    s_all = s_all - slope_ref[:, 0:1] * pos_ref[0:1, :]
    v_self = vn_ref[me, :]
    probs, stats = [], []
    for p in range(len(DILATIONS)):
        s = s_all - pos_ref[p + 1:p + 2, :]
        m = jnp.maximum(jnp.max(s, axis=-1, keepdims=True), s_self)
        pe = jnp.exp(s - m)
        p_self = jnp.exp(s_self - m)
        probs.append(pe)
        stats.append((m, jnp.sum(pe, axis=-1, keepdims=True) + p_self, p_self))
    o_all = _dot_nt(jnp.concatenate(probs, axis=0).astype(BF16), vt_ref[...].astype(BF16))
    outs, lses = [], []
    for p, (m, l, p_self) in enumerate(stats):
        outs.append((o_all[8 * p:8 * p + 8, :] + p_self * v_self) * (1.0 / l))
        lses.append(m + jnp.log(l))
    m = jnp.maximum(jnp.maximum(lses[0], lses[1]), lses[2])
    ws = [jnp.exp(x - m) for x in lses]
    o = (ws[0] * outs[0] + ws[1] * outs[1] + ws[2] * outs[2]) * (1.0 / (ws[0] + ws[1] + ws[2]))
    o = jnp.sum(jnp.where(own, o, 0.0), axis=0, keepdims=True)
    yb_ref[me, :] = o * sgb_ref[me, :]


def _attn_sample(q, kn, vn, sgb, cache_kt, cache_vt):
    rows, _, wb = cache_kt.shape
    gh = 8
    gw = gh * HD_B
    ng = H_B // gh
    back = wb - np.arange(wb, dtype=np.float64)
    pos = [back] + [np.where((back % d == 0) & (back <= NK * d), 0.0, MASK_ADD) for d in DILATIONS]
    slopes = (2.0 ** (-8.0 * np.arange(1, H_B + 1) / H_B)).reshape(ng, gh)
    slope_arr = jnp.asarray(np.broadcast_to(slopes[:, :, None], (ng, 8, LANES)), F32)
    row_spec = pl.BlockSpec((8, gw), lambda g, b: (b // 8, g))
    t_spec = pl.BlockSpec((None, gw, wb), lambda g, b: (b, g, 0))
    return pl.pallas_call(
        _attn_sample_kernel,
        grid=(ng, rows),
        in_specs=[pl.BlockSpec((None, 8, LANES), lambda g, b: (g, 0, 0)),
                  pl.BlockSpec((4, wb), lambda g, b: (0, 0)),
                  row_spec, row_spec, row_spec, row_spec, t_spec, t_spec],
        out_specs=row_spec,
        out_shape=jax.ShapeDtypeStruct((rows, D_MODEL), F32),
        compiler_params=_cparams(("arbitrary", "arbitrary")),
        name="attn_sample",
    )(slope_arr, jnp.asarray(np.stack(pos), F32), q, kn, vn, sgb, cache_kt, cache_vt)


def _odd_front_sample_kernel(h_ref, wu, wgc, wq, wk, wv, wgd, pw_ref, ps_ref, prev_ref,
                             yc_ref, q_ref, k_ref, v_ref, sgd_ref, pn_ref):
    j = pl.program_id(0)
    h = h_ref[...]
    u = _dot(h, wu[...])
    n_prev = prev_ref.shape[0]
    sums = []
    s = jnp.zeros_like(u)
    back = 0
    for w in POOL_SIZES:
        while back < w - 1:
            s = s + prev_ref[n_prev - 1 - back]
            back += 1
        sums.append(s)
    win = _pick_by_chunk(j, sums) + u
    inv_w = _pick_by_chunk(j, [1.0 / w for w in POOL_SIZES])
    pooled = win * inv_w - u
    mixed = _dot(pooled.astype(BF16), pw_ref[...]) * ps_ref[...]
    yc_ref[...] = (mixed * _silu(_dot(h, wgc[...]))).astype(BF16)
    q_ref[...] = _dot(h, wq[...])
    k_ref[...] = _dot(h, wk[...]) * (DK_D ** -0.5)
    v_ref[...] = _dot(h, wv[...])
    sgd_ref[...] = _silu(_dot(h, wgd[...]))
    for t in range(n_prev - 1):
        pn_ref[t] = prev_ref[t + 1]
    pn_ref[n_prev - 1] = u


def _odd_front_sample(h2d, w_in, pw, ps, prev_t):
    rows = h2d.shape[0]
    n_prev = prev_t.shape[0]
    nj = D_MODEL // TN

    def wspec(s):
        return pl.BlockSpec((D_MODEL, TN), lambda j, s=s: (0, s * nj + j))

    col = pl.BlockSpec((rows, TN), lambda j: (0, j))
    st = pl.BlockSpec((n_prev, rows, TN), lambda j: (0, 0, j))
    colf = jax.ShapeDtypeStruct((rows, D_MODEL), F32)
    return pl.pallas_call(
        _odd_front_sample_kernel,
        grid=(nj,),
        in_specs=[pl.BlockSpec((rows, D_MODEL), lambda j: (0, 0))] + [wspec(s) for s in range(6)] + [
            pl.BlockSpec((None, TN, TN), lambda j: (j, 0, 0)), pl.BlockSpec((1, TN), lambda j: (0, j)), st],
        out_specs=[col, col, col, col, col, st],
        out_shape=[jax.ShapeDtypeStruct((rows, D_MODEL), BF16), colf, colf, colf, colf,
                   jax.ShapeDtypeStruct((n_prev, rows, D_MODEL), F32)],
        compiler_params=_cparams(("arbitrary",)),
        name="odd_front_sample",
    )(h2d, *([w_in] * 6), pw, ps.reshape(1, D_MODEL), prev_t)


RET_SAMPLE_ROWS = 8


def _ret_sample_kernel(q_ref, k_ref, v_ref, sgd_ref, st_ref, yd_ref, sn_ref):
    row = lax.broadcasted_iota(jnp.int32, (DK_D, DK_D), 0)
    col = lax.broadcasted_iota(jnp.int32, (DK_D, DK_D), 1)
    qs, ks, vs, gs = q_ref[...], k_ref[...], v_ref[...], sgd_ref[...]
    out_rows = []
    for i in range(RET_SAMPLE_ROWS):
        out_heads = []
        for h in range(H_D):
            g = float(np.exp(RET_LOG_G[h]))
            cols = slice(h * DK_D, (h + 1) * DK_D)
            q, k, v = qs[i:i + 1, cols], ks[i:i + 1, cols], vs[i:i + 1, cols]
            state = st_ref[i, h]
            qk = jnp.sum(q * k, axis=-1, keepdims=True)
            q8 = jnp.broadcast_to(q, (8, DK_D)).astype(BF16)
            cross = _dot(q8, state.astype(BF16))[0:1, :]
            o = qk * v + g * cross
            k_diag = jnp.where(row == col, jnp.broadcast_to(k, (DK_D, DK_D)), 0.0).astype(BF16)
            v_rows = jnp.broadcast_to(v, (DK_D, DK_D)).astype(BF16)
            sn_ref[i, h] = g * state + _dot(k_diag, v_rows)
            out_heads.append(_rms(o) * gs[i:i + 1, cols])
        out_rows.append(jnp.concatenate(out_heads, axis=1))
    yd_ref[...] = jnp.concatenate(out_rows, axis=0)


def _ret_sample(q, k, v, sgd, state):
    rows = q.shape[0]
    rs = RET_SAMPLE_ROWS
    row_spec = pl.BlockSpec((rs, D_MODEL), lambda i: (i, 0))
    st_spec = pl.BlockSpec((rs, H_D, DK_D, DK_D), lambda i: (i, 0, 0, 0))
    return pl.pallas_call(
        _ret_sample_kernel,
        grid=(rows // rs,),
        in_specs=[row_spec, row_spec, row_spec, row_spec, st_spec],
        out_specs=[row_spec, st_spec],
        out_shape=[jax.ShapeDtypeStruct((rows, D_MODEL), F32),
                   jax.ShapeDtypeStruct((rows, H_D, DK_D, DK_D), F32)],
        compiler_params=_cparams(("arbitrary",)),
        name="ret_sample",
    )(q, k, v, sgd, state)


def kernel(x_prompt, x_sample, c_prompt, c_sample, state_conv, cache_win_k, cache_win_v, state_pool, state_ret,
           norm_e, ada_w_e, ada_b_e, w_in_e, conv_w, conv_b, w_out_e, norm_o, ada_w_o, ada_b_o, w_in_o,
           pool_w, pool_scale, w_out_o, norm_f):
    batch, seq, d = x_prompt.shape
    sb = x_sample.shape[0]
    assert d == D_MODEL and x_sample.shape[1] == 1
    assert norm_e.shape[0] == 1 and norm_o.shape[0] == 1
    wb = cache_win_k.shape[2]
    keep = min(wb, seq)
    assert wb == DILATIONS[-1] * NK and seq % (DILATIONS[-1] * NK) == 0
    tm = TM_OUT

    n_c = batch + sb
    pad = (-n_c) % 8
    c_all = jnp.concatenate([c_prompt, c_sample, jnp.zeros((pad, d), F32)], axis=0)
    mod_e, mod_o = _adaln(c_all, ada_w_e[0], ada_b_e[0], ada_w_o[0], ada_b_o[0])
    mod_e_p, mod_e_s = mod_e[:batch], mod_e[batch:n_c]
    mod_o_p, mod_o_s = mod_o[:batch], mod_o[batch:n_c]

    w_in_e16 = w_in_e[0].astype(BF16)
    w_out_e16 = w_out_e[0].astype(BF16)
    w_in_o16 = w_in_o[0].astype(BF16)
    w_out_o16 = w_out_o[0].astype(BF16)
    pool_w16 = pool_w[0].astype(BF16)

    xp2d = x_prompt.reshape(batch * seq, d)
    ya, q, k, v, sgb, k_new, v_new, conv_tail = _even_front_prompt(
        xp2d, mod_e_p, norm_e[0], w_in_e16, conv_w[0], conv_b[0], batch, seq, keep, TM_FRONT)
    yb = _attn_prompt(q, k, v, sgb, batch, seq).reshape(batch * seq, d)
    x1, h1 = _outproj(ya, yb, xp2d, mod_e_p, w_out_e16, norm_o[0], mod_o_p, tm, seq)
    yc, rq, rk, rv, sgd, pool_tail = _odd_front_prompt(h1, w_in_o16, pool_w16, pool_scale[0], batch, seq, TM_FRONT)
    yd, ret_p = _ret_prompt(rq, rk, rv, sgd, batch, seq)
    y_prompt = _outproj(yc, yd.reshape(batch * seq, d), x1, mod_o_p, w_out_o16, norm_f, None, tm, seq)

    xs2d = x_sample.reshape(sb, d)
    conv_prev_t = jnp.transpose(state_conv[0], (1, 0, 2))
    ya_s, q_s, k_s, v_s, sgb_s, conv_s_t = _even_front_sample(
        xs2d, mod_e_s, norm_e[0], w_in_e16, conv_w[0], conv_b[0], conv_prev_t)
    cache_kt = jnp.transpose(cache_win_k[0], (0, 2, 3, 1)).reshape(sb, d, wb)
    cache_vt = jnp.transpose(cache_win_v[0], (0, 2, 3, 1)).reshape(sb, d, wb)
    yb_s = _attn_sample(q_s, k_s, v_s, sgb_s, cache_kt, cache_vt)
    x1_s, h1_s = _outproj(ya_s, yb_s, xs2d, mod_e_s, w_out_e16, norm_o[0], mod_o_s, sb, 1)
    pool_prev_t = jnp.transpose(state_pool[0], (1, 0, 2))
    yc_s, rq_s, rk_s, rv_s, sgd_s, pool_s_t = _odd_front_sample(h1_s, w_in_o16, pool_w16, pool_scale[0], pool_prev_t)
    yd_s, ret_s = _ret_sample(rq_s, rk_s, rv_s, sgd_s, state_ret[0])
    y_sample = _outproj(yc_s, yd_s, x1_s, mod_o_s, w_out_o16, norm_f, None, sb, 1)

    return (
        y_prompt.reshape(batch, seq, d),
        y_sample.reshape(sb, 1, d),
        conv_tail[:, 6:8][None],
        jnp.transpose(conv_s_t, (1, 0, 2))[None],
        jnp.transpose(k_new.reshape(batch, H_B, HD_B, keep), (0, 3, 1, 2))[None],
        k_s.reshape(1, sb, 1, H_B, HD_B),
        jnp.transpose(v_new.reshape(batch, H_B, HD_B, keep), (0, 3, 1, 2))[None],
        v_s.reshape(1, sb, 1, H_B, HD_B),
        pool_tail[:, 1:][None],
        jnp.transpose(pool_s_t, (1, 0, 2))[None],
        ret_p[None],
        ret_s[None],
    )
```

```python
import functools

import numpy as np
import jax
import jax.numpy as jnp
from jax import lax
from jax.experimental import pallas as pl
from jax.experimental.pallas import tpu as pltpu

F32 = jnp.float32
BF16 = jnp.bfloat16
WORD = jnp.uint32

D_MODEL = 1024
EPS = 1e-6
H_B = 16
HD_B = 64
N_PAIR = H_B // 2
LANES = 128
NK = 128
DILATIONS = (1, 4, 16)
POOL_SIZES = (2, 4, 8, 16)
POOL_PREV = 16
H_D = 4
DK_D = 256
RET_CHUNK = 256
TN = 256
TM_FRONT = 1024
TM_OUT = 1024
MASK_DIST = 1e9
LOG2E = float(np.log2(np.e))
MASK_ADD = 1e30
VMEM_LIMIT = 56 * 1024 * 1024

RET_LOG_G = [float(np.log(1.0 - 2.0 ** (-5.0 - h))) for h in range(H_D)]


def _cparams(sem):
    return pltpu.CompilerParams(dimension_semantics=sem, vmem_limit_bytes=VMEM_LIMIT)


def _silu(x):
    return x * (1.0 / (1.0 + jnp.exp(-x)))


def _dot(a, b):
    return jnp.dot(a, b, preferred_element_type=F32)


def _dot_nt(a, b):
    return lax.dot_general(a, b, (((1,), (1,)), ((), ())), preferred_element_type=F32)


def _rms(x):
    return x * lax.rsqrt(jnp.mean(x * x, axis=-1, keepdims=True) + EPS)


def _adaln_kernel(c_ref, we_ref, be_ref, wo_ref, bo_ref, me_ref, mo_ref):
    sc = _silu(c_ref[...]).astype(BF16)
    me_ref[...] = _dot(sc, we_ref[...].astype(BF16)) + be_ref[...]
    mo_ref[...] = _dot(sc, wo_ref[...].astype(BF16)) + bo_ref[...]


def _adaln(c_all, we, be, wo, bo):
    rows = c_all.shape[0]
    tn = 512
    n = 3 * D_MODEL
    wspec = pl.BlockSpec((D_MODEL, tn), lambda j: (0, j))
    bspec = pl.BlockSpec((1, tn), lambda j: (0, j))
    ospec = pl.BlockSpec((rows, tn), lambda j: (0, j))
    return pl.pallas_call(
        _adaln_kernel,
        grid=(n // tn,),
        in_specs=[pl.BlockSpec((rows, D_MODEL), lambda j: (0, 0)), wspec, bspec, wspec, bspec],
        out_specs=[ospec, ospec],
        out_shape=[jax.ShapeDtypeStruct((rows, n), F32)] * 2,
        compiler_params=_cparams(("arbitrary",)),
        name="adaln",
    )(c_all, we, be.reshape(1, n), wo, bo.reshape(1, n))


def _norm_mod(x, g, mod):
    shift = mod[:, 0:D_MODEL]
    scale = mod[:, D_MODEL:2 * D_MODEL]
    return _rms(x) * g * (1.0 + scale) + shift


def _shift_rows(u, k, prev_rows):
    row = lax.broadcasted_iota(jnp.int32, u.shape, 0)
    out = pltpu.roll(u, k, 0)
    for idx, pr in enumerate(prev_rows):
        out = jnp.where(row == idx, pr, out)
    return out


N_SEG_E = 8
PHASES_E = 8


def _phase_segments(t, n_seg, n_phases):
    return range(t * n_seg // n_phases, (t + 1) * n_seg // n_phases)


def _even_front_kernel(x_ref, mod_ref, g_ref, *refs, tiles_per_batch, n_chunks):
    ws = refs[:N_SEG_E]
    cw_ref, cb_ref = refs[N_SEG_E:N_SEG_E + 2]
    ya_ref, q_ref, k_ref, v_ref, sgb_ref, kn_ref, vn_ref, cn_ref = refs[N_SEG_E + 2:N_SEG_E + 10]
    h_scr, carry_scr, z_a, z_b = refs[N_SEG_E + 10:]
    nj = D_MODEL // TN
    s = pl.program_id(0)
    c_out = jnp.maximum(s - 1, 0)
    j_out = c_out % nj
    il_out = (c_out // nj) % tiles_per_batch

    @pl.when(s == 0)
    def _():
        z_b[...] = jnp.zeros_like(z_b)
        carry_scr[...] = jnp.zeros_like(carry_scr)

    @pl.when((s < n_chunks) & (s % nj == 0))
    def _():
        h_scr[...] = _norm_mod(x_ref[...], g_ref[...], mod_ref[...]).astype(BF16)

    tm = h_scr.shape[0]
    rb = tm // PHASES_E

    def phase(t, z_w, z_r):
        for seg in _phase_segments(t, N_SEG_E, PHASES_E):
            z_w[seg] = _dot(h_scr[...], ws[seg][...])
        rows = slice(t * rb, (t + 1) * rb)
        bg, cg, xv, ga, q, k, v, gb = (z_r[seg, rows, :] for seg in range(N_SEG_E))
        u = cg * xv
        prev = carry_scr[j_out]
        if t == 0:
            prev = jnp.where(il_out == 0, 0.0, prev)
        p2, p1 = prev[6:7, :], prev[7:8, :]
        u1 = _shift_rows(u, 1, [p1])
        u2 = _shift_rows(u, 2, [p2, p1])
        cw = cw_ref[...]
        conv = cb_ref[...] + cw[0:1, :] * u2 + cw[1:2, :] * u1 + cw[2:3, :] * u
        tail = u[rb - 8:rb, :]
        carry_scr[j_out] = tail
        if t == PHASES_E - 1:
            cn_ref[...] = tail
        ya_ref[rows, :] = (bg * conv * _silu(ga)).astype(BF16)
        kn_ref[:, rows] = k.T
        vn_ref[:, rows] = v.T
        wrows = slice(t * rb // 2, (t + 1) * rb // 2)
        for ref, val in ((q_ref, q * (HD_B ** -0.5 * LOG2E)),
                         (k_ref, k), (v_ref, v)):
            words = pltpu.bitcast(val.astype(BF16), WORD)
            for e in range(TN // LANES):
                ref[e, wrows, :] = words[:, e * LANES:(e + 1) * LANES]
        sgb = _silu(gb).astype(BF16)
        for e in range(TN // LANES):
            sgb_ref[e, rows, :] = sgb[:, e * LANES:(e + 1) * LANES]

    for parity, (z_w, z_r) in enumerate(((z_a, z_b), (z_b, z_a))):
        for t in range(PHASES_E):
            pl.when(s % 2 == parity)(functools.partial(phase, t, z_w, z_r))


def _even_front_prompt(x2d, mod, g, w_segs, cw, cb, batch, seq, keep, tm):
    rows = batch * seq
    tpb = seq // tm
    nj = D_MODEL // TN
    n_chunks = (rows // tm) * nj
    off = (seq - keep) // tm
    ppc = TN // LANES

    def c_in(s):
        return jnp.minimum(s, n_chunks - 1)

    def c_out(s):
        return jnp.maximum(s - 1, 0)

    def out_ij(s):
        c = c_out(s)
        return c // nj, c % nj

    def pair_map(s):
        i, j = out_ij(s)
        return (i // tpb, j, i % tpb, 0)

    def keep_map(s):
        i, j = out_ij(s)
        il = i % tpb
        kept = il >= off
        return (i // tpb, jnp.where(kept, j, 0), jnp.where(kept, il - off, 0))

    def tail_map(s):
        i, j = out_ij(s)
        return (i // tpb, 0, jnp.where(i % tpb == tpb - 1, j, 0))

    pair_spec = pl.BlockSpec((None, ppc, tm, LANES), pair_map)
    pair_shape = jax.ShapeDtypeStruct((batch, N_PAIR, seq, LANES), BF16)
    word_spec = pl.BlockSpec((None, ppc, tm // 2, LANES), pair_map)
    word_shape = jax.ShapeDtypeStruct((batch, N_PAIR, seq // 2, LANES), WORD)
    keep_spec = pl.BlockSpec((None, TN, tm), keep_map)
    keep_shape = jax.ShapeDtypeStruct((batch, D_MODEL, keep), F32)
    return pl.pallas_call(
        functools.partial(_even_front_kernel, tiles_per_batch=tpb, n_chunks=n_chunks),
        grid=(n_chunks + 1,),
        in_specs=[
            pl.BlockSpec((tm, D_MODEL), lambda s: (c_in(s) // nj, 0)),
            pl.BlockSpec((None, 1, 3 * D_MODEL), lambda s: (c_in(s) // nj // tpb, 0, 0)),
            pl.BlockSpec((1, D_MODEL), lambda s: (0, 0)),
        ] + [pl.BlockSpec((D_MODEL, TN), lambda s: (0, c_in(s) % nj))] * N_SEG_E + [
            pl.BlockSpec((3, TN), lambda s: (0, c_out(s) % nj)),
            pl.BlockSpec((1, TN), lambda s: (0, c_out(s) % nj)),
        ],
        out_specs=[
            pl.BlockSpec((tm, TN), lambda s: out_ij(s)),
            word_spec, word_spec, word_spec, pair_spec,
            keep_spec, keep_spec,
            pl.BlockSpec((None, 8, TN), tail_map),
        ],
        out_shape=[
            jax.ShapeDtypeStruct((rows, D_MODEL), BF16),
            word_shape, word_shape, word_shape, pair_shape,
            keep_shape, keep_shape,
            jax.ShapeDtypeStruct((batch, 8, D_MODEL), F32),
        ],
        scratch_shapes=[pltpu.VMEM((tm, D_MODEL), BF16), pltpu.VMEM((nj, 8, TN), F32),
                        pltpu.VMEM((N_SEG_E, tm, TN), F32), pltpu.VMEM((N_SEG_E, tm, TN), F32)],
        compiler_params=_cparams(("arbitrary",)),
        name="even_front_prompt",
    )(x2d, mod.reshape(batch, 1, 3 * D_MODEL), g.reshape(1, D_MODEL), *w_segs, cw, cb.reshape(1, D_MODEL))


STAT_PITCH = {1: 1, 4: 4, 16: 24}
ATTN_UNROLL = 32


def _residue_stream(src, r):
    n_out = src.shape[0] // 4
    halves = [pltpu.unpack_elementwise(src[pl.ds(r // 2 + off, n_out, stride=4), :], index=r % 2,
                                       packed_dtype=BF16, unpacked_dtype=F32) for off in (0, 2)]
    return pltpu.pack_elementwise(halves, packed_dtype=BF16)


def _attn_prompt_kernel(slope_ref, dist_ref, q_ref, k_ref, v_ref, sgb_ref, yb_ref,
                        q4_scr, q16_scr, kv4_scr, kv16_scr, acc_scr, m_scr, l_scr, bias_scr, *, seq):
    hp = pl.program_id(1)
    lane = lax.broadcasted_iota(jnp.int32, (1, LANES), 1)
    half0 = lane < HD_B
    dist = dist_ref[...]
    masked = jnp.full((NK, NK), MASK_DIST, F32)
    for p, d in enumerate(DILATIONS):
        for e in range(2):
            bias = dist * (slope_ref[2 * hp + e] * (float(d) * LOG2E))
            bias_scr[2 * p + e] = bias
            bias_scr[6 + 2 * p + e] = jnp.concatenate([bias[:, NK:], masked], axis=1)

    for a, ref in enumerate((q_ref, k_ref, v_ref)):
        for r in range(4):
            s = _residue_stream(ref, r)
            if a == 0:
                q4_scr[r] = s
            else:
                kv4_scr[a - 1, r] = s
        for r in range(16):
            s = _residue_stream(q4_scr.at[r % 4] if a == 0 else kv4_scr.at[a - 1, r % 4], r // 4)
            if a == 0:
                q16_scr[r] = s
            else:
                kv16_scr[a - 1, r] = s

    ones = jnp.ones((2 * NK, LANES), BF16)
    hw = NK // 2

    def tile(p, d, r, n):
        qw = pl.ds(pl.multiple_of(n * hw, hw), hw)
        kw = pl.ds(pl.multiple_of(jnp.maximum(n - 1, 0) * hw, hw), 2 * hw)
        if d == 1:
            qs, ks, vs = q_ref, k_ref, v_ref
        elif d == 4:
            qs, ks, vs = q4_scr.at[r], kv4_scr.at[0, r], kv4_scr.at[1, r]
        else:
            qs, ks, vs = q16_scr.at[r], kv16_scr.at[0, r], kv16_scr.at[1, r]
        q = pltpu.bitcast(qs[qw, :], BF16)
        kk = pltpu.bitcast(ks[kw, :], BF16)
        vv1 = jnp.concatenate([pltpu.bitcast(vs[kw, :], BF16), ones], axis=1)
        first = jnp.where(n == 0, 1, 0)
        res = []
        for e in range(2):
            qe = jnp.where(half0 if e == 0 else jnp.logical_not(half0), q, jnp.zeros_like(q))
            s = _dot_nt(qe, kk) - bias_scr[6 * first + 2 * p + e]
            m = jnp.max(s, axis=-1, keepdims=True)
            pe = jnp.exp2(s - m).astype(BF16)
            res.append((_dot(pe, vv1), m))
        (a0, m0), (a1, m1) = res
        acc = jnp.where(half0, a0[:, :LANES], a1[:, :LANES])
        l = jnp.where(half0, a0[:, LANES:], a1[:, LANES:])
        m = jnp.where(half0, m0, m1)
        if d > 1:
            pitch = STAT_PITCH[d]
            rows = pl.ds(r + pitch * NK * n, NK, stride=pitch)
            acc_scr[p - 1, rows, :] = acc
            l_scr[p - 1, rows, :] = l
            m_scr[p - 1, rows, :] = m
            return
        accs, ls, ms = [acc], [l], [m]
        for pp in range(1, len(DILATIONS)):
            dd = DILATIONS[pp]
            for ref, vals in ((acc_scr, accs), (l_scr, ls), (m_scr, ms)):
                vals.append(jnp.concatenate(
                    [ref[pp - 1, pl.ds(pl.multiple_of((n * (NK // dd) + g) * STAT_PITCH[dd], 8), dd), :]
                     for g in range(NK // dd)], axis=0) if STAT_PITCH[dd] != dd
                    else ref[pp - 1, pl.ds(pl.multiple_of(n * NK, NK), NK), :])
        top = jnp.maximum(jnp.maximum(ms[0], ms[1]), ms[2])
        ws = [jnp.exp2(x - top) for x in ms]
        num = ws[0] * accs[0] + ws[1] * accs[1] + ws[2] * accs[2]
        den = ws[0] * ls[0] + ws[1] * ls[1] + ws[2] * ls[2]
        rows = pl.ds(pl.multiple_of(n * NK, NK), NK)
        yb_ref[rows, :] = (num * (1.0 / den) * sgb_ref[rows, :].astype(F32)).astype(BF16)

    for p, d in reversed(list(enumerate(DILATIONS))):
        def body(it, c, p=p, d=d):
            for u in range(ATTN_UNROLL):
                t = it * ATTN_UNROLL + u
                if d == 1:
                    tile(p, d, 0, t)
                else:
                    tile(p, d, t % d, t // d)
            return c

        lax.fori_loop(0, seq // NK // ATTN_UNROLL, body, 0)


def _band_distance():
    qi = np.arange(NK)[:, None]
    kj = np.arange(2 * NK)[None, :]
    dist = NK + qi - kj
    return jnp.asarray(np.where((dist >= 0) & (dist <= NK), dist, MASK_DIST), dtype=F32)


def _alibi_slopes():
    return jnp.asarray(2.0 ** (-8.0 * np.arange(1, H_B + 1) / H_B), dtype=F32)


def _attn_prompt(q, k, v, sgb, batch, seq):
    word_spec = pl.BlockSpec((None, None, seq // 2, LANES), lambda b, hp: (b, hp, 0, 0))
    stat_rows = max(seq // d * STAT_PITCH[d] for d in DILATIONS)
    stat = pltpu.VMEM((len(DILATIONS) - 1, stat_rows, LANES), F32)
    return pl.pallas_call(
        functools.partial(_attn_prompt_kernel, seq=seq),
        grid=(batch, N_PAIR),
        in_specs=[pl.BlockSpec(memory_space=pltpu.SMEM),
                  pl.BlockSpec((NK, 2 * NK), lambda b, hp: (0, 0)),
                  word_spec, word_spec, word_spec,
                  pl.BlockSpec((None, None, seq, LANES), lambda b, hp: (b, hp, 0, 0))],
        out_specs=pl.BlockSpec((None, seq, LANES), lambda b, hp: (b, 0, hp)),
        out_shape=jax.ShapeDtypeStruct((batch, seq, D_MODEL), BF16),
        scratch_shapes=[pltpu.VMEM((4, seq // 8, LANES), WORD),
                        pltpu.VMEM((16, seq // 32, LANES), WORD),
                        pltpu.VMEM((2, 4, seq // 8, LANES), WORD),
                        pltpu.VMEM((2, 16, seq // 32, LANES), WORD),
                        stat, stat, stat,
                        pltpu.VMEM((12, NK, 2 * NK), F32)],
        compiler_params=_cparams(("arbitrary", "arbitrary")),
        name="attn_prompt",
    )(_alibi_slopes(), _band_distance(), q, k, v, sgb)


def _outproj_mid_kernel(ya_ref, yb_ref, x_ref, mod_ref, w1_ref, w2_ref, g_ref, mod2_ref, x1_ref, h_ref):
    gate = mod_ref[:, 2 * D_MODEL:3 * D_MODEL]
    x1 = x_ref[...] + gate * (_dot(ya_ref[...], w1_ref[...]) + _dot(yb_ref[...].astype(BF16), w2_ref[...]))
    x1_ref[...] = x1
    h_ref[...] = _norm_mod(x1, g_ref[...], mod2_ref[...]).astype(BF16)


def _outproj_final_kernel(ya_ref, yb_ref, x_ref, mod_ref, w1_ref, w2_ref, g_ref, y_ref):
    gate = mod_ref[:, 2 * D_MODEL:3 * D_MODEL]
    x1 = x_ref[...] + gate * (_dot(ya_ref[...], w1_ref[...]) + _dot(yb_ref[...].astype(BF16), w2_ref[...]))
    y_ref[...] = _rms(x1) * g_ref[...]


def _outproj(ya, yb, x2d, mod, w_out, g, mod2, tm, rows_per_mod):
    rows = x2d.shape[0]
    half = w_out.shape[0] // 2
    row_spec = pl.BlockSpec((tm, D_MODEL), lambda i: (i, 0))
    if rows_per_mod == 1:
        mod_spec = pl.BlockSpec((tm, 3 * D_MODEL), lambda i: (i, 0))
        mods = (mod, mod2)
    else:
        tpb = rows_per_mod // tm
        mod_spec = pl.BlockSpec((None, 1, 3 * D_MODEL), lambda i: (i // tpb, 0, 0))
        mods = tuple(None if m is None else m.reshape(-1, 1, 3 * D_MODEL) for m in (mod, mod2))
    w1_spec = pl.BlockSpec((half, D_MODEL), lambda i: (0, 0))
    w2_spec = pl.BlockSpec((half, D_MODEL), lambda i: (1, 0))
    g_spec = pl.BlockSpec((1, D_MODEL), lambda i: (0, 0))
    common = dict(grid=(rows // tm,), compiler_params=_cparams(("arbitrary",)))
    if mod2 is None:
        return pl.pallas_call(
            _outproj_final_kernel,
            in_specs=[row_spec, row_spec, row_spec, mod_spec, w1_spec, w2_spec, g_spec],
            out_specs=row_spec,
            out_shape=jax.ShapeDtypeStruct((rows, D_MODEL), F32),
            name="outproj_final", **common,
        )(ya, yb, x2d, mods[0], w_out, w_out, g.reshape(1, D_MODEL))
    return pl.pallas_call(
        _outproj_mid_kernel,
        in_specs=[row_spec, row_spec, row_spec, mod_spec, w1_spec, w2_spec, g_spec, mod_spec],
        out_specs=[row_spec, row_spec],
        out_shape=[jax.ShapeDtypeStruct((rows, D_MODEL), F32), jax.ShapeDtypeStruct((rows, D_MODEL), BF16)],
        name="outproj_mid", **common,
    )(ya, yb, x2d, mods[0], w_out, w_out, g.reshape(1, D_MODEL), mods[1])


def _pick_by_chunk(j, vals):
    out = vals[-1]
    for idx in range(len(vals) - 2, -1, -1):
        out = jnp.where(j == idx, vals[idx], out)
    return out


N_SEG_O = 6
PHASES_O = 4


def _odd_front_kernel(h_ref, *refs, tiles_per_batch):
    ws = refs[:N_SEG_O]
    pw_ref, ps_ref = refs[N_SEG_O:N_SEG_O + 2]
    yc_ref, q_ref, kt_ref, v_ref, sgd_ref, pn_ref = refs[N_SEG_O + 2:N_SEG_O + 8]
    carry_scr, z_a, z_b = refs[N_SEG_O + 8:]
    nj = D_MODEL // TN
    s = pl.program_id(0)
    c_out = jnp.maximum(s - 1, 0)
    j_out = c_out % nj
    il_out = (c_out // nj) % tiles_per_batch
    tm = h_ref.shape[0]
    rb = tm // PHASES_O

    @pl.when(s == 0)
    def _():
        z_b[...] = jnp.zeros_like(z_b)
        carry_scr[...] = jnp.zeros_like(carry_scr)

    def phase(t, z_w, z_r):
        for seg in _phase_segments(t, N_SEG_O, PHASES_O):
            z_w[seg] = _dot(h_ref[...], ws[seg][...])
        rows = slice(t * rb, (t + 1) * rb)
        u, gc, q, k, v, gd = (z_r[seg, rows, :] for seg in range(N_SEG_O))
        prev = carry_scr[j_out]
        if t == 0:
            prev = jnp.where(il_out == 0, 0.0, prev)
        tail = u[rb - POOL_PREV:rb, :]
        carry_scr[j_out] = tail
        if t == PHASES_O - 1:
            pn_ref[...] = tail
        ext = jnp.concatenate([prev, u], axis=0)
        sums = []
        acc = ext
        for sh in (1, 2, 4, 8):
            acc = acc + pltpu.roll(acc, sh, 0)
            sums.append(acc[POOL_PREV:, :])
        win = _pick_by_chunk(j_out, sums)
        width = _pick_by_chunk(j_out, [float(w) for w in POOL_SIZES])
        pos = (il_out * tm + t * rb + lax.broadcasted_iota(jnp.int32, (rb, 1), 0)).astype(F32)
        pooled = win / jnp.minimum(width, pos + 1.0) - u
        mixed = _dot(pooled.astype(BF16), pw_ref[...]) * ps_ref[...]
        yc_ref[rows, :] = (mixed * _silu(gc)).astype(BF16)
        q_ref[rows, :] = q.astype(BF16)
        kt_ref[:, rows] = (k * (DK_D ** -0.5)).T.astype(BF16)
        v_ref[rows, :] = v.astype(BF16)
        sgd_ref[rows, :] = _silu(gd).astype(BF16)

    for parity, (z_w, z_r) in enumerate(((z_a, z_b), (z_b, z_a))):
        for t in range(PHASES_O):
            pl.when(s % 2 == parity)(functools.partial(phase, t, z_w, z_r))


def _odd_front_prompt(h2d, w_segs, pw, ps, batch, seq, tm):
    rows = batch * seq
    tpb = seq // tm
    nj = D_MODEL // TN
    n_chunks = (rows // tm) * nj

    def c_in(s):
        return jnp.minimum(s, n_chunks - 1)

    def out_ij(s):
        c = jnp.maximum(s - 1, 0)
        return c // nj, c % nj

    def head_map(s):
        i, j = out_ij(s)
        return (i // tpb, j, i % tpb, 0)

    def kt_map(s):
        i, j = out_ij(s)
        return (i // tpb, j, 0, i % tpb)

    def tail_map(s):
        i, j = out_ij(s)
        return (i // tpb, 0, jnp.where(i % tpb == tpb - 1, j, 0))

    head_spec = pl.BlockSpec((None, None, tm, TN), head_map)
    head_shape = jax.ShapeDtypeStruct((batch, H_D, seq, DK_D), BF16)
    tok_spec = pl.BlockSpec((tm, TN), out_ij)
    tok_shape = jax.ShapeDtypeStruct((rows, D_MODEL), BF16)
    zbuf = pltpu.VMEM((N_SEG_O, tm, TN), F32)
    return pl.pallas_call(
        functools.partial(_odd_front_kernel, tiles_per_batch=tpb),
        grid=(n_chunks + 1,),
        in_specs=[pl.BlockSpec((tm, D_MODEL), lambda s: (c_in(s) // nj, 0))] + [
                  pl.BlockSpec((D_MODEL, TN), lambda s: (0, c_in(s) % nj))] * N_SEG_O + [
                  pl.BlockSpec((None, TN, TN), lambda s: (out_ij(s)[1], 0, 0)),
                  pl.BlockSpec((1, TN), lambda s: (0, out_ij(s)[1]))],
        out_specs=[tok_spec, head_spec, pl.BlockSpec((None, None, TN, tm), kt_map), head_spec, tok_spec,
                   pl.BlockSpec((None, POOL_PREV, TN), tail_map)],
        out_shape=[tok_shape, head_shape, jax.ShapeDtypeStruct((batch, H_D, DK_D, seq), BF16),
                   head_shape, tok_shape,
                   jax.ShapeDtypeStruct((batch, POOL_PREV, D_MODEL), F32)],
        scratch_shapes=[pltpu.VMEM((nj, POOL_PREV, TN), F32), zbuf, zbuf],
        compiler_params=_cparams(("arbitrary",)),
        name="odd_front_prompt",
    )(h2d, *w_segs, pw, ps.reshape(1, D_MODEL))


RET_SBLK = 1024


def _ret_prompt_kernel(q_ref, kt_ref, v_ref, sgd_ref, dec_ref, cdec_ref, kdec_ref,
                       yd_ref, st_ref, state_scr, *, sblk, n_sblk):
    sb = pl.program_id(1)

    @pl.when(sb == 0)
    def _():
        state_scr[...] = jnp.zeros_like(state_scr)

    def chunk(c, carry):
        rows = pl.ds(pl.multiple_of(c * RET_CHUNK, RET_CHUNK), RET_CHUNK)
        for h in range(H_D):
            cols = slice(h * DK_D, (h + 1) * DK_D)
            q = q_ref[h, rows, :]
            kt = kt_ref[h, :, rows]
            v = v_ref[h, rows, :]
            state = state_scr[h]
            scores = _dot(q, kt) * dec_ref[h]
            o = _dot(scores.astype(BF16), v) + _dot(q, state.astype(BF16)) * cdec_ref[h]
            kd = (kt.astype(F32) * kdec_ref[h]).astype(BF16)
            state_scr[h] = float(np.exp(RET_CHUNK * RET_LOG_G[h])) * state + _dot(kd, v)
            yd_ref[rows, cols] = (_rms(o) * sgd_ref[rows, cols].astype(F32)).astype(BF16)
        return carry

    lax.fori_loop(0, sblk // RET_CHUNK, chunk, 0, unroll=4)

    @pl.when(sb == n_sblk - 1)
    def _():
        st_ref[...] = state_scr[...]


def _ret_consts():
    t = np.arange(RET_CHUNK, dtype=np.float64)
    diff = t[:, None] - t[None, :]
    lg = np.asarray(RET_LOG_G)[:, None, None]
    dec = np.where(diff >= 0, np.exp(np.maximum(diff, 0.0)[None] * lg), 0.0)
    cdec = np.broadcast_to(np.exp((t + 1.0)[None, :, None] * lg), (H_D, RET_CHUNK, DK_D))
    kdec = np.broadcast_to(np.exp((RET_CHUNK - 1.0 - t)[None, None, :] * lg), (H_D, DK_D, RET_CHUNK))
    return jnp.asarray(dec, F32), jnp.asarray(cdec, F32), jnp.asarray(kdec, F32)


def _ret_prompt(q, kt, v, sgd, batch, seq):
    dec, cdec, kdec = _ret_consts()
    n_sblk = seq // RET_SBLK
    head_spec = pl.BlockSpec((None, H_D, RET_SBLK, DK_D), lambda b, s: (b, 0, s, 0))
    tok_spec = pl.BlockSpec((None, RET_SBLK, D_MODEL), lambda b, s: (b, s, 0))

    def const_spec(shape):
        return pl.BlockSpec(shape, lambda b, s: (0, 0, 0))

    return pl.pallas_call(
        functools.partial(_ret_prompt_kernel, sblk=RET_SBLK, n_sblk=n_sblk),
        grid=(batch, n_sblk),
        in_specs=[head_spec, pl.BlockSpec((None, H_D, DK_D, RET_SBLK), lambda b, s: (b, 0, 0, s)), head_spec,
                  tok_spec, const_spec(dec.shape), const_spec(cdec.shape), const_spec(kdec.shape)],
        out_specs=[tok_spec, pl.BlockSpec((None, H_D, DK_D, DK_D), lambda b, s: (b, 0, 0, 0))],
        out_shape=[jax.ShapeDtypeStruct((batch, seq, D_MODEL), BF16),
                   jax.ShapeDtypeStruct((batch, H_D, DK_D, DK_D), F32)],
        scratch_shapes=[pltpu.VMEM((H_D, DK_D, DK_D), F32)],
        compiler_params=_cparams(("arbitrary", "arbitrary")),
        name="ret_prompt",
    )(q, kt, v, sgd.reshape(batch, seq, D_MODEL), dec, cdec, kdec)


def _even_front_sample_kernel(x_ref, mod_ref, g_ref, *refs):
    ws32 = refs[:N_SEG_E]
    cw_ref, cb_ref, prev_ref = refs[N_SEG_E:N_SEG_E + 3]
    ya_ref, q_ref, k_ref, v_ref, sgb_ref, cn_ref = refs[N_SEG_E + 3:N_SEG_E + 9]
    w16_refs = refs[N_SEG_E + 9:]
    ws = [w[...].astype(BF16) for w in ws32]
    for w16_ref, w in zip(w16_refs, ws):
        w16_ref[...] = w
    wbg, wcg, wxv, wga, wq, wk, wv, wgb = ws
    h = _norm_mod(x_ref[...], g_ref[...], mod_ref[...]).astype(BF16)
    u = _dot(h, wcg) * _dot(h, wxv)
    cw = cw_ref[...]
    conv = cb_ref[...] + cw[0:1, :] * prev_ref[0] + cw[1:2, :] * prev_ref[1] + cw[2:3, :] * u
    cn_ref[0] = prev_ref[1]
    cn_ref[1] = u
    ya_ref[...] = (_dot(h, wbg) * conv * _silu(_dot(h, wga))).astype(BF16)
    q_ref[...] = _dot(h, wq) * (HD_B ** -0.5)
    k_ref[...] = _dot(h, wk)
    v_ref[...] = _dot(h, wv)
    sgb_ref[...] = _silu(_dot(h, wgb))


def _even_front_sample(x2d, mod, g, w_in, cw, cb, prev_t):
    rows = x2d.shape[0]
    nj = D_MODEL // TN

    def wspec(s):
        return pl.BlockSpec((D_MODEL, TN), lambda j, s=s: (0, s * nj + j))

    full = pl.BlockSpec((rows, D_MODEL), lambda j: (0, 0))
    col = pl.BlockSpec((rows, TN), lambda j: (0, j))
    st = pl.BlockSpec((2, rows, TN), lambda j: (0, 0, j))
    colf = jax.ShapeDtypeStruct((rows, D_MODEL), F32)
    w16_spec = pl.BlockSpec((D_MODEL, TN), lambda j: (0, j))
    w16_shape = jax.ShapeDtypeStruct((D_MODEL, D_MODEL), BF16)
    outs = pl.pallas_call(
        _even_front_sample_kernel,
        grid=(nj,),
        in_specs=[full, pl.BlockSpec((rows, 3 * D_MODEL), lambda j: (0, 0)),
                  pl.BlockSpec((1, D_MODEL), lambda j: (0, 0))] + [wspec(s) for s in range(N_SEG_E)] + [
            pl.BlockSpec((3, TN), lambda j: (0, j)), pl.BlockSpec((1, TN), lambda j: (0, j)), st],
        out_specs=[col, col, col, col, col, st] + [w16_spec] * N_SEG_E,
        out_shape=[jax.ShapeDtypeStruct((rows, D_MODEL), BF16), colf, colf, colf, colf,
                   jax.ShapeDtypeStruct((2, rows, D_MODEL), F32)] + [w16_shape] * N_SEG_E,
        compiler_params=_cparams(("arbitrary",)),
        name="even_front_sample",
    )(x2d, mod, g.reshape(1, D_MODEL), *([w_in] * N_SEG_E), cw, cb.reshape(1, D_MODEL), prev_t)
    return outs[:6], outs[6:]


def _attn_sample_kernel(slope_ref, pos_ref, q_ref, kn_ref, vn_ref, sgb_ref, kt_ref, vt_ref, yb_ref):
    gw = q_ref.shape[-1]
    me = pl.ds(pl.program_id(1) % 8, 1)
    row = lax.broadcasted_iota(jnp.int32, (8, gw), 0)
    col = lax.broadcasted_iota(jnp.int32, (8, gw), 1)
    own = (col >= row * HD_B) & (col < (row + 1) * HD_B)
    qm = jnp.where(own, q_ref[me, :], 0.0)
    s_self = jnp.sum(qm * kn_ref[me, :], axis=-1, keepdims=True)
    s_all = _dot(qm.astype(BF16), kt_ref[...].astype(BF16))
    s_all = s_all - slope_ref[:, 0:1] * pos_ref[0:1, :]
    v_self = vn_ref[me, :]
    probs, stats = [], []
    for p in range(len(DILATIONS)):
        s = s_all - pos_ref[p + 1:p + 2, :]
        m = jnp.maximum(jnp.max(s, axis=-1, keepdims=True), s_self)
        pe = jnp.exp(s - m)
        p_self = jnp.exp(s_self - m)
        probs.append(pe)
        stats.append((m, jnp.sum(pe, axis=-1, keepdims=True) + p_self, p_self))
    o_all = _dot_nt(jnp.concatenate(probs, axis=0).astype(BF16), vt_ref[...].astype(BF16))
    outs, lses = [], []
    for p, (m, l, p_self) in enumerate(stats):
        outs.append((o_all[8 * p:8 * p + 8, :] + p_self * v_self) * (1.0 / l))
        lses.append(m + jnp.log(l))
    m = jnp.maximum(jnp.maximum(lses[0], lses[1]), lses[2])
    ws = [jnp.exp(x - m) for x in lses]
    o = (ws[0] * outs[0] + ws[1] * outs[1] + ws[2] * outs[2]) * (1.0 / (ws[0] + ws[1] + ws[2]))
    o = jnp.sum(jnp.where(own, o, 0.0), axis=0, keepdims=True)
    yb_ref[me, :] = o * sgb_ref[me, :]


def _attn_sample(q, kn, vn, sgb, cache_kt, cache_vt):
    rows, _, wb = cache_kt.shape
    gh = 8
    gw = gh * HD_B
    ng = H_B // gh
    back = wb - np.arange(wb, dtype=np.float64)
    pos = [back] + [np.where((back % d == 0) & (back <= NK * d), 0.0, MASK_ADD) for d in DILATIONS]
    slopes = (2.0 ** (-8.0 * np.arange(1, H_B + 1) / H_B)).reshape(ng, gh)
    slope_arr = jnp.asarray(np.broadcast_to(slopes[:, :, None], (ng, 8, LANES)), F32)
    row_spec = pl.BlockSpec((8, gw), lambda g, b: (b // 8, g))
    t_spec = pl.BlockSpec((None, gw, wb), lambda g, b: (b, g, 0))
    return pl.pallas_call(
        _attn_sample_kernel,
        grid=(ng, rows),
        in_specs=[pl.BlockSpec((None, 8, LANES), lambda g, b: (g, 0, 0)),
                  pl.BlockSpec((4, wb), lambda g, b: (0, 0)),
                  row_spec, row_spec, row_spec, row_spec, t_spec, t_spec],
        out_specs=row_spec,
        out_shape=jax.ShapeDtypeStruct((rows, D_MODEL), F32),
        compiler_params=_cparams(("arbitrary", "arbitrary")),
        name="attn_sample",
    )(slope_arr, jnp.asarray(np.stack(pos), F32), q, kn, vn, sgb, cache_kt, cache_vt)


def _odd_front_sample_kernel(h_ref, *refs):
    ws32 = refs[:N_SEG_O]
    pw_ref, ps_ref, prev_ref = refs[N_SEG_O:N_SEG_O + 3]
    yc_ref, q_ref, k_ref, v_ref, sgd_ref, pn_ref = refs[N_SEG_O + 3:N_SEG_O + 9]
    w16_refs = refs[N_SEG_O + 9:]
    ws = [w[...].astype(BF16) for w in ws32]
    for w16_ref, w in zip(w16_refs, ws):
        w16_ref[...] = w
    wu, wgc, wq, wk, wv, wgd = ws
    j = pl.program_id(0)
    h = h_ref[...]
    u = _dot(h, wu)
    n_prev = prev_ref.shape[0]
    sums = []
    s = jnp.zeros_like(u)
    back = 0
    for w in POOL_SIZES:
        while back < w - 1:
            s = s + prev_ref[n_prev - 1 - back]
            back += 1
        sums.append(s)
    win = _pick_by_chunk(j, sums) + u
    inv_w = _pick_by_chunk(j, [1.0 / w for w in POOL_SIZES])
    pooled = win * inv_w - u
    mixed = _dot(pooled.astype(BF16), pw_ref[...]) * ps_ref[...]
    yc_ref[...] = (mixed * _silu(_dot(h, wgc))).astype(BF16)
    q_ref[...] = _dot(h, wq)
    k_ref[...] = _dot(h, wk) * (DK_D ** -0.5)
    v_ref[...] = _dot(h, wv)
    sgd_ref[...] = _silu(_dot(h, wgd))
    for t in range(n_prev - 1):
        pn_ref[t] = prev_ref[t + 1]
    pn_ref[n_prev - 1] = u


def _odd_front_sample(h2d, w_in, pw, ps, prev_t):
    rows = h2d.shape[0]
    n_prev = prev_t.shape[0]
    nj = D_MODEL // TN

    def wspec(s):
        return pl.BlockSpec((D_MODEL, TN), lambda j, s=s: (0, s * nj + j))

    col = pl.BlockSpec((rows, TN), lambda j: (0, j))
    st = pl.BlockSpec((n_prev, rows, TN), lambda j: (0, 0, j))
    colf = jax.ShapeDtypeStruct((rows, D_MODEL), F32)
    w16_spec = pl.BlockSpec((D_MODEL, TN), lambda j: (0, j))
    w16_shape = jax.ShapeDtypeStruct((D_MODEL, D_MODEL), BF16)
    outs = pl.pallas_call(
        _odd_front_sample_kernel,
        grid=(nj,),
        in_specs=[pl.BlockSpec((rows, D_MODEL), lambda j: (0, 0))] + [wspec(s) for s in range(N_SEG_O)] + [
            pl.BlockSpec((None, TN, TN), lambda j: (j, 0, 0)), pl.BlockSpec((1, TN), lambda j: (0, j)), st],
        out_specs=[col, col, col, col, col, st] + [w16_spec] * N_SEG_O,
        out_shape=[jax.ShapeDtypeStruct((rows, D_MODEL), BF16), colf, colf, colf, colf,
                   jax.ShapeDtypeStruct((n_prev, rows, D_MODEL), F32)] + [w16_shape] * N_SEG_O,
        compiler_params=_cparams(("arbitrary",)),
        name="odd_front_sample",
    )(h2d, *([w_in] * N_SEG_O), pw, ps.reshape(1, D_MODEL), prev_t)
    return outs[:6], outs[6:]


RET_SAMPLE_ROWS = 8


def _ret_sample_kernel(q_ref, k_ref, v_ref, sgd_ref, st_ref, yd_ref, sn_ref):
    row = lax.broadcasted_iota(jnp.int32, (DK_D, DK_D), 0)
    col = lax.broadcasted_iota(jnp.int32, (DK_D, DK_D), 1)
    qs, ks, vs, gs = q_ref[...], k_ref[...], v_ref[...], sgd_ref[...]
    out_rows = []
    for i in range(RET_SAMPLE_ROWS):
        out_heads = []
        for h in range(H_D):
            g = float(np.exp(RET_LOG_G[h]))
            cols = slice(h * DK_D, (h + 1) * DK_D)
            q, k, v = qs[i:i + 1, cols], ks[i:i + 1, cols], vs[i:i + 1, cols]
            state = st_ref[i, h]
            qk = jnp.sum(q * k, axis=-1, keepdims=True)
            q8 = jnp.broadcast_to(q, (8, DK_D)).astype(BF16)
            cross = _dot(q8, state.astype(BF16))[0:1, :]
            o = qk * v + g * cross
            k_diag = jnp.where(row == col, jnp.broadcast_to(k, (DK_D, DK_D)), 0.0).astype(BF16)
            v_rows = jnp.broadcast_to(v, (DK_D, DK_D)).astype(BF16)
            sn_ref[i, h] = g * state + _dot(k_diag, v_rows)
            out_heads.append(_rms(o) * gs[i:i + 1, cols])
        out_rows.append(jnp.concatenate(out_heads, axis=1))
    yd_ref[...] = jnp.concatenate(out_rows, axis=0)


def _ret_sample(q, k, v, sgd, state):
    rows = q.shape[0]
    rs = RET_SAMPLE_ROWS
    row_spec = pl.BlockSpec((rs, D_MODEL), lambda i: (i, 0))
    st_spec = pl.BlockSpec((rs, H_D, DK_D, DK_D), lambda i: (i, 0, 0, 0))
    return pl.pallas_call(
        _ret_sample_kernel,
        grid=(rows // rs,),
        in_specs=[row_spec, row_spec, row_spec, row_spec, st_spec],
        out_specs=[row_spec, st_spec],
        out_shape=[jax.ShapeDtypeStruct((rows, D_MODEL), F32),
                   jax.ShapeDtypeStruct((rows, H_D, DK_D, DK_D), F32)],
        compiler_params=_cparams(("arbitrary",)),
        name="ret_sample",
    )(q, k, v, sgd, state)


def kernel(x_prompt, x_sample, c_prompt, c_sample, state_conv, cache_win_k, cache_win_v, state_pool, state_ret,
           norm_e, ada_w_e, ada_b_e, w_in_e, conv_w, conv_b, w_out_e, norm_o, ada_w_o, ada_b_o, w_in_o,
           pool_w, pool_scale, w_out_o, norm_f):
    batch, seq, d = x_prompt.shape
    sb = x_sample.shape[0]
    assert d == D_MODEL and x_sample.shape[1] == 1
    assert norm_e.shape[0] == 1 and norm_o.shape[0] == 1
    wb = cache_win_k.shape[2]
    keep = min(wb, seq)
    assert wb == DILATIONS[-1] * NK and seq % (DILATIONS[-1] * NK) == 0
    tm = TM_OUT

    n_c = batch + sb
    pad = (-n_c) % 8
    c_all = jnp.concatenate([c_prompt, c_sample, jnp.zeros((pad, d), F32)], axis=0)
    mod_e, mod_o = _adaln(c_all, ada_w_e[0], ada_b_e[0], ada_w_o[0], ada_b_o[0])
    mod_e_p, mod_e_s = mod_e[:batch], mod_e[batch:n_c]
    mod_o_p, mod_o_s = mod_o[:batch], mod_o[batch:n_c]

    w_out_e16 = w_out_e[0].astype(BF16)
    w_out_o16 = w_out_o[0].astype(BF16)
    pool_w16 = pool_w[0].astype(BF16)

    xs2d = x_sample.reshape(sb, d)
    conv_prev_t = jnp.transpose(state_conv[0], (1, 0, 2))
    (ya_s, q_s, k_s, v_s, sgb_s, conv_s_t), w_e_segs = _even_front_sample(
        xs2d, mod_e_s, norm_e[0], w_in_e[0], conv_w[0], conv_b[0], conv_prev_t)
    cache_kt = jnp.transpose(cache_win_k[0], (0, 2, 3, 1)).reshape(sb, d, wb)
    cache_vt = jnp.transpose(cache_win_v[0], (0, 2, 3, 1)).reshape(sb, d, wb)
    yb_s = _attn_sample(q_s, k_s, v_s, sgb_s, cache_kt, cache_vt)
    x1_s, h1_s = _outproj(ya_s, yb_s, xs2d, mod_e_s, w_out_e16, norm_o[0], mod_o_s, sb, 1)
    pool_prev_t = jnp.transpose(state_pool[0], (1, 0, 2))
    (yc_s, rq_s, rk_s, rv_s, sgd_s, pool_s_t), w_o_segs = _odd_front_sample(
        h1_s, w_in_o[0], pool_w16, pool_scale[0], pool_prev_t)
    yd_s, ret_s = _ret_sample(rq_s, rk_s, rv_s, sgd_s, state_ret[0])
    y_sample = _outproj(yc_s, yd_s, x1_s, mod_o_s, w_out_o16, norm_f, None, sb, 1)

    xp2d = x_prompt.reshape(batch * seq, d)
    ya, q, k, v, sgb, k_new, v_new, conv_tail = _even_front_prompt(
        xp2d, mod_e_p, norm_e[0], w_e_segs, conv_w[0], conv_b[0], batch, seq, keep, TM_FRONT)
    yb = _attn_prompt(q, k, v, sgb, batch, seq).reshape(batch * seq, d)
    x1, h1 = _outproj(ya, yb, xp2d, mod_e_p, w_out_e16, norm_o[0], mod_o_p, tm, seq)
    yc, rq, rk, rv, sgd, pool_tail = _odd_front_prompt(h1, w_o_segs, pool_w16, pool_scale[0], batch, seq, TM_FRONT)
    yd, ret_p = _ret_prompt(rq, rk, rv, sgd, batch, seq)
    y_prompt = _outproj(yc, yd.reshape(batch * seq, d), x1, mod_o_p, w_out_o16, norm_f, None, tm, seq)

    return (
        y_prompt.reshape(batch, seq, d),
        y_sample.reshape(sb, 1, d),
        conv_tail[:, 6:8][None],
        jnp.transpose(conv_s_t, (1, 0, 2))[None],
        jnp.transpose(k_new.reshape(batch, H_B, HD_B, keep), (0, 3, 1, 2))[None],
        k_s.reshape(1, sb, 1, H_B, HD_B),
        jnp.transpose(v_new.reshape(batch, H_B, HD_B, keep), (0, 3, 1, 2))[None],
        v_s.reshape(1, sb, 1, H_B, HD_B),
        pool_tail[:, 1:][None],
        jnp.transpose(pool_s_t, (1, 0, 2))[None],
        ret_p[None],
        ret_s[None],
    )
```

```python
import functools

import numpy as np
import jax
import jax.numpy as jnp
from jax import lax
from jax.experimental import pallas as pl
from jax.experimental.pallas import tpu as pltpu

F32 = jnp.float32
BF16 = jnp.bfloat16
WORD = jnp.uint32

D_MODEL = 1024
EPS = 1e-6
H_B = 16
HD_B = 64
N_PAIR = H_B // 2
LANES = 128
NK = 128
DILATIONS = (1, 4, 16)
POOL_SIZES = (2, 4, 8, 16)
POOL_PREV = 16
H_D = 4
DK_D = 256
RET_CHUNK = 256
TN = 256
TM_FRONT = 1024
TM_OUT = 1024
MASK_DIST = 1e9
LOG2E = float(np.log2(np.e))
MASK_ADD = 1e30
VMEM_LIMIT = 56 * 1024 * 1024

RET_LOG_G = [float(np.log(1.0 - 2.0 ** (-5.0 - h))) for h in range(H_D)]


def _cparams(sem):
    return pltpu.CompilerParams(dimension_semantics=sem, vmem_limit_bytes=VMEM_LIMIT)


def _silu(x):
    return x * (1.0 / (1.0 + jnp.exp(-x)))


def _dot(a, b):
    return jnp.dot(a, b, preferred_element_type=F32)


def _dot_nt(a, b):
    return lax.dot_general(a, b, (((1,), (1,)), ((), ())), preferred_element_type=F32)


def _rms(x):
    return x * lax.rsqrt(jnp.mean(x * x, axis=-1, keepdims=True) + EPS)


def _adaln_kernel(c_ref, we_ref, be_ref, wo_ref, bo_ref, me_ref, mo_ref):
    sc = _silu(c_ref[...]).astype(BF16)
    me_ref[...] = _dot(sc, we_ref[...].astype(BF16)) + be_ref[...]
    mo_ref[...] = _dot(sc, wo_ref[...].astype(BF16)) + bo_ref[...]


def _adaln(c_all, we, be, wo, bo):
    rows = c_all.shape[0]
    tn = 512
    n = 3 * D_MODEL
    wspec = pl.BlockSpec((D_MODEL, tn), lambda j: (0, j))
    bspec = pl.BlockSpec((1, tn), lambda j: (0, j))
    ospec = pl.BlockSpec((rows, tn), lambda j: (0, j))
    return pl.pallas_call(
        _adaln_kernel,
        grid=(n // tn,),
        in_specs=[pl.BlockSpec((rows, D_MODEL), lambda j: (0, 0)), wspec, bspec, wspec, bspec],
        out_specs=[ospec, ospec],
        out_shape=[jax.ShapeDtypeStruct((rows, n), F32)] * 2,
        compiler_params=_cparams(("arbitrary",)),
        name="adaln",
    )(c_all, we, be.reshape(1, n), wo, bo.reshape(1, n))


def _norm_mod(x, g, mod):
    shift = mod[:, 0:D_MODEL]
    scale = mod[:, D_MODEL:2 * D_MODEL]
    return _rms(x) * g * (1.0 + scale) + shift


def _shift_rows(u, k, prev_rows):
    row = lax.broadcasted_iota(jnp.int32, u.shape, 0)
    out = pltpu.roll(u, k, 0)
    for idx, pr in enumerate(prev_rows):
        out = jnp.where(row == idx, pr, out)
    return out


N_SEG_E = 8
PHASES_E = 16


def _phase_segments(t, n_seg, n_phases):
    return range(t * n_seg // n_phases, (t + 1) * n_seg // n_phases)


def _even_front_kernel(x_ref, mod_ref, g_ref, *refs, tiles_per_batch, n_chunks):
    ws = refs[:N_SEG_E]
    cw_ref, cb_ref = refs[N_SEG_E:N_SEG_E + 2]
    ya_ref, q_ref, k_ref, v_ref, sgb_ref, kn_ref, vn_ref, cn_ref = refs[N_SEG_E + 2:N_SEG_E + 10]
    h_scr, carry_scr, z_a, z_b = refs[N_SEG_E + 10:]
    nj = D_MODEL // TN
    s = pl.program_id(0)
    c_out = jnp.maximum(s - 1, 0)
    j_out = c_out % nj
    il_out = (c_out // nj) % tiles_per_batch

    @pl.when(s == 0)
    def _():
        z_b[...] = jnp.zeros_like(z_b)
        carry_scr[...] = jnp.zeros_like(carry_scr)

    @pl.when((s < n_chunks) & (s % nj == 0))
    def _():
        h_scr[...] = _norm_mod(x_ref[...], g_ref[...], mod_ref[...]).astype(BF16)

    tm = h_scr.shape[0]
    rb = tm // PHASES_E

    def phase(t, z_w, z_r):
        for seg in _phase_segments(t, N_SEG_E, PHASES_E):
            z_w[seg] = _dot(h_scr[...], ws[seg][...])
        rows = slice(t * rb, (t + 1) * rb)
        bg, cg, xv, ga, q, k, v, gb = (z_r[seg, rows, :] for seg in range(N_SEG_E))
        u = cg * xv
        prev = carry_scr[j_out]
        if t == 0:
            prev = jnp.where(il_out == 0, 0.0, prev)
        p2, p1 = prev[6:7, :], prev[7:8, :]
        u1 = _shift_rows(u, 1, [p1])
        u2 = _shift_rows(u, 2, [p2, p1])
        cw = cw_ref[...]
        conv = cb_ref[...] + cw[0:1, :] * u2 + cw[1:2, :] * u1 + cw[2:3, :] * u
        tail = u[rb - 8:rb, :]
        carry_scr[j_out] = tail
        if t == PHASES_E - 1:
            cn_ref[...] = tail
        ya_ref[rows, :] = (bg * conv * _silu(ga)).astype(BF16)
        kn_ref[:, rows] = k.T
        vn_ref[:, rows] = v.T
        wrows = slice(t * rb // 2, (t + 1) * rb // 2)
        for ref, val in ((q_ref, q * (HD_B ** -0.5 * LOG2E)),
                         (k_ref, k), (v_ref, v)):
            words = pltpu.bitcast(val.astype(BF16), WORD)
            for e in range(TN // LANES):
                ref[e, wrows, :] = words[:, e * LANES:(e + 1) * LANES]
        sgb = _silu(gb).astype(BF16)
        for e in range(TN // LANES):
            sgb_ref[e, rows, :] = sgb[:, e * LANES:(e + 1) * LANES]

    for parity, (z_w, z_r) in enumerate(((z_a, z_b), (z_b, z_a))):
        for t in range(PHASES_E):
            pl.when(s % 2 == parity)(functools.partial(phase, t, z_w, z_r))


def _even_front_prompt(x2d, mod, g, w_segs, cw, cb, batch, seq, keep, tm):
    rows = batch * seq
    tpb = seq // tm
    nj = D_MODEL // TN
    n_chunks = (rows // tm) * nj
    off = (seq - keep) // tm
    ppc = TN // LANES

    def c_in(s):
        return jnp.minimum(s, n_chunks - 1)

    def c_out(s):
        return jnp.maximum(s - 1, 0)

    def out_ij(s):
        c = c_out(s)
        return c // nj, c % nj

    def pair_map(s):
        i, j = out_ij(s)
        return (i // tpb, j, i % tpb, 0)

    def keep_map(s):
        i, j = out_ij(s)
        il = i % tpb
        kept = il >= off
        return (i // tpb, jnp.where(kept, j, 0), jnp.where(kept, il - off, 0))

    def tail_map(s):
        i, j = out_ij(s)
        return (i // tpb, 0, jnp.where(i % tpb == tpb - 1, j, 0))

    pair_spec = pl.BlockSpec((None, ppc, tm, LANES), pair_map)
    pair_shape = jax.ShapeDtypeStruct((batch, N_PAIR, seq, LANES), BF16)
    word_spec = pl.BlockSpec((None, ppc, tm // 2, LANES), pair_map)
    word_shape = jax.ShapeDtypeStruct((batch, N_PAIR, seq // 2, LANES), WORD)
    keep_spec = pl.BlockSpec((None, TN, tm), keep_map)
    keep_shape = jax.ShapeDtypeStruct((batch, D_MODEL, keep), F32)
    return pl.pallas_call(
        functools.partial(_even_front_kernel, tiles_per_batch=tpb, n_chunks=n_chunks),
        grid=(n_chunks + 1,),
        in_specs=[
            pl.BlockSpec((tm, D_MODEL), lambda s: (c_in(s) // nj, 0)),
            pl.BlockSpec((None, 1, 3 * D_MODEL), lambda s: (c_in(s) // nj // tpb, 0, 0)),
            pl.BlockSpec((1, D_MODEL), lambda s: (0, 0)),
        ] + [pl.BlockSpec((D_MODEL, TN), lambda s: (0, c_in(s) % nj))] * N_SEG_E + [
            pl.BlockSpec((3, TN), lambda s: (0, c_out(s) % nj)),
            pl.BlockSpec((1, TN), lambda s: (0, c_out(s) % nj)),
        ],
        out_specs=[
            pl.BlockSpec((tm, TN), lambda s: out_ij(s)),
            word_spec, word_spec, word_spec, pair_spec,
            keep_spec, keep_spec,
            pl.BlockSpec((None, 8, TN), tail_map),
        ],
        out_shape=[
            jax.ShapeDtypeStruct((rows, D_MODEL), BF16),
            word_shape, word_shape, word_shape, pair_shape,
            keep_shape, keep_shape,
            jax.ShapeDtypeStruct((batch, 8, D_MODEL), F32),
        ],
        scratch_shapes=[pltpu.VMEM((tm, D_MODEL), BF16), pltpu.VMEM((nj, 8, TN), F32),
                        pltpu.VMEM((N_SEG_E, tm, TN), F32), pltpu.VMEM((N_SEG_E, tm, TN), F32)],
        compiler_params=_cparams(("arbitrary",)),
        name="even_front_prompt",
    )(x2d, mod.reshape(batch, 1, 3 * D_MODEL), g.reshape(1, D_MODEL), *w_segs, cw, cb.reshape(1, D_MODEL))


STAT_PITCH = {1: 1, 4: 4, 16: 24}
ATTN_UNROLL = 32


def _residue_stream(src, r):
    n_out = src.shape[0] // 4
    halves = [pltpu.unpack_elementwise(src[pl.ds(r // 2 + off, n_out, stride=4), :], index=r % 2,
                                       packed_dtype=BF16, unpacked_dtype=F32) for off in (0, 2)]
    return pltpu.pack_elementwise(halves, packed_dtype=BF16)


def _attn_prompt_kernel(slope_ref, dist_ref, q_ref, k_ref, v_ref, sgb_ref, yb_ref,
                        q4_scr, q16_scr, kv4_scr, kv16_scr, acc_scr, m_scr, l_scr, bias_scr, *, seq):
    hp = pl.program_id(1)
    lane = lax.broadcasted_iota(jnp.int32, (1, LANES), 1)
    half0 = lane < HD_B
    dist = dist_ref[...]
    masked = jnp.full((NK, NK), MASK_DIST, F32)
    for p, d in enumerate(DILATIONS):
        for e in range(2):
            bias = dist * (slope_ref[2 * hp + e] * (float(d) * LOG2E))
            bias_scr[2 * p + e] = bias
            bias_scr[6 + 2 * p + e] = jnp.concatenate([bias[:, NK:], masked], axis=1)

    for a, ref in enumerate((q_ref, k_ref, v_ref)):
        for r in range(4):
            s = _residue_stream(ref, r)
            if a == 0:
                q4_scr[r] = s
            else:
                kv4_scr[a - 1, r] = s
        for r in range(16):
            s = _residue_stream(q4_scr.at[r % 4] if a == 0 else kv4_scr.at[a - 1, r % 4], r // 4)
            if a == 0:
                q16_scr[r] = s
            else:
                kv16_scr[a - 1, r] = s

    ones = jnp.ones((2 * NK, LANES), BF16)
    hw = NK // 2

    def tile(p, d, r, n):
        qw = pl.ds(pl.multiple_of(n * hw, hw), hw)
        kw = pl.ds(pl.multiple_of(jnp.maximum(n - 1, 0) * hw, hw), 2 * hw)
        if d == 1:
            qs, ks, vs = q_ref, k_ref, v_ref
        elif d == 4:
            qs, ks, vs = q4_scr.at[r], kv4_scr.at[0, r], kv4_scr.at[1, r]
        else:
            qs, ks, vs = q16_scr.at[r], kv16_scr.at[0, r], kv16_scr.at[1, r]
        q = pltpu.bitcast(qs[qw, :], BF16)
        kk = pltpu.bitcast(ks[kw, :], BF16)
        vv1 = jnp.concatenate([pltpu.bitcast(vs[kw, :], BF16), ones], axis=1)
        first = jnp.where(n == 0, 1, 0)
        res = []
        for e in range(2):
            qe = jnp.where(half0 if e == 0 else jnp.logical_not(half0), q, jnp.zeros_like(q))
            s = _dot_nt(qe, kk) - bias_scr[6 * first + 2 * p + e]
            m = jnp.max(s, axis=-1, keepdims=True)
            pe = jnp.exp2(s - m).astype(BF16)
            res.append((_dot(pe, vv1), m))
        (a0, m0), (a1, m1) = res
        acc = jnp.where(half0, a0[:, :LANES], a1[:, :LANES])
        l = jnp.where(half0, a0[:, LANES:], a1[:, LANES:])
        m = jnp.where(half0, m0, m1)
        if d > 1:
            pitch = STAT_PITCH[d]
            rows = pl.ds(r + pitch * NK * n, NK, stride=pitch)
            acc_scr[p - 1, rows, :] = acc
            l_scr[p - 1, rows, :] = l
            m_scr[p - 1, rows, :] = m
            return
        accs, ls, ms = [acc], [l], [m]
        for pp in range(1, len(DILATIONS)):
            dd = DILATIONS[pp]
            for ref, vals in ((acc_scr, accs), (l_scr, ls), (m_scr, ms)):
                vals.append(jnp.concatenate(
                    [ref[pp - 1, pl.ds(pl.multiple_of((n * (NK // dd) + g) * STAT_PITCH[dd], 8), dd), :]
                     for g in range(NK // dd)], axis=0) if STAT_PITCH[dd] != dd
                    else ref[pp - 1, pl.ds(pl.multiple_of(n * NK, NK), NK), :])
        top = jnp.maximum(jnp.maximum(ms[0], ms[1]), ms[2])
        ws = [jnp.exp2(x - top) for x in ms]
        num = ws[0] * accs[0] + ws[1] * accs[1] + ws[2] * accs[2]
        den = ws[0] * ls[0] + ws[1] * ls[1] + ws[2] * ls[2]
        rows = pl.ds(pl.multiple_of(n * NK, NK), NK)
        yb_ref[rows, :] = (num * (1.0 / den) * sgb_ref[rows, :].astype(F32)).astype(BF16)

    for p, d in reversed(list(enumerate(DILATIONS))):
        def body(it, c, p=p, d=d):
            for u in range(ATTN_UNROLL):
                t = it * ATTN_UNROLL + u
                if d == 1:
                    tile(p, d, 0, t)
                else:
                    tile(p, d, t % d, t // d)
            return c

        lax.fori_loop(0, seq // NK // ATTN_UNROLL, body, 0)


def _band_distance():
    qi = np.arange(NK)[:, None]
    kj = np.arange(2 * NK)[None, :]
    dist = NK + qi - kj
    return jnp.asarray(np.where((dist >= 0) & (dist <= NK), dist, MASK_DIST), dtype=F32)


def _alibi_slopes():
    return jnp.asarray(2.0 ** (-8.0 * np.arange(1, H_B + 1) / H_B), dtype=F32)


def _attn_prompt(q, k, v, sgb, batch, seq):
    word_spec = pl.BlockSpec((None, None, seq // 2, LANES), lambda b, hp: (b, hp, 0, 0))
    stat_rows = max(seq // d * STAT_PITCH[d] for d in DILATIONS)
    stat = pltpu.VMEM((len(DILATIONS) - 1, stat_rows, LANES), F32)
    return pl.pallas_call(
        functools.partial(_attn_prompt_kernel, seq=seq),
        grid=(batch, N_PAIR),
        in_specs=[pl.BlockSpec(memory_space=pltpu.SMEM),
                  pl.BlockSpec((NK, 2 * NK), lambda b, hp: (0, 0)),
                  word_spec, word_spec, word_spec,
                  pl.BlockSpec((None, None, seq, LANES), lambda b, hp: (b, hp, 0, 0))],
        out_specs=pl.BlockSpec((None, seq, LANES), lambda b, hp: (b, 0, hp)),
        out_shape=jax.ShapeDtypeStruct((batch, seq, D_MODEL), BF16),
        scratch_shapes=[pltpu.VMEM((4, seq // 8, LANES), WORD),
                        pltpu.VMEM((16, seq // 32, LANES), WORD),
                        pltpu.VMEM((2, 4, seq // 8, LANES), WORD),
                        pltpu.VMEM((2, 16, seq // 32, LANES), WORD),
                        stat, stat, stat,
                        pltpu.VMEM((12, NK, 2 * NK), F32)],
        compiler_params=_cparams(("arbitrary", "arbitrary")),
        name="attn_prompt",
    )(_alibi_slopes(), _band_distance(), q, k, v, sgb)


def _outproj_mid_kernel(ya_ref, yb_ref, x_ref, mod_ref, w1_ref, w2_ref, g_ref, mod2_ref, x1_ref, h_ref):
    gate = mod_ref[:, 2 * D_MODEL:3 * D_MODEL]
    x1 = x_ref[...] + gate * (_dot(ya_ref[...], w1_ref[...]) + _dot(yb_ref[...].astype(BF16), w2_ref[...]))
    x1_ref[...] = x1
    h_ref[...] = _norm_mod(x1, g_ref[...], mod2_ref[...]).astype(BF16)


def _outproj_final_kernel(ya_ref, yb_ref, x_ref, mod_ref, w1_ref, w2_ref, g_ref, y_ref):
    gate = mod_ref[:, 2 * D_MODEL:3 * D_MODEL]
    x1 = x_ref[...] + gate * (_dot(ya_ref[...], w1_ref[...]) + _dot(yb_ref[...].astype(BF16), w2_ref[...]))
    y_ref[...] = _rms(x1) * g_ref[...]


def _outproj(ya, yb, x2d, mod, w_out, g, mod2, tm, rows_per_mod):
    rows = x2d.shape[0]
    half = w_out.shape[0] // 2
    row_spec = pl.BlockSpec((tm, D_MODEL), lambda i: (i, 0))
    if rows_per_mod == 1:
        mod_spec = pl.BlockSpec((tm, 3 * D_MODEL), lambda i: (i, 0))
        mods = (mod, mod2)
    else:
        tpb = rows_per_mod // tm
        mod_spec = pl.BlockSpec((None, 1, 3 * D_MODEL), lambda i: (i // tpb, 0, 0))
        mods = tuple(None if m is None else m.reshape(-1, 1, 3 * D_MODEL) for m in (mod, mod2))
    w1_spec = pl.BlockSpec((half, D_MODEL), lambda i: (0, 0))
    w2_spec = pl.BlockSpec((half, D_MODEL), lambda i: (1, 0))
    g_spec = pl.BlockSpec((1, D_MODEL), lambda i: (0, 0))
    common = dict(grid=(rows // tm,), compiler_params=_cparams(("arbitrary",)))
    if mod2 is None:
        return pl.pallas_call(
            _outproj_final_kernel,
            in_specs=[row_spec, row_spec, row_spec, mod_spec, w1_spec, w2_spec, g_spec],
            out_specs=row_spec,
            out_shape=jax.ShapeDtypeStruct((rows, D_MODEL), F32),
            name="outproj_final", **common,
        )(ya, yb, x2d, mods[0], w_out, w_out, g.reshape(1, D_MODEL))
    return pl.pallas_call(
        _outproj_mid_kernel,
        in_specs=[row_spec, row_spec, row_spec, mod_spec, w1_spec, w2_spec, g_spec, mod_spec],
        out_specs=[row_spec, row_spec],
        out_shape=[jax.ShapeDtypeStruct((rows, D_MODEL), F32), jax.ShapeDtypeStruct((rows, D_MODEL), BF16)],
        name="outproj_mid", **common,
    )(ya, yb, x2d, mods[0], w_out, w_out, g.reshape(1, D_MODEL), mods[1])


def _pick_by_chunk(j, vals):
    out = vals[-1]
    for idx in range(len(vals) - 2, -1, -1):
        out = jnp.where(j == idx, vals[idx], out)
    return out


N_SEG_O = 6
PHASES_O = 16


def _odd_front_kernel(h_ref, *refs, tiles_per_batch):
    ws = refs[:N_SEG_O]
    pw_ref, ps_ref = refs[N_SEG_O:N_SEG_O + 2]
    yc_ref, q_ref, kt_ref, v_ref, sgd_ref, pn_ref = refs[N_SEG_O + 2:N_SEG_O + 8]
    carry_scr, z_a, z_b = refs[N_SEG_O + 8:]
    nj = D_MODEL // TN
    s = pl.program_id(0)
    c_out = jnp.maximum(s - 1, 0)
    j_out = c_out % nj
    il_out = (c_out // nj) % tiles_per_batch
    tm = h_ref.shape[0]
    rb = tm // PHASES_O

    @pl.when(s == 0)
    def _():
        z_b[...] = jnp.zeros_like(z_b)
        carry_scr[...] = jnp.zeros_like(carry_scr)

    def phase(t, z_w, z_r):
        for seg in _phase_segments(t, N_SEG_O, PHASES_O):
            z_w[seg] = _dot(h_ref[...], ws[seg][...])
        rows = slice(t * rb, (t + 1) * rb)
        u, gc, q, k, v, gd = (z_r[seg, rows, :] for seg in range(N_SEG_O))
        prev = carry_scr[j_out]
        if t == 0:
            prev = jnp.where(il_out == 0, 0.0, prev)
        tail = u[rb - POOL_PREV:rb, :]
        carry_scr[j_out] = tail
        if t == PHASES_O - 1:
            pn_ref[...] = tail
        ext = jnp.concatenate([prev, u], axis=0)
        sums = []
        acc = ext
        for sh in (1, 2, 4, 8):
            acc = acc + pltpu.roll(acc, sh, 0)
            sums.append(acc[POOL_PREV:, :])
        win = _pick_by_chunk(j_out, sums)
        width = _pick_by_chunk(j_out, [float(w) for w in POOL_SIZES])
        pos = (il_out * tm + t * rb + lax.broadcasted_iota(jnp.int32, (rb, 1), 0)).astype(F32)
        pooled = win / jnp.minimum(width, pos + 1.0) - u
        mixed = _dot(pooled.astype(BF16), pw_ref[...]) * ps_ref[...]
        yc_ref[rows, :] = (mixed * _silu(gc)).astype(BF16)
        q_ref[rows, :] = q.astype(BF16)
        kt_ref[:, rows] = (k * (DK_D ** -0.5)).T.astype(BF16)
        v_ref[rows, :] = v.astype(BF16)
        sgd_ref[rows, :] = _silu(gd).astype(BF16)

    for parity, (z_w, z_r) in enumerate(((z_a, z_b), (z_b, z_a))):
        for t in range(PHASES_O):
            pl.when(s % 2 == parity)(functools.partial(phase, t, z_w, z_r))


def _odd_front_prompt(h2d, w_segs, pw, ps, batch, seq, tm):
    rows = batch * seq
    tpb = seq // tm
    nj = D_MODEL // TN
    n_chunks = (rows // tm) * nj

    def c_in(s):
        return jnp.minimum(s, n_chunks - 1)

    def out_ij(s):
        c = jnp.maximum(s - 1, 0)
        return c // nj, c % nj

    def head_map(s):
        i, j = out_ij(s)
        return (i // tpb, j, i % tpb, 0)

    def kt_map(s):
        i, j = out_ij(s)
        return (i // tpb, j, 0, i % tpb)

    def tail_map(s):
        i, j = out_ij(s)
        return (i // tpb, 0, jnp.where(i % tpb == tpb - 1, j, 0))

    head_spec = pl.BlockSpec((None, None, tm, TN), head_map)
    head_shape = jax.ShapeDtypeStruct((batch, H_D, seq, DK_D), BF16)
    tok_spec = pl.BlockSpec((tm, TN), out_ij)
    tok_shape = jax.ShapeDtypeStruct((rows, D_MODEL), BF16)
    zbuf = pltpu.VMEM((N_SEG_O, tm, TN), F32)
    return pl.pallas_call(
        functools.partial(_odd_front_kernel, tiles_per_batch=tpb),
        grid=(n_chunks + 1,),
        in_specs=[pl.BlockSpec((tm, D_MODEL), lambda s: (c_in(s) // nj, 0))] + [
                  pl.BlockSpec((D_MODEL, TN), lambda s: (0, c_in(s) % nj))] * N_SEG_O + [
                  pl.BlockSpec((None, TN, TN), lambda s: (out_ij(s)[1], 0, 0)),
                  pl.BlockSpec((1, TN), lambda s: (0, out_ij(s)[1]))],
        out_specs=[tok_spec, head_spec, pl.BlockSpec((None, None, TN, tm), kt_map), head_spec, tok_spec,
                   pl.BlockSpec((None, POOL_PREV, TN), tail_map)],
        out_shape=[tok_shape, head_shape, jax.ShapeDtypeStruct((batch, H_D, DK_D, seq), BF16),
                   head_shape, tok_shape,
                   jax.ShapeDtypeStruct((batch, POOL_PREV, D_MODEL), F32)],
        scratch_shapes=[pltpu.VMEM((nj, POOL_PREV, TN), F32), zbuf, zbuf],
        compiler_params=_cparams(("arbitrary",)),
        name="odd_front_prompt",
    )(h2d, *w_segs, pw, ps.reshape(1, D_MODEL))


RET_SBLK = 1024


def _ret_prompt_kernel(q_ref, kt_ref, v_ref, sgd_ref, dec_ref, cdec_ref, kdec_ref,
                       yd_ref, st_ref, state_scr, *, sblk, n_sblk):
    sb = pl.program_id(1)

    @pl.when(sb == 0)
    def _():
        state_scr[...] = jnp.zeros_like(state_scr)

    def chunk(c, carry):
        rows = pl.ds(pl.multiple_of(c * RET_CHUNK, RET_CHUNK), RET_CHUNK)
        for h in range(H_D):
            cols = slice(h * DK_D, (h + 1) * DK_D)
            q = q_ref[h, rows, :]
            kt = kt_ref[h, :, rows]
            v = v_ref[h, rows, :]
            state = state_scr[h]
            scores = _dot(q, kt) * dec_ref[h]
            o = _dot(scores.astype(BF16), v) + _dot(q, state.astype(BF16)) * cdec_ref[h]
            kd = (kt.astype(F32) * kdec_ref[h]).astype(BF16)
            state_scr[h] = float(np.exp(RET_CHUNK * RET_LOG_G[h])) * state + _dot(kd, v)
            yd_ref[rows, cols] = (_rms(o) * sgd_ref[rows, cols].astype(F32)).astype(BF16)
        return carry

    lax.fori_loop(0, sblk // RET_CHUNK, chunk, 0, unroll=4)

    @pl.when(sb == n_sblk - 1)
    def _():
        st_ref[...] = state_scr[...]


def _ret_consts():
    t = np.arange(RET_CHUNK, dtype=np.float64)
    diff = t[:, None] - t[None, :]
    lg = np.asarray(RET_LOG_G)[:, None, None]
    dec = np.where(diff >= 0, np.exp(np.maximum(diff, 0.0)[None] * lg), 0.0)
    cdec = np.broadcast_to(np.exp((t + 1.0)[None, :, None] * lg), (H_D, RET_CHUNK, DK_D))
    kdec = np.broadcast_to(np.exp((RET_CHUNK - 1.0 - t)[None, None, :] * lg), (H_D, DK_D, RET_CHUNK))
    return jnp.asarray(dec, F32), jnp.asarray(cdec, F32), jnp.asarray(kdec, F32)


def _ret_prompt(q, kt, v, sgd, batch, seq):
    dec, cdec, kdec = _ret_consts()
    n_sblk = seq // RET_SBLK
    head_spec = pl.BlockSpec((None, H_D, RET_SBLK, DK_D), lambda b, s: (b, 0, s, 0))
    tok_spec = pl.BlockSpec((None, RET_SBLK, D_MODEL), lambda b, s: (b, s, 0))

    def const_spec(shape):
        return pl.BlockSpec(shape, lambda b, s: (0, 0, 0))

    return pl.pallas_call(
        functools.partial(_ret_prompt_kernel, sblk=RET_SBLK, n_sblk=n_sblk),
        grid=(batch, n_sblk),
        in_specs=[head_spec, pl.BlockSpec((None, H_D, DK_D, RET_SBLK), lambda b, s: (b, 0, 0, s)), head_spec,
                  tok_spec, const_spec(dec.shape), const_spec(cdec.shape), const_spec(kdec.shape)],
        out_specs=[tok_spec, pl.BlockSpec((None, H_D, DK_D, DK_D), lambda b, s: (b, 0, 0, 0))],
        out_shape=[jax.ShapeDtypeStruct((batch, seq, D_MODEL), BF16),
                   jax.ShapeDtypeStruct((batch, H_D, DK_D, DK_D), F32)],
        scratch_shapes=[pltpu.VMEM((H_D, DK_D, DK_D), F32)],
        compiler_params=_cparams(("arbitrary", "arbitrary")),
        name="ret_prompt",
    )(q, kt, v, sgd.reshape(batch, seq, D_MODEL), dec, cdec, kdec)


def _even_front_sample_kernel(x_ref, mod_ref, g_ref, *refs):
    ws32 = refs[:N_SEG_E]
    cw_ref, cb_ref, prev_ref = refs[N_SEG_E:N_SEG_E + 3]
    ya_ref, q_ref, k_ref, v_ref, sgb_ref, cn_ref = refs[N_SEG_E + 3:N_SEG_E + 9]
    w16_refs = refs[N_SEG_E + 9:]
    ws = [w[...].astype(BF16) for w in ws32]
    for w16_ref, w in zip(w16_refs, ws):
        w16_ref[...] = w
    wbg, wcg, wxv, wga, wq, wk, wv, wgb = ws
    h = _norm_mod(x_ref[...], g_ref[...], mod_ref[...]).astype(BF16)
    u = _dot(h, wcg) * _dot(h, wxv)
    cw = cw_ref[...]
    conv = cb_ref[...] + cw[0:1, :] * prev_ref[0] + cw[1:2, :] * prev_ref[1] + cw[2:3, :] * u
    cn_ref[0] = prev_ref[1]
    cn_ref[1] = u
    ya_ref[...] = (_dot(h, wbg) * conv * _silu(_dot(h, wga))).astype(BF16)
    q_ref[...] = _dot(h, wq) * (HD_B ** -0.5)
    k_ref[...] = _dot(h, wk)
    v_ref[...] = _dot(h, wv)
    sgb_ref[...] = _silu(_dot(h, wgb))


def _even_front_sample(x2d, mod, g, w_in, cw, cb, prev_t):
    rows = x2d.shape[0]
    nj = D_MODEL // TN

    def wspec(s):
        return pl.BlockSpec((D_MODEL, TN), lambda j, s=s: (0, s * nj + j))

    full = pl.BlockSpec((rows, D_MODEL), lambda j: (0, 0))
    col = pl.BlockSpec((rows, TN), lambda j: (0, j))
    st = pl.BlockSpec((2, rows, TN), lambda j: (0, 0, j))
    colf = jax.ShapeDtypeStruct((rows, D_MODEL), F32)
    w16_spec = pl.BlockSpec((D_MODEL, TN), lambda j: (0, j))
    w16_shape = jax.ShapeDtypeStruct((D_MODEL, D_MODEL), BF16)
    outs = pl.pallas_call(
        _even_front_sample_kernel,
        grid=(nj,),
        in_specs=[full, pl.BlockSpec((rows, 3 * D_MODEL), lambda j: (0, 0)),
                  pl.BlockSpec((1, D_MODEL), lambda j: (0, 0))] + [wspec(s) for s in range(N_SEG_E)] + [
            pl.BlockSpec((3, TN), lambda j: (0, j)), pl.BlockSpec((1, TN), lambda j: (0, j)), st],
        out_specs=[col, col, col, col, col, st] + [w16_spec] * N_SEG_E,
        out_shape=[jax.ShapeDtypeStruct((rows, D_MODEL), BF16), colf, colf, colf, colf,
                   jax.ShapeDtypeStruct((2, rows, D_MODEL), F32)] + [w16_shape] * N_SEG_E,
        compiler_params=_cparams(("arbitrary",)),
        name="even_front_sample",
    )(x2d, mod, g.reshape(1, D_MODEL), *([w_in] * N_SEG_E), cw, cb.reshape(1, D_MODEL), prev_t)
    return outs[:6], outs[6:]


def _attn_sample_kernel(slope_ref, pos_ref, q_ref, kn_ref, vn_ref, sgb_ref, kt_ref, vt_ref, yb_ref):
    gw = q_ref.shape[-1]
    me = pl.ds(pl.program_id(1) % 8, 1)
    row = lax.broadcasted_iota(jnp.int32, (8, gw), 0)
    col = lax.broadcasted_iota(jnp.int32, (8, gw), 1)
    own = (col >= row * HD_B) & (col < (row + 1) * HD_B)
    qm = jnp.where(own, q_ref[me, :], 0.0)
    s_self = jnp.sum(qm * kn_ref[me, :], axis=-1, keepdims=True)
    s_all = _dot(qm.astype(BF16), kt_ref[...].astype(BF16))
    s_all = s_all - slope_ref[:, 0:1] * pos_ref[0:1, :]
    v_self = vn_ref[me, :]
    probs, stats = [], []
    for p in range(len(DILATIONS)):
        s = s_all - pos_ref[p + 1:p + 2, :]
        m = jnp.maximum(jnp.max(s, axis=-1, keepdims=True), s_self)
        pe = jnp.exp(s - m)
        p_self = jnp.exp(s_self - m)
        probs.append(pe)
        stats.append((m, jnp.sum(pe, axis=-1, keepdims=True) + p_self, p_self))
    o_all = _dot_nt(jnp.concatenate(probs, axis=0).astype(BF16), vt_ref[...].astype(BF16))
    outs, lses = [], []
    for p, (m, l, p_self) in enumerate(stats):
        outs.append((o_all[8 * p:8 * p + 8, :] + p_self * v_self) * (1.0 / l))
        lses.append(m + jnp.log(l))
    m = jnp.maximum(jnp.maximum(lses[0], lses[1]), lses[2])
    ws = [jnp.exp(x - m) for x in lses]
    o = (ws[0] * outs[0] + ws[1] * outs[1] + ws[2] * outs[2]) * (1.0 / (ws[0] + ws[1] + ws[2]))
    o = jnp.sum(jnp.where(own, o, 0.0), axis=0, keepdims=True)
    yb_ref[me, :] = o * sgb_ref[me, :]


def _attn_sample(q, kn, vn, sgb, cache_kt, cache_vt):
    rows, _, wb = cache_kt.shape
    gh = 8
    gw = gh * HD_B
    ng = H_B // gh
    back = wb - np.arange(wb, dtype=np.float64)
    pos = [back] + [np.where((back % d == 0) & (back <= NK * d), 0.0, MASK_ADD) for d in DILATIONS]
    slopes = (2.0 ** (-8.0 * np.arange(1, H_B + 1) / H_B)).reshape(ng, gh)
    slope_arr = jnp.asarray(np.broadcast_to(slopes[:, :, None], (ng, 8, LANES)), F32)
    row_spec = pl.BlockSpec((8, gw), lambda g, b: (b // 8, g))
    t_spec = pl.BlockSpec((None, gw, wb), lambda g, b: (b, g, 0))
    return pl.pallas_call(
        _attn_sample_kernel,
        grid=(ng, rows),
        in_specs=[pl.BlockSpec((None, 8, LANES), lambda g, b: (g, 0, 0)),
                  pl.BlockSpec((4, wb), lambda g, b: (0, 0)),
                  row_spec, row_spec, row_spec, row_spec, t_spec, t_spec],
        out_specs=row_spec,
        out_shape=jax.ShapeDtypeStruct((rows, D_MODEL), F32),
        compiler_params=_cparams(("arbitrary", "arbitrary")),
        name="attn_sample",
    )(slope_arr, jnp.asarray(np.stack(pos), F32), q, kn, vn, sgb, cache_kt, cache_vt)


def _odd_front_sample_kernel(h_ref, *refs):
    ws32 = refs[:N_SEG_O]
    pw_ref, ps_ref, prev_ref = refs[N_SEG_O:N_SEG_O + 3]
    yc_ref, q_ref, k_ref, v_ref, sgd_ref, pn_ref = refs[N_SEG_O + 3:N_SEG_O + 9]
    w16_refs = refs[N_SEG_O + 9:]
    ws = [w[...].astype(BF16) for w in ws32]
    for w16_ref, w in zip(w16_refs, ws):
        w16_ref[...] = w
    wu, wgc, wq, wk, wv, wgd = ws
    j = pl.program_id(0)
    h = h_ref[...]
    u = _dot(h, wu)
    n_prev = prev_ref.shape[0]
    sums = []
    s = jnp.zeros_like(u)
    back = 0
    for w in POOL_SIZES:
        while back < w - 1:
            s = s + prev_ref[n_prev - 1 - back]
            back += 1
        sums.append(s)
    win = _pick_by_chunk(j, sums) + u
    inv_w = _pick_by_chunk(j, [1.0 / w for w in POOL_SIZES])
    pooled = win * inv_w - u
    mixed = _dot(pooled.astype(BF16), pw_ref[...]) * ps_ref[...]
    yc_ref[...] = (mixed * _silu(_dot(h, wgc))).astype(BF16)
    q_ref[...] = _dot(h, wq)
    k_ref[...] = _dot(h, wk) * (DK_D ** -0.5)
    v_ref[...] = _dot(h, wv)
    sgd_ref[...] = _silu(_dot(h, wgd))
    for t in range(n_prev - 1):
        pn_ref[t] = prev_ref[t + 1]
    pn_ref[n_prev - 1] = u


def _odd_front_sample(h2d, w_in, pw, ps, prev_t):
    rows = h2d.shape[0]
    n_prev = prev_t.shape[0]
    nj = D_MODEL // TN

    def wspec(s):
        return pl.BlockSpec((D_MODEL, TN), lambda j, s=s: (0, s * nj + j))

    col = pl.BlockSpec((rows, TN), lambda j: (0, j))
    st = pl.BlockSpec((n_prev, rows, TN), lambda j: (0, 0, j))
    colf = jax.ShapeDtypeStruct((rows, D_MODEL), F32)
    w16_spec = pl.BlockSpec((D_MODEL, TN), lambda j: (0, j))
    w16_shape = jax.ShapeDtypeStruct((D_MODEL, D_MODEL), BF16)
    outs = pl.pallas_call(
        _odd_front_sample_kernel,
        grid=(nj,),
        in_specs=[pl.BlockSpec((rows, D_MODEL), lambda j: (0, 0))] + [wspec(s) for s in range(N_SEG_O)] + [
            pl.BlockSpec((None, TN, TN), lambda j: (j, 0, 0)), pl.BlockSpec((1, TN), lambda j: (0, j)), st],
        out_specs=[col, col, col, col, col, st] + [w16_spec] * N_SEG_O,
        out_shape=[jax.ShapeDtypeStruct((rows, D_MODEL), BF16), colf, colf, colf, colf,
                   jax.ShapeDtypeStruct((n_prev, rows, D_MODEL), F32)] + [w16_shape] * N_SEG_O,
        compiler_params=_cparams(("arbitrary",)),
        name="odd_front_sample",
    )(h2d, *([w_in] * N_SEG_O), pw, ps.reshape(1, D_MODEL), prev_t)
    return outs[:6], outs[6:]


RET_SAMPLE_ROWS = 8


def _ret_sample_kernel(q_ref, k_ref, v_ref, sgd_ref, st_ref, yd_ref, sn_ref):
    row = lax.broadcasted_iota(jnp.int32, (DK_D, DK_D), 0)
    col = lax.broadcasted_iota(jnp.int32, (DK_D, DK_D), 1)
    qs, ks, vs, gs = q_ref[...], k_ref[...], v_ref[...], sgd_ref[...]
    out_rows = []
    for i in range(RET_SAMPLE_ROWS):
        out_heads = []
        for h in range(H_D):
            g = float(np.exp(RET_LOG_G[h]))
            cols = slice(h * DK_D, (h + 1) * DK_D)
            q, k, v = qs[i:i + 1, cols], ks[i:i + 1, cols], vs[i:i + 1, cols]
            state = st_ref[i, h]
            qk = jnp.sum(q * k, axis=-1, keepdims=True)
            q8 = jnp.broadcast_to(q, (8, DK_D)).astype(BF16)
            cross = _dot(q8, state.astype(BF16))[0:1, :]
            o = qk * v + g * cross
            k_diag = jnp.where(row == col, jnp.broadcast_to(k, (DK_D, DK_D)), 0.0).astype(BF16)
            v_rows = jnp.broadcast_to(v, (DK_D, DK_D)).astype(BF16)
            sn_ref[i, h] = g * state + _dot(k_diag, v_rows)
            out_heads.append(_rms(o) * gs[i:i + 1, cols])
        out_rows.append(jnp.concatenate(out_heads, axis=1))
    yd_ref[...] = jnp.concatenate(out_rows, axis=0)


def _ret_sample(q, k, v, sgd, state):
    rows = q.shape[0]
    rs = RET_SAMPLE_ROWS
    row_spec = pl.BlockSpec((rs, D_MODEL), lambda i: (i, 0))
    st_spec = pl.BlockSpec((rs, H_D, DK_D, DK_D), lambda i: (i, 0, 0, 0))
    return pl.pallas_call(
        _ret_sample_kernel,
        grid=(rows // rs,),
        in_specs=[row_spec, row_spec, row_spec, row_spec, st_spec],
        out_specs=[row_spec, st_spec],
        out_shape=[jax.ShapeDtypeStruct((rows, D_MODEL), F32),
                   jax.ShapeDtypeStruct((rows, H_D, DK_D, DK_D), F32)],
        compiler_params=_cparams(("arbitrary",)),
        name="ret_sample",
    )(q, k, v, sgd, state)


def kernel(x_prompt, x_sample, c_prompt, c_sample, state_conv, cache_win_k, cache_win_v, state_pool, state_ret,
           norm_e, ada_w_e, ada_b_e, w_in_e, conv_w, conv_b, w_out_e, norm_o, ada_w_o, ada_b_o, w_in_o,
           pool_w, pool_scale, w_out_o, norm_f):
    batch, seq, d = x_prompt.shape
    sb = x_sample.shape[0]
    assert d == D_MODEL and x_sample.shape[1] == 1
    assert norm_e.shape[0] == 1 and norm_o.shape[0] == 1
    wb = cache_win_k.shape[2]
    keep = min(wb, seq)
    assert wb == DILATIONS[-1] * NK and seq % (DILATIONS[-1] * NK) == 0
    tm = TM_OUT

    n_c = batch + sb
    pad = (-n_c) % 8
    c_all = jnp.concatenate([c_prompt, c_sample, jnp.zeros((pad, d), F32)], axis=0)
    mod_e, mod_o = _adaln(c_all, ada_w_e[0], ada_b_e[0], ada_w_o[0], ada_b_o[0])
    mod_e_p, mod_e_s = mod_e[:batch], mod_e[batch:n_c]
    mod_o_p, mod_o_s = mod_o[:batch], mod_o[batch:n_c]

    w_out_e16 = w_out_e[0].astype(BF16)
    w_out_o16 = w_out_o[0].astype(BF16)
    pool_w16 = pool_w[0].astype(BF16)

    xs2d = x_sample.reshape(sb, d)
    conv_prev_t = jnp.transpose(state_conv[0], (1, 0, 2))
    (ya_s, q_s, k_s, v_s, sgb_s, conv_s_t), w_e_segs = _even_front_sample(
        xs2d, mod_e_s, norm_e[0], w_in_e[0], conv_w[0], conv_b[0], conv_prev_t)
    cache_kt = jnp.transpose(cache_win_k[0], (0, 2, 3, 1)).reshape(sb, d, wb)
    cache_vt = jnp.transpose(cache_win_v[0], (0, 2, 3, 1)).reshape(sb, d, wb)
    yb_s = _attn_sample(q_s, k_s, v_s, sgb_s, cache_kt, cache_vt)
    x1_s, h1_s = _outproj(ya_s, yb_s, xs2d, mod_e_s, w_out_e16, norm_o[0], mod_o_s, sb, 1)
    pool_prev_t = jnp.transpose(state_pool[0], (1, 0, 2))
    (yc_s, rq_s, rk_s, rv_s, sgd_s, pool_s_t), w_o_segs = _odd_front_sample(
        h1_s, w_in_o[0], pool_w16, pool_scale[0], pool_prev_t)
    yd_s, ret_s = _ret_sample(rq_s, rk_s, rv_s, sgd_s, state_ret[0])
    y_sample = _outproj(yc_s, yd_s, x1_s, mod_o_s, w_out_o16, norm_f, None, sb, 1)

    xp2d = x_prompt.reshape(batch * seq, d)
    ya, q, k, v, sgb, k_new, v_new, conv_tail = _even_front_prompt(
        xp2d, mod_e_p, norm_e[0], w_e_segs, conv_w[0], conv_b[0], batch, seq, keep, TM_FRONT)
    yb = _attn_prompt(q, k, v, sgb, batch, seq).reshape(batch * seq, d)
    x1, h1 = _outproj(ya, yb, xp2d, mod_e_p, w_out_e16, norm_o[0], mod_o_p, tm, seq)
    yc, rq, rk, rv, sgd, pool_tail = _odd_front_prompt(h1, w_o_segs, pool_w16, pool_scale[0], batch, seq, TM_FRONT)
    yd, ret_p = _ret_prompt(rq, rk, rv, sgd, batch, seq)
    y_prompt = _outproj(yc, yd.reshape(batch * seq, d), x1, mod_o_p, w_out_o16, norm_f, None, tm, seq)

    return (
        y_prompt.reshape(batch, seq, d),
        y_sample.reshape(sb, 1, d),
        conv_tail[:, 6:8][None],
        jnp.transpose(conv_s_t, (1, 0, 2))[None],
        jnp.transpose(k_new.reshape(batch, H_B, HD_B, keep), (0, 3, 1, 2))[None],
        k_s.reshape(1, sb, 1, H_B, HD_B),
        jnp.transpose(v_new.reshape(batch, H_B, HD_B, keep), (0, 3, 1, 2))[None],
        v_s.reshape(1, sb, 1, H_B, HD_B),
        pool_tail[:, 1:][None],
        jnp.transpose(pool_s_t, (1, 0, 2))[None],
        ret_p[None],
        ret_s[None],
    )
```

```python
import functools

import numpy as np
import jax
import jax.numpy as jnp
from jax import lax
from jax.experimental import pallas as pl
from jax.experimental.pallas import tpu as pltpu

F32 = jnp.float32
BF16 = jnp.bfloat16
WORD = jnp.uint32

D_MODEL = 1024
EPS = 1e-6
H_B = 16
HD_B = 64
N_PAIR = H_B // 2
LANES = 128
NK = 128
DILATIONS = (1, 4, 16)
POOL_SIZES = (2, 4, 8, 16)
POOL_PREV = 16
H_D = 4
DK_D = 256
RET_CHUNK = 256
TN = 256
TM_FRONT = 1024
TM_OUT = 1024
MASK_DIST = 1e9
LOG2E = float(np.log2(np.e))
MASK_ADD = 1e30
VMEM_LIMIT = 56 * 1024 * 1024

RET_LOG_G = [float(np.log(1.0 - 2.0 ** (-5.0 - h))) for h in range(H_D)]


def _cparams(sem):
    return pltpu.CompilerParams(dimension_semantics=sem, vmem_limit_bytes=VMEM_LIMIT)


def _silu(x):
    return x * (1.0 / (1.0 + jnp.exp(-x)))


def _dot(a, b):
    return jnp.dot(a, b, preferred_element_type=F32)


def _dot_nt(a, b):
    return lax.dot_general(a, b, (((1,), (1,)), ((), ())), preferred_element_type=F32)


def _rms(x):
    return x * lax.rsqrt(jnp.mean(x * x, axis=-1, keepdims=True) + EPS)


def _adaln_kernel(c_ref, we_ref, be_ref, wo_ref, bo_ref, me_ref, mo_ref):
    sc = _silu(c_ref[...]).astype(BF16)
    me_ref[...] = _dot(sc, we_ref[...].astype(BF16)) + be_ref[...]
    mo_ref[...] = _dot(sc, wo_ref[...].astype(BF16)) + bo_ref[...]


def _adaln(c_all, we, be, wo, bo):
    rows = c_all.shape[0]
    tn = 512
    n = 3 * D_MODEL
    wspec = pl.BlockSpec((D_MODEL, tn), lambda j: (0, j))
    bspec = pl.BlockSpec((1, tn), lambda j: (0, j))
    ospec = pl.BlockSpec((rows, tn), lambda j: (0, j))
    return pl.pallas_call(
        _adaln_kernel,
        grid=(n // tn,),
        in_specs=[pl.BlockSpec((rows, D_MODEL), lambda j: (0, 0)), wspec, bspec, wspec, bspec],
        out_specs=[ospec, ospec],
        out_shape=[jax.ShapeDtypeStruct((rows, n), F32)] * 2,
        compiler_params=_cparams(("arbitrary",)),
        name="adaln",
    )(c_all, we, be.reshape(1, n), wo, bo.reshape(1, n))


def _norm_mod(x, g, mod):
    shift = mod[:, 0:D_MODEL]
    scale = mod[:, D_MODEL:2 * D_MODEL]
    return _rms(x) * g * (1.0 + scale) + shift


def _shift_rows(u, k, prev_rows):
    row = lax.broadcasted_iota(jnp.int32, u.shape, 0)
    out = pltpu.roll(u, k, 0)
    for idx, pr in enumerate(prev_rows):
        out = jnp.where(row == idx, pr, out)
    return out


N_SEG_E = 8
PHASES_E = 16


def _phase_segments(t, n_seg, n_phases):
    return range(t * n_seg // n_phases, (t + 1) * n_seg // n_phases)


def _even_front_kernel(x_ref, mod_ref, g_ref, *refs, tiles_per_batch, n_chunks):
    ws = refs[:N_SEG_E]
    cw_ref, cb_ref = refs[N_SEG_E:N_SEG_E + 2]
    ya_ref, q_ref, k_ref, v_ref, sgb_ref, kn_ref, vn_ref, cn_ref = refs[N_SEG_E + 2:N_SEG_E + 10]
    h_scr, carry_scr, z_a, z_b = refs[N_SEG_E + 10:]
    nj = D_MODEL // TN
    s = pl.program_id(0)
    c_out = jnp.maximum(s - 1, 0)
    j_out = c_out % nj
    il_out = (c_out // nj) % tiles_per_batch

    @pl.when(s == 0)
    def _():
        z_b[...] = jnp.zeros_like(z_b)
        carry_scr[...] = jnp.zeros_like(carry_scr)

    @pl.when((s < n_chunks) & (s % nj == 0))
    def _():
        h_scr[...] = _norm_mod(x_ref[...], g_ref[...], mod_ref[...]).astype(BF16)

    tm = h_scr.shape[0]
    rb = tm // PHASES_E

    def phase(t, z_w, z_r):
        for seg in _phase_segments(t, N_SEG_E, PHASES_E):
            z_w[seg] = _dot(h_scr[...], ws[seg][...])
        rows = slice(t * rb, (t + 1) * rb)
        bg, cg, xv, ga, q, k, v, gb = (z_r[seg, rows, :] for seg in range(N_SEG_E))
        u = cg * xv
        prev = carry_scr[j_out]
        if t == 0:
            prev = jnp.where(il_out == 0, 0.0, prev)
        p2, p1 = prev[6:7, :], prev[7:8, :]
        u1 = _shift_rows(u, 1, [p1])
        u2 = _shift_rows(u, 2, [p2, p1])
        cw = cw_ref[...]
        conv = cb_ref[...] + cw[0:1, :] * u2 + cw[1:2, :] * u1 + cw[2:3, :] * u
        tail = u[rb - 8:rb, :]
        carry_scr[j_out] = tail
        if t == PHASES_E - 1:
            cn_ref[...] = tail
        ya_ref[rows, :] = (bg * conv * _silu(ga)).astype(BF16)
        kn_ref[:, rows] = k.T
        vn_ref[:, rows] = v.T
        wrows = slice(t * rb // 2, (t + 1) * rb // 2)
        for ref, val in ((q_ref, q * (HD_B ** -0.5 * LOG2E)),
                         (k_ref, k), (v_ref, v)):
            words = pltpu.bitcast(val.astype(BF16), WORD)
            for e in range(TN // LANES):
                ref[e, wrows, :] = words[:, e * LANES:(e + 1) * LANES]
        sgb = _silu(gb).astype(BF16)
        for e in range(TN // LANES):
            sgb_ref[e, rows, :] = sgb[:, e * LANES:(e + 1) * LANES]

    for parity, (z_w, z_r) in enumerate(((z_a, z_b), (z_b, z_a))):
        for t in range(PHASES_E):
            pl.when(s % 2 == parity)(functools.partial(phase, t, z_w, z_r))


def _even_front_prompt(x2d, mod, g, w_segs, cw, cb, batch, seq, keep, tm):
    rows = batch * seq
    tpb = seq // tm
    nj = D_MODEL // TN
    n_chunks = (rows // tm) * nj
    off = (seq - keep) // tm
    ppc = TN // LANES

    def c_in(s):
        return jnp.minimum(s, n_chunks - 1)

    def c_out(s):
        return jnp.maximum(s - 1, 0)

    def out_ij(s):
        c = c_out(s)
        return c // nj, c % nj

    def pair_map(s):
        i, j = out_ij(s)
        return (i // tpb, j, i % tpb, 0)

    def keep_map(s):
        i, j = out_ij(s)
        il = i % tpb
        kept = il >= off
        return (i // tpb, jnp.where(kept, j, 0), jnp.where(kept, il - off, 0))

    def tail_map(s):
        i, j = out_ij(s)
        return (i // tpb, 0, jnp.where(i % tpb == tpb - 1, j, 0))

    pair_spec = pl.BlockSpec((None, ppc, tm, LANES), pair_map)
    pair_shape = jax.ShapeDtypeStruct((batch, N_PAIR, seq, LANES), BF16)
    word_spec = pl.BlockSpec((None, ppc, tm // 2, LANES), pair_map)
    word_shape = jax.ShapeDtypeStruct((batch, N_PAIR, seq // 2, LANES), WORD)
    keep_spec = pl.BlockSpec((None, TN, tm), keep_map)
    keep_shape = jax.ShapeDtypeStruct((batch, D_MODEL, keep), F32)
    return pl.pallas_call(
        functools.partial(_even_front_kernel, tiles_per_batch=tpb, n_chunks=n_chunks),
        grid=(n_chunks + 1,),
        in_specs=[
            pl.BlockSpec((tm, D_MODEL), lambda s: (c_in(s) // nj, 0)),
            pl.BlockSpec((None, 1, 3 * D_MODEL), lambda s: (c_in(s) // nj // tpb, 0, 0)),
            pl.BlockSpec((1, D_MODEL), lambda s: (0, 0)),
        ] + [pl.BlockSpec((D_MODEL, TN), lambda s: (0, c_in(s) % nj))] * N_SEG_E + [
            pl.BlockSpec((3, TN), lambda s: (0, c_out(s) % nj)),
            pl.BlockSpec((1, TN), lambda s: (0, c_out(s) % nj)),
        ],
        out_specs=[
            pl.BlockSpec((tm, TN), lambda s: out_ij(s)),
            word_spec, word_spec, word_spec, pair_spec,
            keep_spec, keep_spec,
            pl.BlockSpec((None, 8, TN), tail_map),
        ],
        out_shape=[
            jax.ShapeDtypeStruct((rows, D_MODEL), BF16),
            word_shape, word_shape, word_shape, pair_shape,
            keep_shape, keep_shape,
            jax.ShapeDtypeStruct((batch, 8, D_MODEL), F32),
        ],
        scratch_shapes=[pltpu.VMEM((tm, D_MODEL), BF16), pltpu.VMEM((nj, 8, TN), F32),
                        pltpu.VMEM((N_SEG_E, tm, TN), F32), pltpu.VMEM((N_SEG_E, tm, TN), F32)],
        compiler_params=_cparams(("arbitrary",)),
        name="even_front_prompt",
    )(x2d, mod.reshape(batch, 1, 3 * D_MODEL), g.reshape(1, D_MODEL), *w_segs, cw, cb.reshape(1, D_MODEL))


STAT_PITCH = {1: 1, 4: 4, 16: 24}
ATTN_UNROLL = 32


def _residue_stream(src, r):
    n_out = src.shape[0] // 4
    halves = [pltpu.unpack_elementwise(src[pl.ds(r // 2 + off, n_out, stride=4), :], index=r % 2,
                                       packed_dtype=BF16, unpacked_dtype=F32) for off in (0, 2)]
    return pltpu.pack_elementwise(halves, packed_dtype=BF16)


def _attn_prompt_kernel(slope_ref, dist_ref, q_ref, k_ref, v_ref, sgb_ref, yb_ref,
                        q4_scr, q16_scr, kv4_scr, kv16_scr, acc_scr, m_scr, l_scr, bias_scr, *, seq):
    hp = pl.program_id(1)
    lane = lax.broadcasted_iota(jnp.int32, (1, LANES), 1)
    half0 = lane < HD_B
    dist = dist_ref[...]
    masked = jnp.full((NK, NK), MASK_DIST, F32)
    for p, d in enumerate(DILATIONS):
        for e in range(2):
            bias = dist * (slope_ref[2 * hp + e] * (float(d) * LOG2E))
            bias_scr[2 * p + e] = bias
            bias_scr[6 + 2 * p + e] = jnp.concatenate([bias[:, NK:], masked], axis=1)

    for a, ref in enumerate((q_ref, k_ref, v_ref)):
        for r in range(4):
            s = _residue_stream(ref, r)
            if a == 0:
                q4_scr[r] = s
            else:
                kv4_scr[a - 1, r] = s
        for r in range(16):
            s = _residue_stream(q4_scr.at[r % 4] if a == 0 else kv4_scr.at[a - 1, r % 4], r // 4)
            if a == 0:
                q16_scr[r] = s
            else:
                kv16_scr[a - 1, r] = s

    ones = jnp.ones((2 * NK, LANES), BF16)
    hw = NK // 2

    def tile(p, d, r, n):
        qw = pl.ds(pl.multiple_of(n * hw, hw), hw)
        kw = pl.ds(pl.multiple_of(jnp.maximum(n - 1, 0) * hw, hw), 2 * hw)
        if d == 1:
            qs, ks, vs = q_ref, k_ref, v_ref
        elif d == 4:
            qs, ks, vs = q4_scr.at[r], kv4_scr.at[0, r], kv4_scr.at[1, r]
        else:
            qs, ks, vs = q16_scr.at[r], kv16_scr.at[0, r], kv16_scr.at[1, r]
        q = pltpu.bitcast(qs[qw, :], BF16)
        kk = pltpu.bitcast(ks[kw, :], BF16)
        vv1 = jnp.concatenate([pltpu.bitcast(vs[kw, :], BF16), ones], axis=1)
        first = jnp.where(n == 0, 1, 0)
        res = []
        for e in range(2):
            qe = jnp.where(half0 if e == 0 else jnp.logical_not(half0), q, jnp.zeros_like(q))
            s = _dot_nt(qe, kk) - bias_scr[6 * first + 2 * p + e]
            m = jnp.max(s, axis=-1, keepdims=True)
            pe = jnp.exp2(s - m).astype(BF16)
            res.append((_dot(pe, vv1), m))
        (a0, m0), (a1, m1) = res
        acc = jnp.where(half0, a0[:, :LANES], a1[:, :LANES])
        l = jnp.where(half0, a0[:, LANES:], a1[:, LANES:])
        m = jnp.where(half0, m0, m1)
        if d > 1:
            pitch = STAT_PITCH[d]
            rows = pl.ds(r + pitch * NK * n, NK, stride=pitch)
            acc_scr[p - 1, rows, :] = acc
            l_scr[p - 1, rows, :] = l
            m_scr[p - 1, rows, :] = m
            return
        accs, ls, ms = [acc], [l], [m]
        for pp in range(1, len(DILATIONS)):
            dd = DILATIONS[pp]
            for ref, vals in ((acc_scr, accs), (l_scr, ls), (m_scr, ms)):
                vals.append(jnp.concatenate(
                    [ref[pp - 1, pl.ds(pl.multiple_of((n * (NK // dd) + g) * STAT_PITCH[dd], 8), dd), :]
                     for g in range(NK // dd)], axis=0) if STAT_PITCH[dd] != dd
                    else ref[pp - 1, pl.ds(pl.multiple_of(n * NK, NK), NK), :])
        top = jnp.maximum(jnp.maximum(ms[0], ms[1]), ms[2])
        ws = [jnp.exp2(x - top) for x in ms]
        num = ws[0] * accs[0] + ws[1] * accs[1] + ws[2] * accs[2]
        den = ws[0] * ls[0] + ws[1] * ls[1] + ws[2] * ls[2]
        rows = pl.ds(pl.multiple_of(n * NK, NK), NK)
        yb_ref[rows, :] = (num * (1.0 / den) * sgb_ref[rows, :].astype(F32)).astype(BF16)

    for p, d in reversed(list(enumerate(DILATIONS))):
        def body(it, c, p=p, d=d):
            for u in range(ATTN_UNROLL):
                t = it * ATTN_UNROLL + u
                if d == 1:
                    tile(p, d, 0, t)
                else:
                    tile(p, d, t % d, t // d)
            return c

        lax.fori_loop(0, seq // NK // ATTN_UNROLL, body, 0)


def _band_distance():
    qi = np.arange(NK)[:, None]
    kj = np.arange(2 * NK)[None, :]
    dist = NK + qi - kj
    return jnp.asarray(np.where((dist >= 0) & (dist <= NK), dist, MASK_DIST), dtype=F32)


def _alibi_slopes():
    return jnp.asarray(2.0 ** (-8.0 * np.arange(1, H_B + 1) / H_B), dtype=F32)


def _attn_prompt(q, k, v, sgb, batch, seq):
    word_spec = pl.BlockSpec((None, None, seq // 2, LANES), lambda b, hp: (b, hp, 0, 0))
    stat_rows = max(seq // d * STAT_PITCH[d] for d in DILATIONS)
    stat = pltpu.VMEM((len(DILATIONS) - 1, stat_rows, LANES), F32)
    return pl.pallas_call(
        functools.partial(_attn_prompt_kernel, seq=seq),
        grid=(batch, N_PAIR),
        in_specs=[pl.BlockSpec(memory_space=pltpu.SMEM),
                  pl.BlockSpec((NK, 2 * NK), lambda b, hp: (0, 0)),
                  word_spec, word_spec, word_spec,
                  pl.BlockSpec((None, None, seq, LANES), lambda b, hp: (b, hp, 0, 0))],
        out_specs=pl.BlockSpec((None, seq, LANES), lambda b, hp: (b, 0, hp)),
        out_shape=jax.ShapeDtypeStruct((batch, seq, D_MODEL), BF16),
        scratch_shapes=[pltpu.VMEM((4, seq // 8, LANES), WORD),
                        pltpu.VMEM((16, seq // 32, LANES), WORD),
                        pltpu.VMEM((2, 4, seq // 8, LANES), WORD),
                        pltpu.VMEM((2, 16, seq // 32, LANES), WORD),
                        stat, stat, stat,
                        pltpu.VMEM((12, NK, 2 * NK), F32)],
        compiler_params=_cparams(("arbitrary", "arbitrary")),
        name="attn_prompt",
    )(_alibi_slopes(), _band_distance(), q, k, v, sgb)


def _outproj_mid_kernel(ya_ref, yb_ref, x_ref, mod_ref, w1_ref, w2_ref, g_ref, mod2_ref, x1_ref, h_ref):
    gate = mod_ref[:, 2 * D_MODEL:3 * D_MODEL]
    x1 = x_ref[...] + gate * (_dot(ya_ref[...], w1_ref[...]) + _dot(yb_ref[...].astype(BF16), w2_ref[...]))
    x1_ref[...] = x1
    h_ref[...] = _norm_mod(x1, g_ref[...], mod2_ref[...]).astype(BF16)


def _outproj_final_kernel(ya_ref, yb_ref, x_ref, mod_ref, w1_ref, w2_ref, g_ref, y_ref):
    gate = mod_ref[:, 2 * D_MODEL:3 * D_MODEL]
    x1 = x_ref[...] + gate * (_dot(ya_ref[...], w1_ref[...]) + _dot(yb_ref[...].astype(BF16), w2_ref[...]))
    y_ref[...] = _rms(x1) * g_ref[...]


def _outproj(ya, yb, x2d, mod, w_out, g, mod2, tm, rows_per_mod):
    rows = x2d.shape[0]
    half = w_out.shape[0] // 2
    row_spec = pl.BlockSpec((tm, D_MODEL), lambda i: (i, 0))
    if rows_per_mod == 1:
        mod_spec = pl.BlockSpec((tm, 3 * D_MODEL), lambda i: (i, 0))
        mods = (mod, mod2)
    else:
        tpb = rows_per_mod // tm
        mod_spec = pl.BlockSpec((None, 1, 3 * D_MODEL), lambda i: (i // tpb, 0, 0))
        mods = tuple(None if m is None else m.reshape(-1, 1, 3 * D_MODEL) for m in (mod, mod2))
    w1_spec = pl.BlockSpec((half, D_MODEL), lambda i: (0, 0))
    w2_spec = pl.BlockSpec((half, D_MODEL), lambda i: (1, 0))
    g_spec = pl.BlockSpec((1, D_MODEL), lambda i: (0, 0))
    common = dict(grid=(rows // tm,), compiler_params=_cparams(("arbitrary",)))
    if mod2 is None:
        return pl.pallas_call(
            _outproj_final_kernel,
            in_specs=[row_spec, row_spec, row_spec, mod_spec, w1_spec, w2_spec, g_spec],
            out_specs=row_spec,
            out_shape=jax.ShapeDtypeStruct((rows, D_MODEL), F32),
            name="outproj_final", **common,
        )(ya, yb, x2d, mods[0], w_out, w_out, g.reshape(1, D_MODEL))
    return pl.pallas_call(
        _outproj_mid_kernel,
        in_specs=[row_spec, row_spec, row_spec, mod_spec, w1_spec, w2_spec, g_spec, mod_spec],
        out_specs=[row_spec, row_spec],
        out_shape=[jax.ShapeDtypeStruct((rows, D_MODEL), F32), jax.ShapeDtypeStruct((rows, D_MODEL), BF16)],
        name="outproj_mid", **common,
    )(ya, yb, x2d, mods[0], w_out, w_out, g.reshape(1, D_MODEL), mods[1])


def _pick_by_chunk(j, vals):
    out = vals[-1]
    for idx in range(len(vals) - 2, -1, -1):
        out = jnp.where(j == idx, vals[idx], out)
    return out


N_SEG_O = 6
PHASES_O = 4


def _odd_front_kernel(h_ref, *refs, tiles_per_batch):
    ws = refs[:N_SEG_O]
    pw_ref, ps_ref = refs[N_SEG_O:N_SEG_O + 2]
    yc_ref, q_ref, kt_ref, v_ref, sgd_ref, pn_ref = refs[N_SEG_O + 2:N_SEG_O + 8]
    carry_scr, z_a, z_b = refs[N_SEG_O + 8:]
    nj = D_MODEL // TN
    s = pl.program_id(0)
    c_out = jnp.maximum(s - 1, 0)
    j_out = c_out % nj
    il_out = (c_out // nj) % tiles_per_batch
    tm = h_ref.shape[0]
    rb = tm // PHASES_O

    @pl.when(s == 0)
    def _():
        z_b[...] = jnp.zeros_like(z_b)
        carry_scr[...] = jnp.zeros_like(carry_scr)

    def phase(t, z_w, z_r):
        for seg in _phase_segments(t, N_SEG_O, PHASES_O):
            z_w[seg] = _dot(h_ref[...], ws[seg][...])
        rows = slice(t * rb, (t + 1) * rb)
        u, gc, q, k, v, gd = (z_r[seg, rows, :] for seg in range(N_SEG_O))
        prev = carry_scr[j_out]
        if t == 0:
            prev = jnp.where(il_out == 0, 0.0, prev)
        tail = u[rb - POOL_PREV:rb, :]
        carry_scr[j_out] = tail
        if t == PHASES_O - 1:
            pn_ref[...] = tail
        ext = jnp.concatenate([prev, u], axis=0)
        sums = []
        acc = ext
        for sh in (1, 2, 4, 8):
            acc = acc + pltpu.roll(acc, sh, 0)
            sums.append(acc[POOL_PREV:, :])
        win = _pick_by_chunk(j_out, sums)
        width = _pick_by_chunk(j_out, [float(w) for w in POOL_SIZES])
        pos = (il_out * tm + t * rb + lax.broadcasted_iota(jnp.int32, (rb, 1), 0)).astype(F32)
        pooled = win / jnp.minimum(width, pos + 1.0) - u
        mixed = _dot(pooled.astype(BF16), pw_ref[...]) * ps_ref[...]
        yc_ref[rows, :] = (mixed * _silu(gc)).astype(BF16)
        q_ref[rows, :] = q.astype(BF16)
        kt_ref[:, rows] = (k * (DK_D ** -0.5)).T.astype(BF16)
        v_ref[rows, :] = v.astype(BF16)
        sgd_ref[rows, :] = _silu(gd).astype(BF16)

    for parity, (z_w, z_r) in enumerate(((z_a, z_b), (z_b, z_a))):
        for t in range(PHASES_O):
            pl.when(s % 2 == parity)(functools.partial(phase, t, z_w, z_r))


def _odd_front_prompt(h2d, w_segs, pw, ps, batch, seq, tm):
    rows = batch * seq
    tpb = seq // tm
    nj = D_MODEL // TN
    n_chunks = (rows // tm) * nj

    def c_in(s):
        return jnp.minimum(s, n_chunks - 1)

    def out_ij(s):
        c = jnp.maximum(s - 1, 0)
        return c // nj, c % nj

    def head_map(s):
        i, j = out_ij(s)
        return (i // tpb, j, i % tpb, 0)

    def kt_map(s):
        i, j = out_ij(s)
        return (i // tpb, j, 0, i % tpb)

    def tail_map(s):
        i, j = out_ij(s)
        return (i // tpb, 0, jnp.where(i % tpb == tpb - 1, j, 0))

    head_spec = pl.BlockSpec((None, None, tm, TN), head_map)
    head_shape = jax.ShapeDtypeStruct((batch, H_D, seq, DK_D), BF16)
    tok_spec = pl.BlockSpec((tm, TN), out_ij)
    tok_shape = jax.ShapeDtypeStruct((rows, D_MODEL), BF16)
    zbuf = pltpu.VMEM((N_SEG_O, tm, TN), F32)
    return pl.pallas_call(
        functools.partial(_odd_front_kernel, tiles_per_batch=tpb),
        grid=(n_chunks + 1,),
        in_specs=[pl.BlockSpec((tm, D_MODEL), lambda s: (c_in(s) // nj, 0))] + [
                  pl.BlockSpec((D_MODEL, TN), lambda s: (0, c_in(s) % nj))] * N_SEG_O + [
                  pl.BlockSpec((None, TN, TN), lambda s: (out_ij(s)[1], 0, 0)),
                  pl.BlockSpec((1, TN), lambda s: (0, out_ij(s)[1]))],
        out_specs=[tok_spec, head_spec, pl.BlockSpec((None, None, TN, tm), kt_map), head_spec, tok_spec,
                   pl.BlockSpec((None, POOL_PREV, TN), tail_map)],
        out_shape=[tok_shape, head_shape, jax.ShapeDtypeStruct((batch, H_D, DK_D, seq), BF16),
                   head_shape, tok_shape,
                   jax.ShapeDtypeStruct((batch, POOL_PREV, D_MODEL), F32)],
        scratch_shapes=[pltpu.VMEM((nj, POOL_PREV, TN), F32), zbuf, zbuf],
        compiler_params=_cparams(("arbitrary",)),
        name="odd_front_prompt",
    )(h2d, *w_segs, pw, ps.reshape(1, D_MODEL))


RET_SBLK = 1024


def _ret_prompt_kernel(q_ref, kt_ref, v_ref, sgd_ref, dec_ref, cdec_ref, kdec_ref,
                       yd_ref, st_ref, state_scr, *, sblk, n_sblk):
    sb = pl.program_id(1)

    @pl.when(sb == 0)
    def _():
        state_scr[...] = jnp.zeros_like(state_scr)

    def chunk(c, carry):
        rows = pl.ds(pl.multiple_of(c * RET_CHUNK, RET_CHUNK), RET_CHUNK)
        for h in range(H_D):
            cols = slice(h * DK_D, (h + 1) * DK_D)
            q = q_ref[h, rows, :]
            kt = kt_ref[h, :, rows]
            v = v_ref[h, rows, :]
            state = state_scr[h]
            scores = _dot(q, kt) * dec_ref[h]
            o = _dot(scores.astype(BF16), v) + _dot(q, state.astype(BF16)) * cdec_ref[h]
            kd = (kt.astype(F32) * kdec_ref[h]).astype(BF16)
            state_scr[h] = float(np.exp(RET_CHUNK * RET_LOG_G[h])) * state + _dot(kd, v)
            yd_ref[rows, cols] = (_rms(o) * sgd_ref[rows, cols].astype(F32)).astype(BF16)
        return carry

    lax.fori_loop(0, sblk // RET_CHUNK, chunk, 0, unroll=4)

    @pl.when(sb == n_sblk - 1)
    def _():
        st_ref[...] = state_scr[...]


def _ret_consts():
    t = np.arange(RET_CHUNK, dtype=np.float64)
    diff = t[:, None] - t[None, :]
    lg = np.asarray(RET_LOG_G)[:, None, None]
    dec = np.where(diff >= 0, np.exp(np.maximum(diff, 0.0)[None] * lg), 0.0)
    cdec = np.broadcast_to(np.exp((t + 1.0)[None, :, None] * lg), (H_D, RET_CHUNK, DK_D))
    kdec = np.broadcast_to(np.exp((RET_CHUNK - 1.0 - t)[None, None, :] * lg), (H_D, DK_D, RET_CHUNK))
    return jnp.asarray(dec, F32), jnp.asarray(cdec, F32), jnp.asarray(kdec, F32)


def _ret_prompt(q, kt, v, sgd, batch, seq):
    dec, cdec, kdec = _ret_consts()
    n_sblk = seq // RET_SBLK
    head_spec = pl.BlockSpec((None, H_D, RET_SBLK, DK_D), lambda b, s: (b, 0, s, 0))
    tok_spec = pl.BlockSpec((None, RET_SBLK, D_MODEL), lambda b, s: (b, s, 0))

    def const_spec(shape):
        return pl.BlockSpec(shape, lambda b, s: (0, 0, 0))

    return pl.pallas_call(
        functools.partial(_ret_prompt_kernel, sblk=RET_SBLK, n_sblk=n_sblk),
        grid=(batch, n_sblk),
        in_specs=[head_spec, pl.BlockSpec((None, H_D, DK_D, RET_SBLK), lambda b, s: (b, 0, 0, s)), head_spec,
                  tok_spec, const_spec(dec.shape), const_spec(cdec.shape), const_spec(kdec.shape)],
        out_specs=[tok_spec, pl.BlockSpec((None, H_D, DK_D, DK_D), lambda b, s: (b, 0, 0, 0))],
        out_shape=[jax.ShapeDtypeStruct((batch, seq, D_MODEL), BF16),
                   jax.ShapeDtypeStruct((batch, H_D, DK_D, DK_D), F32)],
        scratch_shapes=[pltpu.VMEM((H_D, DK_D, DK_D), F32)],
        compiler_params=_cparams(("arbitrary", "arbitrary")),
        name="ret_prompt",
    )(q, kt, v, sgd.reshape(batch, seq, D_MODEL), dec, cdec, kdec)


def _even_front_sample_kernel(x_ref, mod_ref, g_ref, *refs):
    ws32 = refs[:N_SEG_E]
    cw_ref, cb_ref, prev_ref = refs[N_SEG_E:N_SEG_E + 3]
    ya_ref, q_ref, k_ref, v_ref, sgb_ref, cn_ref = refs[N_SEG_E + 3:N_SEG_E + 9]
    w16_refs = refs[N_SEG_E + 9:]
    ws = [w[...].astype(BF16) for w in ws32]
    for w16_ref, w in zip(w16_refs, ws):
        w16_ref[...] = w
    wbg, wcg, wxv, wga, wq, wk, wv, wgb = ws
    h = _norm_mod(x_ref[...], g_ref[...], mod_ref[...]).astype(BF16)
    u = _dot(h, wcg) * _dot(h, wxv)
    cw = cw_ref[...]
    conv = cb_ref[...] + cw[0:1, :] * prev_ref[0] + cw[1:2, :] * prev_ref[1] + cw[2:3, :] * u
    cn_ref[0] = prev_ref[1]
    cn_ref[1] = u
    ya_ref[...] = (_dot(h, wbg) * conv * _silu(_dot(h, wga))).astype(BF16)
    q_ref[...] = _dot(h, wq) * (HD_B ** -0.5)
    k_ref[...] = _dot(h, wk)
    v_ref[...] = _dot(h, wv)
    sgb_ref[...] = _silu(_dot(h, wgb))


def _even_front_sample(x2d, mod, g, w_in, cw, cb, prev_t):
    rows = x2d.shape[0]
    nj = D_MODEL // TN

    def wspec(s):
        return pl.BlockSpec((D_MODEL, TN), lambda j, s=s: (0, s * nj + j))

    full = pl.BlockSpec((rows, D_MODEL), lambda j: (0, 0))
    col = pl.BlockSpec((rows, TN), lambda j: (0, j))
    st = pl.BlockSpec((2, rows, TN), lambda j: (0, 0, j))
    colf = jax.ShapeDtypeStruct((rows, D_MODEL), F32)
    w16_spec = pl.BlockSpec((D_MODEL, TN), lambda j: (0, j))
    w16_shape = jax.ShapeDtypeStruct((D_MODEL, D_MODEL), BF16)
    outs = pl.pallas_call(
        _even_front_sample_kernel,
        grid=(nj,),
        in_specs=[full, pl.BlockSpec((rows, 3 * D_MODEL), lambda j: (0, 0)),
                  pl.BlockSpec((1, D_MODEL), lambda j: (0, 0))] + [wspec(s) for s in range(N_SEG_E)] + [
            pl.BlockSpec((3, TN), lambda j: (0, j)), pl.BlockSpec((1, TN), lambda j: (0, j)), st],
        out_specs=[col, col, col, col, col, st] + [w16_spec] * N_SEG_E,
        out_shape=[jax.ShapeDtypeStruct((rows, D_MODEL), BF16), colf, colf, colf, colf,
                   jax.ShapeDtypeStruct((2, rows, D_MODEL), F32)] + [w16_shape] * N_SEG_E,
        compiler_params=_cparams(("arbitrary",)),
        name="even_front_sample",
    )(x2d, mod, g.reshape(1, D_MODEL), *([w_in] * N_SEG_E), cw, cb.reshape(1, D_MODEL), prev_t)
    return outs[:6], outs[6:]


def _attn_sample_kernel(slope_ref, pos_ref, q_ref, kn_ref, vn_ref, sgb_ref, kt_ref, vt_ref, yb_ref):
    gw = q_ref.shape[-1]
    me = pl.ds(pl.program_id(1) % 8, 1)
    gh = slope_ref.shape[0]
    row = lax.broadcasted_iota(jnp.int32, (gh, gw), 0)
    col = lax.broadcasted_iota(jnp.int32, (gh, gw), 1)
    own = (col >= row * HD_B) & (col < (row + 1) * HD_B)
    qm = jnp.where(own, q_ref[me, :], 0.0)
    s_self = jnp.sum(qm * kn_ref[me, :], axis=-1, keepdims=True)
    s_all = _dot(qm.astype(BF16), kt_ref[...].astype(BF16))
    s_all = s_all - slope_ref[:, 0:1] * pos_ref[0:1, :]
    v_self = vn_ref[me, :]
    probs, stats = [], []
    for p in range(len(DILATIONS)):
        s = s_all - pos_ref[p + 1:p + 2, :]
        m = jnp.maximum(jnp.max(s, axis=-1, keepdims=True), s_self)
        pe = jnp.exp(s - m)
        p_self = jnp.exp(s_self - m)
        probs.append(pe)
        stats.append((m, jnp.sum(pe, axis=-1, keepdims=True) + p_self, p_self))
    o_all = _dot_nt(jnp.concatenate(probs, axis=0).astype(BF16), vt_ref[...].astype(BF16))
    outs, lses = [], []
    for p, (m, l, p_self) in enumerate(stats):
        outs.append((o_all[gh * p:gh * (p + 1), :] + p_self * v_self) * (1.0 / l))
        lses.append(m + jnp.log(l))
    m = jnp.maximum(jnp.maximum(lses[0], lses[1]), lses[2])
    ws = [jnp.exp(x - m) for x in lses]
    o = (ws[0] * outs[0] + ws[1] * outs[1] + ws[2] * outs[2]) * (1.0 / (ws[0] + ws[1] + ws[2]))
    o = jnp.sum(jnp.where(own, o, 0.0), axis=0, keepdims=True)
    yb_ref[me, :] = o * sgb_ref[me, :]


def _attn_sample(q, kn, vn, sgb, cache_kt, cache_vt):
    rows, _, wb = cache_kt.shape
    gh = H_B
    gw = gh * HD_B
    ng = H_B // gh
    back = wb - np.arange(wb, dtype=np.float64)
    pos = [back] + [np.where((back % d == 0) & (back <= NK * d), 0.0, MASK_ADD) for d in DILATIONS]
    slopes = (2.0 ** (-8.0 * np.arange(1, H_B + 1) / H_B)).reshape(ng, gh)
    slope_arr = jnp.asarray(np.broadcast_to(slopes[:, :, None], (ng, gh, LANES)), F32)
    row_spec = pl.BlockSpec((8, gw), lambda g, b: (b // 8, g))
    t_spec = pl.BlockSpec((None, gw, wb), lambda g, b: (b, g, 0))
    return pl.pallas_call(
        _attn_sample_kernel,
        grid=(ng, rows),
        in_specs=[pl.BlockSpec((None, gh, LANES), lambda g, b: (g, 0, 0)),
                  pl.BlockSpec((4, wb), lambda g, b: (0, 0)),
                  row_spec, row_spec, row_spec, row_spec, t_spec, t_spec],
        out_specs=row_spec,
        out_shape=jax.ShapeDtypeStruct((rows, D_MODEL), F32),
        compiler_params=_cparams(("arbitrary", "arbitrary")),
        name="attn_sample",
    )(slope_arr, jnp.asarray(np.stack(pos), F32), q, kn, vn, sgb, cache_kt, cache_vt)


def _odd_front_sample_kernel(h_ref, *refs):
    ws32 = refs[:N_SEG_O]
    pw_ref, ps_ref, prev_ref = refs[N_SEG_O:N_SEG_O + 3]
    yc_ref, q_ref, k_ref, v_ref, sgd_ref, pn_ref = refs[N_SEG_O + 3:N_SEG_O + 9]
    w16_refs = refs[N_SEG_O + 9:]
    ws = [w[...].astype(BF16) for w in ws32]
    for w16_ref, w in zip(w16_refs, ws):
        w16_ref[...] = w
    wu, wgc, wq, wk, wv, wgd = ws
    j = pl.program_id(0)
    h = h_ref[...]
    u = _dot(h, wu)
    n_prev = prev_ref.shape[0]
    sums = []
    s = jnp.zeros_like(u)
    back = 0
    for w in POOL_SIZES:
        while back < w - 1:
            s = s + prev_ref[n_prev - 1 - back]
            back += 1
        sums.append(s)
    win = _pick_by_chunk(j, sums) + u
    inv_w = _pick_by_chunk(j, [1.0 / w for w in POOL_SIZES])
    pooled = win * inv_w - u
    mixed = _dot(pooled.astype(BF16), pw_ref[...]) * ps_ref[...]
    yc_ref[...] = (mixed * _silu(_dot(h, wgc))).astype(BF16)
    q_ref[...] = _dot(h, wq)
    k_ref[...] = _dot(h, wk) * (DK_D ** -0.5)
    v_ref[...] = _dot(h, wv)
    sgd_ref[...] = _silu(_dot(h, wgd))
    for t in range(n_prev - 1):
        pn_ref[t] = prev_ref[t + 1]
    pn_ref[n_prev - 1] = u


def _odd_front_sample(h2d, w_in, pw, ps, prev_t):
    rows = h2d.shape[0]
    n_prev = prev_t.shape[0]
    nj = D_MODEL // TN

    def wspec(s):
        return pl.BlockSpec((D_MODEL, TN), lambda j, s=s: (0, s * nj + j))

    col = pl.BlockSpec((rows, TN), lambda j: (0, j))
    st = pl.BlockSpec((n_prev, rows, TN), lambda j: (0, 0, j))
    colf = jax.ShapeDtypeStruct((rows, D_MODEL), F32)
    w16_spec = pl.BlockSpec((D_MODEL, TN), lambda j: (0, j))
    w16_shape = jax.ShapeDtypeStruct((D_MODEL, D_MODEL), BF16)
    outs = pl.pallas_call(
        _odd_front_sample_kernel,
        grid=(nj,),
        in_specs=[pl.BlockSpec((rows, D_MODEL), lambda j: (0, 0))] + [wspec(s) for s in range(N_SEG_O)] + [
            pl.BlockSpec((None, TN, TN), lambda j: (j, 0, 0)), pl.BlockSpec((1, TN), lambda j: (0, j)), st],
        out_specs=[col, col, col, col, col, st] + [w16_spec] * N_SEG_O,
        out_shape=[jax.ShapeDtypeStruct((rows, D_MODEL), BF16), colf, colf, colf, colf,
                   jax.ShapeDtypeStruct((n_prev, rows, D_MODEL), F32)] + [w16_shape] * N_SEG_O,
        compiler_params=_cparams(("arbitrary",)),
        name="odd_front_sample",
    )(h2d, *([w_in] * N_SEG_O), pw, ps.reshape(1, D_MODEL), prev_t)
    return outs[:6], outs[6:]


RET_SAMPLE_ROWS = 8


def _ret_sample_kernel(q_ref, k_ref, v_ref, sgd_ref, st_ref, yd_ref, sn_ref):
    row = lax.broadcasted_iota(jnp.int32, (DK_D, DK_D), 0)
    col = lax.broadcasted_iota(jnp.int32, (DK_D, DK_D), 1)
    qs, ks, vs, gs = q_ref[...], k_ref[...], v_ref[...], sgd_ref[...]
    out_rows = []
    for i in range(RET_SAMPLE_ROWS):
        out_heads = []
        for h in range(H_D):
            g = float(np.exp(RET_LOG_G[h]))
            cols = slice(h * DK_D, (h + 1) * DK_D)
            q, k, v = qs[i:i + 1, cols], ks[i:i + 1, cols], vs[i:i + 1, cols]
            state = st_ref[i, h]
            qk = jnp.sum(q * k, axis=-1, keepdims=True)
            q8 = jnp.broadcast_to(q, (8, DK_D)).astype(BF16)
            cross = _dot(q8, state.astype(BF16))[0:1, :]
            o = qk * v + g * cross
            k_diag = jnp.where(row == col, jnp.broadcast_to(k, (DK_D, DK_D)), 0.0).astype(BF16)
            v_rows = jnp.broadcast_to(v, (DK_D, DK_D)).astype(BF16)
            sn_ref[i, h] = g * state + _dot(k_diag, v_rows)
            out_heads.append(_rms(o) * gs[i:i + 1, cols])
        out_rows.append(jnp.concatenate(out_heads, axis=1))
    yd_ref[...] = jnp.concatenate(out_rows, axis=0)


def _ret_sample(q, k, v, sgd, state):
    rows = q.shape[0]
    rs = RET_SAMPLE_ROWS
    row_spec = pl.BlockSpec((rs, D_MODEL), lambda i: (i, 0))
    st_spec = pl.BlockSpec((rs, H_D, DK_D, DK_D), lambda i: (i, 0, 0, 0))
    return pl.pallas_call(
        _ret_sample_kernel,
        grid=(rows // rs,),
        in_specs=[row_spec, row_spec, row_spec, row_spec, st_spec],
        out_specs=[row_spec, st_spec],
        out_shape=[jax.ShapeDtypeStruct((rows, D_MODEL), F32),
                   jax.ShapeDtypeStruct((rows, H_D, DK_D, DK_D), F32)],
        compiler_params=_cparams(("arbitrary",)),
        name="ret_sample",
    )(q, k, v, sgd, state)


def kernel(x_prompt, x_sample, c_prompt, c_sample, state_conv, cache_win_k, cache_win_v, state_pool, state_ret,
           norm_e, ada_w_e, ada_b_e, w_in_e, conv_w, conv_b, w_out_e, norm_o, ada_w_o, ada_b_o, w_in_o,
           pool_w, pool_scale, w_out_o, norm_f):
    batch, seq, d = x_prompt.shape
    sb = x_sample.shape[0]
    assert d == D_MODEL and x_sample.shape[1] == 1
    assert norm_e.shape[0] == 1 and norm_o.shape[0] == 1
    wb = cache_win_k.shape[2]
    keep = min(wb, seq)
    assert wb == DILATIONS[-1] * NK and seq % (DILATIONS[-1] * NK) == 0
    tm = TM_OUT

    n_c = batch + sb
    pad = (-n_c) % 8
    c_all = jnp.concatenate([c_prompt, c_sample, jnp.zeros((pad, d), F32)], axis=0)
    mod_e, mod_o = _adaln(c_all, ada_w_e[0], ada_b_e[0], ada_w_o[0], ada_b_o[0])
    mod_e_p, mod_e_s = mod_e[:batch], mod_e[batch:n_c]
    mod_o_p, mod_o_s = mod_o[:batch], mod_o[batch:n_c]

    w_out_e16 = w_out_e[0].astype(BF16)
    w_out_o16 = w_out_o[0].astype(BF16)
    pool_w16 = pool_w[0].astype(BF16)

    xs2d = x_sample.reshape(sb, d)
    conv_prev_t = jnp.transpose(state_conv[0], (1, 0, 2))
    (ya_s, q_s, k_s, v_s, sgb_s, conv_s_t), w_e_segs = _even_front_sample(
        xs2d, mod_e_s, norm_e[0], w_in_e[0], conv_w[0], conv_b[0], conv_prev_t)
    cache_kt = jnp.transpose(cache_win_k[0], (0, 2, 3, 1)).reshape(sb, d, wb)
    cache_vt = jnp.transpose(cache_win_v[0], (0, 2, 3, 1)).reshape(sb, d, wb)
    yb_s = _attn_sample(q_s, k_s, v_s, sgb_s, cache_kt, cache_vt)
    x1_s, h1_s = _outproj(ya_s, yb_s, xs2d, mod_e_s, w_out_e16, norm_o[0], mod_o_s, sb, 1)
    pool_prev_t = jnp.transpose(state_pool[0], (1, 0, 2))
    (yc_s, rq_s, rk_s, rv_s, sgd_s, pool_s_t), w_o_segs = _odd_front_sample(
        h1_s, w_in_o[0], pool_w16, pool_scale[0], pool_prev_t)
    yd_s, ret_s = _ret_sample(rq_s, rk_s, rv_s, sgd_s, state_ret[0])
    y_sample = _outproj(yc_s, yd_s, x1_s, mod_o_s, w_out_o16, norm_f, None, sb, 1)

    xp2d = x_prompt.reshape(batch * seq, d)
    ya, q, k, v, sgb, k_new, v_new, conv_tail = _even_front_prompt(
        xp2d, mod_e_p, norm_e[0], w_e_segs, conv_w[0], conv_b[0], batch, seq, keep, TM_FRONT)
    yb = _attn_prompt(q, k, v, sgb, batch, seq).reshape(batch * seq, d)
    x1, h1 = _outproj(ya, yb, xp2d, mod_e_p, w_out_e16, norm_o[0], mod_o_p, tm, seq)
    yc, rq, rk, rv, sgd, pool_tail = _odd_front_prompt(h1, w_o_segs, pool_w16, pool_scale[0], batch, seq, TM_FRONT)
    yd, ret_p = _ret_prompt(rq, rk, rv, sgd, batch, seq)
    y_prompt = _outproj(yc, yd.reshape(batch * seq, d), x1, mod_o_p, w_out_o16, norm_f, None, tm, seq)

    return (
        y_prompt.reshape(batch, seq, d),
        y_sample.reshape(sb, 1, d),
        conv_tail[:, 6:8][None],
        jnp.transpose(conv_s_t, (1, 0, 2))[None],
        jnp.transpose(k_new.reshape(batch, H_B, HD_B, keep), (0, 3, 1, 2))[None],
        k_s.reshape(1, sb, 1, H_B, HD_B),
        jnp.transpose(v_new.reshape(batch, H_B, HD_B, keep), (0, 3, 1, 2))[None],
        v_s.reshape(1, sb, 1, H_B, HD_B),
        pool_tail[:, 1:][None],
        jnp.transpose(pool_s_t, (1, 0, 2))[None],
        ret_p[None],
        ret_s[None],
    )
```

```python
import functools

import numpy as np
import jax
import jax.numpy as jnp
from jax import lax
from jax.experimental import pallas as pl
from jax.experimental.pallas import tpu as pltpu

F32 = jnp.float32
BF16 = jnp.bfloat16
WORD = jnp.uint32

D_MODEL = 1024
EPS = 1e-6
H_B = 16
HD_B = 64
N_PAIR = H_B // 2
LANES = 128
NK = 128
DILATIONS = (1, 4, 16)
POOL_SIZES = (2, 4, 8, 16)
POOL_PREV = 16
H_D = 4
DK_D = 256
RET_CHUNK = 256
TN = 256
TM_FRONT = 1024
TM_OUT = 1024
MASK_DIST = 1e9
LOG2E = float(np.log2(np.e))
MASK_ADD = 1e30
VMEM_LIMIT = 56 * 1024 * 1024

RET_LOG_G = [float(np.log(1.0 - 2.0 ** (-5.0 - h))) for h in range(H_D)]


def _cparams(sem):
    return pltpu.CompilerParams(dimension_semantics=sem, vmem_limit_bytes=VMEM_LIMIT)


def _silu(x):
    return x * (1.0 / (1.0 + jnp.exp(-x)))


def _dot(a, b):
    return jnp.dot(a, b, preferred_element_type=F32)


def _dot_nt(a, b):
    return lax.dot_general(a, b, (((1,), (1,)), ((), ())), preferred_element_type=F32)


def _rms(x):
    return x * lax.rsqrt(jnp.mean(x * x, axis=-1, keepdims=True) + EPS)


def _adaln_kernel(c_ref, we_ref, be_ref, wo_ref, bo_ref, me_ref, mo_ref):
    sc = _silu(c_ref[...]).astype(BF16)
    me_ref[...] = _dot(sc, we_ref[...].astype(BF16)) + be_ref[...]
    mo_ref[...] = _dot(sc, wo_ref[...].astype(BF16)) + bo_ref[...]


def _adaln(c_all, we, be, wo, bo):
    rows = c_all.shape[0]
    tn = 512
    n = 3 * D_MODEL
    wspec = pl.BlockSpec((D_MODEL, tn), lambda j: (0, j))
    bspec = pl.BlockSpec((1, tn), lambda j: (0, j))
    ospec = pl.BlockSpec((rows, tn), lambda j: (0, j))
    return pl.pallas_call(
        _adaln_kernel,
        grid=(n // tn,),
        in_specs=[pl.BlockSpec((rows, D_MODEL), lambda j: (0, 0)), wspec, bspec, wspec, bspec],
        out_specs=[ospec, ospec],
        out_shape=[jax.ShapeDtypeStruct((rows, n), F32)] * 2,
        compiler_params=_cparams(("arbitrary",)),
        name="adaln",
    )(c_all, we, be.reshape(1, n), wo, bo.reshape(1, n))


def _norm_mod(x, g, mod):
    shift = mod[:, 0:D_MODEL]
    scale = mod[:, D_MODEL:2 * D_MODEL]
    return _rms(x) * g * (1.0 + scale) + shift


def _shift_rows(u, k, prev_rows):
    row = lax.broadcasted_iota(jnp.int32, u.shape, 0)
    out = pltpu.roll(u, k, 0)
    for idx, pr in enumerate(prev_rows):
        out = jnp.where(row == idx, pr, out)
    return out


N_SEG_E = 8
PHASES_E = 16


def _phase_segments(t, n_seg, n_phases):
    return range(t * n_seg // n_phases, (t + 1) * n_seg // n_phases)


def _even_front_kernel(x_ref, mod_ref, g_ref, *refs, tiles_per_batch, n_chunks):
    ws = refs[:N_SEG_E]
    cw_ref, cb_ref = refs[N_SEG_E:N_SEG_E + 2]
    ya_ref, q_ref, k_ref, v_ref, sgb_ref, kn_ref, vn_ref, cn_ref = refs[N_SEG_E + 2:N_SEG_E + 10]
    h_scr, carry_scr, z_a, z_b = refs[N_SEG_E + 10:]
    nj = D_MODEL // TN
    s = pl.program_id(0)
    c_out = jnp.maximum(s - 1, 0)
    j_out = c_out % nj
    il_out = (c_out // nj) % tiles_per_batch

    @pl.when(s == 0)
    def _():
        z_b[...] = jnp.zeros_like(z_b)
        carry_scr[...] = jnp.zeros_like(carry_scr)

    @pl.when((s < n_chunks) & (s % nj == 0))
    def _():
        h_scr[...] = _norm_mod(x_ref[...], g_ref[...], mod_ref[...]).astype(BF16)

    tm = h_scr.shape[0]
    rb = tm // PHASES_E

    def phase(t, z_w, z_r):
        for seg in _phase_segments(t, N_SEG_E, PHASES_E):
            z_w[seg] = _dot(h_scr[...], ws[seg][...])
        rows = slice(t * rb, (t + 1) * rb)
        bg, cg, xv, ga, q, k, v, gb = (z_r[seg, rows, :] for seg in range(N_SEG_E))
        u = cg * xv
        prev = carry_scr[j_out]
        if t == 0:
            prev = jnp.where(il_out == 0, 0.0, prev)
        p2, p1 = prev[6:7, :], prev[7:8, :]
        u1 = _shift_rows(u, 1, [p1])
        u2 = _shift_rows(u, 2, [p2, p1])
        cw = cw_ref[...]
        conv = cb_ref[...] + cw[0:1, :] * u2 + cw[1:2, :] * u1 + cw[2:3, :] * u
        tail = u[rb - 8:rb, :]
        carry_scr[j_out] = tail
        if t == PHASES_E - 1:
            cn_ref[...] = tail
        ya_ref[rows, :] = (bg * conv * _silu(ga)).astype(BF16)
        kn_ref[:, rows] = k.T
        vn_ref[:, rows] = v.T
        wrows = slice(t * rb // 2, (t + 1) * rb // 2)
        for ref, val in ((q_ref, q * (HD_B ** -0.5 * LOG2E)),
                         (k_ref, k), (v_ref, v)):
            words = pltpu.bitcast(val.astype(BF16), WORD)
            for e in range(TN // LANES):
                ref[e, wrows, :] = words[:, e * LANES:(e + 1) * LANES]
        sgb = _silu(gb).astype(BF16)
        for e in range(TN // LANES):
            sgb_ref[e, rows, :] = sgb[:, e * LANES:(e + 1) * LANES]

    for parity, (z_w, z_r) in enumerate(((z_a, z_b), (z_b, z_a))):
        for t in range(PHASES_E):
            pl.when(s % 2 == parity)(functools.partial(phase, t, z_w, z_r))


def _even_front_prompt(x2d, mod, g, w_segs, cw, cb, batch, seq, keep, tm):
    rows = batch * seq
    tpb = seq // tm
    nj = D_MODEL // TN
    n_chunks = (rows // tm) * nj
    off = (seq - keep) // tm
    ppc = TN // LANES

    def c_in(s):
        return jnp.minimum(s, n_chunks - 1)

    def c_out(s):
        return jnp.maximum(s - 1, 0)

    def out_ij(s):
        c = c_out(s)
        return c // nj, c % nj

    def pair_map(s):
        i, j = out_ij(s)
        return (i // tpb, j, i % tpb, 0)

    def keep_map(s):
        i, j = out_ij(s)
        il = i % tpb
        kept = il >= off
        return (i // tpb, jnp.where(kept, j, 0), jnp.where(kept, il - off, 0))

    def tail_map(s):
        i, j = out_ij(s)
        return (i // tpb, 0, jnp.where(i % tpb == tpb - 1, j, 0))

    pair_spec = pl.BlockSpec((None, ppc, tm, LANES), pair_map)
    pair_shape = jax.ShapeDtypeStruct((batch, N_PAIR, seq, LANES), BF16)
    word_spec = pl.BlockSpec((None, ppc, tm // 2, LANES), pair_map)
    word_shape = jax.ShapeDtypeStruct((batch, N_PAIR, seq // 2, LANES), WORD)
    keep_spec = pl.BlockSpec((None, TN, tm), keep_map)
    keep_shape = jax.ShapeDtypeStruct((batch, D_MODEL, keep), F32)
    return pl.pallas_call(
        functools.partial(_even_front_kernel, tiles_per_batch=tpb, n_chunks=n_chunks),
        grid=(n_chunks + 1,),
        in_specs=[
            pl.BlockSpec((tm, D_MODEL), lambda s: (c_in(s) // nj, 0)),
            pl.BlockSpec((None, 1, 3 * D_MODEL), lambda s: (c_in(s) // nj // tpb, 0, 0)),
            pl.BlockSpec((1, D_MODEL), lambda s: (0, 0)),
        ] + [pl.BlockSpec((D_MODEL, TN), lambda s: (0, c_in(s) % nj))] * N_SEG_E + [
            pl.BlockSpec((3, TN), lambda s: (0, c_out(s) % nj)),
            pl.BlockSpec((1, TN), lambda s: (0, c_out(s) % nj)),
        ],
        out_specs=[
            pl.BlockSpec((tm, TN), lambda s: out_ij(s)),
            word_spec, word_spec, word_spec, pair_spec,
            keep_spec, keep_spec,
            pl.BlockSpec((None, 8, TN), tail_map),
        ],
        out_shape=[
            jax.ShapeDtypeStruct((rows, D_MODEL), BF16),
            word_shape, word_shape, word_shape, pair_shape,
            keep_shape, keep_shape,
            jax.ShapeDtypeStruct((batch, 8, D_MODEL), F32),
        ],
        scratch_shapes=[pltpu.VMEM((tm, D_MODEL), BF16), pltpu.VMEM((nj, 8, TN), F32),
                        pltpu.VMEM((N_SEG_E, tm, TN), F32), pltpu.VMEM((N_SEG_E, tm, TN), F32)],
        compiler_params=_cparams(("arbitrary",)),
        name="even_front_prompt",
    )(x2d, mod.reshape(batch, 1, 3 * D_MODEL), g.reshape(1, D_MODEL), *w_segs, cw, cb.reshape(1, D_MODEL))


STAT_PITCH = {1: 1, 4: 4, 16: 24}
ATTN_UNROLL = 32


def _residue_stream(src, r):
    n_out = src.shape[0] // 4
    halves = [pltpu.unpack_elementwise(src[pl.ds(r // 2 + off, n_out, stride=4), :], index=r % 2,
                                       packed_dtype=BF16, unpacked_dtype=F32) for off in (0, 2)]
    return pltpu.pack_elementwise(halves, packed_dtype=BF16)


def _attn_prompt_kernel(slope_ref, dist_ref, q_ref, k_ref, v_ref, sgb_ref, yb_ref,
                        q4_scr, q16_scr, kv4_scr, kv16_scr, acc_scr, m_scr, l_scr, bias_scr, *, seq):
    hp = pl.program_id(1)
    lane = lax.broadcasted_iota(jnp.int32, (1, LANES), 1)
    half0 = lane < HD_B
    dist = dist_ref[...]
    masked = jnp.full((NK, NK), MASK_DIST, F32)
    for p, d in enumerate(DILATIONS):
        for e in range(2):
            bias = dist * (slope_ref[2 * hp + e] * (float(d) * LOG2E))
            bias_scr[2 * p + e] = bias
            bias_scr[6 + 2 * p + e] = jnp.concatenate([bias[:, NK:], masked], axis=1)

    for a, ref in enumerate((q_ref, k_ref, v_ref)):
        for r in range(4):
            s = _residue_stream(ref, r)
            if a == 0:
                q4_scr[r] = s
            else:
                kv4_scr[a - 1, r] = s
        for r in range(16):
            s = _residue_stream(q4_scr.at[r % 4] if a == 0 else kv4_scr.at[a - 1, r % 4], r // 4)
            if a == 0:
                q16_scr[r] = s
            else:
                kv16_scr[a - 1, r] = s

    ones = jnp.ones((2 * NK, LANES), BF16)
    hw = NK // 2

    def tile(p, d, r, n):
        qw = pl.ds(pl.multiple_of(n * hw, hw), hw)
        kw = pl.ds(pl.multiple_of(jnp.maximum(n - 1, 0) * hw, hw), 2 * hw)
        if d == 1:
            qs, ks, vs = q_ref, k_ref, v_ref
        elif d == 4:
            qs, ks, vs = q4_scr.at[r], kv4_scr.at[0, r], kv4_scr.at[1, r]
        else:
            qs, ks, vs = q16_scr.at[r], kv16_scr.at[0, r], kv16_scr.at[1, r]
        q = pltpu.bitcast(qs[qw, :], BF16)
        kk = pltpu.bitcast(ks[kw, :], BF16)
        vv1 = jnp.concatenate([pltpu.bitcast(vs[kw, :], BF16), ones], axis=1)
        first = jnp.where(n == 0, 1, 0)
        res = []
        for e in range(2):
            qe = jnp.where(half0 if e == 0 else jnp.logical_not(half0), q, jnp.zeros_like(q))
            s = _dot_nt(qe, kk) - bias_scr[6 * first + 2 * p + e]
            m = jnp.max(s, axis=-1, keepdims=True)
            pe = jnp.exp2(s - m).astype(BF16)
            res.append((_dot(pe, vv1), m))
        (a0, m0), (a1, m1) = res
        acc = jnp.where(half0, a0[:, :LANES], a1[:, :LANES])
        l = jnp.where(half0, a0[:, LANES:], a1[:, LANES:])
        m = jnp.where(half0, m0, m1)
        if d > 1:
            pitch = STAT_PITCH[d]
            rows = pl.ds(r + pitch * NK * n, NK, stride=pitch)
            acc_scr[p - 1, rows, :] = acc
            l_scr[p - 1, rows, :] = l
            m_scr[p - 1, rows, :] = m
            return
        accs, ls, ms = [acc], [l], [m]
        for pp in range(1, len(DILATIONS)):
            dd = DILATIONS[pp]
            for ref, vals in ((acc_scr, accs), (l_scr, ls), (m_scr, ms)):
                vals.append(jnp.concatenate(
                    [ref[pp - 1, pl.ds(pl.multiple_of((n * (NK // dd) + g) * STAT_PITCH[dd], 8), dd), :]
                     for g in range(NK // dd)], axis=0) if STAT_PITCH[dd] != dd
                    else ref[pp - 1, pl.ds(pl.multiple_of(n * NK, NK), NK), :])
        top = jnp.maximum(jnp.maximum(ms[0], ms[1]), ms[2])
        ws = [jnp.exp2(x - top) for x in ms]
        num = ws[0] * accs[0] + ws[1] * accs[1] + ws[2] * accs[2]
        den = ws[0] * ls[0] + ws[1] * ls[1] + ws[2] * ls[2]
        rows = pl.ds(pl.multiple_of(n * NK, NK), NK)
        yb_ref[rows, :] = (num * (1.0 / den) * sgb_ref[rows, :].astype(F32)).astype(BF16)

    for p, d in reversed(list(enumerate(DILATIONS))):
        def body(it, c, p=p, d=d):
            for u in range(ATTN_UNROLL):
                t = it * ATTN_UNROLL + u
                if d == 1:
                    tile(p, d, 0, t)
                else:
                    tile(p, d, t % d, t // d)
            return c

        lax.fori_loop(0, seq // NK // ATTN_UNROLL, body, 0)


def _band_distance():
    qi = np.arange(NK)[:, None]
    kj = np.arange(2 * NK)[None, :]
    dist = NK + qi - kj
    return jnp.asarray(np.where((dist >= 0) & (dist <= NK), dist, MASK_DIST), dtype=F32)


def _alibi_slopes():
    return jnp.asarray(2.0 ** (-8.0 * np.arange(1, H_B + 1) / H_B), dtype=F32)


def _attn_prompt(q, k, v, sgb, batch, seq):
    word_spec = pl.BlockSpec((None, None, seq // 2, LANES), lambda b, hp: (b, hp, 0, 0))
    stat_rows = max(seq // d * STAT_PITCH[d] for d in DILATIONS)
    stat = pltpu.VMEM((len(DILATIONS) - 1, stat_rows, LANES), F32)
    return pl.pallas_call(
        functools.partial(_attn_prompt_kernel, seq=seq),
        grid=(batch, N_PAIR),
        in_specs=[pl.BlockSpec(memory_space=pltpu.SMEM),
                  pl.BlockSpec((NK, 2 * NK), lambda b, hp: (0, 0)),
                  word_spec, word_spec, word_spec,
                  pl.BlockSpec((None, None, seq, LANES), lambda b, hp: (b, hp, 0, 0))],
        out_specs=pl.BlockSpec((None, seq, LANES), lambda b, hp: (b, 0, hp)),
        out_shape=jax.ShapeDtypeStruct((batch, seq, D_MODEL), BF16),
        scratch_shapes=[pltpu.VMEM((4, seq // 8, LANES), WORD),
                        pltpu.VMEM((16, seq // 32, LANES), WORD),
                        pltpu.VMEM((2, 4, seq // 8, LANES), WORD),
                        pltpu.VMEM((2, 16, seq // 32, LANES), WORD),
                        stat, stat, stat,
                        pltpu.VMEM((12, NK, 2 * NK), F32)],
        compiler_params=_cparams(("arbitrary", "arbitrary")),
        name="attn_prompt",
    )(_alibi_slopes(), _band_distance(), q, k, v, sgb)


OUT_ROWS = 128


def _outproj_rows(z_scr, finish):
    tm = z_scr.shape[0]
    rb = min(OUT_ROWS, tm)

    def body(i, carry):
        finish(pl.ds(pl.multiple_of(i * rb, rb), rb))
        return carry

    lax.fori_loop(0, tm // rb, body, 0)


def _mod_rows(mod_ref, rows):
    return mod_ref[...] if mod_ref.shape[0] == 1 else mod_ref[rows, :]


def _outproj_mid_kernel(ya_ref, yb_ref, x_ref, mod_ref, w1_ref, w2_ref, g_ref, mod2_ref, x1_ref, h_ref, z_scr):
    z_scr[...] = _dot(ya_ref[...], w1_ref[...]) + _dot(yb_ref[...].astype(BF16), w2_ref[...])

    def finish(rows):
        gate = _mod_rows(mod_ref, rows)[:, 2 * D_MODEL:3 * D_MODEL]
        x1 = x_ref[rows, :] + gate * z_scr[rows, :]
        x1_ref[rows, :] = x1
        h_ref[rows, :] = _norm_mod(x1, g_ref[...], _mod_rows(mod2_ref, rows)).astype(BF16)

    _outproj_rows(z_scr, finish)


def _outproj_final_kernel(ya_ref, yb_ref, x_ref, mod_ref, w1_ref, w2_ref, g_ref, y_ref, z_scr):
    z_scr[...] = _dot(ya_ref[...], w1_ref[...]) + _dot(yb_ref[...].astype(BF16), w2_ref[...])

    def finish(rows):
        gate = _mod_rows(mod_ref, rows)[:, 2 * D_MODEL:3 * D_MODEL]
        x1 = x_ref[rows, :] + gate * z_scr[rows, :]
        y_ref[rows, :] = _rms(x1) * g_ref[...]

    _outproj_rows(z_scr, finish)


def _outproj(ya, yb, x2d, mod, w_out, g, mod2, tm, rows_per_mod):
    rows = x2d.shape[0]
    half = w_out.shape[0] // 2
    row_spec = pl.BlockSpec((tm, D_MODEL), lambda i: (i, 0))
    if rows_per_mod == 1:
        mod_spec = pl.BlockSpec((tm, 3 * D_MODEL), lambda i: (i, 0))
        mods = (mod, mod2)
    else:
        tpb = rows_per_mod // tm
        mod_spec = pl.BlockSpec((None, 1, 3 * D_MODEL), lambda i: (i // tpb, 0, 0))
        mods = tuple(None if m is None else m.reshape(-1, 1, 3 * D_MODEL) for m in (mod, mod2))
    w1_spec = pl.BlockSpec((half, D_MODEL), lambda i: (0, 0))
    w2_spec = pl.BlockSpec((half, D_MODEL), lambda i: (1, 0))
    g_spec = pl.BlockSpec((1, D_MODEL), lambda i: (0, 0))
    common = dict(grid=(rows // tm,), compiler_params=_cparams(("arbitrary",)),
                  scratch_shapes=[pltpu.VMEM((tm, D_MODEL), F32)])
    if mod2 is None:
        return pl.pallas_call(
            _outproj_final_kernel,
            in_specs=[row_spec, row_spec, row_spec, mod_spec, w1_spec, w2_spec, g_spec],
            out_specs=row_spec,
            out_shape=jax.ShapeDtypeStruct((rows, D_MODEL), F32),
            name="outproj_final", **common,
        )(ya, yb, x2d, mods[0], w_out, w_out, g.reshape(1, D_MODEL))
    return pl.pallas_call(
        _outproj_mid_kernel,
        in_specs=[row_spec, row_spec, row_spec, mod_spec, w1_spec, w2_spec, g_spec, mod_spec],
        out_specs=[row_spec, row_spec],
        out_shape=[jax.ShapeDtypeStruct((rows, D_MODEL), F32), jax.ShapeDtypeStruct((rows, D_MODEL), BF16)],
        name="outproj_mid", **common,
    )(ya, yb, x2d, mods[0], w_out, w_out, g.reshape(1, D_MODEL), mods[1])


def _pick_by_chunk(j, vals):
    out = vals[-1]
    for idx in range(len(vals) - 2, -1, -1):
        out = jnp.where(j == idx, vals[idx], out)
    return out


N_SEG_O = 6
PHASES_O = 4


def _odd_front_kernel(h_ref, *refs, tiles_per_batch):
    ws = refs[:N_SEG_O]
    pw_ref, ps_ref = refs[N_SEG_O:N_SEG_O + 2]
    yc_ref, q_ref, kt_ref, v_ref, sgd_ref, pn_ref = refs[N_SEG_O + 2:N_SEG_O + 8]
    carry_scr, z_a, z_b = refs[N_SEG_O + 8:]
    nj = D_MODEL // TN
    s = pl.program_id(0)
    c_out = jnp.maximum(s - 1, 0)
    j_out = c_out % nj
    il_out = (c_out // nj) % tiles_per_batch
    tm = h_ref.shape[0]
    rb = tm // PHASES_O

    @pl.when(s == 0)
    def _():
        z_b[...] = jnp.zeros_like(z_b)
        carry_scr[...] = jnp.zeros_like(carry_scr)

    def phase(t, z_w, z_r):
        for seg in _phase_segments(t, N_SEG_O, PHASES_O):
            z_w[seg] = _dot(h_ref[...], ws[seg][...])
        rows = slice(t * rb, (t + 1) * rb)
        u, gc, q, k, v, gd = (z_r[seg, rows, :] for seg in range(N_SEG_O))
        prev = carry_scr[j_out]
        if t == 0:
            prev = jnp.where(il_out == 0, 0.0, prev)
        tail = u[rb - POOL_PREV:rb, :]
        carry_scr[j_out] = tail
        if t == PHASES_O - 1:
            pn_ref[...] = tail
        ext = jnp.concatenate([prev, u], axis=0)
        sums = []
        acc = ext
        for sh in (1, 2, 4, 8):
            acc = acc + pltpu.roll(acc, sh, 0)
            sums.append(acc[POOL_PREV:, :])
        win = _pick_by_chunk(j_out, sums)
        width = _pick_by_chunk(j_out, [float(w) for w in POOL_SIZES])
        pos = (il_out * tm + t * rb + lax.broadcasted_iota(jnp.int32, (rb, 1), 0)).astype(F32)
        pooled = win / jnp.minimum(width, pos + 1.0) - u
        mixed = _dot(pooled.astype(BF16), pw_ref[...]) * ps_ref[...]
        yc_ref[rows, :] = (mixed * _silu(gc)).astype(BF16)
        q_ref[rows, :] = q.astype(BF16)
        kt_ref[:, rows] = (k * (DK_D ** -0.5)).T.astype(BF16)
        v_ref[rows, :] = v.astype(BF16)
        sgd_ref[rows, :] = _silu(gd).astype(BF16)

    for parity, (z_w, z_r) in enumerate(((z_a, z_b), (z_b, z_a))):
        for t in range(PHASES_O):
            pl.when(s % 2 == parity)(functools.partial(phase, t, z_w, z_r))


def _odd_front_prompt(h2d, w_segs, pw, ps, batch, seq, tm):
    rows = batch * seq
    tpb = seq // tm
    nj = D_MODEL // TN
    n_chunks = (rows // tm) * nj

    def c_in(s):
        return jnp.minimum(s, n_chunks - 1)

    def out_ij(s):
        c = jnp.maximum(s - 1, 0)
        return c // nj, c % nj

    def head_map(s):
        i, j = out_ij(s)
        return (i // tpb, j, i % tpb, 0)

    def kt_map(s):
        i, j = out_ij(s)
        return (i // tpb, j, 0, i % tpb)

    def tail_map(s):
        i, j = out_ij(s)
        return (i // tpb, 0, jnp.where(i % tpb == tpb - 1, j, 0))

    head_spec = pl.BlockSpec((None, None, tm, TN), head_map)
    head_shape = jax.ShapeDtypeStruct((batch, H_D, seq, DK_D), BF16)
    tok_spec = pl.BlockSpec((tm, TN), out_ij)
    tok_shape = jax.ShapeDtypeStruct((rows, D_MODEL), BF16)
    zbuf = pltpu.VMEM((N_SEG_O, tm, TN), F32)
    return pl.pallas_call(
        functools.partial(_odd_front_kernel, tiles_per_batch=tpb),
        grid=(n_chunks + 1,),
        in_specs=[pl.BlockSpec((tm, D_MODEL), lambda s: (c_in(s) // nj, 0))] + [
                  pl.BlockSpec((D_MODEL, TN), lambda s: (0, c_in(s) % nj))] * N_SEG_O + [
                  pl.BlockSpec((None, TN, TN), lambda s: (out_ij(s)[1], 0, 0)),
                  pl.BlockSpec((1, TN), lambda s: (0, out_ij(s)[1]))],
        out_specs=[tok_spec, head_spec, pl.BlockSpec((None, None, TN, tm), kt_map), head_spec, tok_spec,
                   pl.BlockSpec((None, POOL_PREV, TN), tail_map)],
        out_shape=[tok_shape, head_shape, jax.ShapeDtypeStruct((batch, H_D, DK_D, seq), BF16),
                   head_shape, tok_shape,
                   jax.ShapeDtypeStruct((batch, POOL_PREV, D_MODEL), F32)],
        scratch_shapes=[pltpu.VMEM((nj, POOL_PREV, TN), F32), zbuf, zbuf],
        compiler_params=_cparams(("arbitrary",)),
        name="odd_front_prompt",
    )(h2d, *w_segs, pw, ps.reshape(1, D_MODEL))


RET_SBLK = 1024


def _ret_prompt_kernel(q_ref, kt_ref, v_ref, sgd_ref, dec_ref, cdec_ref, kdec_ref,
                       yd_ref, st_ref, state_scr, *, sblk, n_sblk):
    sb = pl.program_id(1)

    @pl.when(sb == 0)
    def _():
        state_scr[...] = jnp.zeros_like(state_scr)

    def chunk(c, carry):
        rows = pl.ds(pl.multiple_of(c * RET_CHUNK, RET_CHUNK), RET_CHUNK)
        for h in range(H_D):
            cols = slice(h * DK_D, (h + 1) * DK_D)
            q = q_ref[h, rows, :]
            kt = kt_ref[h, :, rows]
            v = v_ref[h, rows, :]
            state = state_scr[h]
            scores = _dot(q, kt) * dec_ref[h]
            o = _dot(scores.astype(BF16), v) + _dot(q, state.astype(BF16)) * cdec_ref[h]
            kd = (kt.astype(F32) * kdec_ref[h]).astype(BF16)
            state_scr[h] = float(np.exp(RET_CHUNK * RET_LOG_G[h])) * state + _dot(kd, v)
            yd_ref[rows, cols] = (_rms(o) * sgd_ref[rows, cols].astype(F32)).astype(BF16)
        return carry

    lax.fori_loop(0, sblk // RET_CHUNK, chunk, 0, unroll=4)

    @pl.when(sb == n_sblk - 1)
    def _():
        st_ref[...] = state_scr[...]


def _ret_consts():
    t = np.arange(RET_CHUNK, dtype=np.float64)
    diff = t[:, None] - t[None, :]
    lg = np.asarray(RET_LOG_G)[:, None, None]
    dec = np.where(diff >= 0, np.exp(np.maximum(diff, 0.0)[None] * lg), 0.0)
    cdec = np.broadcast_to(np.exp((t + 1.0)[None, :, None] * lg), (H_D, RET_CHUNK, DK_D))
    kdec = np.broadcast_to(np.exp((RET_CHUNK - 1.0 - t)[None, None, :] * lg), (H_D, DK_D, RET_CHUNK))
    return jnp.asarray(dec, F32), jnp.asarray(cdec, F32), jnp.asarray(kdec, F32)


def _ret_prompt(q, kt, v, sgd, batch, seq):
    dec, cdec, kdec = _ret_consts()
    n_sblk = seq // RET_SBLK
    head_spec = pl.BlockSpec((None, H_D, RET_SBLK, DK_D), lambda b, s: (b, 0, s, 0))
    tok_spec = pl.BlockSpec((None, RET_SBLK, D_MODEL), lambda b, s: (b, s, 0))

    def const_spec(shape):
        return pl.BlockSpec(shape, lambda b, s: (0, 0, 0))

    return pl.pallas_call(
        functools.partial(_ret_prompt_kernel, sblk=RET_SBLK, n_sblk=n_sblk),
        grid=(batch, n_sblk),
        in_specs=[head_spec, pl.BlockSpec((None, H_D, DK_D, RET_SBLK), lambda b, s: (b, 0, 0, s)), head_spec,
                  tok_spec, const_spec(dec.shape), const_spec(cdec.shape), const_spec(kdec.shape)],
        out_specs=[tok_spec, pl.BlockSpec((None, H_D, DK_D, DK_D), lambda b, s: (b, 0, 0, 0))],
        out_shape=[jax.ShapeDtypeStruct((batch, seq, D_MODEL), BF16),
                   jax.ShapeDtypeStruct((batch, H_D, DK_D, DK_D), F32)],
        scratch_shapes=[pltpu.VMEM((H_D, DK_D, DK_D), F32)],
        compiler_params=_cparams(("arbitrary", "arbitrary")),
        name="ret_prompt",
    )(q, kt, v, sgd.reshape(batch, seq, D_MODEL), dec, cdec, kdec)


def _even_front_sample_kernel(x_ref, mod_ref, g_ref, *refs):
    ws32 = refs[:N_SEG_E]
    cw_ref, cb_ref, prev_ref = refs[N_SEG_E:N_SEG_E + 3]
    ya_ref, q_ref, k_ref, v_ref, sgb_ref, cn_ref = refs[N_SEG_E + 3:N_SEG_E + 9]
    w16_refs = refs[N_SEG_E + 9:]
    ws = [w[...].astype(BF16) for w in ws32]
    for w16_ref, w in zip(w16_refs, ws):
        w16_ref[...] = w
    wbg, wcg, wxv, wga, wq, wk, wv, wgb = ws
    h = _norm_mod(x_ref[...], g_ref[...], mod_ref[...]).astype(BF16)
    u = _dot(h, wcg) * _dot(h, wxv)
    cw = cw_ref[...]
    conv = cb_ref[...] + cw[0:1, :] * prev_ref[0] + cw[1:2, :] * prev_ref[1] + cw[2:3, :] * u
    cn_ref[0] = prev_ref[1]
    cn_ref[1] = u
    ya_ref[...] = (_dot(h, wbg) * conv * _silu(_dot(h, wga))).astype(BF16)
    q_ref[...] = _dot(h, wq) * (HD_B ** -0.5)
    k_ref[...] = _dot(h, wk)
    v_ref[...] = _dot(h, wv)
    sgb_ref[...] = _silu(_dot(h, wgb))


def _even_front_sample(x2d, mod, g, w_in, cw, cb, prev_t):
    rows = x2d.shape[0]
    nj = D_MODEL // TN

    def wspec(s):
        return pl.BlockSpec((D_MODEL, TN), lambda j, s=s: (0, s * nj + j))

    full = pl.BlockSpec((rows, D_MODEL), lambda j: (0, 0))
    col = pl.BlockSpec((rows, TN), lambda j: (0, j))
    st = pl.BlockSpec((2, rows, TN), lambda j: (0, 0, j))
    colf = jax.ShapeDtypeStruct((rows, D_MODEL), F32)
    w16_spec = pl.BlockSpec((D_MODEL, TN), lambda j: (0, j))
    w16_shape = jax.ShapeDtypeStruct((D_MODEL, D_MODEL), BF16)
    outs = pl.pallas_call(
        _even_front_sample_kernel,
        grid=(nj,),
        in_specs=[full, pl.BlockSpec((rows, 3 * D_MODEL), lambda j: (0, 0)),
                  pl.BlockSpec((1, D_MODEL), lambda j: (0, 0))] + [wspec(s) for s in range(N_SEG_E)] + [
            pl.BlockSpec((3, TN), lambda j: (0, j)), pl.BlockSpec((1, TN), lambda j: (0, j)), st],
        out_specs=[col, col, col, col, col, st] + [w16_spec] * N_SEG_E,
        out_shape=[jax.ShapeDtypeStruct((rows, D_MODEL), BF16), colf, colf, colf, colf,
                   jax.ShapeDtypeStruct((2, rows, D_MODEL), F32)] + [w16_shape] * N_SEG_E,
        compiler_params=_cparams(("arbitrary",)),
        name="even_front_sample",
    )(x2d, mod, g.reshape(1, D_MODEL), *([w_in] * N_SEG_E), cw, cb.reshape(1, D_MODEL), prev_t)
    return outs[:6], outs[6:]


def _attn_sample_kernel(slope_ref, pos_ref, q_ref, kn_ref, vn_ref, sgb_ref, kt_ref, vt_ref, yb_ref):
    gw = q_ref.shape[-1]
    me = pl.ds(pl.program_id(1) % 8, 1)
    gh = slope_ref.shape[0]
    row = lax.broadcasted_iota(jnp.int32, (gh, gw), 0)
    col = lax.broadcasted_iota(jnp.int32, (gh, gw), 1)
    own = (col >= row * HD_B) & (col < (row + 1) * HD_B)
    qm = jnp.where(own, q_ref[me, :], 0.0)
    s_self = jnp.sum(qm * kn_ref[me, :], axis=-1, keepdims=True)
    s_all = _dot(qm.astype(BF16), kt_ref[...].astype(BF16))
    s_all = s_all - slope_ref[:, 0:1] * pos_ref[0:1, :]
    v_self = vn_ref[me, :]
    probs, stats = [], []
    for p in range(len(DILATIONS)):
        s = s_all - pos_ref[p + 1:p + 2, :]
        m = jnp.maximum(jnp.max(s, axis=-1, keepdims=True), s_self)
        pe = jnp.exp(s - m)
        p_self = jnp.exp(s_self - m)
        probs.append(pe)
        stats.append((m, jnp.sum(pe, axis=-1, keepdims=True) + p_self, p_self))
    o_all = _dot_nt(jnp.concatenate(probs, axis=0).astype(BF16), vt_ref[...].astype(BF16))
    outs, lses = [], []
    for p, (m, l, p_self) in enumerate(stats):
        outs.append((o_all[gh * p:gh * (p + 1), :] + p_self * v_self) * (1.0 / l))
        lses.append(m + jnp.log(l))
    m = jnp.maximum(jnp.maximum(lses[0], lses[1]), lses[2])
    ws = [jnp.exp(x - m) for x in lses]
    o = (ws[0] * outs[0] + ws[1] * outs[1] + ws[2] * outs[2]) * (1.0 / (ws[0] + ws[1] + ws[2]))
    o = jnp.sum(jnp.where(own, o, 0.0), axis=0, keepdims=True)
    yb_ref[me, :] = o * sgb_ref[me, :]


def _attn_sample(q, kn, vn, sgb, cache_kt, cache_vt):
    rows, _, wb = cache_kt.shape
    gh = H_B
    gw = gh * HD_B
    ng = H_B // gh
    back = wb - np.arange(wb, dtype=np.float64)
    pos = [back] + [np.where((back % d == 0) & (back <= NK * d), 0.0, MASK_ADD) for d in DILATIONS]
    slopes = (2.0 ** (-8.0 * np.arange(1, H_B + 1) / H_B)).reshape(ng, gh)
    slope_arr = jnp.asarray(np.broadcast_to(slopes[:, :, None], (ng, gh, LANES)), F32)
    row_spec = pl.BlockSpec((8, gw), lambda g, b: (b // 8, g))
    t_spec = pl.BlockSpec((None, gw, wb), lambda g, b: (b, g, 0))
    return pl.pallas_call(
        _attn_sample_kernel,
        grid=(ng, rows),
        in_specs=[pl.BlockSpec((None, gh, LANES), lambda g, b: (g, 0, 0)),
                  pl.BlockSpec((4, wb), lambda g, b: (0, 0)),
                  row_spec, row_spec, row_spec, row_spec, t_spec, t_spec],
        out_specs=row_spec,
        out_shape=jax.ShapeDtypeStruct((rows, D_MODEL), F32),
        compiler_params=_cparams(("arbitrary", "arbitrary")),
        name="attn_sample",
    )(slope_arr, jnp.asarray(np.stack(pos), F32), q, kn, vn, sgb, cache_kt, cache_vt)


def _odd_front_sample_kernel(h_ref, *refs):
    ws32 = refs[:N_SEG_O]
    pw_ref, ps_ref, prev_ref = refs[N_SEG_O:N_SEG_O + 3]
    yc_ref, q_ref, k_ref, v_ref, sgd_ref, pn_ref = refs[N_SEG_O + 3:N_SEG_O + 9]
    w16_refs = refs[N_SEG_O + 9:]
    ws = [w[...].astype(BF16) for w in ws32]
    for w16_ref, w in zip(w16_refs, ws):
        w16_ref[...] = w
    wu, wgc, wq, wk, wv, wgd = ws
    j = pl.program_id(0)
    h = h_ref[...]
    u = _dot(h, wu)
    n_prev = prev_ref.shape[0]
    sums = []
    s = jnp.zeros_like(u)
    back = 0
    for w in POOL_SIZES:
        while back < w - 1:
            s = s + prev_ref[n_prev - 1 - back]
            back += 1
        sums.append(s)
    win = _pick_by_chunk(j, sums) + u
    inv_w = _pick_by_chunk(j, [1.0 / w for w in POOL_SIZES])
    pooled = win * inv_w - u
    mixed = _dot(pooled.astype(BF16), pw_ref[...]) * ps_ref[...]
    yc_ref[...] = (mixed * _silu(_dot(h, wgc))).astype(BF16)
    q_ref[...] = _dot(h, wq)
    k_ref[...] = _dot(h, wk) * (DK_D ** -0.5)
    v_ref[...] = _dot(h, wv)
    sgd_ref[...] = _silu(_dot(h, wgd))
    for t in range(n_prev - 1):
        pn_ref[t] = prev_ref[t + 1]
    pn_ref[n_prev - 1] = u


def _odd_front_sample(h2d, w_in, pw, ps, prev_t):
    rows = h2d.shape[0]
    n_prev = prev_t.shape[0]
    nj = D_MODEL // TN

    def wspec(s):
        return pl.BlockSpec((D_MODEL, TN), lambda j, s=s: (0, s * nj + j))

    col = pl.BlockSpec((rows, TN), lambda j: (0, j))
    st = pl.BlockSpec((n_prev, rows, TN), lambda j: (0, 0, j))
    colf = jax.ShapeDtypeStruct((rows, D_MODEL), F32)
    w16_spec = pl.BlockSpec((D_MODEL, TN), lambda j: (0, j))
    w16_shape = jax.ShapeDtypeStruct((D_MODEL, D_MODEL), BF16)
    outs = pl.pallas_call(
        _odd_front_sample_kernel,
        grid=(nj,),
        in_specs=[pl.BlockSpec((rows, D_MODEL), lambda j: (0, 0))] + [wspec(s) for s in range(N_SEG_O)] + [
            pl.BlockSpec((None, TN, TN), lambda j: (j, 0, 0)), pl.BlockSpec((1, TN), lambda j: (0, j)), st],
        out_specs=[col, col, col, col, col, st] + [w16_spec] * N_SEG_O,
        out_shape=[jax.ShapeDtypeStruct((rows, D_MODEL), BF16), colf, colf, colf, colf,
                   jax.ShapeDtypeStruct((n_prev, rows, D_MODEL), F32)] + [w16_shape] * N_SEG_O,
        compiler_params=_cparams(("arbitrary",)),
        name="odd_front_sample",
    )(h2d, *([w_in] * N_SEG_O), pw, ps.reshape(1, D_MODEL), prev_t)
    return outs[:6], outs[6:]


RET_SAMPLE_ROWS = 8


def _ret_sample_kernel(q_ref, k_ref, v_ref, sgd_ref, st_ref, yd_ref, sn_ref):
    row = lax.broadcasted_iota(jnp.int32, (DK_D, DK_D), 0)
    col = lax.broadcasted_iota(jnp.int32, (DK_D, DK_D), 1)
    qs, ks, vs, gs = q_ref[...], k_ref[...], v_ref[...], sgd_ref[...]
    out_rows = []
    for i in range(RET_SAMPLE_ROWS):
        out_heads = []
        for h in range(H_D):
            g = float(np.exp(RET_LOG_G[h]))
            cols = slice(h * DK_D, (h + 1) * DK_D)
            q, k, v = qs[i:i + 1, cols], ks[i:i + 1, cols], vs[i:i + 1, cols]
            state = st_ref[i, h]
            qk = jnp.sum(q * k, axis=-1, keepdims=True)
            q8 = jnp.broadcast_to(q, (8, DK_D)).astype(BF16)
            cross = _dot(q8, state.astype(BF16))[0:1, :]
            o = qk * v + g * cross
            k_diag = jnp.where(row == col, jnp.broadcast_to(k, (DK_D, DK_D)), 0.0).astype(BF16)
            v_rows = jnp.broadcast_to(v, (DK_D, DK_D)).astype(BF16)
            sn_ref[i, h] = g * state + _dot(k_diag, v_rows)
            out_heads.append(_rms(o) * gs[i:i + 1, cols])
        out_rows.append(jnp.concatenate(out_heads, axis=1))
    yd_ref[...] = jnp.concatenate(out_rows, axis=0)


def _ret_sample(q, k, v, sgd, state):
    rows = q.shape[0]
    rs = RET_SAMPLE_ROWS
    row_spec = pl.BlockSpec((rs, D_MODEL), lambda i: (i, 0))
    st_spec = pl.BlockSpec((rs, H_D, DK_D, DK_D), lambda i: (i, 0, 0, 0))
    return pl.pallas_call(
        _ret_sample_kernel,
        grid=(rows // rs,),
        in_specs=[row_spec, row_spec, row_spec, row_spec, st_spec],
        out_specs=[row_spec, st_spec],
        out_shape=[jax.ShapeDtypeStruct((rows, D_MODEL), F32),
                   jax.ShapeDtypeStruct((rows, H_D, DK_D, DK_D), F32)],
        compiler_params=_cparams(("arbitrary",)),
        name="ret_sample",
    )(q, k, v, sgd, state)


def kernel(x_prompt, x_sample, c_prompt, c_sample, state_conv, cache_win_k, cache_win_v, state_pool, state_ret,
           norm_e, ada_w_e, ada_b_e, w_in_e, conv_w, conv_b, w_out_e, norm_o, ada_w_o, ada_b_o, w_in_o,
           pool_w, pool_scale, w_out_o, norm_f):
    batch, seq, d = x_prompt.shape
    sb = x_sample.shape[0]
    assert d == D_MODEL and x_sample.shape[1] == 1
    assert norm_e.shape[0] == 1 and norm_o.shape[0] == 1
    wb = cache_win_k.shape[2]
    keep = min(wb, seq)
    assert wb == DILATIONS[-1] * NK and seq % (DILATIONS[-1] * NK) == 0
    tm = TM_OUT

    n_c = batch + sb
    pad = (-n_c) % 8
    c_all = jnp.concatenate([c_prompt, c_sample, jnp.zeros((pad, d), F32)], axis=0)
    mod_e, mod_o = _adaln(c_all, ada_w_e[0], ada_b_e[0], ada_w_o[0], ada_b_o[0])
    mod_e_p, mod_e_s = mod_e[:batch], mod_e[batch:n_c]
    mod_o_p, mod_o_s = mod_o[:batch], mod_o[batch:n_c]

    w_out_e16 = w_out_e[0].astype(BF16)
    w_out_o16 = w_out_o[0].astype(BF16)
    pool_w16 = pool_w[0].astype(BF16)

    xs2d = x_sample.reshape(sb, d)
    conv_prev_t = jnp.transpose(state_conv[0], (1, 0, 2))
    (ya_s, q_s, k_s, v_s, sgb_s, conv_s_t), w_e_segs = _even_front_sample(
        xs2d, mod_e_s, norm_e[0], w_in_e[0], conv_w[0], conv_b[0], conv_prev_t)
    cache_kt = jnp.transpose(cache_win_k[0], (0, 2, 3, 1)).reshape(sb, d, wb)
    cache_vt = jnp.transpose(cache_win_v[0], (0, 2, 3, 1)).reshape(sb, d, wb)
    yb_s = _attn_sample(q_s, k_s, v_s, sgb_s, cache_kt, cache_vt)
    x1_s, h1_s = _outproj(ya_s, yb_s, xs2d, mod_e_s, w_out_e16, norm_o[0], mod_o_s, sb, 1)
    pool_prev_t = jnp.transpose(state_pool[0], (1, 0, 2))
    (yc_s, rq_s, rk_s, rv_s, sgd_s, pool_s_t), w_o_segs = _odd_front_sample(
        h1_s, w_in_o[0], pool_w16, pool_scale[0], pool_prev_t)
    yd_s, ret_s = _ret_sample(rq_s, rk_s, rv_s, sgd_s, state_ret[0])
    y_sample = _outproj(yc_s, yd_s, x1_s, mod_o_s, w_out_o16, norm_f, None, sb, 1)

    xp2d = x_prompt.reshape(batch * seq, d)
    ya, q, k, v, sgb, k_new, v_new, conv_tail = _even_front_prompt(
        xp2d, mod_e_p, norm_e[0], w_e_segs, conv_w[0], conv_b[0], batch, seq, keep, TM_FRONT)
    yb = _attn_prompt(q, k, v, sgb, batch, seq).reshape(batch * seq, d)
    x1, h1 = _outproj(ya, yb, xp2d, mod_e_p, w_out_e16, norm_o[0], mod_o_p, tm, seq)
    yc, rq, rk, rv, sgd, pool_tail = _odd_front_prompt(h1, w_o_segs, pool_w16, pool_scale[0], batch, seq, TM_FRONT)
    yd, ret_p = _ret_prompt(rq, rk, rv, sgd, batch, seq)
    y_prompt = _outproj(yc, yd.reshape(batch * seq, d), x1, mod_o_p, w_out_o16, norm_f, None, tm, seq)

    return (
        y_prompt.reshape(batch, seq, d),
        y_sample.reshape(sb, 1, d),
        conv_tail[:, 6:8][None],
        jnp.transpose(conv_s_t, (1, 0, 2))[None],
        jnp.transpose(k_new.reshape(batch, H_B, HD_B, keep), (0, 3, 1, 2))[None],
        k_s.reshape(1, sb, 1, H_B, HD_B),
        jnp.transpose(v_new.reshape(batch, H_B, HD_B, keep), (0, 3, 1, 2))[None],
        v_s.reshape(1, sb, 1, H_B, HD_B),
        pool_tail[:, 1:][None],
        jnp.transpose(pool_s_t, (1, 0, 2))[None],
        ret_p[None],
        ret_s[None],
    )
```
